```python
import jax
import jax.numpy as jnp
from jax import lax
import numpy as np

D_MODEL = 2048
BATCH = 4
SEQ = 4096
DEPTH = 4

HEAD_DIM = 128
A_GROUPS = 4
A_WIDTH = A_GROUPS * HEAD_DIM
CHUNK = 128
B_HEADS = 4
DILATED_PAIRS = ((128, 1), (512, 4), (2048, 16))
C_HEADS = 8
C_KV_HEADS = 2
C_GROUP = C_HEADS // C_KV_HEADS
CMP_LEN = 32
CMP_STRIDE = 16
SEL_LEN = 64
SEL_TOP = 16
WIN_LEN = 512
NSA_QBLOCK = 64
BAND_BLOCK = 128
MIX_WIDTH = A_WIDTH + (B_HEADS + C_HEADS) * HEAD_DIM
IN_SPLITS = (2 * A_WIDTH,
             3 * B_HEADS * HEAD_DIM,
             C_HEADS * HEAD_DIM,
             2 * C_KV_HEADS * HEAD_DIM,
             2 * C_KV_HEADS * HEAD_DIM,
             2 * C_KV_HEADS * HEAD_DIM,
             3 * C_HEADS)
IN_WIDTH = sum(IN_SPLITS)
N_MEM = 256
X_HEADS = 4
X_WIDTH = X_HEADS * HEAD_DIM
D_FF = 4 * D_MODEL
EPS = 1e-6
NEG_INF = -1e30

kernel_name = 'hybrid_gmlp_dilated_nsa_decoder'


def rms_norm(x, g):
    xf = x.astype(jnp.float32)
    y = xf * lax.rsqrt(jnp.mean(xf * xf, axis=-1, keepdims=True) + EPS)
    return (y * g.astype(jnp.float32)).astype(x.dtype)


def layer_norm(x, g, b):
    xf = x.astype(jnp.float32)
    xc = xf - jnp.mean(xf, axis=-1, keepdims=True)
    y = xc * lax.rsqrt(jnp.mean(xc * xc, axis=-1, keepdims=True) + EPS)
    return (y * g.astype(jnp.float32) + b.astype(jnp.float32)).astype(x.dtype)


def banded_attention(q, k, v, max_dist):
    bsz, ng, nr, L, hd = q.shape
    blk = BAND_BLOCK
    nprev = -(-max_dist // blk)
    nb = -(-L // blk)
    pad = nb * blk - L
    qb = jnp.pad(q, ((0, 0), (0, 0), (0, 0), (0, pad), (0, 0))).reshape(bsz, ng, nr, nb, blk, hd)

    def windows(t):
        tb = jnp.pad(t, ((0, 0), (0, 0), (nprev * blk, pad), (0, 0))).reshape(bsz, ng, nb + nprev, blk, hd)
        return jnp.concatenate([tb[:, :, i:i + nb] for i in range(nprev + 1)], axis=3)

    kw, vw = windows(k), windows(v)
    s = jnp.einsum('bgrnqd,bgnkd->bgrnqk', qb, kw).astype(jnp.float32) * (hd ** -0.5)
    qpos = jnp.arange(nb)[:, None] * blk + jnp.arange(blk)[None, :]
    kpos = (jnp.arange(nb)[:, None] - nprev) * blk + jnp.arange((nprev + 1) * blk)[None, :]
    dist = qpos[:, :, None] - kpos[:, None, :]
    mask = (dist >= 0) & (dist <= max_dist) & (kpos[:, None, :] >= 0)
    s = jnp.where(mask, s, NEG_INF)
    m = jnp.max(s, axis=-1, keepdims=True)
    e = jnp.exp(s - m)
    den = jnp.sum(e, axis=-1, keepdims=True)
    o = jnp.einsum('bgrnqk,bgnkd->bgrnqd', (e / den).astype(v.dtype), vw)
    lse = (m + jnp.log(den))[..., 0]
    o = o.reshape(bsz, ng, nr, nb * blk, hd)[:, :, :, :L]
    lse = lse.reshape(bsz, ng, nr, nb * blk)[:, :, :, :L]
    return o, lse


def spatial_gating(uv, ln_g, ln_b, w_s, b_s):
    bsz, S, _ = uv.shape
    u, v = jnp.split(jax.nn.gelu(uv), 2, axis=-1)
    v = layer_norm(v, ln_g, ln_b).reshape(bsz, S // CHUNK, CHUNK, A_GROUPS, HEAD_DIM)
    w = w_s * jnp.tril(jnp.ones((CHUNK, CHUNK), w_s.dtype))
    sv = jnp.einsum('gts,bcsgh->bctgh', w, v) + b_s.T[:, :, None]
    return u * sv.reshape(bsz, S, A_WIDTH)


def dilated_attention(q, k, v):
    bsz, nh, S, hd = q.shape
    outs, lses = [], []
    for window, d in DILATED_PAIRS:
        def fold(t):
            return t.reshape(bsz, nh, S // d, d, hd).transpose(0, 1, 3, 2, 4).reshape(bsz, nh * d, S // d, hd)
        o, l = banded_attention(fold(q)[:, :, None], fold(k), fold(v), window // d)
        outs.append(o[:, :, 0].reshape(bsz, nh, d, S // d, hd).transpose(0, 1, 3, 2, 4).reshape(bsz, nh, S, hd))
        lses.append(l[:, :, 0].reshape(bsz, nh, d, S // d).transpose(0, 1, 3, 2).reshape(bsz, nh, S))
    w = jax.nn.softmax(jnp.stack(lses), axis=0)
    o = jnp.einsum('pbhs,pbhsd->bhsd', w.astype(q.dtype), jnp.stack(outs))
    return o.transpose(0, 2, 1, 3).reshape(bsz, S, nh * hd)


def compress_blocks(kraw, pos, w1, w2):
    bsz, ng, S, hd = kraw.shape
    sub = kraw.reshape(bsz, ng, S // CMP_STRIDE, CMP_STRIDE, hd)
    blocks = jnp.concatenate([sub[:, :, :-1], sub[:, :, 1:]], axis=3) + pos
    n_c = blocks.shape[2]
    h = jax.nn.gelu(blocks.reshape(bsz, ng, n_c, CMP_LEN * hd) @ w1)
    return h @ w2


def nsa_attention(q, kv_cmp, kv_slc, kv_win, gates, cmp_pos, ck_w1, ck_w2, cv_w1, cv_w2):
    bsz, ng, nr, S, hd = q.shape
    scale = hd ** -0.5
    k_c = compress_blocks(kv_cmp[0], cmp_pos, ck_w1, ck_w2)
    v_c = compress_blocks(kv_cmp[1], cmp_pos, cv_w1, cv_w2)
    n_c = k_c.shape[2]
    n_sel = S // SEL_LEN
    n_top = min(SEL_TOP, n_sel)
    k_s = kv_slc[0].reshape(bsz, ng, n_sel, SEL_LEN, hd)
    v_s = kv_slc[1].reshape(bsz, ng, n_sel, SEL_LEN, hd)
    c_start = np.arange(n_c) * CMP_STRIDE
    s_start = np.arange(n_sel) * SEL_LEN
    overlap = jnp.asarray(((c_start[:, None] <= s_start[None, :] + SEL_LEN - 1)
                           & (c_start[:, None] + CMP_LEN - 1 >= s_start[None, :])).astype(np.float32))
    c_end = jnp.arange(n_c) * CMP_STRIDE + CMP_LEN - 1
    sel_idx = jnp.arange(n_sel)
    b_idx = jnp.arange(bsz)[:, None, None, None]
    g_idx = jnp.arange(ng)[None, :, None, None]
    nqb = S // NSA_QBLOCK
    q_blocks = q.reshape(bsz, ng, nr, nqb, NSA_QBLOCK, hd).transpose(3, 0, 1, 2, 4, 5)

    def block_step(args):
        qb, bi = args
        t = bi * NSA_QBLOCK + jnp.arange(NSA_QBLOCK)
        valid_c = c_end[None, :] <= t[:, None]
        s_c = jnp.einsum('bgrqd,bgnd->bgrqn', qb, k_c).astype(jnp.float32) * scale
        p_c = jax.nn.softmax(jnp.where(valid_c, s_c, NEG_INF), axis=-1) * valid_c
        o_c = jnp.einsum('bgrqn,bgnd->bgrqd', p_c.astype(v_c.dtype), v_c)
        imp = jnp.einsum('bgrqn,nj->bgqj', p_c, overlap)
        jt = t // SEL_LEN
        forced = (sel_idx[None] == 0) | (sel_idx[None] == jt[:, None]) | (sel_idx[None] == jt[:, None] - 1)
        valid_s = sel_idx[None] * SEL_LEN <= t[:, None]
        score = jnp.where(forced, 1e4, jnp.where(valid_s, imp, -1.0))
        _, idx = lax.top_k(score, n_top)
        ks = k_s[b_idx, g_idx, idx]
        vs = v_s[b_idx, g_idx, idx]
        kpos = idx[..., None] * SEL_LEN + jnp.arange(SEL_LEN)
        mask_s = (kpos <= t[None, None, :, None, None])[:, :, None]
        s_s = jnp.einsum('bgrqd,bgqnkd->bgrqnk', qb, ks).astype(jnp.float32) * scale
        s_s = jnp.where(mask_s, s_s, NEG_INF).reshape(bsz, ng, nr, NSA_QBLOCK, n_top * SEL_LEN)
        p_s = jax.nn.softmax(s_s, axis=-1).reshape(bsz, ng, nr, NSA_QBLOCK, n_top, SEL_LEN)
        o_s = jnp.einsum('bgrqnk,bgqnkd->bgrqd', p_s.astype(vs.dtype), vs)
        return o_c, o_s

    o_c, o_s = lax.map(block_step, (q_blocks, jnp.arange(nqb)))
    o_c = o_c.transpose(1, 2, 3, 0, 4, 5).reshape(bsz, ng, nr, S, hd)
    o_s = o_s.transpose(1, 2, 3, 0, 4, 5).reshape(bsz, ng, nr, S, hd)
    o_w, _ = banded_attention(q, kv_win[0], kv_win[1], WIN_LEN - 1)
    g = jax.nn.sigmoid(gates.astype(jnp.float32)).reshape(bsz, S, ng, nr, 3)
    g = g.transpose(4, 0, 2, 3, 1)[..., None].astype(q.dtype)
    o = g[0] * o_c + g[1] * o_s + g[2] * o_w
    return o.transpose(0, 3, 1, 2, 4).reshape(bsz, S, ng * nr * hd)


def setup_inputs(seed: int = 0) -> dict:
    key = jax.random.key(seed)
    ks = jax.random.split(key, 24)
    f32 = jnp.float32

    def normal(k, shape, scale):
        return jax.random.normal(k, shape, f32) * scale

    def gain(k, shape):
        return 1.0 + normal(k, shape, 0.02)

    L = DEPTH
    return {
        'x': normal(ks[0], (BATCH, SEQ, D_MODEL), 1.0),
        'mem': normal(ks[1], (BATCH, N_MEM, D_MODEL), 1.0),
        'norm_mix': gain(ks[2], (L, D_MODEL)),
        'w_in': normal(ks[3], (L, D_MODEL, IN_WIDTH), D_MODEL ** -0.5),
        'gmlp_ln_g': gain(ks[4], (L, A_WIDTH)),
        'gmlp_ln_b': normal(ks[5], (L, A_WIDTH), 0.02),
        'gmlp_w_s': normal(ks[6], (L, A_GROUPS, CHUNK, CHUNK), CHUNK ** -0.5),
        'gmlp_b_s': gain(ks[7], (L, A_GROUPS, CHUNK)),
        'cmp_pos': normal(ks[8], (L, CMP_LEN, HEAD_DIM), 0.02),
        'cmp_k_w1': normal(ks[9], (L, CMP_LEN * HEAD_DIM, HEAD_DIM), (CMP_LEN * HEAD_DIM) ** -0.5),
        'cmp_k_w2': normal(ks[10], (L, HEAD_DIM, HEAD_DIM), HEAD_DIM ** -0.5),
        'cmp_v_w1': normal(ks[11], (L, CMP_LEN * HEAD_DIM, HEAD_DIM), (CMP_LEN * HEAD_DIM) ** -0.5),
        'cmp_v_w2': normal(ks[12], (L, HEAD_DIM, HEAD_DIM), HEAD_DIM ** -0.5),
        'w_out': normal(ks[13], (L, MIX_WIDTH, D_MODEL), 0.5 * MIX_WIDTH ** -0.5),
        'norm_xattn': gain(ks[14], (L, D_MODEL)),
        'norm_mem': gain(ks[15], (L, D_MODEL)),
        'xattn_wq': normal(ks[16], (L, D_MODEL, X_WIDTH), D_MODEL ** -0.5),
        'xattn_wkv': normal(ks[17], (L, D_MODEL, 2 * X_WIDTH), D_MODEL ** -0.5),
        'xattn_wo': normal(ks[18], (L, X_WIDTH, D_MODEL), 0.5 * X_WIDTH ** -0.5),
        'norm_mlp': gain(ks[19], (L, D_MODEL)),
        'w_up': normal(ks[20], (L, D_MODEL, D_FF), D_MODEL ** -0.5),
        'w_down': normal(ks[21], (L, D_FF, D_MODEL), 0.5 * D_FF ** -0.5),
        'final_norm': gain(ks[22], (D_MODEL,)),
    }


def reference(x, mem, norm_mix, w_in, gmlp_ln_g, gmlp_ln_b, gmlp_w_s, gmlp_b_s, cmp_pos,
              cmp_k_w1, cmp_k_w2, cmp_v_w1, cmp_v_w2, w_out, norm_xattn, norm_mem,
              xattn_wq, xattn_wkv, xattn_wo, norm_mlp, w_up, w_down, final_norm):
    bsz, S, _ = x.shape
    n_mem = mem.shape[1]
    split_at = [int(i) for i in np.cumsum(IN_SPLITS)[:-1]]
    x_scale = HEAD_DIM ** -0.5
    for l in range(DEPTH):
        h = rms_norm(x, norm_mix[l])
        z = h @ w_in[l]
        z_a, z_b, z_cq, z_ckv_c, z_ckv_s, z_ckv_w, z_cg = jnp.split(z, split_at, axis=-1)
        out_a = spatial_gating(z_a, gmlp_ln_g[l], gmlp_ln_b[l], gmlp_w_s[l], gmlp_b_s[l])
        qkv_b = z_b.reshape(bsz, S, 3, B_HEADS, HEAD_DIM).transpose(2, 0, 3, 1, 4)
        out_b = dilated_attention(qkv_b[0], qkv_b[1], qkv_b[2])
        q_c = z_cq.reshape(bsz, S, C_KV_HEADS, C_GROUP, HEAD_DIM).transpose(0, 2, 3, 1, 4)

        def kv_heads(t):
            return t.reshape(bsz, S, 2, C_KV_HEADS, HEAD_DIM).transpose(2, 0, 3, 1, 4)

        out_c = nsa_attention(q_c, kv_heads(z_ckv_c), kv_heads(z_ckv_s), kv_heads(z_ckv_w), z_cg,
                              cmp_pos[l], cmp_k_w1[l], cmp_k_w2[l], cmp_v_w1[l], cmp_v_w2[l])
        x = x + jnp.concatenate([out_a, out_b, out_c], axis=-1) @ w_out[l]
        h = rms_norm(x, norm_xattn[l])
        m = rms_norm(mem, norm_mem[l])
        q = (h @ xattn_wq[l]).reshape(bsz, S, X_HEADS, HEAD_DIM)
        kv = (m @ xattn_wkv[l]).reshape(bsz, n_mem, 2, X_HEADS, HEAD_DIM)
        s = jnp.einsum('bshd,bmhd->bhsm', q, kv[:, :, 0]).astype(jnp.float32) * x_scale
        p = jax.nn.softmax(s, axis=-1).astype(x.dtype)
        o = jnp.einsum('bhsm,bmhd->bshd', p, kv[:, :, 1]).reshape(bsz, S, X_WIDTH)
        x = x + o @ xattn_wo[l]
        h = rms_norm(x, norm_mlp[l])
        x = x + jnp.square(jax.nn.relu(h @ w_up[l])) @ w_down[l]
    return rms_norm(x, final_norm)
```

```python
import functools
import math

import numpy as np
import jax
import jax.numpy as jnp
from jax import lax
from jax.experimental import pallas as pl
from jax.experimental.pallas import tpu as pltpu

F32 = jnp.float32
BF16 = jnp.bfloat16

D_MODEL = 2048
DEPTH = 4
HEAD_DIM = 128
A_GROUPS = 4
A_WIDTH = A_GROUPS * HEAD_DIM
CHUNK = 128
B_HEADS = 4
DILATED_PAIRS = ((128, 1), (512, 4), (2048, 16))
C_HEADS = 8
C_KV_HEADS = 2
C_GROUP = C_HEADS // C_KV_HEADS
CMP_LEN = 32
CMP_STRIDE = 16
SEL_LEN = 64
SEL_TOP = 16
WIN_LEN = 512
X_HEADS = 4
X_WIDTH = X_HEADS * HEAD_DIM
EPS = 1e-6
NEG_INF = -1e30

COL_A = 0
COL_B = 2 * A_WIDTH
COL_CQ = COL_B + 3 * B_HEADS * HEAD_DIM
COL_CMP = COL_CQ + C_HEADS * HEAD_DIM
COL_SEL = COL_CMP + 2 * C_KV_HEADS * HEAD_DIM
COL_WIN = COL_SEL + 2 * C_KV_HEADS * HEAD_DIM
COL_GATE = COL_WIN + 2 * C_KV_HEADS * HEAD_DIM
IN_WIDTH = COL_GATE + 3 * C_HEADS
IN_TILE = 768
IN_PAD = 5376

LANE = 128
VMEM_LIMIT = 56 * 1024 * 1024

ATT_TQ = 256
ATT_TK = 256


def _cparams(sem):
    return pltpu.CompilerParams(dimension_semantics=sem, vmem_limit_bytes=VMEM_LIMIT)


def _rms_rows(x, g):
    ms = jnp.mean(x * x, axis=-1, keepdims=True)
    return x * lax.rsqrt(ms + EPS) * g


def _gelu_tanh(x):
    c = math.sqrt(2.0 / math.pi)
    return x * (0.5 * (1.0 + jnp.tanh(c * (x + 0.044715 * (x * x * x)))))


def _dot(a, b):
    return jnp.dot(a, b, preferred_element_type=F32)


def _dot_nt(a, b):
    return lax.dot_general(a, b, (((1,), (1,)), ((), ())), preferred_element_type=F32)


NORM_ROWS = 256


def _norm_into(h_ref, x_ref, g_ref):
    rows = x_ref.shape[0]

    def body(c, carry):
        r0 = pl.multiple_of(c * NORM_ROWS, NORM_ROWS)
        x = x_ref[pl.ds(r0, NORM_ROWS), :]
        h_ref[pl.ds(r0, NORM_ROWS), :] = _rms_rows(x, g_ref[...]).astype(h_ref.dtype)
        return carry

    lax.fori_loop(0, rows // NORM_ROWS, body, 0)


def _norm_matmul_kernel(x_ref, g_ref, w_ref, o_ref, h_ref):
    @pl.when(pl.program_id(1) == 0)
    def _():
        _norm_into(h_ref, x_ref, g_ref)

    o_ref[...] = _dot(h_ref[...], w_ref[...]).astype(o_ref.dtype)


def norm_matmul(x, g, w, *, tm, tn, out_dtype=F32):
    m, k = x.shape
    n = w.shape[1]
    return pl.pallas_call(
        _norm_matmul_kernel,
        out_shape=jax.ShapeDtypeStruct((m, n), out_dtype),
        grid=(m // tm, n // tn),
        in_specs=[
            pl.BlockSpec((tm, k), lambda i, j: (i, 0)),
            pl.BlockSpec((1, k), lambda i, j: (0, 0)),
            pl.BlockSpec((k, tn), lambda i, j: (0, j)),
        ],
        out_specs=pl.BlockSpec((tm, tn), lambda i, j: (i, j)),
        scratch_shapes=[pltpu.VMEM((tm, k), BF16)],
        compiler_params=_cparams(("parallel", "arbitrary")),
        name="norm_matmul",
    )(x, g.reshape(1, k), w)


def _res_matmul_kernel(*refs, n_parts):
    x_ref = refs[0]
    a_refs = refs[1:1 + n_parts]
    w_refs = refs[1 + n_parts:1 + 2 * n_parts]
    o_ref = refs[1 + 2 * n_parts]
    acc = x_ref[...]
    for a_ref, w_ref in zip(a_refs, w_refs):
        acc = acc + _dot(a_ref[...].astype(BF16), w_ref[...])
    o_ref[...] = acc


def res_matmul(x, parts, weights, *, tm, tn):
    m, n = x.shape
    n_parts = len(parts)
    in_specs = [pl.BlockSpec((tm, tn), lambda i, j: (i, j))]
    for a in parts:
        in_specs.append(pl.BlockSpec((tm, a.shape[1]), lambda i, j: (i, 0)))
    for w in weights:
        in_specs.append(pl.BlockSpec((w.shape[0], tn), lambda i, j: (0, j)))
    return pl.pallas_call(
        functools.partial(_res_matmul_kernel, n_parts=n_parts),
        out_shape=jax.ShapeDtypeStruct((m, n), F32),
        grid=(m // tm, n // tn),
        in_specs=in_specs,
        out_specs=pl.BlockSpec((tm, tn), lambda i, j: (i, j)),
        compiler_params=_cparams(("parallel", "parallel")),
        name="res_matmul",
    )(x, *parts, *weights)


def _mlp_kernel(x_ref, g_ref, wu_ref, wd_ref, o_ref, h_ref):
    f = pl.program_id(1)

    @pl.when(f == 0)
    def _():
        _norm_into(h_ref, x_ref, g_ref)
        o_ref[...] = x_ref[...]

    a = _dot(h_ref[...], wu_ref[...])
    a = jnp.square(jnp.maximum(a, 0.0)).astype(BF16)
    o_ref[...] += _dot(a, wd_ref[...])


def mlp(x, g, wu, wd, *, tm, tf):
    m, d = x.shape
    ff = wu.shape[1]
    return pl.pallas_call(
        _mlp_kernel,
        out_shape=jax.ShapeDtypeStruct((m, d), F32),
        grid=(m // tm, ff // tf),
        in_specs=[
            pl.BlockSpec((tm, d), lambda i, f: (i, 0)),
            pl.BlockSpec((1, d), lambda i, f: (0, 0)),
            pl.BlockSpec((d, tf), lambda i, f: (0, f)),
            pl.BlockSpec((tf, d), lambda i, f: (f, 0)),
        ],
        out_specs=pl.BlockSpec((tm, d), lambda i, f: (i, 0)),
        scratch_shapes=[pltpu.VMEM((tm, d), BF16)],
        compiler_params=_cparams(("parallel", "arbitrary")),
        name="mlp",
    )(x, g.reshape(1, d), wu, wd)


def _rmsnorm_kernel(x_ref, g_ref, o_ref):
    o_ref[...] = _rms_rows(x_ref[...], g_ref[...])


def rmsnorm(x, g, *, tm):
    m, d = x.shape
    return pl.pallas_call(
        _rmsnorm_kernel,
        out_shape=jax.ShapeDtypeStruct((m, d), F32),
        grid=(m // tm,),
        in_specs=[pl.BlockSpec((tm, d), lambda i: (i, 0)),
                  pl.BlockSpec((1, d), lambda i: (0, 0))],
        out_specs=pl.BlockSpec((tm, d), lambda i: (i, 0)),
        compiler_params=_cparams(("parallel",)),
        name="final_rmsnorm",
    )(x, g.reshape(1, d))


def _gmlp_kernel(z_ref, lng_ref, lnb_ref, ws_ref, bst_ref, o_ref, *, n_chunks):
    row = lax.broadcasted_iota(jnp.int32, (CHUNK, CHUNK), 0)
    col = lax.broadcasted_iota(jnp.int32, (CHUNK, CHUNK), 1)
    causal = row >= col
    w_tril = [jnp.where(causal, ws_ref[gi], 0.0).astype(BF16) for gi in range(A_GROUPS)]
    for c in range(n_chunks):
        rows = slice(c * CHUNK, (c + 1) * CHUNK)
        u = _gelu_tanh(z_ref[rows, 0:A_WIDTH])
        v = _gelu_tanh(z_ref[rows, A_WIDTH:2 * A_WIDTH])
        vc = v - jnp.mean(v, axis=-1, keepdims=True)
        vn = vc * lax.rsqrt(jnp.mean(vc * vc, axis=-1, keepdims=True) + EPS)
        vn = vn * lng_ref[...] + lnb_ref[...]
        for gi in range(A_GROUPS):
            cols = slice(gi * HEAD_DIM, (gi + 1) * HEAD_DIM)
            sv = _dot(w_tril[gi], vn[:, cols].astype(BF16)) + bst_ref[:, gi:gi + 1]
            o_ref[rows, cols] = (u[:, cols] * sv).astype(o_ref.dtype)


def gmlp(z, ln_g, ln_b, w_s, b_s, *, tm):
    t = z.shape[0]
    return pl.pallas_call(
        functools.partial(_gmlp_kernel, n_chunks=tm // CHUNK),
        out_shape=jax.ShapeDtypeStruct((t, A_WIDTH), F32),
        grid=(t // tm,),
        in_specs=[
            pl.BlockSpec((tm, 2 * A_WIDTH), lambda i: (i, 0)),
            pl.BlockSpec((1, A_WIDTH), lambda i: (0, 0)),
            pl.BlockSpec((1, A_WIDTH), lambda i: (0, 0)),
            pl.BlockSpec((A_GROUPS, CHUNK, CHUNK), lambda i: (0, 0, 0)),
            pl.BlockSpec((CHUNK, A_GROUPS), lambda i: (0, 0)),
        ],
        out_specs=pl.BlockSpec((tm, A_WIDTH), lambda i: (i, 0)),
        compiler_params=_cparams(("parallel",)),
        name="gmlp_gating",
    )(z, ln_g.reshape(1, A_WIDTH), ln_b.reshape(1, A_WIDTH), w_s, b_s.T)


def _flash_kernel(*refs, n_rep, mode, max_dist, want_lse):
    refs = list(refs)
    q_ref, k_ref, v_ref = refs[:3]
    pos = 3
    sel_ref = None
    if mode == "select":
        sel_ref = refs[pos]
        pos += 1
    o_ref = refs[pos]
    pos += 1
    lse_ref = None
    if want_lse:
        lse_ref = refs[pos]
        pos += 1
    qs_ref, m_ref, l_ref, acc_ref = refs[pos:pos + 4]

    tq, tk = ATT_TQ, ATT_TK
    i = pl.program_id(2)
    q0 = i * tq
    scale = HEAD_DIM ** -0.5

    for r in range(n_rep):
        qs_ref[r * tq:(r + 1) * tq, :] = q_ref[:, r * HEAD_DIM:(r + 1) * HEAD_DIM].astype(BF16)
    m_ref[...] = jnp.full(m_ref.shape, NEG_INF, F32)
    l_ref[...] = jnp.zeros(l_ref.shape, F32)
    acc_ref[...] = jnp.zeros(acc_ref.shape, F32)

    if mode == "band":
        n_prev = -(-max_dist // tk)
        kt_lo = jnp.maximum(i - n_prev, 0)
    else:
        kt_lo = 0

    def body(kt, carry):
        k0 = pl.multiple_of(kt * tk, tk)
        k = k_ref[pl.ds(k0, tk), :].astype(BF16)
        v = v_ref[pl.ds(k0, tk), :].astype(BF16)
        s = _dot_nt(qs_ref[...], k) * scale
        tpos = q0 + lax.broadcasted_iota(jnp.int32, (tq, tk), 0)
        kpos = k0 + lax.broadcasted_iota(jnp.int32, (tq, tk), 1)
        dist = tpos - kpos
        if mode == "band":
            mask = (dist >= 0) & (dist <= max_dist)
        else:
            blk = lax.broadcasted_iota(jnp.int32, (LANE, tk), 0)
            kblk = (k0 + lax.broadcasted_iota(jnp.int32, (LANE, tk), 1)) >> 6
            expand = jnp.where(blk == kblk, 1.0, 0.0).astype(BF16)
            picked = _dot(sel_ref[...].astype(BF16), expand)
            mask = (picked > 0.5) & (dist >= 0)
        bias = jnp.where(mask, 0.0, NEG_INF)
        if n_rep > 1:
            s = (s.reshape(n_rep, tq, tk) + bias[None]).reshape(n_rep * tq, tk)
        else:
            s = s + bias
        m_old = m_ref[...]
        m_new = jnp.maximum(m_old, jnp.max(s, axis=-1, keepdims=True))
        alpha = jnp.exp(m_old - m_new)
        p = jnp.exp(s - m_new)
        l_ref[...] = alpha * l_ref[...] + jnp.sum(p, axis=-1, keepdims=True)
        acc_ref[...] = alpha * acc_ref[...] + _dot(p.astype(BF16), v)
        m_ref[...] = m_new
        return carry

    lax.fori_loop(kt_lo, i + 1, body, 0)

    for r in range(n_rep):
        rows = slice(r * tq, (r + 1) * tq)
        cols = slice(r * HEAD_DIM, (r + 1) * HEAD_DIM)
        l = l_ref[rows, :]
        o_ref[:, cols] = (acc_ref[rows, :] / l).astype(o_ref.dtype)
        if want_lse:
            lse = m_ref[rows, :] + jnp.log(l)
            lse_ref[:, cols] = jnp.broadcast_to(lse, (tq, HEAD_DIM))


def flash_attention(q_arr, k_arr, v_arr, sel_arr, *, n_batch, n_grp, n_rep, seq, q_col, k_col,
                    v_col, o_col, out_cols, mode, max_dist=0, want_lse=False):
    tq = ATT_TQ
    nq = seq // tq
    qw = n_rep * HEAD_DIM
    in_specs = [
        pl.BlockSpec((tq, qw), lambda b, n, i: (b * nq + i, q_col(n))),
        pl.BlockSpec((seq, HEAD_DIM), lambda b, n, i: (b, k_col(n))),
        pl.BlockSpec((seq, HEAD_DIM), lambda b, n, i: (b, v_col(n))),
    ]
    args = [q_arr, k_arr, v_arr]
    if mode == "select":
        in_specs.append(pl.BlockSpec((tq, LANE), lambda b, n, i: ((b * n_grp + n) * nq + i, 0)))
        args.append(sel_arr)
    out_shape = [jax.ShapeDtypeStruct((n_batch * seq, out_cols), F32)]
    out_specs = [pl.BlockSpec((tq, qw), lambda b, n, i: (b * nq + i, o_col(n)))]
    if want_lse:
        out_shape.append(jax.ShapeDtypeStruct((n_batch * seq, out_cols), F32))
        out_specs.append(pl.BlockSpec((tq, qw), lambda b, n, i: (b * nq + i, o_col(n))))
    res = pl.pallas_call(
        functools.partial(_flash_kernel, n_rep=n_rep, mode=mode, max_dist=max_dist, want_lse=want_lse),
        out_shape=out_shape,
        grid=(n_batch, n_grp, nq),
        in_specs=in_specs,
        out_specs=out_specs,
        scratch_shapes=[
            pltpu.VMEM((n_rep * tq, HEAD_DIM), BF16),
            pltpu.VMEM((n_rep * tq, 1), F32),
            pltpu.VMEM((n_rep * tq, 1), F32),
            pltpu.VMEM((n_rep * tq, HEAD_DIM), F32),
        ],
        compiler_params=_cparams(("parallel", "parallel", "parallel")),
        name="flash_" + mode,
    )(*args)
    return res if want_lse else res[0]


N_SUB = 256
SUB_W = CMP_STRIDE * HEAD_DIM


def _compress_kernel(a_ref, pos_ref, w1_ref, w2_ref, o_ref):
    a = a_ref[...]
    top = (a + pos_ref[0:1, :]).astype(BF16)
    bot = (a + pos_ref[1:2, :]).astype(BF16)
    p = _dot(top, w1_ref[0:SUB_W, :])
    q = _dot(bot, w1_ref[SUB_W:2 * SUB_W, :])
    h = _gelu_tanh(p + pltpu.roll(q, N_SUB - 1, 0))
    o_ref[...] = _dot(h.astype(BF16), w2_ref[...]).astype(o_ref.dtype)


def compress(a, pos2, w1, w2, *, n_batch):
    return pl.pallas_call(
        _compress_kernel,
        out_shape=jax.ShapeDtypeStruct((n_batch * 4 * N_SUB, HEAD_DIM), BF16),
        grid=(n_batch, 4),
        in_specs=[
            pl.BlockSpec((N_SUB, SUB_W), lambda b, n: (b * 4 + n, 0)),
            pl.BlockSpec((2, SUB_W), lambda b, n: (0, 0)),
            pl.BlockSpec((None, 2 * SUB_W, HEAD_DIM), lambda b, n: (n // 2, 0, 0)),
            pl.BlockSpec((None, HEAD_DIM, HEAD_DIM), lambda b, n: (n // 2, 0, 0)),
        ],
        out_specs=pl.BlockSpec((N_SUB, HEAD_DIM), lambda b, n: (b * 4 + n, 0)),
        compiler_params=_cparams(("parallel", "parallel")),
        name="nsa_compress",
    )(a, pos2, w1, w2)


N_SEL = 64


def _csel_kernel(q_ref, kc_ref, vc_ref, ovt_ref, oc_ref, sel_ref, sc_ref):
    tq = ATT_TQ
    q0 = pl.program_id(2) * tq
    scale = HEAD_DIM ** -0.5
    t_rows = q0 + lax.broadcasted_iota(jnp.int32, (tq, N_SUB), 0)
    n_cols = lax.broadcasted_iota(jnp.int32, (tq, N_SUB), 1)
    valid = (n_cols * CMP_STRIDE + (CMP_LEN - 1)) <= t_rows
    valid_f = jnp.where(valid, 1.0, 0.0)
    kc = kc_ref[...]
    vc = vc_ref[...]
    psum = jnp.zeros((tq, N_SUB), F32)
    for r in range(C_GROUP):
        cols = slice(r * HEAD_DIM, (r + 1) * HEAD_DIM)
        s = _dot_nt(q_ref[:, cols].astype(BF16), kc) * scale
        s = jnp.where(valid, s, NEG_INF)
        m = jnp.max(s, axis=-1, keepdims=True)
        e = jnp.exp(s - m) * valid_f
        den = jnp.sum(e, axis=-1, keepdims=True)
        p = e * jnp.where(den > 0.0, 1.0 / den, 0.0)
        oc_ref[:, cols] = _dot(p.astype(BF16), vc)
        psum = psum + p
    hi = psum.astype(BF16)
    lo = (psum - hi.astype(F32)).astype(BF16)
    ovt = ovt_ref[...]
    imp_t = _dot_nt(ovt, hi) + _dot_nt(ovt, lo)
    jrow = lax.broadcasted_iota(jnp.int32, (LANE, tq), 0)
    tl = q0 + lax.broadcasted_iota(jnp.int32, (LANE, tq), 1)
    jt = tl >> 6
    forced = (jrow == 0) | (jrow == jt) | (jrow == jt - 1)
    valid_s = jrow * SEL_LEN <= tl
    score = jnp.where(forced, 1e4, jnp.where(valid_s, imp_t, -1.0))
    sc_ref[...] = score
    sc = sc_ref[0:N_SEL, :]
    jr = lax.broadcasted_iota(jnp.int32, (N_SEL, tq), 0)
    cnt = jnp.zeros((N_SEL, tq), F32)
    for j in range(N_SEL):
        row = sc_ref[j:j + 1, :]
        beats = (row > sc) | ((row == sc) & (jr > j))
        cnt = cnt + jnp.where(beats, 1.0, 0.0)
    sel_t = jnp.where(cnt < float(SEL_TOP), 1.0, 0.0)
    sel_pad = jnp.concatenate([sel_t, jnp.zeros((LANE - N_SEL, tq), F32)], axis=0)
    sel_ref[...] = sel_pad.T


def compressed_select(z, kvc, ovt, *, n_batch, seq):
    tq = ATT_TQ
    nq = seq // tq
    qw = C_GROUP * HEAD_DIM
    return pl.pallas_call(
        _csel_kernel,
        out_shape=[jax.ShapeDtypeStruct((n_batch * seq, C_HEADS * HEAD_DIM), F32),
                   jax.ShapeDtypeStruct((n_batch * C_KV_HEADS * seq, LANE), F32)],
        grid=(n_batch, C_KV_HEADS, nq),
        in_specs=[
            pl.BlockSpec((tq, qw), lambda b, g, i: (b * nq + i, COL_CQ // qw + g)),
            pl.BlockSpec((N_SUB, HEAD_DIM), lambda b, g, i: (b * 4 + g, 0)),
            pl.BlockSpec((N_SUB, HEAD_DIM), lambda b, g, i: (b * 4 + 2 + g, 0)),
            pl.BlockSpec((LANE, N_SUB), lambda b, g, i: (0, 0)),
        ],
        out_specs=[
            pl.BlockSpec((tq, qw), lambda b, g, i: (b * nq + i, g)),
            pl.BlockSpec((tq, LANE), lambda b, g, i: ((b * C_KV_HEADS + g) * nq + i, 0)),
        ],
        scratch_shapes=[pltpu.VMEM((LANE, tq), F32)],
        compiler_params=_cparams(("parallel", "parallel", "parallel")),
        name="nsa_compressed_select",
    )(z, kvc, kvc, ovt)


def _overlap_t():
    n_c = N_SUB - 1
    c_start = np.arange(n_c) * CMP_STRIDE
    s_start = np.arange(N_SEL) * SEL_LEN
    ov = ((c_start[:, None] <= s_start[None, :] + SEL_LEN - 1)
          & (c_start[:, None] + CMP_LEN - 1 >= s_start[None, :])).astype(np.float32)
    out = np.zeros((LANE, N_SUB), np.float32)
    out[:N_SEL, :n_c] = ov.T
    return out


def _combine_kernel(o1_ref, o4_ref, o16_ref, l1_ref, l4_ref, l16_ref, oc_ref, os_ref, ow_ref, gate_ref,
                    ob_ref, ocm_ref):
    l1, l4, l16 = l1_ref[...], l4_ref[...], l16_ref[...]
    mx = jnp.maximum(jnp.maximum(l1, l4), l16)
    e1, e4, e16 = jnp.exp(l1 - mx), jnp.exp(l4 - mx), jnp.exp(l16 - mx)
    inv = 1.0 / (e1 + e4 + e16)
    ob = (e1 * inv) * o1_ref[...] + (e4 * inv) * o4_ref[...] + (e16 * inv) * o16_ref[...]
    ob_ref[...] = ob.astype(ob_ref.dtype)
    g = 1.0 / (1.0 + jnp.exp(-gate_ref[...]))
    for h in range(C_HEADS):
        cols = slice(h * HEAD_DIM, (h + 1) * HEAD_DIM)
        o = (g[:, 3 * h:3 * h + 1] * oc_ref[:, cols]
             + g[:, 3 * h + 1:3 * h + 2] * os_ref[:, cols]
             + g[:, 3 * h + 2:3 * h + 3] * ow_ref[:, cols])
        ocm_ref[:, cols] = o.astype(ocm_ref.dtype)


def combine(o1, o4, o16, l1, l4, l16, oc, os_, ow, z, *, tm):
    t = o1.shape[0]
    bw = B_HEADS * HEAD_DIM
    cw = C_HEADS * HEAD_DIM
    spec_b = pl.BlockSpec((tm, bw), lambda i: (i, 0))
    spec_c = pl.BlockSpec((tm, cw), lambda i: (i, 0))
    return pl.pallas_call(
        _combine_kernel,
        out_shape=[jax.ShapeDtypeStruct((t, bw), F32), jax.ShapeDtypeStruct((t, cw), F32)],
        grid=(t // tm,),
        in_specs=[spec_b] * 6 + [spec_c] * 3
        + [pl.BlockSpec((tm, LANE), lambda i: (i, COL_GATE // LANE))],
        out_specs=[spec_b, spec_c],
        compiler_params=_cparams(("parallel",)),
        name="mix_combine",
    )(o1, o4, o16, l1, l4, l16, oc, os_, ow, z)


def _xattn_kernel(q_ref, k_ref, v_ref, o_ref):
    scale = HEAD_DIM ** -0.5
    for h in range(X_HEADS):
        cols = slice(h * HEAD_DIM, (h + 1) * HEAD_DIM)
        s = _dot_nt(q_ref[:, cols].astype(BF16), k_ref[:, cols].astype(BF16)) * scale
        m = jnp.max(s, axis=-1, keepdims=True)
        e = jnp.exp(s - m)
        den = jnp.sum(e, axis=-1, keepdims=True)
        o_ref[:, cols] = (_dot(e.astype(BF16), v_ref[:, cols].astype(BF16)) / den).astype(o_ref.dtype)


def cross_attention(q, kv, *, n_batch, seq, n_mem, tq):
    nq = seq // tq
    return pl.pallas_call(
        _xattn_kernel,
        out_shape=jax.ShapeDtypeStruct((n_batch * seq, X_WIDTH), F32),
        grid=(n_batch, nq),
        in_specs=[
            pl.BlockSpec((tq, X_WIDTH), lambda b, i: (b * nq + i, 0)),
            pl.BlockSpec((n_mem, X_WIDTH), lambda b, i: (b, 0)),
            pl.BlockSpec((n_mem, X_WIDTH), lambda b, i: (b, 1)),
        ],
        out_specs=pl.BlockSpec((tq, X_WIDTH), lambda b, i: (b * nq + i, 0)),
        compiler_params=_cparams(("parallel", "parallel")),
        name="cross_attention",
    )(q, kv, kv)


def kernel(x, mem, norm_mix, w_in, gmlp_ln_g, gmlp_ln_b, gmlp_w_s, gmlp_b_s, cmp_pos, cmp_k_w1, cmp_k_w2,
           cmp_v_w1, cmp_v_w2, w_out, norm_xattn, norm_mem, xattn_wq, xattn_wkv, xattn_wo, norm_mlp, w_up,
           w_down, final_norm):
    bsz, seq, d = x.shape
    n_mem = mem.shape[1]
    t = bsz * seq
    assert d == D_MODEL and seq // CMP_STRIDE == N_SUB and seq // SEL_LEN == N_SEL
    assert seq % (ATT_TQ * 16) == 0

    w_in_p = jnp.pad(w_in, ((0, 0), (0, 0), (0, IN_PAD - IN_WIDTH))).astype(BF16)
    w_out_b = w_out.astype(BF16)
    wq_b = xattn_wq.astype(BF16)
    wkv_b = xattn_wkv.astype(BF16)
    wo_b = xattn_wo.astype(BF16)
    w_up_b = w_up.astype(BF16)
    w_down_b = w_down.astype(BF16)
    cmp_w1 = jnp.stack([cmp_k_w1, cmp_v_w1], axis=1).astype(BF16)
    cmp_w2 = jnp.stack([cmp_k_w2, cmp_v_w2], axis=1).astype(BF16)
    ovt = jnp.asarray(_overlap_t()).astype(BF16)

    xf = x.reshape(t, d)
    memf = mem.reshape(bsz * n_mem, d)
    bw = B_HEADS * HEAD_DIM
    cw = C_HEADS * HEAD_DIM
    gw = C_GROUP * HEAD_DIM

    for l in range(DEPTH):
        z = norm_matmul(xf, norm_mix[l], w_in_p[l], tm=512, tn=IN_TILE)

        out_a = gmlp(z, gmlp_ln_g[l], gmlp_ln_b[l], gmlp_w_s[l], gmlp_b_s[l], tm=512)

        zb = z[:, COL_B:COL_CQ]
        outs, lses = [], []
        for window, dil in DILATED_PAIRS:
            if dil == 1:
                src, base = z, COL_B // HEAD_DIM
            else:
                src, base = zb.reshape(t // dil, dil * 3 * bw), 0
            stride = 3 * B_HEADS if dil > 1 else 0
            o, lse = flash_attention(
                src, src, src, None, n_batch=bsz, n_grp=B_HEADS * dil, n_rep=1, seq=seq // dil,
                q_col=lambda n, s=stride, b=base, dd=dil: b + (n % dd) * s + n // dd,
                k_col=lambda n, s=stride, b=base, dd=dil: b + (n % dd) * s + B_HEADS + n // dd,
                v_col=lambda n, s=stride, b=base, dd=dil: b + (n % dd) * s + 2 * B_HEADS + n // dd,
                o_col=lambda n, dd=dil: (n % dd) * B_HEADS + n // dd,
                out_cols=dil * bw, mode="band", max_dist=window // dil, want_lse=True)
            outs.append(o.reshape(t, bw))
            lses.append(lse.reshape(t, bw))

        zc = z[:, COL_CMP:COL_SEL].reshape(bsz, N_SUB, CMP_STRIDE, 4, HEAD_DIM)
        a_cmp = zc.transpose(0, 3, 1, 2, 4).reshape(bsz * 4 * N_SUB, SUB_W)
        kvc = compress(a_cmp, cmp_pos[l].reshape(2, SUB_W), cmp_w1[l], cmp_w2[l], n_batch=bsz)
        o_c, sel = compressed_select(z, kvc, ovt, n_batch=bsz, seq=seq)
        o_s = flash_attention(
            z, z, z, sel, n_batch=bsz, n_grp=C_KV_HEADS, n_rep=C_GROUP, seq=seq,
            q_col=lambda n: COL_CQ // gw + n,
            k_col=lambda n: COL_SEL // HEAD_DIM + n,
            v_col=lambda n: COL_SEL // HEAD_DIM + C_KV_HEADS + n,
            o_col=lambda n: n, out_cols=cw, mode="select")
        o_w = flash_attention(
            z, z, z, None, n_batch=bsz, n_grp=C_KV_HEADS, n_rep=C_GROUP, seq=seq,
            q_col=lambda n: COL_CQ // gw + n,
            k_col=lambda n: COL_WIN // HEAD_DIM + n,
            v_col=lambda n: COL_WIN // HEAD_DIM + C_KV_HEADS + n,
            o_col=lambda n: n, out_cols=cw, mode="band", max_dist=WIN_LEN - 1)

        out_b, out_c = combine(outs[0], outs[1], outs[2], lses[0], lses[1], lses[2], o_c, o_s, o_w, z, tm=512)
        wo_l = w_out_b[l]
        xf = res_matmul(xf, [out_a, out_b, out_c],
                        [wo_l[0:A_WIDTH], wo_l[A_WIDTH:A_WIDTH + bw], wo_l[A_WIDTH + bw:]], tm=512, tn=1024)

        q = norm_matmul(xf, norm_xattn[l], wq_b[l], tm=512, tn=X_WIDTH)
        kv = norm_matmul(memf, norm_mem[l], wkv_b[l], tm=512, tn=2 * X_WIDTH)
        o = cross_attention(q, kv, n_batch=bsz, seq=seq, n_mem=n_mem, tq=512)
        xf = res_matmul(xf, [o], [wo_b[l]], tm=512, tn=1024)

        xf = mlp(xf, norm_mlp[l], w_up_b[l], w_down_b[l], tm=512, tf=512)

    return rmsnorm(xf, final_norm, tm=512).reshape(bsz, seq, d)
```

```python
import functools
import math

import numpy as np
import jax
import jax.numpy as jnp
from jax import lax
from jax.experimental import pallas as pl
from jax.experimental.pallas import tpu as pltpu

F32 = jnp.float32
BF16 = jnp.bfloat16

D_MODEL = 2048
DEPTH = 4
HEAD_DIM = 128
A_GROUPS = 4
A_WIDTH = A_GROUPS * HEAD_DIM
CHUNK = 128
B_HEADS = 4
DILATED_PAIRS = ((128, 1), (512, 4), (2048, 16))
C_HEADS = 8
C_KV_HEADS = 2
C_GROUP = C_HEADS // C_KV_HEADS
CMP_LEN = 32
CMP_STRIDE = 16
SEL_LEN = 64
SEL_TOP = 16
WIN_LEN = 512
X_HEADS = 4
X_WIDTH = X_HEADS * HEAD_DIM
EPS = 1e-6
NEG_INF = -1e30

COL_A = 0
COL_B = 2 * A_WIDTH
COL_CQ = COL_B + 3 * B_HEADS * HEAD_DIM
COL_CMP = COL_CQ + C_HEADS * HEAD_DIM
COL_SEL = COL_CMP + 2 * C_KV_HEADS * HEAD_DIM
COL_WIN = COL_SEL + 2 * C_KV_HEADS * HEAD_DIM
COL_GATE = COL_WIN + 2 * C_KV_HEADS * HEAD_DIM
IN_WIDTH = COL_GATE + 3 * C_HEADS
IN_TILE = 768
IN_PAD = 5376

LANE = 128
VMEM_LIMIT = 56 * 1024 * 1024

ATT_TQ = 256
ATT_TK = 256
ATT_CH = 128
LOG2E = math.log2(math.e)
LN2 = math.log(2.0)


def _cparams(sem):
    return pltpu.CompilerParams(dimension_semantics=sem, vmem_limit_bytes=VMEM_LIMIT)


def _rms_rows(x, g):
    ms = jnp.mean(x * x, axis=-1, keepdims=True)
    return x * lax.rsqrt(ms + EPS) * g


def _gelu_tanh(x):
    c = math.sqrt(2.0 / math.pi)
    return x * (0.5 * (1.0 + jnp.tanh(c * (x + 0.044715 * (x * x * x)))))


def _dot(a, b):
    return jnp.dot(a, b, preferred_element_type=F32)


def _dot_nt(a, b):
    return lax.dot_general(a, b, (((1,), (1,)), ((), ())), preferred_element_type=F32)


NORM_ROWS = 256


def _norm_into(h_ref, x_ref, g_ref):
    rows = x_ref.shape[0]

    def body(c, carry):
        r0 = pl.multiple_of(c * NORM_ROWS, NORM_ROWS)
        x = x_ref[pl.ds(r0, NORM_ROWS), :]
        h_ref[pl.ds(r0, NORM_ROWS), :] = _rms_rows(x, g_ref[...]).astype(h_ref.dtype)
        return carry

    lax.fori_loop(0, rows // NORM_ROWS, body, 0)


def _norm_matmul_kernel(x_ref, g_ref, w_ref, o_ref, h_ref):
    @pl.when(pl.program_id(1) == 0)
    def _():
        _norm_into(h_ref, x_ref, g_ref)

    o_ref[...] = _dot(h_ref[...], w_ref[...]).astype(o_ref.dtype)


def norm_matmul(x, g, w, *, tm, tn, out_dtype=F32):
    m, k = x.shape
    n = w.shape[1]
    return pl.pallas_call(
        _norm_matmul_kernel,
        out_shape=jax.ShapeDtypeStruct((m, n), out_dtype),
        grid=(m // tm, n // tn),
        in_specs=[
            pl.BlockSpec((tm, k), lambda i, j: (i, 0)),
            pl.BlockSpec((1, k), lambda i, j: (0, 0)),
            pl.BlockSpec((k, tn), lambda i, j: (0, j)),
        ],
        out_specs=pl.BlockSpec((tm, tn), lambda i, j: (i, j)),
        scratch_shapes=[pltpu.VMEM((tm, k), BF16)],
        compiler_params=_cparams(("parallel", "arbitrary")),
        name="norm_matmul",
    )(x, g.reshape(1, k), w)


def _res_matmul_kernel(*refs, n_parts):
    x_ref = refs[0]
    a_refs = refs[1:1 + n_parts]
    w_refs = refs[1 + n_parts:1 + 2 * n_parts]
    o_ref = refs[1 + 2 * n_parts]
    acc = x_ref[...]
    for a_ref, w_ref in zip(a_refs, w_refs):
        acc = acc + _dot(a_ref[...].astype(BF16), w_ref[...])
    o_ref[...] = acc


def res_matmul(x, parts, weights, *, tm, tn):
    m, n = x.shape
    n_parts = len(parts)
    in_specs = [pl.BlockSpec((tm, tn), lambda i, j: (i, j))]
    for a in parts:
        in_specs.append(pl.BlockSpec((tm, a.shape[1]), lambda i, j: (i, 0)))
    for w in weights:
        in_specs.append(pl.BlockSpec((w.shape[0], tn), lambda i, j: (0, j)))
    return pl.pallas_call(
        functools.partial(_res_matmul_kernel, n_parts=n_parts),
        out_shape=jax.ShapeDtypeStruct((m, n), F32),
        grid=(m // tm, n // tn),
        in_specs=in_specs,
        out_specs=pl.BlockSpec((tm, tn), lambda i, j: (i, j)),
        compiler_params=_cparams(("parallel", "parallel")),
        name="res_matmul",
    )(x, *parts, *weights)


def _mlp_kernel(x_ref, g_ref, wu_ref, wd_ref, o_ref, h_ref):
    f = pl.program_id(1)

    @pl.when(f == 0)
    def _():
        _norm_into(h_ref, x_ref, g_ref)
        o_ref[...] = x_ref[...]

    a = _dot(h_ref[...], wu_ref[...])
    a = jnp.square(jnp.maximum(a, 0.0)).astype(BF16)
    o_ref[...] += _dot(a, wd_ref[...])


def mlp(x, g, wu, wd, *, tm, tf):
    m, d = x.shape
    ff = wu.shape[1]
    return pl.pallas_call(
        _mlp_kernel,
        out_shape=jax.ShapeDtypeStruct((m, d), F32),
        grid=(m // tm, ff // tf),
        in_specs=[
            pl.BlockSpec((tm, d), lambda i, f: (i, 0)),
            pl.BlockSpec((1, d), lambda i, f: (0, 0)),
            pl.BlockSpec((d, tf), lambda i, f: (0, f)),
            pl.BlockSpec((tf, d), lambda i, f: (f, 0)),
        ],
        out_specs=pl.BlockSpec((tm, d), lambda i, f: (i, 0)),
        scratch_shapes=[pltpu.VMEM((tm, d), BF16)],
        compiler_params=_cparams(("parallel", "arbitrary")),
        name="mlp",
    )(x, g.reshape(1, d), wu, wd)


def _rmsnorm_kernel(x_ref, g_ref, o_ref):
    o_ref[...] = _rms_rows(x_ref[...], g_ref[...])


def rmsnorm(x, g, *, tm):
    m, d = x.shape
    return pl.pallas_call(
        _rmsnorm_kernel,
        out_shape=jax.ShapeDtypeStruct((m, d), F32),
        grid=(m // tm,),
        in_specs=[pl.BlockSpec((tm, d), lambda i: (i, 0)),
                  pl.BlockSpec((1, d), lambda i: (0, 0))],
        out_specs=pl.BlockSpec((tm, d), lambda i: (i, 0)),
        compiler_params=_cparams(("parallel",)),
        name="final_rmsnorm",
    )(x, g.reshape(1, d))


def _gmlp_kernel(z_ref, lng_ref, lnb_ref, ws_ref, bst_ref, o_ref, *, n_chunks):
    row = lax.broadcasted_iota(jnp.int32, (CHUNK, CHUNK), 0)
    col = lax.broadcasted_iota(jnp.int32, (CHUNK, CHUNK), 1)
    causal = row >= col
    w_tril = [jnp.where(causal, ws_ref[gi], 0.0).astype(BF16) for gi in range(A_GROUPS)]
    for c in range(n_chunks):
        rows = slice(c * CHUNK, (c + 1) * CHUNK)
        u = _gelu_tanh(z_ref[rows, 0:A_WIDTH])
        v = _gelu_tanh(z_ref[rows, A_WIDTH:2 * A_WIDTH])
        vc = v - jnp.mean(v, axis=-1, keepdims=True)
        vn = vc * lax.rsqrt(jnp.mean(vc * vc, axis=-1, keepdims=True) + EPS)
        vn = vn * lng_ref[...] + lnb_ref[...]
        for gi in range(A_GROUPS):
            cols = slice(gi * HEAD_DIM, (gi + 1) * HEAD_DIM)
            sv = _dot(w_tril[gi], vn[:, cols].astype(BF16)) + bst_ref[:, gi:gi + 1]
            o_ref[rows, cols] = (u[:, cols] * sv).astype(o_ref.dtype)


def gmlp(z, ln_g, ln_b, w_s, b_s, *, tm):
    t = z.shape[0]
    return pl.pallas_call(
        functools.partial(_gmlp_kernel, n_chunks=tm // CHUNK),
        out_shape=jax.ShapeDtypeStruct((t, A_WIDTH), F32),
        grid=(t // tm,),
        in_specs=[
            pl.BlockSpec((tm, 2 * A_WIDTH), lambda i: (i, 0)),
            pl.BlockSpec((1, A_WIDTH), lambda i: (0, 0)),
            pl.BlockSpec((1, A_WIDTH), lambda i: (0, 0)),
            pl.BlockSpec((A_GROUPS, CHUNK, CHUNK), lambda i: (0, 0, 0)),
            pl.BlockSpec((CHUNK, A_GROUPS), lambda i: (0, 0)),
        ],
        out_specs=pl.BlockSpec((tm, A_WIDTH), lambda i: (i, 0)),
        compiler_params=_cparams(("parallel",)),
        name="gmlp_gating",
    )(z, ln_g.reshape(1, A_WIDTH), ln_b.reshape(1, A_WIDTH), w_s, b_s.T)


def _flash_kernel(*refs, n_rep, mode, max_dist, want_lse):
    refs = list(refs)
    q_ref, k_ref, v_ref = refs[:3]
    pos = 3
    sel_ref = None
    if mode == "select":
        sel_ref = refs[pos]
        pos += 1
    o_ref = refs[pos]
    pos += 1
    lse_ref = None
    if want_lse:
        lse_ref = refs[pos]
        pos += 1
    qs_ref, kb_ref, vb_ref, m_ref, accl_ref = refs[pos:pos + 5]

    tq, tk, ch = ATT_TQ, ATT_TK, ATT_CH
    seq = k_ref.shape[0]
    i = pl.program_id(2)
    select = mode == "select"

    @pl.when(i == 0)
    def _():
        def prep(c, carry):
            r0 = pl.multiple_of(c * tk, tk)
            kb_ref[pl.ds(r0, tk), 0:HEAD_DIM] = k_ref[pl.ds(r0, tk), :].astype(BF16)
            if select:
                blk = (r0 + lax.broadcasted_iota(jnp.int32, (tk, LANE), 0)) >> 6
                lane = lax.broadcasted_iota(jnp.int32, (tk, LANE), 1)
                kb_ref[pl.ds(r0, tk), HEAD_DIM:HEAD_DIM + LANE] = jnp.where(blk == lane, 1.0, 0.0).astype(BF16)
            vb_ref[pl.ds(r0, tk), 0:HEAD_DIM] = v_ref[pl.ds(r0, tk), :].astype(BF16)
            vb_ref[pl.ds(r0, tk), HEAD_DIM:2 * HEAD_DIM] = jnp.ones((tk, HEAD_DIM), BF16)
            return carry

        lax.fori_loop(0, seq // tk, prep, 0)

    qscale = (HEAD_DIM ** -0.5) * LOG2E
    if select:
        pen = jnp.where(sel_ref[...] > 0.5, 0.0, NEG_INF).astype(BF16)
    for r in range(n_rep):
        rows = slice(r * tq, (r + 1) * tq)
        qs_ref[rows, 0:HEAD_DIM] = (q_ref[:, r * HEAD_DIM:(r + 1) * HEAD_DIM] * qscale).astype(BF16)
        if select:
            qs_ref[rows, HEAD_DIM:HEAD_DIM + LANE] = pen
    m_ref[...] = jnp.full(m_ref.shape, NEG_INF, F32)
    accl_ref[...] = jnp.zeros(accl_ref.shape, F32)

    row = lax.broadcasted_iota(jnp.int32, (tq, tk), 0)
    col = lax.broadcasted_iota(jnp.int32, (tq, tk), 1)

    def tile_bias(off):
        dist = off * tk + row - col
        if off == 0:
            ok = dist >= 0
            if not select and max_dist < tk - 1:
                ok = ok & (dist <= max_dist)
        elif not select and (off + 1) * tk - 1 > max_dist:
            ok = dist <= max_dist
        else:
            return None
        return jnp.where(ok, 0.0, NEG_INF)

    n_ch = n_rep * tq // ch

    def steps(tiles):
        scores = []
        for kt, bias in tiles:
            k0 = pl.multiple_of(kt * tk, tk)
            scores.append((_dot_nt(qs_ref[...], kb_ref[pl.ds(k0, tk), :]), k0, bias))
        for s_all, k0, bias in scores:
            ps, alphas = [], []
            for c in range(n_ch):
                rows = slice(c * ch, (c + 1) * ch)
                s = s_all[rows, :]
                if bias is not None:
                    b0 = (c % (tq // ch)) * ch
                    s = s + bias[b0:b0 + ch, :]
                m_old = m_ref[rows, :]
                m_new = jnp.maximum(m_old, jnp.max(s, axis=-1, keepdims=True))
                alphas.append(jnp.exp2(m_old - m_new))
                ps.append(jnp.exp2(s - jnp.concatenate([m_new] * (tk // LANE), axis=1)).astype(BF16))
                m_ref[rows, :] = m_new
            pv_all = _dot(jnp.concatenate(ps, axis=0), vb_ref[pl.ds(k0, tk), :])
            for c in range(n_ch):
                rows = slice(c * ch, (c + 1) * ch)
                alpha2 = jnp.concatenate([alphas[c], alphas[c]], axis=1)
                accl_ref[rows, :] = alpha2 * accl_ref[rows, :] + pv_all[rows, :]

    if select:
        def body(j, carry):
            steps([(2 * j, None), (2 * j + 1, None)])
            return carry

        lax.fori_loop(0, i >> 1, body, 0)

        @pl.when((i & 1) == 1)
        def _():
            steps([(i - 1, None), (i, tile_bias(0))])

        @pl.when((i & 1) == 0)
        def _():
            steps([(i, tile_bias(0))])
    else:
        n_prev = -(-max_dist // tk)

        @pl.when(i >= n_prev)
        def _():
            steps([(i - off, tile_bias(off)) for off in range(n_prev, -1, -1)])

        for first in range(n_prev):
            @pl.when(i == first)
            def _(first=first):
                steps([(first - off, tile_bias(off)) for off in range(first, -1, -1)])

    for r in range(n_rep):
        rows = slice(r * tq, (r + 1) * tq)
        cols = slice(r * HEAD_DIM, (r + 1) * HEAD_DIM)
        l = accl_ref[rows, HEAD_DIM:2 * HEAD_DIM]
        o_ref[:, cols] = (accl_ref[rows, 0:HEAD_DIM] / l).astype(o_ref.dtype)
        if want_lse:
            lse_ref[:, cols] = (m_ref[rows, :] + jnp.log2(l)) * LN2


def flash_attention(q_arr, k_arr, v_arr, sel_arr, *, n_batch, n_grp, n_rep, seq, q_col, k_col,
                    v_col, o_col, out_cols, mode, max_dist=0, want_lse=False):
    tq = ATT_TQ
    nq = seq // tq
    qw = n_rep * HEAD_DIM
    kw = HEAD_DIM + (LANE if mode == "select" else 0)
    in_specs = [
        pl.BlockSpec((tq, qw), lambda b, n, i: (b * nq + i, q_col(n))),
        pl.BlockSpec((seq, HEAD_DIM), lambda b, n, i: (b, k_col(n))),
        pl.BlockSpec((seq, HEAD_DIM), lambda b, n, i: (b, v_col(n))),
    ]
    args = [q_arr, k_arr, v_arr]
    if mode == "select":
        in_specs.append(pl.BlockSpec((tq, LANE), lambda b, n, i: ((b * n_grp + n) * nq + i, 0)))
        args.append(sel_arr)
    out_shape = [jax.ShapeDtypeStruct((n_batch * seq, out_cols), F32)]
    out_specs = [pl.BlockSpec((tq, qw), lambda b, n, i: (b * nq + i, o_col(n)))]
    if want_lse:
        out_shape.append(jax.ShapeDtypeStruct((n_batch * seq, out_cols), F32))
        out_specs.append(pl.BlockSpec((tq, qw), lambda b, n, i: (b * nq + i, o_col(n))))
    res = pl.pallas_call(
        functools.partial(_flash_kernel, n_rep=n_rep, mode=mode, max_dist=max_dist, want_lse=want_lse),
        out_shape=out_shape,
        grid=(n_batch, n_grp, nq),
        in_specs=in_specs,
        out_specs=out_specs,
        scratch_shapes=[
            pltpu.VMEM((n_rep * tq, kw), BF16),
            pltpu.VMEM((seq, kw), BF16),
            pltpu.VMEM((seq, 2 * HEAD_DIM), BF16),
            pltpu.VMEM((n_rep * tq, LANE), F32),
            pltpu.VMEM((n_rep * tq, 2 * HEAD_DIM), F32),
        ],
        compiler_params=_cparams(("parallel", "parallel", "arbitrary")),
        name="flash_" + mode,
    )(*args)
    return res if want_lse else res[0]


N_SUB = 256
SUB_W = CMP_STRIDE * HEAD_DIM


def _compress_kernel(a_ref, pos_ref, w1_ref, w2_ref, o_ref):
    a = a_ref[...]
    top = (a + pos_ref[0:1, :]).astype(BF16)
    bot = (a + pos_ref[1:2, :]).astype(BF16)
    p = _dot(top, w1_ref[0:SUB_W, :])
    q = _dot(bot, w1_ref[SUB_W:2 * SUB_W, :])
    h = _gelu_tanh(p + pltpu.roll(q, N_SUB - 1, 0))
    o_ref[...] = _dot(h.astype(BF16), w2_ref[...]).astype(o_ref.dtype)


def compress(a, pos2, w1, w2, *, n_batch):
    return pl.pallas_call(
        _compress_kernel,
        out_shape=jax.ShapeDtypeStruct((n_batch * 4 * N_SUB, HEAD_DIM), BF16),
        grid=(n_batch, 4),
        in_specs=[
            pl.BlockSpec((N_SUB, SUB_W), lambda b, n: (b * 4 + n, 0)),
            pl.BlockSpec((2, SUB_W), lambda b, n: (0, 0)),
            pl.BlockSpec((None, 2 * SUB_W, HEAD_DIM), lambda b, n: (n // 2, 0, 0)),
            pl.BlockSpec((None, HEAD_DIM, HEAD_DIM), lambda b, n: (n // 2, 0, 0)),
        ],
        out_specs=pl.BlockSpec((N_SUB, HEAD_DIM), lambda b, n: (b * 4 + n, 0)),
        compiler_params=_cparams(("parallel", "parallel")),
        name="nsa_compress",
    )(a, pos2, w1, w2)


N_SEL = 64


def _csel_kernel(q_ref, kc_ref, vc_ref, ovt_ref, oc_ref, sel_ref, sc_ref):
    tq = ATT_TQ
    q0 = pl.program_id(2) * tq
    scale = HEAD_DIM ** -0.5
    t_rows = q0 + lax.broadcasted_iota(jnp.int32, (tq, N_SUB), 0)
    n_cols = lax.broadcasted_iota(jnp.int32, (tq, N_SUB), 1)
    valid = (n_cols * CMP_STRIDE + (CMP_LEN - 1)) <= t_rows
    valid_f = jnp.where(valid, 1.0, 0.0)
    kc = kc_ref[...]
    vc = vc_ref[...]
    psum = jnp.zeros((tq, N_SUB), F32)
    for r in range(C_GROUP):
        cols = slice(r * HEAD_DIM, (r + 1) * HEAD_DIM)
        s = _dot_nt(q_ref[:, cols].astype(BF16), kc) * scale
        s = jnp.where(valid, s, NEG_INF)
        m = jnp.max(s, axis=-1, keepdims=True)
        e = jnp.exp(s - m) * valid_f
        den = jnp.sum(e, axis=-1, keepdims=True)
        p = e * jnp.where(den > 0.0, 1.0 / den, 0.0)
        oc_ref[:, cols] = _dot(p.astype(BF16), vc)
        psum = psum + p
    hi = psum.astype(BF16)
    lo = (psum - hi.astype(F32)).astype(BF16)
    ovt = ovt_ref[...]
    imp_t = _dot_nt(ovt, hi) + _dot_nt(ovt, lo)
    jrow = lax.broadcasted_iota(jnp.int32, (LANE, tq), 0)
    tl = q0 + lax.broadcasted_iota(jnp.int32, (LANE, tq), 1)
    jt = tl >> 6
    forced = (jrow == 0) | (jrow == jt) | (jrow == jt - 1)
    valid_s = jrow * SEL_LEN <= tl
    score = jnp.where(forced, 1e4, jnp.where(valid_s, imp_t, -1.0))
    sc_ref[...] = score
    sc = sc_ref[0:N_SEL, :]
    jr = lax.broadcasted_iota(jnp.int32, (N_SEL, tq), 0)
    cnt = jnp.zeros((N_SEL, tq), F32)
    for j in range(N_SEL):
        row = sc_ref[j:j + 1, :]
        beats = (row > sc) | ((row == sc) & (jr > j))
        cnt = cnt + jnp.where(beats, 1.0, 0.0)
    sel_t = jnp.where(cnt < float(SEL_TOP), 1.0, 0.0)
    sel_pad = jnp.concatenate([sel_t, jnp.zeros((LANE - N_SEL, tq), F32)], axis=0)
    sel_ref[...] = sel_pad.T


def compressed_select(z, kvc, ovt, *, n_batch, seq):
    tq = ATT_TQ
    nq = seq // tq
    qw = C_GROUP * HEAD_DIM
    return pl.pallas_call(
        _csel_kernel,
        out_shape=[jax.ShapeDtypeStruct((n_batch * seq, C_HEADS * HEAD_DIM), F32),
                   jax.ShapeDtypeStruct((n_batch * C_KV_HEADS * seq, LANE), F32)],
        grid=(n_batch, C_KV_HEADS, nq),
        in_specs=[
            pl.BlockSpec((tq, qw), lambda b, g, i: (b * nq + i, COL_CQ // qw + g)),
            pl.BlockSpec((N_SUB, HEAD_DIM), lambda b, g, i: (b * 4 + g, 0)),
            pl.BlockSpec((N_SUB, HEAD_DIM), lambda b, g, i: (b * 4 + 2 + g, 0)),
            pl.BlockSpec((LANE, N_SUB), lambda b, g, i: (0, 0)),
        ],
        out_specs=[
            pl.BlockSpec((tq, qw), lambda b, g, i: (b * nq + i, g)),
            pl.BlockSpec((tq, LANE), lambda b, g, i: ((b * C_KV_HEADS + g) * nq + i, 0)),
        ],
        scratch_shapes=[pltpu.VMEM((LANE, tq), F32)],
        compiler_params=_cparams(("parallel", "parallel", "parallel")),
        name="nsa_compressed_select",
    )(z, kvc, kvc, ovt)


def _overlap_t():
    n_c = N_SUB - 1
    c_start = np.arange(n_c) * CMP_STRIDE
    s_start = np.arange(N_SEL) * SEL_LEN
    ov = ((c_start[:, None] <= s_start[None, :] + SEL_LEN - 1)
          & (c_start[:, None] + CMP_LEN - 1 >= s_start[None, :])).astype(np.float32)
    out = np.zeros((LANE, N_SUB), np.float32)
    out[:N_SEL, :n_c] = ov.T
    return out


def _combine_kernel(o1_ref, o4_ref, o16_ref, l1_ref, l4_ref, l16_ref, oc_ref, os_ref, ow_ref, gate_ref,
                    ob_ref, ocm_ref):
    l1, l4, l16 = l1_ref[...], l4_ref[...], l16_ref[...]
    mx = jnp.maximum(jnp.maximum(l1, l4), l16)
    e1, e4, e16 = jnp.exp(l1 - mx), jnp.exp(l4 - mx), jnp.exp(l16 - mx)
    inv = 1.0 / (e1 + e4 + e16)
    ob = (e1 * inv) * o1_ref[...] + (e4 * inv) * o4_ref[...] + (e16 * inv) * o16_ref[...]
    ob_ref[...] = ob.astype(ob_ref.dtype)
    g = 1.0 / (1.0 + jnp.exp(-gate_ref[...]))
    for h in range(C_HEADS):
        cols = slice(h * HEAD_DIM, (h + 1) * HEAD_DIM)
        o = (g[:, 3 * h:3 * h + 1] * oc_ref[:, cols]
             + g[:, 3 * h + 1:3 * h + 2] * os_ref[:, cols]
             + g[:, 3 * h + 2:3 * h + 3] * ow_ref[:, cols])
        ocm_ref[:, cols] = o.astype(ocm_ref.dtype)


def combine(o1, o4, o16, l1, l4, l16, oc, os_, ow, z, *, tm):
    t = o1.shape[0]
    bw = B_HEADS * HEAD_DIM
    cw = C_HEADS * HEAD_DIM
    spec_b = pl.BlockSpec((tm, bw), lambda i: (i, 0))
    spec_c = pl.BlockSpec((tm, cw), lambda i: (i, 0))
    return pl.pallas_call(
        _combine_kernel,
        out_shape=[jax.ShapeDtypeStruct((t, bw), F32), jax.ShapeDtypeStruct((t, cw), F32)],
        grid=(t // tm,),
        in_specs=[spec_b] * 6 + [spec_c] * 3
        + [pl.BlockSpec((tm, LANE), lambda i: (i, COL_GATE // LANE))],
        out_specs=[spec_b, spec_c],
        compiler_params=_cparams(("parallel",)),
        name="mix_combine",
    )(o1, o4, o16, l1, l4, l16, oc, os_, ow, z)


def _xattn_kernel(q_ref, k_ref, v_ref, o_ref):
    scale = HEAD_DIM ** -0.5
    for h in range(X_HEADS):
        cols = slice(h * HEAD_DIM, (h + 1) * HEAD_DIM)
        s = _dot_nt(q_ref[:, cols].astype(BF16), k_ref[:, cols].astype(BF16)) * scale
        m = jnp.max(s, axis=-1, keepdims=True)
        e = jnp.exp(s - m)
        den = jnp.sum(e, axis=-1, keepdims=True)
        o_ref[:, cols] = (_dot(e.astype(BF16), v_ref[:, cols].astype(BF16)) / den).astype(o_ref.dtype)


def cross_attention(q, kv, *, n_batch, seq, n_mem, tq):
    nq = seq // tq
    return pl.pallas_call(
        _xattn_kernel,
        out_shape=jax.ShapeDtypeStruct((n_batch * seq, X_WIDTH), F32),
        grid=(n_batch, nq),
        in_specs=[
            pl.BlockSpec((tq, X_WIDTH), lambda b, i: (b * nq + i, 0)),
            pl.BlockSpec((n_mem, X_WIDTH), lambda b, i: (b, 0)),
            pl.BlockSpec((n_mem, X_WIDTH), lambda b, i: (b, 1)),
        ],
        out_specs=pl.BlockSpec((tq, X_WIDTH), lambda b, i: (b * nq + i, 0)),
        compiler_params=_cparams(("parallel", "parallel")),
        name="cross_attention",
    )(q, kv, kv)


def kernel(x, mem, norm_mix, w_in, gmlp_ln_g, gmlp_ln_b, gmlp_w_s, gmlp_b_s, cmp_pos, cmp_k_w1, cmp_k_w2,
           cmp_v_w1, cmp_v_w2, w_out, norm_xattn, norm_mem, xattn_wq, xattn_wkv, xattn_wo, norm_mlp, w_up,
           w_down, final_norm):
    bsz, seq, d = x.shape
    n_mem = mem.shape[1]
    t = bsz * seq
    assert d == D_MODEL and seq // CMP_STRIDE == N_SUB and seq // SEL_LEN == N_SEL
    assert seq % (ATT_TQ * 16) == 0

    w_in_p = jnp.pad(w_in, ((0, 0), (0, 0), (0, IN_PAD - IN_WIDTH))).astype(BF16)
    w_out_b = w_out.astype(BF16)
    wq_b = xattn_wq.astype(BF16)
    wkv_b = xattn_wkv.astype(BF16)
    wo_b = xattn_wo.astype(BF16)
    w_up_b = w_up.astype(BF16)
    w_down_b = w_down.astype(BF16)
    cmp_w1 = jnp.stack([cmp_k_w1, cmp_v_w1], axis=1).astype(BF16)
    cmp_w2 = jnp.stack([cmp_k_w2, cmp_v_w2], axis=1).astype(BF16)
    ovt = jnp.asarray(_overlap_t()).astype(BF16)

    xf = x.reshape(t, d)
    memf = mem.reshape(bsz * n_mem, d)
    bw = B_HEADS * HEAD_DIM
    cw = C_HEADS * HEAD_DIM
    gw = C_GROUP * HEAD_DIM

    for l in range(DEPTH):
        z = norm_matmul(xf, norm_mix[l], w_in_p[l], tm=512, tn=IN_TILE)

        out_a = gmlp(z, gmlp_ln_g[l], gmlp_ln_b[l], gmlp_w_s[l], gmlp_b_s[l], tm=512)

        zb = z[:, COL_B:COL_CQ]
        outs, lses = [], []
        for window, dil in DILATED_PAIRS:
            if dil == 1:
                src, base = z, COL_B // HEAD_DIM
            else:
                src, base = zb.reshape(t // dil, dil * 3 * bw), 0
            stride = 3 * B_HEADS if dil > 1 else 0
            o, lse = flash_attention(
                src, src, src, None, n_batch=bsz, n_grp=B_HEADS * dil, n_rep=1, seq=seq // dil,
                q_col=lambda n, s=stride, b=base, dd=dil: b + (n % dd) * s + n // dd,
                k_col=lambda n, s=stride, b=base, dd=dil: b + (n % dd) * s + B_HEADS + n // dd,
                v_col=lambda n, s=stride, b=base, dd=dil: b + (n % dd) * s + 2 * B_HEADS + n // dd,
                o_col=lambda n, dd=dil: (n % dd) * B_HEADS + n // dd,
                out_cols=dil * bw, mode="band", max_dist=window // dil, want_lse=True)
            outs.append(o.reshape(t, bw))
            lses.append(lse.reshape(t, bw))

        zc = z[:, COL_CMP:COL_SEL].reshape(bsz, N_SUB, CMP_STRIDE, 4, HEAD_DIM)
        a_cmp = zc.transpose(0, 3, 1, 2, 4).reshape(bsz * 4 * N_SUB, SUB_W)
        kvc = compress(a_cmp, cmp_pos[l].reshape(2, SUB_W), cmp_w1[l], cmp_w2[l], n_batch=bsz)
        o_c, sel = compressed_select(z, kvc, ovt, n_batch=bsz, seq=seq)
        o_s = flash_attention(
            z, z, z, sel, n_batch=bsz, n_grp=C_KV_HEADS, n_rep=C_GROUP, seq=seq,
            q_col=lambda n: COL_CQ // gw + n,
            k_col=lambda n: COL_SEL // HEAD_DIM + n,
            v_col=lambda n: COL_SEL // HEAD_DIM + C_KV_HEADS + n,
            o_col=lambda n: n, out_cols=cw, mode="select")
        o_w = flash_attention(
            z, z, z, None, n_batch=bsz, n_grp=C_KV_HEADS, n_rep=C_GROUP, seq=seq,
            q_col=lambda n: COL_CQ // gw + n,
            k_col=lambda n: COL_WIN // HEAD_DIM + n,
            v_col=lambda n: COL_WIN // HEAD_DIM + C_KV_HEADS + n,
            o_col=lambda n: n, out_cols=cw, mode="band", max_dist=WIN_LEN - 1)

        out_b, out_c = combine(outs[0], outs[1], outs[2], lses[0], lses[1], lses[2], o_c, o_s, o_w, z, tm=512)
        wo_l = w_out_b[l]
        xf = res_matmul(xf, [out_a, out_b, out_c],
                        [wo_l[0:A_WIDTH], wo_l[A_WIDTH:A_WIDTH + bw], wo_l[A_WIDTH + bw:]], tm=512, tn=1024)

        q = norm_matmul(xf, norm_xattn[l], wq_b[l], tm=512, tn=X_WIDTH)
        kv = norm_matmul(memf, norm_mem[l], wkv_b[l], tm=512, tn=2 * X_WIDTH)
        o = cross_attention(q, kv, n_batch=bsz, seq=seq, n_mem=n_mem, tq=512)
        xf = res_matmul(xf, [o], [wo_b[l]], tm=512, tn=1024)

        xf = mlp(xf, norm_mlp[l], w_up_b[l], w_down_b[l], tm=512, tf=512)

    return rmsnorm(xf, final_norm, tm=512).reshape(bsz, seq, d)
```

```python
import functools
import math

import numpy as np
import jax
import jax.numpy as jnp
from jax import lax
from jax.experimental import pallas as pl
from jax.experimental.pallas import tpu as pltpu

F32 = jnp.float32
BF16 = jnp.bfloat16

D_MODEL = 2048
DEPTH = 4
HEAD_DIM = 128
A_GROUPS = 4
A_WIDTH = A_GROUPS * HEAD_DIM
CHUNK = 128
B_HEADS = 4
DILATED_PAIRS = ((128, 1), (512, 4), (2048, 16))
C_HEADS = 8
C_KV_HEADS = 2
C_GROUP = C_HEADS // C_KV_HEADS
CMP_LEN = 32
CMP_STRIDE = 16
SEL_LEN = 64
SEL_TOP = 16
WIN_LEN = 512
X_HEADS = 4
X_WIDTH = X_HEADS * HEAD_DIM
EPS = 1e-6
NEG_INF = -1e30

COL_A = 0
COL_B = 2 * A_WIDTH
COL_CQ = COL_B + 3 * B_HEADS * HEAD_DIM
COL_CMP = COL_CQ + C_HEADS * HEAD_DIM
COL_SEL = COL_CMP + 2 * C_KV_HEADS * HEAD_DIM
COL_WIN = COL_SEL + 2 * C_KV_HEADS * HEAD_DIM
COL_GATE = COL_WIN + 2 * C_KV_HEADS * HEAD_DIM
IN_WIDTH = COL_GATE + 3 * C_HEADS
IN_TILE = 768
IN_PAD = 5376

LANE = 128
VMEM_LIMIT = 56 * 1024 * 1024

ATT_TQ = 256
ATT_TK = 256
ATT_CH = 128
LOG2E = math.log2(math.e)
LN2 = math.log(2.0)


def _cparams(sem):
    return pltpu.CompilerParams(dimension_semantics=sem, vmem_limit_bytes=VMEM_LIMIT)


def _rms_rows(x, g):
    ms = jnp.mean(x * x, axis=-1, keepdims=True)
    return x * lax.rsqrt(ms + EPS) * g


def _gelu_tanh(x):
    c = math.sqrt(2.0 / math.pi)
    return x * (0.5 * (1.0 + jnp.tanh(c * (x + 0.044715 * (x * x * x)))))


def _dot(a, b):
    return jnp.dot(a, b, preferred_element_type=F32)


def _dot_nt(a, b):
    return lax.dot_general(a, b, (((1,), (1,)), ((), ())), preferred_element_type=F32)


NORM_ROWS = 256


def _norm_into(h_ref, x_ref, g_ref):
    rows = x_ref.shape[0]

    def body(c, carry):
        r0 = pl.multiple_of(c * NORM_ROWS, NORM_ROWS)
        x = x_ref[pl.ds(r0, NORM_ROWS), :]
        h_ref[pl.ds(r0, NORM_ROWS), :] = _rms_rows(x, g_ref[...]).astype(h_ref.dtype)
        return carry

    lax.fori_loop(0, rows // NORM_ROWS, body, 0)


def _norm_matmul_kernel(x_ref, g_ref, w_ref, o_ref, h_ref):
    @pl.when(pl.program_id(1) == 0)
    def _():
        _norm_into(h_ref, x_ref, g_ref)

    o_ref[...] = _dot(h_ref[...], w_ref[...]).astype(o_ref.dtype)


def norm_matmul(x, g, w, *, tm, tn, out_dtype=F32):
    m, k = x.shape
    n = w.shape[1]
    return pl.pallas_call(
        _norm_matmul_kernel,
        out_shape=jax.ShapeDtypeStruct((m, n), out_dtype),
        grid=(m // tm, n // tn),
        in_specs=[
            pl.BlockSpec((tm, k), lambda i, j: (i, 0)),
            pl.BlockSpec((1, k), lambda i, j: (0, 0)),
            pl.BlockSpec((k, tn), lambda i, j: (0, j)),
        ],
        out_specs=pl.BlockSpec((tm, tn), lambda i, j: (i, j)),
        scratch_shapes=[pltpu.VMEM((tm, k), BF16)],
        compiler_params=_cparams(("parallel", "arbitrary")),
        name="norm_matmul",
    )(x, g.reshape(1, k), w)


def _res_matmul_kernel(*refs, n_parts):
    x_ref = refs[0]
    a_refs = refs[1:1 + n_parts]
    w_refs = refs[1 + n_parts:1 + 2 * n_parts]
    o_ref = refs[1 + 2 * n_parts]
    acc = x_ref[...]
    for a_ref, w_ref in zip(a_refs, w_refs):
        acc = acc + _dot(a_ref[...].astype(BF16), w_ref[...])
    o_ref[...] = acc


def res_matmul(x, parts, weights, *, tm, tn):
    m, n = x.shape
    n_parts = len(parts)
    in_specs = [pl.BlockSpec((tm, tn), lambda i, j: (i, j))]
    for a in parts:
        in_specs.append(pl.BlockSpec((tm, a.shape[1]), lambda i, j: (i, 0)))
    for w in weights:
        in_specs.append(pl.BlockSpec((w.shape[0], tn), lambda i, j: (0, j)))
    return pl.pallas_call(
        functools.partial(_res_matmul_kernel, n_parts=n_parts),
        out_shape=jax.ShapeDtypeStruct((m, n), F32),
        grid=(m // tm, n // tn),
        in_specs=in_specs,
        out_specs=pl.BlockSpec((tm, tn), lambda i, j: (i, j)),
        compiler_params=_cparams(("parallel", "parallel")),
        name="res_matmul",
    )(x, *parts, *weights)


def _mlp_kernel(x_ref, g_ref, wu_ref, wd_ref, o_ref, h_ref):
    f = pl.program_id(1)

    @pl.when(f == 0)
    def _():
        _norm_into(h_ref, x_ref, g_ref)
        o_ref[...] = x_ref[...]

    a = _dot(h_ref[...], wu_ref[...])
    a = jnp.square(jnp.maximum(a, 0.0)).astype(BF16)
    o_ref[...] += _dot(a, wd_ref[...])


def mlp(x, g, wu, wd, *, tm, tf):
    m, d = x.shape
    ff = wu.shape[1]
    return pl.pallas_call(
        _mlp_kernel,
        out_shape=jax.ShapeDtypeStruct((m, d), F32),
        grid=(m // tm, ff // tf),
        in_specs=[
            pl.BlockSpec((tm, d), lambda i, f: (i, 0)),
            pl.BlockSpec((1, d), lambda i, f: (0, 0)),
            pl.BlockSpec((d, tf), lambda i, f: (0, f)),
            pl.BlockSpec((tf, d), lambda i, f: (f, 0)),
        ],
        out_specs=pl.BlockSpec((tm, d), lambda i, f: (i, 0)),
        scratch_shapes=[pltpu.VMEM((tm, d), BF16)],
        compiler_params=_cparams(("parallel", "arbitrary")),
        name="mlp",
    )(x, g.reshape(1, d), wu, wd)


def _rmsnorm_kernel(x_ref, g_ref, o_ref):
    o_ref[...] = _rms_rows(x_ref[...], g_ref[...])


def rmsnorm(x, g, *, tm):
    m, d = x.shape
    return pl.pallas_call(
        _rmsnorm_kernel,
        out_shape=jax.ShapeDtypeStruct((m, d), F32),
        grid=(m // tm,),
        in_specs=[pl.BlockSpec((tm, d), lambda i: (i, 0)),
                  pl.BlockSpec((1, d), lambda i: (0, 0))],
        out_specs=pl.BlockSpec((tm, d), lambda i: (i, 0)),
        compiler_params=_cparams(("parallel",)),
        name="final_rmsnorm",
    )(x, g.reshape(1, d))


def _gmlp_kernel(z_ref, lng_ref, lnb_ref, ws_ref, bst_ref, o_ref, *, n_chunks):
    row = lax.broadcasted_iota(jnp.int32, (CHUNK, CHUNK), 0)
    col = lax.broadcasted_iota(jnp.int32, (CHUNK, CHUNK), 1)
    causal = row >= col
    w_tril = [jnp.where(causal, ws_ref[gi], 0.0).astype(BF16) for gi in range(A_GROUPS)]
    for c in range(n_chunks):
        rows = slice(c * CHUNK, (c + 1) * CHUNK)
        u = _gelu_tanh(z_ref[rows, 0:A_WIDTH])
        v = _gelu_tanh(z_ref[rows, A_WIDTH:2 * A_WIDTH])
        vc = v - jnp.mean(v, axis=-1, keepdims=True)
        vn = vc * lax.rsqrt(jnp.mean(vc * vc, axis=-1, keepdims=True) + EPS)
        vn = vn * lng_ref[...] + lnb_ref[...]
        for gi in range(A_GROUPS):
            cols = slice(gi * HEAD_DIM, (gi + 1) * HEAD_DIM)
            sv = _dot(w_tril[gi], vn[:, cols].astype(BF16)) + bst_ref[:, gi:gi + 1]
            o_ref[rows, cols] = (u[:, cols] * sv).astype(o_ref.dtype)


def gmlp(z, ln_g, ln_b, w_s, b_s, *, tm):
    t = z.shape[0]
    return pl.pallas_call(
        functools.partial(_gmlp_kernel, n_chunks=tm // CHUNK),
        out_shape=jax.ShapeDtypeStruct((t, A_WIDTH), F32),
        grid=(t // tm,),
        in_specs=[
            pl.BlockSpec((tm, 2 * A_WIDTH), lambda i: (i, 0)),
            pl.BlockSpec((1, A_WIDTH), lambda i: (0, 0)),
            pl.BlockSpec((1, A_WIDTH), lambda i: (0, 0)),
            pl.BlockSpec((A_GROUPS, CHUNK, CHUNK), lambda i: (0, 0, 0)),
            pl.BlockSpec((CHUNK, A_GROUPS), lambda i: (0, 0)),
        ],
        out_specs=pl.BlockSpec((tm, A_WIDTH), lambda i: (i, 0)),
        compiler_params=_cparams(("parallel",)),
        name="gmlp_gating",
    )(z, ln_g.reshape(1, A_WIDTH), ln_b.reshape(1, A_WIDTH), w_s, b_s.T)


def _flash_kernel(*refs, n_rep, mode, max_dist, want_lse):
    refs = list(refs)
    q_ref, k_ref, v_ref = refs[:3]
    pos = 3
    sel_ref = None
    if mode == "select":
        sel_ref = refs[pos]
        pos += 1
    o_ref = refs[pos]
    pos += 1
    lse_ref = None
    if want_lse:
        lse_ref = refs[pos]
        pos += 1
    qs_ref, kb_ref, vb_ref, m_ref, accl_ref = refs[pos:pos + 5]

    tq, tk, ch = ATT_TQ, ATT_TK, ATT_CH
    seq = k_ref.shape[0]
    i = pl.program_id(2)
    select = mode == "select"

    @pl.when(i == 0)
    def _():
        def prep(c, carry):
            r0 = pl.multiple_of(c * tk, tk)
            kb_ref[pl.ds(r0, tk), 0:HEAD_DIM] = k_ref[pl.ds(r0, tk), :].astype(BF16)
            if select:
                blk = (r0 + lax.broadcasted_iota(jnp.int32, (tk, LANE), 0)) >> 6
                lane = lax.broadcasted_iota(jnp.int32, (tk, LANE), 1)
                kb_ref[pl.ds(r0, tk), HEAD_DIM:HEAD_DIM + LANE] = jnp.where(blk == lane, 1.0, 0.0).astype(BF16)
            vb_ref[pl.ds(r0, tk), 0:HEAD_DIM] = v_ref[pl.ds(r0, tk), :].astype(BF16)
            vb_ref[pl.ds(r0, tk), HEAD_DIM:2 * HEAD_DIM] = jnp.ones((tk, HEAD_DIM), BF16)
            return carry

        lax.fori_loop(0, seq // tk, prep, 0)

    qscale = (HEAD_DIM ** -0.5) * LOG2E
    if select:
        pen = jnp.where(sel_ref[...] > 0.5, 0.0, NEG_INF).astype(BF16)
    for r in range(n_rep):
        rows = slice(r * tq, (r + 1) * tq)
        qs_ref[rows, 0:HEAD_DIM] = (q_ref[:, r * HEAD_DIM:(r + 1) * HEAD_DIM] * qscale).astype(BF16)
        if select:
            qs_ref[rows, HEAD_DIM:HEAD_DIM + LANE] = pen
    m_ref[...] = jnp.full(m_ref.shape, NEG_INF, F32)
    accl_ref[...] = jnp.zeros(accl_ref.shape, F32)

    row = lax.broadcasted_iota(jnp.int32, (tq, tk), 0)
    col = lax.broadcasted_iota(jnp.int32, (tq, tk), 1)

    def tile_bias(off):
        dist = off * tk + row - col
        if off == 0:
            ok = dist >= 0
            if not select and max_dist < tk - 1:
                ok = ok & (dist <= max_dist)
        elif not select and (off + 1) * tk - 1 > max_dist:
            ok = dist <= max_dist
        else:
            return None
        return jnp.where(ok, 0.0, NEG_INF)

    n_ch = n_rep * tq // ch

    def steps(tiles):
        scores = []
        for kt, bias in tiles:
            k0 = pl.multiple_of(kt * tk, tk)
            scores.append((_dot_nt(qs_ref[...], kb_ref[pl.ds(k0, tk), :]), k0, bias))
        for s_all, k0, bias in scores:
            ps, alphas = [], []
            for c in range(n_ch):
                rows = slice(c * ch, (c + 1) * ch)
                s = s_all[rows, :]
                if bias is not None:
                    b0 = (c % (tq // ch)) * ch
                    s = s + bias[b0:b0 + ch, :]
                m_old = m_ref[rows, :]
                m_new = jnp.maximum(m_old, jnp.max(s, axis=-1, keepdims=True))
                alphas.append(jnp.exp2(m_old - m_new))
                ps.append(jnp.exp2(s - jnp.concatenate([m_new] * (tk // LANE), axis=1)).astype(BF16))
                m_ref[rows, :] = m_new
            pv_all = _dot(jnp.concatenate(ps, axis=0), vb_ref[pl.ds(k0, tk), :])
            for c in range(n_ch):
                rows = slice(c * ch, (c + 1) * ch)
                alpha2 = jnp.concatenate([alphas[c], alphas[c]], axis=1)
                accl_ref[rows, :] = alpha2 * accl_ref[rows, :] + pv_all[rows, :]

    if select:
        def body(j, carry):
            steps([(2 * j, None), (2 * j + 1, None)])
            return carry

        lax.fori_loop(0, i >> 1, body, 0)

        @pl.when((i & 1) == 1)
        def _():
            steps([(i - 1, None), (i, tile_bias(0))])

        @pl.when((i & 1) == 0)
        def _():
            steps([(i, tile_bias(0))])
    else:
        n_prev = -(-max_dist // tk)

        @pl.when(i >= n_prev)
        def _():
            steps([(i - off, tile_bias(off)) for off in range(n_prev, -1, -1)])

        for first in range(n_prev):
            @pl.when(i == first)
            def _(first=first):
                steps([(first - off, tile_bias(off)) for off in range(first, -1, -1)])

    for r in range(n_rep):
        rows = slice(r * tq, (r + 1) * tq)
        cols = slice(r * HEAD_DIM, (r + 1) * HEAD_DIM)
        l = accl_ref[rows, HEAD_DIM:2 * HEAD_DIM]
        o_ref[:, cols] = (accl_ref[rows, 0:HEAD_DIM] / l).astype(o_ref.dtype)
        if want_lse:
            lse_ref[:, cols] = (m_ref[rows, :] + jnp.log2(l)) * LN2


def flash_attention(q_arr, k_arr, v_arr, sel_arr, *, n_batch, n_grp, n_rep, seq, q_col, k_col,
                    v_col, o_col, out_cols, mode, max_dist=0, want_lse=False):
    tq = ATT_TQ
    nq = seq // tq
    qw = n_rep * HEAD_DIM
    kw = HEAD_DIM + (LANE if mode == "select" else 0)
    in_specs = [
        pl.BlockSpec((tq, qw), lambda b, n, i: (b * nq + i, q_col(n))),
        pl.BlockSpec((seq, HEAD_DIM), lambda b, n, i: (b, k_col(n))),
        pl.BlockSpec((seq, HEAD_DIM), lambda b, n, i: (b, v_col(n))),
    ]
    args = [q_arr, k_arr, v_arr]
    if mode == "select":
        in_specs.append(pl.BlockSpec((tq, LANE), lambda b, n, i: ((b * n_grp + n) * nq + i, 0)))
        args.append(sel_arr)
    out_shape = [jax.ShapeDtypeStruct((n_batch * seq, out_cols), F32)]
    out_specs = [pl.BlockSpec((tq, qw), lambda b, n, i: (b * nq + i, o_col(n)))]
    if want_lse:
        out_shape.append(jax.ShapeDtypeStruct((n_batch * seq, out_cols), F32))
        out_specs.append(pl.BlockSpec((tq, qw), lambda b, n, i: (b * nq + i, o_col(n))))
    res = pl.pallas_call(
        functools.partial(_flash_kernel, n_rep=n_rep, mode=mode, max_dist=max_dist, want_lse=want_lse),
        out_shape=out_shape,
        grid=(n_batch, n_grp, nq),
        in_specs=in_specs,
        out_specs=out_specs,
        scratch_shapes=[
            pltpu.VMEM((n_rep * tq, kw), BF16),
            pltpu.VMEM((seq, kw), BF16),
            pltpu.VMEM((seq, 2 * HEAD_DIM), BF16),
            pltpu.VMEM((n_rep * tq, LANE), F32),
            pltpu.VMEM((n_rep * tq, 2 * HEAD_DIM), F32),
        ],
        compiler_params=_cparams(("parallel", "parallel", "arbitrary")),
        name="flash_" + mode,
    )(*args)
    return res if want_lse else res[0]


DIL_CH = 128
DIL_KW = 2 * DIL_CH
DIL_UNITS = 8


def _dilated_kernel(q_ref, k_ref, v_ref, o_ref, m_ref, acc_ref, l_ref):
    ch, kw = DIL_CH, DIL_KW
    seq = q_ref.shape[0]
    qscale = (HEAD_DIM ** -0.5) * LOG2E
    row = lax.broadcasted_iota(jnp.int32, (ch, kw), 0)
    col = lax.broadcasted_iota(jnp.int32, (ch, kw), 1)
    ones = jnp.ones((kw, HEAD_DIM), BF16)
    patterns = sorted(DILATED_PAIRS, key=lambda wd: -wd[1])

    def rows(start, n, dil):
        return pl.ds(start, n) if dil == 1 else pl.ds(start, n, stride=dil)

    def band_bias(first, span):
        dist = row - col if first else ch + row - col
        return jnp.where((dist >= 0) & (dist <= span), 0.0, NEG_INF)

    def run_units(units, dil, span, phase):
        windows = {}
        scores = []
        for q0, k0, first, wid in units:
            if wid not in windows:
                k = k_ref[rows(k0, kw, dil), :].astype(BF16)
                v = v_ref[rows(k0, kw, dil), :].astype(BF16)
                windows[wid] = (k, jnp.concatenate([v, ones], axis=1))
            q = (q_ref[rows(q0, ch, dil), :] * qscale).astype(BF16)
            scores.append(_dot_nt(q, windows[wid][0]) + band_bias(first, span))
        for (q0, k0, first, wid), s in zip(units, scores):
            qrows = rows(q0, ch, dil)
            m_cur = jnp.max(s, axis=-1, keepdims=True)
            if phase == "first":
                m_new = jnp.broadcast_to(m_cur, (ch, LANE))
            else:
                m_old = m_ref[qrows, :]
                m_new = jnp.maximum(m_old, m_cur)
                alpha = jnp.exp2(m_old - m_new)
            p = jnp.exp2(s - jnp.concatenate([m_new] * (kw // LANE), axis=1)).astype(BF16)
            pv = _dot(p, windows[wid][1])
            acc, l = pv[:, 0:HEAD_DIM], pv[:, HEAD_DIM:2 * HEAD_DIM]
            if phase != "first":
                acc = alpha * acc_ref[qrows, :] + acc
                l = alpha * l_ref[qrows, :] + l
            if phase == "last":
                o_ref[qrows, :] = (acc / l).astype(o_ref.dtype)
            else:
                m_ref[qrows, :] = m_new
                acc_ref[qrows, :] = acc
                l_ref[qrows, :] = l

    for idx, (window, dil) in enumerate(patterns):
        phase = "first" if idx == 0 else ("last" if idx == len(patterns) - 1 else "mid")
        assert phase != "last" or dil == 1
        span = window // dil
        assert span <= ch
        n_units = seq // dil // ch
        per_class = min(n_units, max(DIL_UNITS // dil, 2))
        n_classes = DIL_UNITS // per_class
        assert n_units % per_class == 0 and dil % n_classes == 0

        def group(r0, j0, head, dil=dil, span=span, phase=phase, per_class=per_class, n_classes=n_classes):
            units = []
            for rc in range(n_classes):
                r = r0 + rc
                for u in range(per_class):
                    first = head and u == 0
                    q0 = r + dil * ch * (j0 + u)
                    k0 = r if (head and u <= 1) else r + dil * ch * (j0 + u - 1)
                    wid = (rc, 0) if (head and u <= 1) else (rc, u)
                    units.append((q0, k0, first, wid))
            run_units(units, dil, span, phase)

        def class_loop(g, carry, group=group, per_class=per_class, n_classes=n_classes, n_units=n_units):
            r0 = g * n_classes
            group(r0, 0, True)
            if n_units > per_class:
                def tail(jb, c):
                    group(r0, jb * per_class, False)
                    return c
                lax.fori_loop(1, n_units // per_class, tail, 0)
            return carry

        if dil // n_classes == 1:
            class_loop(0, 0)
        else:
            lax.fori_loop(0, dil // n_classes, class_loop, 0)


def dilated_attention(z, *, n_batch, seq):
    bw = B_HEADS * HEAD_DIM
    base = COL_B // HEAD_DIM
    blk = (seq, HEAD_DIM)
    return pl.pallas_call(
        _dilated_kernel,
        out_shape=jax.ShapeDtypeStruct((n_batch * seq, bw), F32),
        grid=(n_batch, B_HEADS),
        in_specs=[
            pl.BlockSpec(blk, lambda b, h: (b, base + h)),
            pl.BlockSpec(blk, lambda b, h: (b, base + B_HEADS + h)),
            pl.BlockSpec(blk, lambda b, h: (b, base + 2 * B_HEADS + h)),
        ],
        out_specs=pl.BlockSpec(blk, lambda b, h: (b, h)),
        scratch_shapes=[pltpu.VMEM(blk, F32)] * 3,
        compiler_params=_cparams(("parallel", "parallel")),
        name="dilated_attention",
    )(z, z, z)


N_SUB = 256
SUB_W = CMP_STRIDE * HEAD_DIM


def _compress_kernel(a_ref, pos_ref, w1_ref, w2_ref, o_ref):
    a = a_ref[...]
    top = (a + pos_ref[0:1, :]).astype(BF16)
    bot = (a + pos_ref[1:2, :]).astype(BF16)
    p = _dot(top, w1_ref[0:SUB_W, :])
    q = _dot(bot, w1_ref[SUB_W:2 * SUB_W, :])
    h = _gelu_tanh(p + pltpu.roll(q, N_SUB - 1, 0))
    o_ref[...] = _dot(h.astype(BF16), w2_ref[...]).astype(o_ref.dtype)


def compress(a, pos2, w1, w2, *, n_batch):
    return pl.pallas_call(
        _compress_kernel,
        out_shape=jax.ShapeDtypeStruct((n_batch * 4 * N_SUB, HEAD_DIM), BF16),
        grid=(n_batch, 4),
        in_specs=[
            pl.BlockSpec((N_SUB, SUB_W), lambda b, n: (b * 4 + n, 0)),
            pl.BlockSpec((2, SUB_W), lambda b, n: (0, 0)),
            pl.BlockSpec((None, 2 * SUB_W, HEAD_DIM), lambda b, n: (n // 2, 0, 0)),
            pl.BlockSpec((None, HEAD_DIM, HEAD_DIM), lambda b, n: (n // 2, 0, 0)),
        ],
        out_specs=pl.BlockSpec((N_SUB, HEAD_DIM), lambda b, n: (b * 4 + n, 0)),
        compiler_params=_cparams(("parallel", "parallel")),
        name="nsa_compress",
    )(a, pos2, w1, w2)


N_SEL = 64


def _csel_kernel(q_ref, kc_ref, vc_ref, ovt_ref, oc_ref, sel_ref, sc_ref):
    tq = ATT_TQ
    q0 = pl.program_id(2) * tq
    scale = HEAD_DIM ** -0.5
    t_rows = q0 + lax.broadcasted_iota(jnp.int32, (tq, N_SUB), 0)
    n_cols = lax.broadcasted_iota(jnp.int32, (tq, N_SUB), 1)
    valid = (n_cols * CMP_STRIDE + (CMP_LEN - 1)) <= t_rows
    valid_f = jnp.where(valid, 1.0, 0.0)
    kc = kc_ref[...]
    vc = vc_ref[...]
    psum = jnp.zeros((tq, N_SUB), F32)
    for r in range(C_GROUP):
        cols = slice(r * HEAD_DIM, (r + 1) * HEAD_DIM)
        s = _dot_nt(q_ref[:, cols].astype(BF16), kc) * scale
        s = jnp.where(valid, s, NEG_INF)
        m = jnp.max(s, axis=-1, keepdims=True)
        e = jnp.exp(s - m) * valid_f
        den = jnp.sum(e, axis=-1, keepdims=True)
        p = e * jnp.where(den > 0.0, 1.0 / den, 0.0)
        oc_ref[:, cols] = _dot(p.astype(BF16), vc)
        psum = psum + p
    hi = psum.astype(BF16)
    lo = (psum - hi.astype(F32)).astype(BF16)
    ovt = ovt_ref[...]
    imp_t = _dot_nt(ovt, hi) + _dot_nt(ovt, lo)
    jrow = lax.broadcasted_iota(jnp.int32, (LANE, tq), 0)
    tl = q0 + lax.broadcasted_iota(jnp.int32, (LANE, tq), 1)
    jt = tl >> 6
    forced = (jrow == 0) | (jrow == jt) | (jrow == jt - 1)
    valid_s = jrow * SEL_LEN <= tl
    score = jnp.where(forced, 1e4, jnp.where(valid_s, imp_t, -1.0))
    sc_ref[...] = score
    sc = sc_ref[0:N_SEL, :]
    jr = lax.broadcasted_iota(jnp.int32, (N_SEL, tq), 0)
    cnt = jnp.zeros((N_SEL, tq), F32)
    for j in range(N_SEL):
        row = sc_ref[j:j + 1, :]
        beats = (row > sc) | ((row == sc) & (jr > j))
        cnt = cnt + jnp.where(beats, 1.0, 0.0)
    sel_t = jnp.where(cnt < float(SEL_TOP), 1.0, 0.0)
    sel_pad = jnp.concatenate([sel_t, jnp.zeros((LANE - N_SEL, tq), F32)], axis=0)
    sel_ref[...] = sel_pad.T


def compressed_select(z, kvc, ovt, *, n_batch, seq):
    tq = ATT_TQ
    nq = seq // tq
    qw = C_GROUP * HEAD_DIM
    return pl.pallas_call(
        _csel_kernel,
        out_shape=[jax.ShapeDtypeStruct((n_batch * seq, C_HEADS * HEAD_DIM), F32),
                   jax.ShapeDtypeStruct((n_batch * C_KV_HEADS * seq, LANE), F32)],
        grid=(n_batch, C_KV_HEADS, nq),
        in_specs=[
            pl.BlockSpec((tq, qw), lambda b, g, i: (b * nq + i, COL_CQ // qw + g)),
            pl.BlockSpec((N_SUB, HEAD_DIM), lambda b, g, i: (b * 4 + g, 0)),
            pl.BlockSpec((N_SUB, HEAD_DIM), lambda b, g, i: (b * 4 + 2 + g, 0)),
            pl.BlockSpec((LANE, N_SUB), lambda b, g, i: (0, 0)),
        ],
        out_specs=[
            pl.BlockSpec((tq, qw), lambda b, g, i: (b * nq + i, g)),
            pl.BlockSpec((tq, LANE), lambda b, g, i: ((b * C_KV_HEADS + g) * nq + i, 0)),
        ],
        scratch_shapes=[pltpu.VMEM((LANE, tq), F32)],
        compiler_params=_cparams(("parallel", "parallel", "parallel")),
        name="nsa_compressed_select",
    )(z, kvc, kvc, ovt)


def _overlap_t():
    n_c = N_SUB - 1
    c_start = np.arange(n_c) * CMP_STRIDE
    s_start = np.arange(N_SEL) * SEL_LEN
    ov = ((c_start[:, None] <= s_start[None, :] + SEL_LEN - 1)
          & (c_start[:, None] + CMP_LEN - 1 >= s_start[None, :])).astype(np.float32)
    out = np.zeros((LANE, N_SUB), np.float32)
    out[:N_SEL, :n_c] = ov.T
    return out


def _mix_out_kernel(x_ref, oa_ref, ob_ref, oc_ref, os_ref, ow_ref, gate_ref, w_ref, o_ref, mix_ref):
    @pl.when(pl.program_id(1) == 0)
    def _():
        bw = B_HEADS * HEAD_DIM
        mix_ref[:, 0:A_WIDTH] = oa_ref[...].astype(BF16)
        mix_ref[:, A_WIDTH:A_WIDTH + bw] = ob_ref[...].astype(BF16)
        g = 1.0 / (1.0 + jnp.exp(-gate_ref[...]))
        for h in range(C_HEADS):
            cols = slice(h * HEAD_DIM, (h + 1) * HEAD_DIM)
            o = (g[:, 3 * h:3 * h + 1] * oc_ref[:, cols]
                 + g[:, 3 * h + 1:3 * h + 2] * os_ref[:, cols]
                 + g[:, 3 * h + 2:3 * h + 3] * ow_ref[:, cols])
            c0 = A_WIDTH + bw + h * HEAD_DIM
            mix_ref[:, c0:c0 + HEAD_DIM] = o.astype(BF16)

    o_ref[...] = x_ref[...] + _dot(mix_ref[...], w_ref[...])


def mix_out(x, out_a, out_b, oc, os_, ow, z, w, *, tm, tn):
    m, n = x.shape
    bw = B_HEADS * HEAD_DIM
    cw = C_HEADS * HEAD_DIM
    kdim = w.shape[0]
    spec_c = pl.BlockSpec((tm, cw), lambda i, j: (i, 0))
    return pl.pallas_call(
        _mix_out_kernel,
        out_shape=jax.ShapeDtypeStruct((m, n), F32),
        grid=(m // tm, n // tn),
        in_specs=[
            pl.BlockSpec((tm, tn), lambda i, j: (i, j)),
            pl.BlockSpec((tm, A_WIDTH), lambda i, j: (i, 0)),
            pl.BlockSpec((tm, bw), lambda i, j: (i, 0)),
            spec_c, spec_c, spec_c,
            pl.BlockSpec((tm, LANE), lambda i, j: (i, COL_GATE // LANE)),
            pl.BlockSpec((kdim, tn), lambda i, j: (0, j)),
        ],
        out_specs=pl.BlockSpec((tm, tn), lambda i, j: (i, j)),
        scratch_shapes=[pltpu.VMEM((tm, kdim), BF16)],
        compiler_params=_cparams(("parallel", "arbitrary")),
        name="mix_out",
    )(x, out_a, out_b, oc, os_, ow, z, w)


def _xattn_kernel(q_ref, k_ref, v_ref, o_ref):
    scale = HEAD_DIM ** -0.5
    for h in range(X_HEADS):
        cols = slice(h * HEAD_DIM, (h + 1) * HEAD_DIM)
        s = _dot_nt(q_ref[:, cols].astype(BF16), k_ref[:, cols].astype(BF16)) * scale
        m = jnp.max(s, axis=-1, keepdims=True)
        e = jnp.exp(s - m)
        den = jnp.sum(e, axis=-1, keepdims=True)
        o_ref[:, cols] = (_dot(e.astype(BF16), v_ref[:, cols].astype(BF16)) / den).astype(o_ref.dtype)


def cross_attention(q, kv, *, n_batch, seq, n_mem, tq):
    nq = seq // tq
    return pl.pallas_call(
        _xattn_kernel,
        out_shape=jax.ShapeDtypeStruct((n_batch * seq, X_WIDTH), F32),
        grid=(n_batch, nq),
        in_specs=[
            pl.BlockSpec((tq, X_WIDTH), lambda b, i: (b * nq + i, 0)),
            pl.BlockSpec((n_mem, X_WIDTH), lambda b, i: (b, 0)),
            pl.BlockSpec((n_mem, X_WIDTH), lambda b, i: (b, 1)),
        ],
        out_specs=pl.BlockSpec((tq, X_WIDTH), lambda b, i: (b * nq + i, 0)),
        compiler_params=_cparams(("parallel", "parallel")),
        name="cross_attention",
    )(q, kv, kv)


def kernel(x, mem, norm_mix, w_in, gmlp_ln_g, gmlp_ln_b, gmlp_w_s, gmlp_b_s, cmp_pos, cmp_k_w1, cmp_k_w2,
           cmp_v_w1, cmp_v_w2, w_out, norm_xattn, norm_mem, xattn_wq, xattn_wkv, xattn_wo, norm_mlp, w_up,
           w_down, final_norm):
    bsz, seq, d = x.shape
    n_mem = mem.shape[1]
    t = bsz * seq
    assert d == D_MODEL and seq // CMP_STRIDE == N_SUB and seq // SEL_LEN == N_SEL
    assert seq % (ATT_TQ * 16) == 0

    w_in_p = jnp.pad(w_in, ((0, 0), (0, 0), (0, IN_PAD - IN_WIDTH))).astype(BF16)
    w_out_b = w_out.astype(BF16)
    wq_b = xattn_wq.astype(BF16)
    wkv_b = xattn_wkv.astype(BF16)
    wo_b = xattn_wo.astype(BF16)
    w_up_b = w_up.astype(BF16)
    w_down_b = w_down.astype(BF16)
    cmp_w1 = jnp.stack([cmp_k_w1, cmp_v_w1], axis=1).astype(BF16)
    cmp_w2 = jnp.stack([cmp_k_w2, cmp_v_w2], axis=1).astype(BF16)
    ovt = jnp.asarray(_overlap_t()).astype(BF16)

    xf = x.reshape(t, d)
    memf = mem.reshape(bsz * n_mem, d)
    bw = B_HEADS * HEAD_DIM
    cw = C_HEADS * HEAD_DIM
    gw = C_GROUP * HEAD_DIM

    for l in range(DEPTH):
        z = norm_matmul(xf, norm_mix[l], w_in_p[l], tm=512, tn=IN_TILE)

        out_a = gmlp(z, gmlp_ln_g[l], gmlp_ln_b[l], gmlp_w_s[l], gmlp_b_s[l], tm=512)

        out_b = dilated_attention(z, n_batch=bsz, seq=seq)

        zc = z[:, COL_CMP:COL_SEL].reshape(bsz, N_SUB, CMP_STRIDE, 4, HEAD_DIM)
        a_cmp = zc.transpose(0, 3, 1, 2, 4).reshape(bsz * 4 * N_SUB, SUB_W)
        kvc = compress(a_cmp, cmp_pos[l].reshape(2, SUB_W), cmp_w1[l], cmp_w2[l], n_batch=bsz)
        o_c, sel = compressed_select(z, kvc, ovt, n_batch=bsz, seq=seq)
        o_s = flash_attention(
            z, z, z, sel, n_batch=bsz, n_grp=C_KV_HEADS, n_rep=C_GROUP, seq=seq,
            q_col=lambda n: COL_CQ // gw + n,
            k_col=lambda n: COL_SEL // HEAD_DIM + n,
            v_col=lambda n: COL_SEL // HEAD_DIM + C_KV_HEADS + n,
            o_col=lambda n: n, out_cols=cw, mode="select")
        o_w = flash_attention(
            z, z, z, None, n_batch=bsz, n_grp=C_KV_HEADS, n_rep=C_GROUP, seq=seq,
            q_col=lambda n: COL_CQ // gw + n,
            k_col=lambda n: COL_WIN // HEAD_DIM + n,
            v_col=lambda n: COL_WIN // HEAD_DIM + C_KV_HEADS + n,
            o_col=lambda n: n, out_cols=cw, mode="band", max_dist=WIN_LEN - 1)

        xf = mix_out(xf, out_a, out_b, o_c, o_s, o_w, z, w_out_b[l], tm=512, tn=1024)

        q = norm_matmul(xf, norm_xattn[l], wq_b[l], tm=512, tn=X_WIDTH)
        kv = norm_matmul(memf, norm_mem[l], wkv_b[l], tm=512, tn=2 * X_WIDTH)
        o = cross_attention(q, kv, n_batch=bsz, seq=seq, n_mem=n_mem, tq=512)
        xf = res_matmul(xf, [o], [wo_b[l]], tm=512, tn=1024)

        xf = mlp(xf, norm_mlp[l], w_up_b[l], w_down_b[l], tm=512, tf=512)

    return rmsnorm(xf, final_norm, tm=512).reshape(bsz, seq, d)
```

```python
import functools
import math

import numpy as np
import jax
import jax.numpy as jnp
from jax import lax
from jax.experimental import pallas as pl
from jax.experimental.pallas import tpu as pltpu

F32 = jnp.float32
BF16 = jnp.bfloat16

D_MODEL = 2048
DEPTH = 4
HEAD_DIM = 128
A_GROUPS = 4
A_WIDTH = A_GROUPS * HEAD_DIM
CHUNK = 128
B_HEADS = 4
DILATED_PAIRS = ((128, 1), (512, 4), (2048, 16))
C_HEADS = 8
C_KV_HEADS = 2
C_GROUP = C_HEADS // C_KV_HEADS
CMP_LEN = 32
CMP_STRIDE = 16
SEL_LEN = 64
SEL_TOP = 16
WIN_LEN = 512
X_HEADS = 4
X_WIDTH = X_HEADS * HEAD_DIM
EPS = 1e-6
NEG_INF = -1e30

COL_A = 0
COL_B = 2 * A_WIDTH
COL_CQ = COL_B + 3 * B_HEADS * HEAD_DIM
COL_CMP = COL_CQ + C_HEADS * HEAD_DIM
COL_SEL = COL_CMP + 2 * C_KV_HEADS * HEAD_DIM
COL_WIN = COL_SEL + 2 * C_KV_HEADS * HEAD_DIM
COL_GATE = COL_WIN + 2 * C_KV_HEADS * HEAD_DIM
IN_WIDTH = COL_GATE + 3 * C_HEADS
IN_TILE = 768
IN_PAD = 5376

LANE = 128
SUBLANE = 8
VMEM_LIMIT = 56 * 1024 * 1024

ATT_TQ = 256
ATT_TK = 256
ATT_CH = 128
LOG2E = math.log2(math.e)
LN2 = math.log(2.0)


def _cparams(sem):
    return pltpu.CompilerParams(dimension_semantics=sem, vmem_limit_bytes=VMEM_LIMIT)


def _rms_rows(x, g):
    ms = jnp.mean(x * x, axis=-1, keepdims=True)
    return x * lax.rsqrt(ms + EPS) * g


def _gelu_tanh(x):
    c = math.sqrt(2.0 / math.pi)
    return x * (0.5 * (1.0 + jnp.tanh(c * (x + 0.044715 * (x * x * x)))))


def _dot(a, b):
    return jnp.dot(a, b, preferred_element_type=F32)


def _dot_nt(a, b):
    return lax.dot_general(a, b, (((1,), (1,)), ((), ())), preferred_element_type=F32)


NORM_ROWS = 256


def _norm_into(h_ref, x_ref, g_ref):
    rows = x_ref.shape[0]

    def body(c, carry):
        r0 = pl.multiple_of(c * NORM_ROWS, NORM_ROWS)
        x = x_ref[pl.ds(r0, NORM_ROWS), :]
        h_ref[pl.ds(r0, NORM_ROWS), :] = _rms_rows(x, g_ref[...]).astype(h_ref.dtype)
        return carry

    lax.fori_loop(0, rows // NORM_ROWS, body, 0)


def _norm_matmul_kernel(x_ref, g_ref, w_ref, o_ref, h_ref):
    @pl.when(pl.program_id(1) == 0)
    def _():
        _norm_into(h_ref, x_ref, g_ref)

    o_ref[...] = _dot(h_ref[...], w_ref[...]).astype(o_ref.dtype)


def norm_matmul(x, g, w, *, tm, tn, out_dtype=F32):
    m, k = x.shape
    n = w.shape[1]
    return pl.pallas_call(
        _norm_matmul_kernel,
        out_shape=jax.ShapeDtypeStruct((m, n), out_dtype),
        grid=(m // tm, n // tn),
        in_specs=[
            pl.BlockSpec((tm, k), lambda i, j: (i, 0)),
            pl.BlockSpec((1, k), lambda i, j: (0, 0)),
            pl.BlockSpec((k, tn), lambda i, j: (0, j)),
        ],
        out_specs=pl.BlockSpec((tm, tn), lambda i, j: (i, j)),
        scratch_shapes=[pltpu.VMEM((tm, k), BF16)],
        compiler_params=_cparams(("parallel", "arbitrary")),
        name="norm_matmul",
    )(x, g.reshape(1, k), w)


def _res_matmul_kernel(*refs, n_parts):
    x_ref = refs[0]
    a_refs = refs[1:1 + n_parts]
    w_refs = refs[1 + n_parts:1 + 2 * n_parts]
    o_ref = refs[1 + 2 * n_parts]
    acc = x_ref[...]
    for a_ref, w_ref in zip(a_refs, w_refs):
        acc = acc + _dot(a_ref[...].astype(BF16), w_ref[...])
    o_ref[...] = acc


def res_matmul(x, parts, weights, *, tm, tn):
    m, n = x.shape
    n_parts = len(parts)
    in_specs = [pl.BlockSpec((tm, tn), lambda i, j: (i, j))]
    for a in parts:
        in_specs.append(pl.BlockSpec((tm, a.shape[1]), lambda i, j: (i, 0)))
    for w in weights:
        in_specs.append(pl.BlockSpec((w.shape[0], tn), lambda i, j: (0, j)))
    return pl.pallas_call(
        functools.partial(_res_matmul_kernel, n_parts=n_parts),
        out_shape=jax.ShapeDtypeStruct((m, n), F32),
        grid=(m // tm, n // tn),
        in_specs=in_specs,
        out_specs=pl.BlockSpec((tm, tn), lambda i, j: (i, j)),
        compiler_params=_cparams(("parallel", "parallel")),
        name="res_matmul",
    )(x, *parts, *weights)


def _mlp_kernel(x_ref, g_ref, wu_ref, wd_ref, gout_ref, o_ref, h_ref, *, norm_out):
    f = pl.program_id(1)

    @pl.when(f == 0)
    def _():
        _norm_into(h_ref, x_ref, g_ref)
        o_ref[...] = x_ref[...]

    a = _dot(h_ref[...], wu_ref[...])
    a = jnp.square(jnp.maximum(a, 0.0)).astype(BF16)
    o_ref[...] += _dot(a, wd_ref[...])

    if norm_out:
        @pl.when(f == pl.num_programs(1) - 1)
        def _():
            _norm_into(o_ref, o_ref, gout_ref)


def mlp(x, g, wu, wd, g_out, *, tm, tf, norm_out):
    m, d = x.shape
    ff = wu.shape[1]
    return pl.pallas_call(
        functools.partial(_mlp_kernel, norm_out=norm_out),
        out_shape=jax.ShapeDtypeStruct((m, d), F32),
        grid=(m // tm, ff // tf),
        in_specs=[
            pl.BlockSpec((tm, d), lambda i, f: (i, 0)),
            pl.BlockSpec((1, d), lambda i, f: (0, 0)),
            pl.BlockSpec((d, tf), lambda i, f: (0, f)),
            pl.BlockSpec((tf, d), lambda i, f: (f, 0)),
            pl.BlockSpec((1, d), lambda i, f: (0, 0)),
        ],
        out_specs=pl.BlockSpec((tm, d), lambda i, f: (i, 0)),
        scratch_shapes=[pltpu.VMEM((tm, d), BF16)],
        compiler_params=_cparams(("parallel", "arbitrary")),
        name="mlp",
    )(x, g.reshape(1, d), wu, wd, g_out.reshape(1, d))


def _rmsnorm_kernel(x_ref, g_ref, o_ref):
    o_ref[...] = _rms_rows(x_ref[...], g_ref[...])


def rmsnorm(x, g, *, tm):
    m, d = x.shape
    return pl.pallas_call(
        _rmsnorm_kernel,
        out_shape=jax.ShapeDtypeStruct((m, d), F32),
        grid=(m // tm,),
        in_specs=[pl.BlockSpec((tm, d), lambda i: (i, 0)),
                  pl.BlockSpec((1, d), lambda i: (0, 0))],
        out_specs=pl.BlockSpec((tm, d), lambda i: (i, 0)),
        compiler_params=_cparams(("parallel",)),
        name="final_rmsnorm",
    )(x, g.reshape(1, d))


def _gmlp_kernel(z_ref, lng_ref, lnb_ref, ws_ref, bst_ref, o_ref, *, n_chunks):
    row = lax.broadcasted_iota(jnp.int32, (CHUNK, CHUNK), 0)
    col = lax.broadcasted_iota(jnp.int32, (CHUNK, CHUNK), 1)
    causal = row >= col
    w_tril = [jnp.where(causal, ws_ref[gi], 0.0).astype(BF16) for gi in range(A_GROUPS)]
    for c in range(n_chunks):
        rows = slice(c * CHUNK, (c + 1) * CHUNK)
        u = _gelu_tanh(z_ref[rows, 0:A_WIDTH])
        v = _gelu_tanh(z_ref[rows, A_WIDTH:2 * A_WIDTH])
        vc = v - jnp.mean(v, axis=-1, keepdims=True)
        vn = vc * lax.rsqrt(jnp.mean(vc * vc, axis=-1, keepdims=True) + EPS)
        vn = vn * lng_ref[...] + lnb_ref[...]
        for gi in range(A_GROUPS):
            cols = slice(gi * HEAD_DIM, (gi + 1) * HEAD_DIM)
            sv = _dot(w_tril[gi], vn[:, cols].astype(BF16)) + bst_ref[:, gi:gi + 1]
            o_ref[rows, cols] = (u[:, cols] * sv).astype(o_ref.dtype)


def gmlp(z, ln_g, ln_b, w_s, b_s, *, tm):
    t = z.shape[0]
    return pl.pallas_call(
        functools.partial(_gmlp_kernel, n_chunks=tm // CHUNK),
        out_shape=jax.ShapeDtypeStruct((t, A_WIDTH), F32),
        grid=(t // tm,),
        in_specs=[
            pl.BlockSpec((tm, 2 * A_WIDTH), lambda i: (i, 0)),
            pl.BlockSpec((1, A_WIDTH), lambda i: (0, 0)),
            pl.BlockSpec((1, A_WIDTH), lambda i: (0, 0)),
            pl.BlockSpec((A_GROUPS, CHUNK, CHUNK), lambda i: (0, 0, 0)),
            pl.BlockSpec((CHUNK, A_GROUPS), lambda i: (0, 0)),
        ],
        out_specs=pl.BlockSpec((tm, A_WIDTH), lambda i: (i, 0)),
        compiler_params=_cparams(("parallel",)),
        name="gmlp_gating",
    )(z, ln_g.reshape(1, A_WIDTH), ln_b.reshape(1, A_WIDTH), w_s, b_s.T)


def _flash_kernel(*refs, n_rep, mode, max_dist, want_lse):
    refs = list(refs)
    q_ref, k_ref, v_ref = refs[:3]
    pos = 3
    sel_ref = None
    if mode == "select":
        sel_ref = refs[pos]
        pos += 1
    o_ref = refs[pos]
    pos += 1
    lse_ref = None
    if want_lse:
        lse_ref = refs[pos]
        pos += 1
    qs_ref, kb_ref, vb_ref, m_ref, accl_ref = refs[pos:pos + 5]

    tq, tk, ch = ATT_TQ, ATT_TK, ATT_CH
    seq = k_ref.shape[0]
    i = pl.program_id(2)
    select = mode == "select"

    @pl.when(i == 0)
    def _():
        def prep(c, carry):
            r0 = pl.multiple_of(c * tk, tk)
            kb_ref[pl.ds(r0, tk), 0:HEAD_DIM] = k_ref[pl.ds(r0, tk), :].astype(BF16)
            if select:
                blk = (r0 + lax.broadcasted_iota(jnp.int32, (tk, LANE), 0)) >> 6
                lane = lax.broadcasted_iota(jnp.int32, (tk, LANE), 1)
                kb_ref[pl.ds(r0, tk), HEAD_DIM:HEAD_DIM + LANE] = jnp.where(blk == lane, 1.0, 0.0).astype(BF16)
            vb_ref[pl.ds(r0, tk), 0:HEAD_DIM] = v_ref[pl.ds(r0, tk), :].astype(BF16)
            vb_ref[pl.ds(r0, tk), HEAD_DIM:2 * HEAD_DIM] = jnp.ones((tk, HEAD_DIM), BF16)
            return carry

        lax.fori_loop(0, seq // tk, prep, 0)

    qscale = (HEAD_DIM ** -0.5) * LOG2E
    if select:
        pen = jnp.where(sel_ref[...] > 0.5, 0.0, NEG_INF).astype(BF16)
    for r in range(n_rep):
        rows = slice(r * tq, (r + 1) * tq)
        qs_ref[rows, 0:HEAD_DIM] = (q_ref[:, r * HEAD_DIM:(r + 1) * HEAD_DIM] * qscale).astype(BF16)
        if select:
            qs_ref[rows, HEAD_DIM:HEAD_DIM + LANE] = pen
    m_ref[...] = jnp.full(m_ref.shape, NEG_INF, F32)
    accl_ref[...] = jnp.zeros(accl_ref.shape, F32)

    row = lax.broadcasted_iota(jnp.int32, (tq, tk), 0)
    col = lax.broadcasted_iota(jnp.int32, (tq, tk), 1)

    def tile_bias(off):
        dist = off * tk + row - col
        if off == 0:
            ok = dist >= 0
            if not select and max_dist < tk - 1:
                ok = ok & (dist <= max_dist)
        elif not select and (off + 1) * tk - 1 > max_dist:
            ok = dist <= max_dist
        else:
            return None
        return jnp.where(ok, 0.0, NEG_INF)

    n_ch = n_rep * tq // ch

    def steps(tiles):
        scores = []
        for kt, bias in tiles:
            k0 = pl.multiple_of(kt * tk, tk)
            scores.append((_dot_nt(qs_ref[...], kb_ref[pl.ds(k0, tk), :]), k0, bias))
        for s_all, k0, bias in scores:
            ps, alphas = [], []
            for c in range(n_ch):
                rows = slice(c * ch, (c + 1) * ch)
                s = s_all[rows, :]
                if bias is not None:
                    b0 = (c % (tq // ch)) * ch
                    s = s + bias[b0:b0 + ch, :]
                m_old = m_ref[rows, :]
                m_new = jnp.maximum(m_old, jnp.max(s, axis=-1, keepdims=True))
                alphas.append(jnp.exp2(m_old - m_new))
                ps.append(jnp.exp2(s - jnp.concatenate([m_new] * (tk // LANE), axis=1)).astype(BF16))
                m_ref[rows, :] = m_new
            pv_all = _dot(jnp.concatenate(ps, axis=0), vb_ref[pl.ds(k0, tk), :])
            for c in range(n_ch):
                rows = slice(c * ch, (c + 1) * ch)
                alpha2 = jnp.concatenate([alphas[c], alphas[c]], axis=1)
                accl_ref[rows, :] = alpha2 * accl_ref[rows, :] + pv_all[rows, :]

    if select:
        def body(j, carry):
            steps([(2 * j, None), (2 * j + 1, None)])
            return carry

        lax.fori_loop(0, i >> 1, body, 0)

        @pl.when((i & 1) == 1)
        def _():
            steps([(i - 1, None), (i, tile_bias(0))])

        @pl.when((i & 1) == 0)
        def _():
            steps([(i, tile_bias(0))])
    else:
        n_prev = -(-max_dist // tk)

        @pl.when(i >= n_prev)
        def _():
            steps([(i - off, tile_bias(off)) for off in range(n_prev, -1, -1)])

        for first in range(n_prev):
            @pl.when(i == first)
            def _(first=first):
                steps([(first - off, tile_bias(off)) for off in range(first, -1, -1)])

    for r in range(n_rep):
        rows = slice(r * tq, (r + 1) * tq)
        cols = slice(r * HEAD_DIM, (r + 1) * HEAD_DIM)
        l = accl_ref[rows, HEAD_DIM:2 * HEAD_DIM]
        o_ref[:, cols] = (accl_ref[rows, 0:HEAD_DIM] / l).astype(o_ref.dtype)
        if want_lse:
            lse_ref[:, cols] = (m_ref[rows, :] + jnp.log2(l)) * LN2


def flash_attention(q_arr, k_arr, v_arr, sel_arr, *, n_batch, n_grp, n_rep, seq, q_col, k_col,
                    v_col, o_col, out_cols, mode, max_dist=0, want_lse=False):
    tq = ATT_TQ
    nq = seq // tq
    qw = n_rep * HEAD_DIM
    kw = HEAD_DIM + (LANE if mode == "select" else 0)
    in_specs = [
        pl.BlockSpec((tq, qw), lambda b, n, i: (b * nq + i, q_col(n))),
        pl.BlockSpec((seq, HEAD_DIM), lambda b, n, i: (b, k_col(n))),
        pl.BlockSpec((seq, HEAD_DIM), lambda b, n, i: (b, v_col(n))),
    ]
    args = [q_arr, k_arr, v_arr]
    if mode == "select":
        in_specs.append(pl.BlockSpec((tq, LANE), lambda b, n, i: ((b * n_grp + n) * nq + i, 0)))
        args.append(sel_arr)
    out_shape = [jax.ShapeDtypeStruct((n_batch * seq, out_cols), F32)]
    out_specs = [pl.BlockSpec((tq, qw), lambda b, n, i: (b * nq + i, o_col(n)))]
    if want_lse:
        out_shape.append(jax.ShapeDtypeStruct((n_batch * seq, out_cols), F32))
        out_specs.append(pl.BlockSpec((tq, qw), lambda b, n, i: (b * nq + i, o_col(n))))
    res = pl.pallas_call(
        functools.partial(_flash_kernel, n_rep=n_rep, mode=mode, max_dist=max_dist, want_lse=want_lse),
        out_shape=out_shape,
        grid=(n_batch, n_grp, nq),
        in_specs=in_specs,
        out_specs=out_specs,
        scratch_shapes=[
            pltpu.VMEM((n_rep * tq, kw), BF16),
            pltpu.VMEM((seq, kw), BF16),
            pltpu.VMEM((seq, 2 * HEAD_DIM), BF16),
            pltpu.VMEM((n_rep * tq, LANE), F32),
            pltpu.VMEM((n_rep * tq, 2 * HEAD_DIM), F32),
        ],
        compiler_params=_cparams(("parallel", "parallel", "arbitrary")),
        name="flash_" + mode,
    )(*args)
    return res if want_lse else res[0]


DIL_CH = 128
DIL_KW = 2 * DIL_CH
DIL_UNITS = 8


def _dilated_kernel(q_ref, k_ref, v_ref, o_ref, m_ref, acc_ref, l_ref):
    ch, kw = DIL_CH, DIL_KW
    seq = q_ref.shape[0]
    qscale = (HEAD_DIM ** -0.5) * LOG2E
    row = lax.broadcasted_iota(jnp.int32, (ch, kw), 0)
    col = lax.broadcasted_iota(jnp.int32, (ch, kw), 1)
    ones = jnp.ones((kw, HEAD_DIM), BF16)
    patterns = sorted(DILATED_PAIRS, key=lambda wd: -wd[1])

    def rows(start, n, dil):
        return pl.ds(start, n) if dil == 1 else pl.ds(start, n, stride=dil)

    def band_bias(first, span):
        dist = row - col if first else ch + row - col
        return jnp.where((dist >= 0) & (dist <= span), 0.0, NEG_INF)

    def run_units(units, dil, span, phase):
        windows = {}
        scores = []
        for q0, k0, first, wid in units:
            if wid not in windows:
                k = k_ref[rows(k0, kw, dil), :].astype(BF16)
                v = v_ref[rows(k0, kw, dil), :].astype(BF16)
                windows[wid] = (k, jnp.concatenate([v, ones], axis=1))
            q = (q_ref[rows(q0, ch, dil), :] * qscale).astype(BF16)
            scores.append(_dot_nt(q, windows[wid][0]) + band_bias(first, span))
        for (q0, k0, first, wid), s in zip(units, scores):
            qrows = rows(q0, ch, dil)
            m_cur = jnp.max(s, axis=-1, keepdims=True)
            if phase == "first":
                m_new = jnp.broadcast_to(m_cur, (ch, LANE))
            else:
                m_old = m_ref[qrows, :]
                m_new = jnp.maximum(m_old, m_cur)
                alpha = jnp.exp2(m_old - m_new)
            p = jnp.exp2(s - jnp.concatenate([m_new] * (kw // LANE), axis=1)).astype(BF16)
            pv = _dot(p, windows[wid][1])
            acc, l = pv[:, 0:HEAD_DIM], pv[:, HEAD_DIM:2 * HEAD_DIM]
            if phase != "first":
                acc = alpha * acc_ref[qrows, :] + acc
                l = alpha * l_ref[qrows, :] + l
            if phase == "last":
                o_ref[qrows, :] = (acc / l).astype(o_ref.dtype)
            else:
                m_ref[qrows, :] = m_new
                acc_ref[qrows, :] = acc
                l_ref[qrows, :] = l

    for idx, (window, dil) in enumerate(patterns):
        phase = "first" if idx == 0 else ("last" if idx == len(patterns) - 1 else "mid")
        assert phase != "last" or dil == 1
        span = window // dil
        assert span <= ch
        n_units = seq // dil // ch
        per_class = min(n_units, max(DIL_UNITS // dil, 2))
        n_classes = DIL_UNITS // per_class
        assert n_units % per_class == 0 and dil % n_classes == 0

        def group(r0, j0, head, dil=dil, span=span, phase=phase, per_class=per_class, n_classes=n_classes):
            units = []
            for rc in range(n_classes):
                r = r0 + rc
                for u in range(per_class):
                    first = head and u == 0
                    q0 = r + dil * ch * (j0 + u)
                    k0 = r if (head and u <= 1) else r + dil * ch * (j0 + u - 1)
                    wid = (rc, 0) if (head and u <= 1) else (rc, u)
                    units.append((q0, k0, first, wid))
            run_units(units, dil, span, phase)

        def class_loop(g, carry, group=group, per_class=per_class, n_classes=n_classes, n_units=n_units):
            r0 = g * n_classes
            group(r0, 0, True)
            if n_units > per_class:
                def tail(jb, c):
                    group(r0, jb * per_class, False)
                    return c
                lax.fori_loop(1, n_units // per_class, tail, 0)
            return carry

        if dil // n_classes == 1:
            class_loop(0, 0)
        else:
            lax.fori_loop(0, dil // n_classes, class_loop, 0)


def dilated_attention(z, *, n_batch, seq):
    bw = B_HEADS * HEAD_DIM
    base = COL_B // HEAD_DIM
    blk = (seq, HEAD_DIM)
    return pl.pallas_call(
        _dilated_kernel,
        out_shape=jax.ShapeDtypeStruct((n_batch * seq, bw), F32),
        grid=(n_batch, B_HEADS),
        in_specs=[
            pl.BlockSpec(blk, lambda b, h: (b, base + h)),
            pl.BlockSpec(blk, lambda b, h: (b, base + B_HEADS + h)),
            pl.BlockSpec(blk, lambda b, h: (b, base + 2 * B_HEADS + h)),
        ],
        out_specs=pl.BlockSpec(blk, lambda b, h: (b, h)),
        scratch_shapes=[pltpu.VMEM(blk, F32)] * 3,
        compiler_params=_cparams(("parallel", "parallel")),
        name="dilated_attention",
    )(z, z, z)


N_SUB = 256
SUB_W = CMP_STRIDE * HEAD_DIM


def _compress_kernel(x_ref, pos_ref, w1_ref, w2_ref, o_ref):
    p = jnp.zeros((N_SUB, HEAD_DIM), F32)
    q = jnp.zeros((N_SUB, HEAD_DIM), F32)
    for i in range(CMP_STRIDE):
        a = x_ref[pl.ds(i, N_SUB, stride=CMP_STRIDE), :]
        top = (a + pos_ref[i:i + 1, :]).astype(BF16)
        bot = (a + pos_ref[CMP_STRIDE + i:CMP_STRIDE + i + 1, :]).astype(BF16)
        p = p + _dot(top, w1_ref[i * HEAD_DIM:(i + 1) * HEAD_DIM, :])
        q = q + _dot(bot, w1_ref[(CMP_STRIDE + i) * HEAD_DIM:(CMP_STRIDE + i + 1) * HEAD_DIM, :])
    h = _gelu_tanh(p + pltpu.roll(q, N_SUB - 1, 0))
    o_ref[...] = _dot(h.astype(BF16), w2_ref[...]).astype(o_ref.dtype)


def compress(z, pos, w1, w2, *, n_batch, seq):
    return pl.pallas_call(
        _compress_kernel,
        out_shape=jax.ShapeDtypeStruct((n_batch * 4 * N_SUB, HEAD_DIM), BF16),
        grid=(n_batch, 4),
        in_specs=[
            pl.BlockSpec((seq, HEAD_DIM), lambda b, n: (b, COL_CMP // HEAD_DIM + n)),
            pl.BlockSpec((CMP_LEN, HEAD_DIM), lambda b, n: (0, 0)),
            pl.BlockSpec((None, CMP_LEN * HEAD_DIM, HEAD_DIM), lambda b, n: (n // 2, 0, 0)),
            pl.BlockSpec((None, HEAD_DIM, HEAD_DIM), lambda b, n: (n // 2, 0, 0)),
        ],
        out_specs=pl.BlockSpec((N_SUB, HEAD_DIM), lambda b, n: (b * 4 + n, 0)),
        compiler_params=_cparams(("parallel", "parallel")),
        name="nsa_compress",
    )(z, pos, w1, w2)


N_SEL = 64


def _csel_kernel(q_ref, kc_ref, vc_ref, ovt_ref, oc_ref, sel_ref, sc_ref):
    tq = ATT_TQ
    q0 = pl.program_id(2) * tq
    qscale = (HEAD_DIM ** -0.5) * LOG2E
    t_rows = q0 + lax.broadcasted_iota(jnp.int32, (tq, N_SUB), 0)
    n_cols = lax.broadcasted_iota(jnp.int32, (tq, N_SUB), 1)
    bias = jnp.where((n_cols * CMP_STRIDE + (CMP_LEN - 1)) <= t_rows, 0.0, NEG_INF)
    has_any = (q0 + lax.broadcasted_iota(jnp.int32, (tq, 1), 0)) >= CMP_LEN - 1
    kc = kc_ref[...]
    vc = vc_ref[...]
    psum = jnp.zeros((tq, N_SUB), F32)
    for r in range(C_GROUP):
        cols = slice(r * HEAD_DIM, (r + 1) * HEAD_DIM)
        s = _dot_nt((q_ref[:, cols] * qscale).astype(BF16), kc) + bias
        e = jnp.exp2(s - jnp.max(s, axis=-1, keepdims=True))
        den = jnp.sum(e, axis=-1, keepdims=True)
        p = e * jnp.where(has_any, 1.0 / den, 0.0)
        oc_ref[:, cols] = _dot(p.astype(BF16), vc)
        psum = psum + p
    hi = psum.astype(BF16)
    lo = (psum - hi.astype(F32)).astype(BF16)
    ovt = ovt_ref[...]
    imp_t = _dot_nt(ovt, hi) + _dot_nt(ovt, lo)
    jrow = lax.broadcasted_iota(jnp.int32, (LANE, tq), 0)
    tl = q0 + lax.broadcasted_iota(jnp.int32, (LANE, tq), 1)
    jt = tl >> 6
    forced = (jrow == 0) | (jrow == jt) | (jrow == jt - 1)
    valid_s = jrow * SEL_LEN <= tl
    score = jnp.where(forced, 1e4, jnp.where(valid_s, imp_t, -1.0))
    sc_ref[...] = score
    n_grp = N_SEL // SUBLANE
    grp = [sc_ref[g * SUBLANE:(g + 1) * SUBLANE, :] for g in range(n_grp)]
    cnt = [jnp.zeros((SUBLANE, tq), F32) for _ in range(n_grp)]
    sub = lax.broadcasted_iota(jnp.int32, (SUBLANE, tq), 0)
    for j in range(N_SEL):
        row = sc_ref[j:j + 1, :]
        for g in range(n_grp):
            if g < j // SUBLANE:
                beats = row > grp[g]
            elif g > j // SUBLANE:
                beats = row >= grp[g]
            else:
                beats = (row > grp[g]) | ((row == grp[g]) & (sub > j % SUBLANE))
            cnt[g] = cnt[g] + jnp.where(beats, 1.0, 0.0)
    sel_t = [jnp.where(c < float(SEL_TOP), 1.0, 0.0) for c in cnt]
    sel_pad = jnp.concatenate(sel_t + [jnp.zeros((LANE - N_SEL, tq), F32)], axis=0)
    sel_ref[...] = sel_pad.T


def compressed_select(z, kvc, ovt, *, n_batch, seq):
    tq = ATT_TQ
    nq = seq // tq
    qw = C_GROUP * HEAD_DIM
    return pl.pallas_call(
        _csel_kernel,
        out_shape=[jax.ShapeDtypeStruct((n_batch * seq, C_HEADS * HEAD_DIM), F32),
                   jax.ShapeDtypeStruct((n_batch * C_KV_HEADS * seq, LANE), F32)],
        grid=(n_batch, C_KV_HEADS, nq),
        in_specs=[
            pl.BlockSpec((tq, qw), lambda b, g, i: (b * nq + i, COL_CQ // qw + g)),
            pl.BlockSpec((N_SUB, HEAD_DIM), lambda b, g, i: (b * 4 + g, 0)),
            pl.BlockSpec((N_SUB, HEAD_DIM), lambda b, g, i: (b * 4 + 2 + g, 0)),
            pl.BlockSpec((LANE, N_SUB), lambda b, g, i: (0, 0)),
        ],
        out_specs=[
            pl.BlockSpec((tq, qw), lambda b, g, i: (b * nq + i, g)),
            pl.BlockSpec((tq, LANE), lambda b, g, i: ((b * C_KV_HEADS + g) * nq + i, 0)),
        ],
        scratch_shapes=[pltpu.VMEM((LANE, tq), F32)],
        compiler_params=_cparams(("parallel", "parallel", "parallel")),
        name="nsa_compressed_select",
    )(z, kvc, kvc, ovt)


def _overlap_t():
    n_c = N_SUB - 1
    c_start = np.arange(n_c) * CMP_STRIDE
    s_start = np.arange(N_SEL) * SEL_LEN
    ov = ((c_start[:, None] <= s_start[None, :] + SEL_LEN - 1)
          & (c_start[:, None] + CMP_LEN - 1 >= s_start[None, :])).astype(np.float32)
    out = np.zeros((LANE, N_SUB), np.float32)
    out[:N_SEL, :n_c] = ov.T
    return out


def _mix_out_kernel(x_ref, oa_ref, ob_ref, oc_ref, os_ref, ow_ref, gate_ref, w_ref, o_ref, mix_ref):
    @pl.when(pl.program_id(1) == 0)
    def _():
        bw = B_HEADS * HEAD_DIM
        mix_ref[:, 0:A_WIDTH] = oa_ref[...].astype(BF16)
        mix_ref[:, A_WIDTH:A_WIDTH + bw] = ob_ref[...].astype(BF16)
        g = 1.0 / (1.0 + jnp.exp(-gate_ref[...]))
        for h in range(C_HEADS):
            cols = slice(h * HEAD_DIM, (h + 1) * HEAD_DIM)
            o = (g[:, 3 * h:3 * h + 1] * oc_ref[:, cols]
                 + g[:, 3 * h + 1:3 * h + 2] * os_ref[:, cols]
                 + g[:, 3 * h + 2:3 * h + 3] * ow_ref[:, cols])
            c0 = A_WIDTH + bw + h * HEAD_DIM
            mix_ref[:, c0:c0 + HEAD_DIM] = o.astype(BF16)

    o_ref[...] = x_ref[...] + _dot(mix_ref[...], w_ref[...])


def mix_out(x, out_a, out_b, oc, os_, ow, z, w, *, tm, tn):
    m, n = x.shape
    bw = B_HEADS * HEAD_DIM
    cw = C_HEADS * HEAD_DIM
    kdim = w.shape[0]
    spec_c = pl.BlockSpec((tm, cw), lambda i, j: (i, 0))
    return pl.pallas_call(
        _mix_out_kernel,
        out_shape=jax.ShapeDtypeStruct((m, n), F32),
        grid=(m // tm, n // tn),
        in_specs=[
            pl.BlockSpec((tm, tn), lambda i, j: (i, j)),
            pl.BlockSpec((tm, A_WIDTH), lambda i, j: (i, 0)),
            pl.BlockSpec((tm, bw), lambda i, j: (i, 0)),
            spec_c, spec_c, spec_c,
            pl.BlockSpec((tm, LANE), lambda i, j: (i, COL_GATE // LANE)),
            pl.BlockSpec((kdim, tn), lambda i, j: (0, j)),
        ],
        out_specs=pl.BlockSpec((tm, tn), lambda i, j: (i, j)),
        scratch_shapes=[pltpu.VMEM((tm, kdim), BF16)],
        compiler_params=_cparams(("parallel", "arbitrary")),
        name="mix_out",
    )(x, out_a, out_b, oc, os_, ow, z, w)


def _xattn_kernel(x_ref, g_ref, wq_ref, k_ref, v_ref, wo_ref, o_ref, h_ref, a_ref):
    n_mem = k_ref.shape[0]
    _norm_into(h_ref, x_ref, g_ref)
    q = _dot(h_ref[...], wq_ref[...]) * ((HEAD_DIM ** -0.5) * LOG2E)
    ones = jnp.ones((n_mem, HEAD_DIM), BF16)
    for h in range(X_HEADS):
        cols = slice(h * HEAD_DIM, (h + 1) * HEAD_DIM)
        s = _dot_nt(q[:, cols].astype(BF16), k_ref[:, cols].astype(BF16))
        e = jnp.exp2(s - jnp.max(s, axis=-1, keepdims=True))
        pv = _dot(e.astype(BF16), jnp.concatenate([v_ref[:, cols].astype(BF16), ones], axis=1))
        a_ref[:, cols] = (pv[:, 0:HEAD_DIM] / pv[:, HEAD_DIM:2 * HEAD_DIM]).astype(BF16)
    o_ref[...] = x_ref[...] + _dot(a_ref[...], wo_ref[...])


def cross_attention_block(x, g, wq, kv, wo, *, n_batch, seq, n_mem, tm):
    nq = seq // tm
    d = x.shape[1]
    return pl.pallas_call(
        _xattn_kernel,
        out_shape=jax.ShapeDtypeStruct(x.shape, F32),
        grid=(n_batch, nq),
        in_specs=[
            pl.BlockSpec((tm, d), lambda b, i: (b * nq + i, 0)),
            pl.BlockSpec((1, d), lambda b, i: (0, 0)),
            pl.BlockSpec((d, X_WIDTH), lambda b, i: (0, 0)),
            pl.BlockSpec((n_mem, X_WIDTH), lambda b, i: (b, 0)),
            pl.BlockSpec((n_mem, X_WIDTH), lambda b, i: (b, 1)),
            pl.BlockSpec((X_WIDTH, d), lambda b, i: (0, 0)),
        ],
        out_specs=pl.BlockSpec((tm, d), lambda b, i: (b * nq + i, 0)),
        scratch_shapes=[pltpu.VMEM((tm, d), BF16), pltpu.VMEM((tm, X_WIDTH), BF16)],
        compiler_params=_cparams(("parallel", "parallel")),
        name="cross_attention",
    )(x, g.reshape(1, d), wq, kv, kv, wo)


def kernel(x, mem, norm_mix, w_in, gmlp_ln_g, gmlp_ln_b, gmlp_w_s, gmlp_b_s, cmp_pos, cmp_k_w1, cmp_k_w2,
           cmp_v_w1, cmp_v_w2, w_out, norm_xattn, norm_mem, xattn_wq, xattn_wkv, xattn_wo, norm_mlp, w_up,
           w_down, final_norm):
    bsz, seq, d = x.shape
    n_mem = mem.shape[1]
    t = bsz * seq
    assert d == D_MODEL and seq // CMP_STRIDE == N_SUB and seq // SEL_LEN == N_SEL
    assert seq % (ATT_TQ * 16) == 0

    w_in_p = jnp.pad(w_in, ((0, 0), (0, 0), (0, IN_PAD - IN_WIDTH))).astype(BF16)
    w_out_b = w_out.astype(BF16)
    wq_b = xattn_wq.astype(BF16)
    wkv_b = xattn_wkv.astype(BF16)
    wo_b = xattn_wo.astype(BF16)
    w_up_b = w_up.astype(BF16)
    w_down_b = w_down.astype(BF16)
    cmp_w1 = jnp.stack([cmp_k_w1, cmp_v_w1], axis=1).astype(BF16)
    cmp_w2 = jnp.stack([cmp_k_w2, cmp_v_w2], axis=1).astype(BF16)
    ovt = jnp.asarray(_overlap_t()).astype(BF16)

    xf = x.reshape(t, d)
    memf = mem.reshape(bsz * n_mem, d)
    bw = B_HEADS * HEAD_DIM
    cw = C_HEADS * HEAD_DIM
    gw = C_GROUP * HEAD_DIM

    for l in range(DEPTH):
        z = norm_matmul(xf, norm_mix[l], w_in_p[l], tm=1024, tn=IN_TILE)

        out_a = gmlp(z, gmlp_ln_g[l], gmlp_ln_b[l], gmlp_w_s[l], gmlp_b_s[l], tm=512)

        out_b = dilated_attention(z, n_batch=bsz, seq=seq)

        kvc = compress(z, cmp_pos[l], cmp_w1[l], cmp_w2[l], n_batch=bsz, seq=seq)
        o_c, sel = compressed_select(z, kvc, ovt, n_batch=bsz, seq=seq)
        o_s = flash_attention(
            z, z, z, sel, n_batch=bsz, n_grp=C_KV_HEADS, n_rep=C_GROUP, seq=seq,
            q_col=lambda n: COL_CQ // gw + n,
            k_col=lambda n: COL_SEL // HEAD_DIM + n,
            v_col=lambda n: COL_SEL // HEAD_DIM + C_KV_HEADS + n,
            o_col=lambda n: n, out_cols=cw, mode="select")
        o_w = flash_attention(
            z, z, z, None, n_batch=bsz, n_grp=C_KV_HEADS, n_rep=C_GROUP, seq=seq,
            q_col=lambda n: COL_CQ // gw + n,
            k_col=lambda n: COL_WIN // HEAD_DIM + n,
            v_col=lambda n: COL_WIN // HEAD_DIM + C_KV_HEADS + n,
            o_col=lambda n: n, out_cols=cw, mode="band", max_dist=WIN_LEN - 1)

        xf = mix_out(xf, out_a, out_b, o_c, o_s, o_w, z, w_out_b[l], tm=512, tn=1024)

        kv = norm_matmul(memf, norm_mem[l], wkv_b[l], tm=512, tn=2 * X_WIDTH)
        xf = cross_attention_block(xf, norm_xattn[l], wq_b[l], kv, wo_b[l], n_batch=bsz, seq=seq, n_mem=n_mem,
                                   tm=512)

        xf = mlp(xf, norm_mlp[l], w_up_b[l], w_down_b[l], final_norm, tm=1024, tf=512,
                 norm_out=(l == DEPTH - 1))

    return xf.reshape(bsz, seq, d)
```

```python
import functools
import math

import numpy as np
import jax
import jax.numpy as jnp
from jax import lax
from jax.experimental import pallas as pl
from jax.experimental.pallas import tpu as pltpu

F32 = jnp.float32
BF16 = jnp.bfloat16

D_MODEL = 2048
DEPTH = 4
HEAD_DIM = 128
A_GROUPS = 4
A_WIDTH = A_GROUPS * HEAD_DIM
CHUNK = 128
B_HEADS = 4
DILATED_PAIRS = ((128, 1), (512, 4), (2048, 16))
C_HEADS = 8
C_KV_HEADS = 2
C_GROUP = C_HEADS // C_KV_HEADS
CMP_LEN = 32
CMP_STRIDE = 16
SEL_LEN = 64
SEL_TOP = 16
WIN_LEN = 512
X_HEADS = 4
X_WIDTH = X_HEADS * HEAD_DIM
EPS = 1e-6
NEG_INF = -1e30

COL_A = 0
COL_B = 2 * A_WIDTH
COL_CQ = COL_B + 3 * B_HEADS * HEAD_DIM
COL_CMP = COL_CQ + C_HEADS * HEAD_DIM
COL_SEL = COL_CMP + 2 * C_KV_HEADS * HEAD_DIM
COL_WIN = COL_SEL + 2 * C_KV_HEADS * HEAD_DIM
COL_GATE = COL_WIN + 2 * C_KV_HEADS * HEAD_DIM
IN_WIDTH = COL_GATE + 3 * C_HEADS
IN_TILE = 768
IN_PAD = 5376

LANE = 128
SUBLANE = 8
VMEM_LIMIT = 56 * 1024 * 1024

ATT_TQ = 256
ATT_TK = 256
ATT_CH = 128
LOG2E = math.log2(math.e)
LN2 = math.log(2.0)


def _cparams(sem):
    return pltpu.CompilerParams(dimension_semantics=sem, vmem_limit_bytes=VMEM_LIMIT)


def _rms_rows(x, g):
    ms = jnp.mean(x * x, axis=-1, keepdims=True)
    return x * lax.rsqrt(ms + EPS) * g


def _gelu_tanh(x):
    c = math.sqrt(2.0 / math.pi)
    return x * (0.5 * (1.0 + jnp.tanh(c * (x + 0.044715 * (x * x * x)))))


def _dot(a, b):
    return jnp.dot(a, b, preferred_element_type=F32)


def _dot_nt(a, b):
    return lax.dot_general(a, b, (((1,), (1,)), ((), ())), preferred_element_type=F32)


NORM_ROWS = 256


def _norm_into(h_ref, x_ref, g_ref):
    rows = x_ref.shape[0]

    def body(c, carry):
        r0 = pl.multiple_of(c * NORM_ROWS, NORM_ROWS)
        x = x_ref[pl.ds(r0, NORM_ROWS), :]
        h_ref[pl.ds(r0, NORM_ROWS), :] = _rms_rows(x, g_ref[...]).astype(h_ref.dtype)
        return carry

    lax.fori_loop(0, rows // NORM_ROWS, body, 0)


def _norm_matmul_kernel(x_ref, g_ref, w_ref, o_ref, h_ref):
    @pl.when(pl.program_id(1) == 0)
    def _():
        _norm_into(h_ref, x_ref, g_ref)

    o_ref[...] = _dot(h_ref[...], w_ref[...]).astype(o_ref.dtype)


def norm_matmul(x, g, w, *, tm, tn, out_dtype=F32):
    m, k = x.shape
    n = w.shape[1]
    return pl.pallas_call(
        _norm_matmul_kernel,
        out_shape=jax.ShapeDtypeStruct((m, n), out_dtype),
        grid=(m // tm, n // tn),
        in_specs=[
            pl.BlockSpec((tm, k), lambda i, j: (i, 0)),
            pl.BlockSpec((1, k), lambda i, j: (0, 0)),
            pl.BlockSpec((k, tn), lambda i, j: (0, j)),
        ],
        out_specs=pl.BlockSpec((tm, tn), lambda i, j: (i, j)),
        scratch_shapes=[pltpu.VMEM((tm, k), BF16)],
        compiler_params=_cparams(("parallel", "arbitrary")),
        name="norm_matmul",
    )(x, g.reshape(1, k), w)


def _res_matmul_kernel(*refs, n_parts):
    x_ref = refs[0]
    a_refs = refs[1:1 + n_parts]
    w_refs = refs[1 + n_parts:1 + 2 * n_parts]
    o_ref = refs[1 + 2 * n_parts]
    acc = x_ref[...]
    for a_ref, w_ref in zip(a_refs, w_refs):
        acc = acc + _dot(a_ref[...].astype(BF16), w_ref[...])
    o_ref[...] = acc


def res_matmul(x, parts, weights, *, tm, tn):
    m, n = x.shape
    n_parts = len(parts)
    in_specs = [pl.BlockSpec((tm, tn), lambda i, j: (i, j))]
    for a in parts:
        in_specs.append(pl.BlockSpec((tm, a.shape[1]), lambda i, j: (i, 0)))
    for w in weights:
        in_specs.append(pl.BlockSpec((w.shape[0], tn), lambda i, j: (0, j)))
    return pl.pallas_call(
        functools.partial(_res_matmul_kernel, n_parts=n_parts),
        out_shape=jax.ShapeDtypeStruct((m, n), F32),
        grid=(m // tm, n // tn),
        in_specs=in_specs,
        out_specs=pl.BlockSpec((tm, tn), lambda i, j: (i, j)),
        compiler_params=_cparams(("parallel", "parallel")),
        name="res_matmul",
    )(x, *parts, *weights)


def _mlp_kernel(x_ref, g_ref, wu_ref, wd_ref, gout_ref, o_ref, h_ref, *, norm_out):
    f = pl.program_id(1)

    @pl.when(f == 0)
    def _():
        _norm_into(h_ref, x_ref, g_ref)
        o_ref[...] = x_ref[...]

    a = _dot(h_ref[...], wu_ref[...])
    a = jnp.square(jnp.maximum(a, 0.0)).astype(BF16)
    o_ref[...] += _dot(a, wd_ref[...])

    if norm_out:
        @pl.when(f == pl.num_programs(1) - 1)
        def _():
            _norm_into(o_ref, o_ref, gout_ref)


def mlp(x, g, wu, wd, g_out, *, tm, tf, norm_out):
    m, d = x.shape
    ff = wu.shape[1]
    return pl.pallas_call(
        functools.partial(_mlp_kernel, norm_out=norm_out),
        out_shape=jax.ShapeDtypeStruct((m, d), F32),
        grid=(m // tm, ff // tf),
        in_specs=[
            pl.BlockSpec((tm, d), lambda i, f: (i, 0)),
            pl.BlockSpec((1, d), lambda i, f: (0, 0)),
            pl.BlockSpec((d, tf), lambda i, f: (0, f)),
            pl.BlockSpec((tf, d), lambda i, f: (f, 0)),
            pl.BlockSpec((1, d), lambda i, f: (0, 0)),
        ],
        out_specs=pl.BlockSpec((tm, d), lambda i, f: (i, 0)),
        scratch_shapes=[pltpu.VMEM((tm, d), BF16)],
        compiler_params=_cparams(("parallel", "arbitrary")),
        name="mlp",
    )(x, g.reshape(1, d), wu, wd, g_out.reshape(1, d))


def _rmsnorm_kernel(x_ref, g_ref, o_ref):
    o_ref[...] = _rms_rows(x_ref[...], g_ref[...])


def rmsnorm(x, g, *, tm):
    m, d = x.shape
    return pl.pallas_call(
        _rmsnorm_kernel,
        out_shape=jax.ShapeDtypeStruct((m, d), F32),
        grid=(m // tm,),
        in_specs=[pl.BlockSpec((tm, d), lambda i: (i, 0)),
                  pl.BlockSpec((1, d), lambda i: (0, 0))],
        out_specs=pl.BlockSpec((tm, d), lambda i: (i, 0)),
        compiler_params=_cparams(("parallel",)),
        name="final_rmsnorm",
    )(x, g.reshape(1, d))


def _gmlp_kernel(z_ref, lng_ref, lnb_ref, ws_ref, bst_ref, o_ref, *, n_chunks):
    row = lax.broadcasted_iota(jnp.int32, (CHUNK, CHUNK), 0)
    col = lax.broadcasted_iota(jnp.int32, (CHUNK, CHUNK), 1)
    causal = row >= col
    w_tril = [jnp.where(causal, ws_ref[gi], 0.0).astype(BF16) for gi in range(A_GROUPS)]
    for c in range(n_chunks):
        rows = slice(c * CHUNK, (c + 1) * CHUNK)
        u = _gelu_tanh(z_ref[rows, 0:A_WIDTH])
        v = _gelu_tanh(z_ref[rows, A_WIDTH:2 * A_WIDTH])
        vc = v - jnp.mean(v, axis=-1, keepdims=True)
        vn = vc * lax.rsqrt(jnp.mean(vc * vc, axis=-1, keepdims=True) + EPS)
        vn = vn * lng_ref[...] + lnb_ref[...]
        for gi in range(A_GROUPS):
            cols = slice(gi * HEAD_DIM, (gi + 1) * HEAD_DIM)
            sv = _dot(w_tril[gi], vn[:, cols].astype(BF16)) + bst_ref[:, gi:gi + 1]
            o_ref[rows, cols] = (u[:, cols] * sv).astype(o_ref.dtype)


def gmlp(z, ln_g, ln_b, w_s, b_s, *, tm):
    t = z.shape[0]
    return pl.pallas_call(
        functools.partial(_gmlp_kernel, n_chunks=tm // CHUNK),
        out_shape=jax.ShapeDtypeStruct((t, A_WIDTH), BF16),
        grid=(t // tm,),
        in_specs=[
            pl.BlockSpec((tm, 2 * A_WIDTH), lambda i: (i, 0)),
            pl.BlockSpec((1, A_WIDTH), lambda i: (0, 0)),
            pl.BlockSpec((1, A_WIDTH), lambda i: (0, 0)),
            pl.BlockSpec((A_GROUPS, CHUNK, CHUNK), lambda i: (0, 0, 0)),
            pl.BlockSpec((CHUNK, A_GROUPS), lambda i: (0, 0)),
        ],
        out_specs=pl.BlockSpec((tm, A_WIDTH), lambda i: (i, 0)),
        compiler_params=_cparams(("parallel",)),
        name="gmlp_gating",
    )(z, ln_g.reshape(1, A_WIDTH), ln_b.reshape(1, A_WIDTH), w_s, b_s.T)


def _gate_column(gate_ref, col):
    g = 1.0 / (1.0 + jnp.exp(-gate_ref[...]))
    lane = lax.broadcasted_iota(jnp.int32, g.shape, 1)
    return jnp.sum(jnp.where(lane == col, g, 0.0), axis=-1, keepdims=True)


def _flash_kernel(q_ref, k_ref, v_ref, sel_ref, gate_ref, o_ref, qs_ref, kb_ref, vb_ref, m_ref, accl_ref, *,
                  n_rep, mode, max_dist, branch):

    tq, tk, ch = ATT_TQ, ATT_TK, ATT_CH
    seq = k_ref.shape[0]
    i = pl.program_id(2)
    select = mode == "select"

    @pl.when(i == 0)
    def _():
        def prep(c, carry):
            r0 = pl.multiple_of(c * tk, tk)
            kb_ref[pl.ds(r0, tk), 0:HEAD_DIM] = k_ref[pl.ds(r0, tk), :].astype(BF16)
            if select:
                blk = (r0 + lax.broadcasted_iota(jnp.int32, (tk, LANE), 0)) >> 6
                lane = lax.broadcasted_iota(jnp.int32, (tk, LANE), 1)
                kb_ref[pl.ds(r0, tk), HEAD_DIM:HEAD_DIM + LANE] = jnp.where(blk == lane, 1.0, 0.0).astype(BF16)
            vb_ref[pl.ds(r0, tk), 0:HEAD_DIM] = v_ref[pl.ds(r0, tk), :].astype(BF16)
            vb_ref[pl.ds(r0, tk), HEAD_DIM:2 * HEAD_DIM] = jnp.ones((tk, HEAD_DIM), BF16)
            return carry

        lax.fori_loop(0, seq // tk, prep, 0)

    qscale = (HEAD_DIM ** -0.5) * LOG2E
    if select:
        pen = jnp.where(sel_ref[...] > 0.5, 0.0, NEG_INF).astype(BF16)
    for r in range(n_rep):
        rows = slice(r * tq, (r + 1) * tq)
        qs_ref[rows, 0:HEAD_DIM] = (q_ref[:, r * HEAD_DIM:(r + 1) * HEAD_DIM] * qscale).astype(BF16)
        if select:
            qs_ref[rows, HEAD_DIM:HEAD_DIM + LANE] = pen
    m_ref[...] = jnp.full(m_ref.shape, NEG_INF, F32)
    accl_ref[...] = jnp.zeros(accl_ref.shape, F32)

    row = lax.broadcasted_iota(jnp.int32, (tq, tk), 0)
    col = lax.broadcasted_iota(jnp.int32, (tq, tk), 1)

    def tile_bias(off):
        dist = off * tk + row - col
        if off == 0:
            ok = dist >= 0
            if not select and max_dist < tk - 1:
                ok = ok & (dist <= max_dist)
        elif not select and (off + 1) * tk - 1 > max_dist:
            ok = dist <= max_dist
        else:
            return None
        return jnp.where(ok, 0.0, NEG_INF)

    n_ch = n_rep * tq // ch

    def steps(tiles):
        scores = []
        for kt, bias in tiles:
            k0 = pl.multiple_of(kt * tk, tk)
            scores.append((_dot_nt(qs_ref[...], kb_ref[pl.ds(k0, tk), :]), k0, bias))
        for s_all, k0, bias in scores:
            ps, alphas = [], []
            for c in range(n_ch):
                rows = slice(c * ch, (c + 1) * ch)
                s = s_all[rows, :]
                if bias is not None:
                    b0 = (c % (tq // ch)) * ch
                    s = s + bias[b0:b0 + ch, :]
                m_old = m_ref[rows, :]
                m_new = jnp.maximum(m_old, jnp.max(s, axis=-1, keepdims=True))
                alphas.append(jnp.exp2(m_old - m_new))
                ps.append(jnp.exp2(s - jnp.concatenate([m_new] * (tk // LANE), axis=1)).astype(BF16))
                m_ref[rows, :] = m_new
            pv_all = _dot(jnp.concatenate(ps, axis=0), vb_ref[pl.ds(k0, tk), :])
            for c in range(n_ch):
                rows = slice(c * ch, (c + 1) * ch)
                alpha2 = jnp.concatenate([alphas[c], alphas[c]], axis=1)
                accl_ref[rows, :] = alpha2 * accl_ref[rows, :] + pv_all[rows, :]

    if select:
        def body(j, carry):
            steps([(2 * j, None), (2 * j + 1, None)])
            return carry

        lax.fori_loop(0, i >> 1, body, 0)

        @pl.when((i & 1) == 1)
        def _():
            steps([(i - 1, None), (i, tile_bias(0))])

        @pl.when((i & 1) == 0)
        def _():
            steps([(i, tile_bias(0))])
    else:
        n_prev = -(-max_dist // tk)

        @pl.when(i >= n_prev)
        def _():
            steps([(i - off, tile_bias(off)) for off in range(n_prev, -1, -1)])

        for first in range(n_prev):
            @pl.when(i == first)
            def _(first=first):
                steps([(first - off, tile_bias(off)) for off in range(first, -1, -1)])

    head0 = pl.program_id(1) * n_rep
    for r in range(n_rep):
        rows = slice(r * tq, (r + 1) * tq)
        cols = slice(r * HEAD_DIM, (r + 1) * HEAD_DIM)
        gate = _gate_column(gate_ref, (head0 + r) * 3 + branch)
        o = accl_ref[rows, 0:HEAD_DIM] / accl_ref[rows, HEAD_DIM:2 * HEAD_DIM]
        o_ref[:, cols] = (gate * o).astype(o_ref.dtype)


def _flash_band_kernel(q_ref, k_ref, v_ref, gate_ref, o_ref, *scratch, **static):
    _flash_kernel(q_ref, k_ref, v_ref, None, gate_ref, o_ref, *scratch, **static)


def flash_attention(z, sel_arr, *, n_batch, n_grp, n_rep, seq, q_col, k_col, v_col, mode, branch, max_dist=0):
    tq = ATT_TQ
    nq = seq // tq
    qw = n_rep * HEAD_DIM
    kw = HEAD_DIM + (LANE if mode == "select" else 0)
    in_specs = [
        pl.BlockSpec((tq, qw), lambda b, n, i: (b * nq + i, q_col(n))),
        pl.BlockSpec((seq, HEAD_DIM), lambda b, n, i: (b, k_col(n))),
        pl.BlockSpec((seq, HEAD_DIM), lambda b, n, i: (b, v_col(n))),
    ]
    args = [z, z, z]
    if mode == "select":
        in_specs.append(pl.BlockSpec((tq, LANE), lambda b, n, i: ((b * n_grp + n) * nq + i, 0)))
        args.append(sel_arr)
    in_specs.append(pl.BlockSpec((tq, LANE), lambda b, n, i: (b * nq + i, COL_GATE // LANE)))
    args.append(z)
    body = _flash_kernel if mode == "select" else _flash_band_kernel
    return pl.pallas_call(
        functools.partial(body, n_rep=n_rep, mode=mode, max_dist=max_dist, branch=branch),
        out_shape=jax.ShapeDtypeStruct((n_batch * seq, n_grp * qw), F32),
        grid=(n_batch, n_grp, nq),
        in_specs=in_specs,
        out_specs=pl.BlockSpec((tq, qw), lambda b, n, i: (b * nq + i, n)),
        scratch_shapes=[
            pltpu.VMEM((n_rep * tq, kw), BF16),
            pltpu.VMEM((seq, kw), BF16),
            pltpu.VMEM((seq, 2 * HEAD_DIM), BF16),
            pltpu.VMEM((n_rep * tq, LANE), F32),
            pltpu.VMEM((n_rep * tq, 2 * HEAD_DIM), F32),
        ],
        compiler_params=_cparams(("parallel", "parallel", "arbitrary")),
        name="flash_" + mode,
    )(*args)


DIL_CH = 128
DIL_KW = 2 * DIL_CH
DIL_UNITS = 8


def _dilated_kernel(q_ref, k_ref, v_ref, o_ref, m_ref, acc_ref, l_ref):
    ch, kw = DIL_CH, DIL_KW
    seq = q_ref.shape[0]
    qscale = (HEAD_DIM ** -0.5) * LOG2E
    row = lax.broadcasted_iota(jnp.int32, (ch, kw), 0)
    col = lax.broadcasted_iota(jnp.int32, (ch, kw), 1)
    ones = jnp.ones((kw, HEAD_DIM), BF16)
    patterns = sorted(DILATED_PAIRS, key=lambda wd: -wd[1])

    def rows(start, n, dil):
        return pl.ds(start, n) if dil == 1 else pl.ds(start, n, stride=dil)

    def band_bias(first, span):
        dist = row - col if first else ch + row - col
        return jnp.where((dist >= 0) & (dist <= span), 0.0, NEG_INF)

    def run_units(units, dil, span, phase):
        windows = {}
        scores = []
        for q0, k0, first, wid in units:
            if wid not in windows:
                k = k_ref[rows(k0, kw, dil), :].astype(BF16)
                v = v_ref[rows(k0, kw, dil), :].astype(BF16)
                windows[wid] = (k, jnp.concatenate([v, ones], axis=1))
            q = (q_ref[rows(q0, ch, dil), :] * qscale).astype(BF16)
            scores.append(_dot_nt(q, windows[wid][0]) + band_bias(first, span))
        for (q0, k0, first, wid), s in zip(units, scores):
            qrows = rows(q0, ch, dil)
            m_cur = jnp.max(s, axis=-1, keepdims=True)
            if phase == "first":
                m_new = jnp.broadcast_to(m_cur, (ch, LANE))
            else:
                m_old = m_ref[qrows, :]
                m_new = jnp.maximum(m_old, m_cur)
                alpha = jnp.exp2(m_old - m_new)
            p = jnp.exp2(s - jnp.concatenate([m_new] * (kw // LANE), axis=1)).astype(BF16)
            pv = _dot(p, windows[wid][1])
            acc, l = pv[:, 0:HEAD_DIM], pv[:, HEAD_DIM:2 * HEAD_DIM]
            if phase != "first":
                acc = alpha * acc_ref[qrows, :] + acc
                l = alpha * l_ref[qrows, :] + l
            if phase == "last":
                o_ref[pl.ds(pl.multiple_of(q0, ch), ch), :] = (acc / l).astype(o_ref.dtype)
            else:
                m_ref[qrows, :] = m_new
                acc_ref[qrows, :] = acc
                l_ref[qrows, :] = l

    for idx, (window, dil) in enumerate(patterns):
        phase = "first" if idx == 0 else ("last" if idx == len(patterns) - 1 else "mid")
        assert phase != "last" or dil == 1
        span = window // dil
        assert span <= ch
        n_units = seq // dil // ch
        per_class = min(n_units, max(DIL_UNITS // dil, 2))
        n_classes = DIL_UNITS // per_class
        assert n_units % per_class == 0 and dil % n_classes == 0

        def group(r0, j0, head, dil=dil, span=span, phase=phase, per_class=per_class, n_classes=n_classes):
            units = []
            for rc in range(n_classes):
                r = r0 + rc
                for u in range(per_class):
                    first = head and u == 0
                    q0 = r + dil * ch * (j0 + u)
                    k0 = r if (head and u <= 1) else r + dil * ch * (j0 + u - 1)
                    wid = (rc, 0) if (head and u <= 1) else (rc, u)
                    units.append((q0, k0, first, wid))
            run_units(units, dil, span, phase)

        def class_loop(g, carry, group=group, per_class=per_class, n_classes=n_classes, n_units=n_units):
            r0 = g * n_classes
            group(r0, 0, True)
            if n_units > per_class:
                def tail(jb, c):
                    group(r0, jb * per_class, False)
                    return c
                lax.fori_loop(1, n_units // per_class, tail, 0)
            return carry

        if dil // n_classes == 1:
            class_loop(0, 0)
        else:
            lax.fori_loop(0, dil // n_classes, class_loop, 0)


def dilated_attention(z, *, n_batch, seq):
    bw = B_HEADS * HEAD_DIM
    base = COL_B // HEAD_DIM
    blk = (seq, HEAD_DIM)
    return pl.pallas_call(
        _dilated_kernel,
        out_shape=jax.ShapeDtypeStruct((n_batch * seq, bw), BF16),
        grid=(n_batch, B_HEADS),
        in_specs=[
            pl.BlockSpec(blk, lambda b, h: (b, base + h)),
            pl.BlockSpec(blk, lambda b, h: (b, base + B_HEADS + h)),
            pl.BlockSpec(blk, lambda b, h: (b, base + 2 * B_HEADS + h)),
        ],
        out_specs=pl.BlockSpec(blk, lambda b, h: (b, h)),
        scratch_shapes=[pltpu.VMEM(blk, F32)] * 3,
        compiler_params=_cparams(("parallel", "parallel")),
        name="dilated_attention",
    )(z, z, z)


N_SUB = 256
SUB_W = CMP_STRIDE * HEAD_DIM


def _compress_kernel(x_ref, pos_ref, w1_ref, w2_ref, o_ref):
    p = jnp.zeros((N_SUB, HEAD_DIM), F32)
    q = jnp.zeros((N_SUB, HEAD_DIM), F32)
    for i in range(CMP_STRIDE):
        a = x_ref[pl.ds(i, N_SUB, stride=CMP_STRIDE), :]
        top = (a + pos_ref[i:i + 1, :]).astype(BF16)
        bot = (a + pos_ref[CMP_STRIDE + i:CMP_STRIDE + i + 1, :]).astype(BF16)
        p = p + _dot(top, w1_ref[i * HEAD_DIM:(i + 1) * HEAD_DIM, :])
        q = q + _dot(bot, w1_ref[(CMP_STRIDE + i) * HEAD_DIM:(CMP_STRIDE + i + 1) * HEAD_DIM, :])
    h = _gelu_tanh(p + pltpu.roll(q, N_SUB - 1, 0))
    o_ref[...] = _dot(h.astype(BF16), w2_ref[...]).astype(o_ref.dtype)


def compress(z, pos, w1, w2, *, n_batch, seq):
    return pl.pallas_call(
        _compress_kernel,
        out_shape=jax.ShapeDtypeStruct((n_batch * 4 * N_SUB, HEAD_DIM), BF16),
        grid=(n_batch, 4),
        in_specs=[
            pl.BlockSpec((seq, HEAD_DIM), lambda b, n: (b, COL_CMP // HEAD_DIM + n)),
            pl.BlockSpec((CMP_LEN, HEAD_DIM), lambda b, n: (0, 0)),
            pl.BlockSpec((None, CMP_LEN * HEAD_DIM, HEAD_DIM), lambda b, n: (n // 2, 0, 0)),
            pl.BlockSpec((None, HEAD_DIM, HEAD_DIM), lambda b, n: (n // 2, 0, 0)),
        ],
        out_specs=pl.BlockSpec((N_SUB, HEAD_DIM), lambda b, n: (b * 4 + n, 0)),
        compiler_params=_cparams(("parallel", "parallel")),
        name="nsa_compress",
    )(z, pos, w1, w2)


N_SEL = 64


def _csel_kernel(q_ref, kc_ref, vc_ref, ovt_ref, gate_ref, oc_ref, sel_ref, qs_ref, sc_ref):
    tq, ch = ATT_TQ, ATT_CH
    n_sub = tq // ch
    q0 = pl.program_id(2) * tq
    qscale = (HEAD_DIM ** -0.5) * LOG2E
    for r in range(C_GROUP):
        qs_ref[r * tq:(r + 1) * tq, :] = (q_ref[:, r * HEAD_DIM:(r + 1) * HEAD_DIM] * qscale).astype(BF16)
    s_all = _dot_nt(qs_ref[...], kc_ref[...])
    n_cols = lax.broadcasted_iota(jnp.int32, (ch, N_SUB), 1)
    bias, has_any = [], []
    for c in range(n_sub):
        t = q0 + c * ch + lax.broadcasted_iota(jnp.int32, (ch, N_SUB), 0)
        bias.append(jnp.where((n_cols * CMP_STRIDE + (CMP_LEN - 1)) <= t, 0.0, NEG_INF))
        has_any.append(t[:, 0:1] >= CMP_LEN - 1)
    ps = []
    psum = [jnp.zeros((ch, N_SUB), F32) for _ in range(n_sub)]
    for c in range(C_GROUP * n_sub):
        s = s_all[c * ch:(c + 1) * ch, :] + bias[c % n_sub]
        e = jnp.exp2(s - jnp.max(s, axis=-1, keepdims=True))
        den = jnp.sum(e, axis=-1, keepdims=True)
        p = e * jnp.where(has_any[c % n_sub], 1.0 / den, 0.0)
        ps.append(p.astype(BF16))
        psum[c % n_sub] = psum[c % n_sub] + p
    o_all = _dot(jnp.concatenate(ps, axis=0), vc_ref[...])
    head0 = pl.program_id(1) * C_GROUP
    for r in range(C_GROUP):
        gate = _gate_column(gate_ref, (head0 + r) * 3)
        oc_ref[:, r * HEAD_DIM:(r + 1) * HEAD_DIM] = gate * o_all[r * tq:(r + 1) * tq, :]
    psum = jnp.concatenate(psum, axis=0)
    hi = psum.astype(BF16)
    lo = (psum - hi.astype(F32)).astype(BF16)
    ovt = ovt_ref[...]
    imp_t = _dot_nt(ovt, hi) + _dot_nt(ovt, lo)
    jrow = lax.broadcasted_iota(jnp.int32, (LANE, tq), 0)
    tl = q0 + lax.broadcasted_iota(jnp.int32, (LANE, tq), 1)
    jt = tl >> 6
    forced = (jrow == 0) | (jrow == jt) | (jrow == jt - 1)
    valid_s = jrow * SEL_LEN <= tl
    score = jnp.where(forced, 1e4, jnp.where(valid_s, imp_t, -1.0))
    sc_ref[...] = score
    n_grp = N_SEL // SUBLANE
    grp = [sc_ref[g * SUBLANE:(g + 1) * SUBLANE, :] for g in range(n_grp)]
    cnt = [jnp.zeros((SUBLANE, tq), F32) for _ in range(n_grp)]
    sub = lax.broadcasted_iota(jnp.int32, (SUBLANE, tq), 0)
    for j in range(N_SEL):
        row = sc_ref[j:j + 1, :]
        for g in range(n_grp):
            if g < j // SUBLANE:
                beats = row > grp[g]
            elif g > j // SUBLANE:
                beats = row >= grp[g]
            else:
                beats = (row > grp[g]) | ((row == grp[g]) & (sub > j % SUBLANE))
            cnt[g] = cnt[g] + jnp.where(beats, 1.0, 0.0)
    sel_t = [jnp.where(c < float(SEL_TOP), 1.0, 0.0) for c in cnt]
    sel_pad = jnp.concatenate(sel_t + [jnp.zeros((LANE - N_SEL, tq), F32)], axis=0)
    sel_ref[...] = sel_pad.T


def compressed_select(z, kvc, ovt, *, n_batch, seq):
    tq = ATT_TQ
    nq = seq // tq
    qw = C_GROUP * HEAD_DIM
    return pl.pallas_call(
        _csel_kernel,
        out_shape=[jax.ShapeDtypeStruct((n_batch * seq, C_HEADS * HEAD_DIM), F32),
                   jax.ShapeDtypeStruct((n_batch * C_KV_HEADS * seq, LANE), F32)],
        grid=(n_batch, C_KV_HEADS, nq),
        in_specs=[
            pl.BlockSpec((tq, qw), lambda b, g, i: (b * nq + i, COL_CQ // qw + g)),
            pl.BlockSpec((N_SUB, HEAD_DIM), lambda b, g, i: (b * 4 + g, 0)),
            pl.BlockSpec((N_SUB, HEAD_DIM), lambda b, g, i: (b * 4 + 2 + g, 0)),
            pl.BlockSpec((LANE, N_SUB), lambda b, g, i: (0, 0)),
            pl.BlockSpec((tq, LANE), lambda b, g, i: (b * nq + i, COL_GATE // LANE)),
        ],
        out_specs=[
            pl.BlockSpec((tq, qw), lambda b, g, i: (b * nq + i, g)),
            pl.BlockSpec((tq, LANE), lambda b, g, i: ((b * C_KV_HEADS + g) * nq + i, 0)),
        ],
        scratch_shapes=[pltpu.VMEM((C_GROUP * tq, HEAD_DIM), BF16), pltpu.VMEM((LANE, tq), F32)],
        compiler_params=_cparams(("parallel", "parallel", "parallel")),
        name="nsa_compressed_select",
    )(z, kvc, kvc, ovt, z)


def _overlap_t():
    n_c = N_SUB - 1
    c_start = np.arange(n_c) * CMP_STRIDE
    s_start = np.arange(N_SEL) * SEL_LEN
    ov = ((c_start[:, None] <= s_start[None, :] + SEL_LEN - 1)
          & (c_start[:, None] + CMP_LEN - 1 >= s_start[None, :])).astype(np.float32)
    out = np.zeros((LANE, N_SUB), np.float32)
    out[:N_SEL, :n_c] = ov.T
    return out


def _mix_out_kernel(x_ref, oa_ref, ob_ref, oc_ref, os_ref, ow_ref, w_ref, o_ref):
    mix = jnp.concatenate([oa_ref[...], ob_ref[...], (oc_ref[...] + os_ref[...] + ow_ref[...]).astype(BF16)],
                          axis=1)
    o_ref[...] = x_ref[...] + _dot(mix, w_ref[...])


def mix_out(x, out_a, out_b, oc, os_, ow, w, *, tm):
    m, n = x.shape
    bw = B_HEADS * HEAD_DIM
    cw = C_HEADS * HEAD_DIM
    spec_c = pl.BlockSpec((tm, cw), lambda i: (i, 0))
    return pl.pallas_call(
        _mix_out_kernel,
        out_shape=jax.ShapeDtypeStruct((m, n), F32),
        grid=(m // tm,),
        in_specs=[
            pl.BlockSpec((tm, n), lambda i: (i, 0)),
            pl.BlockSpec((tm, A_WIDTH), lambda i: (i, 0)),
            pl.BlockSpec((tm, bw), lambda i: (i, 0)),
            spec_c, spec_c, spec_c,
            pl.BlockSpec(w.shape, lambda i: (0, 0), pipeline_mode=pl.Buffered(1)),
        ],
        out_specs=pl.BlockSpec((tm, n), lambda i: (i, 0)),
        compiler_params=_cparams(("parallel",)),
        name="mix_out",
    )(x, out_a, out_b, oc, os_, ow, w)


def _xattn_kernel(x_ref, g_ref, wq_ref, k_ref, v_ref, wo_ref, o_ref, h_ref, a_ref):
    n_mem = k_ref.shape[0]
    _norm_into(h_ref, x_ref, g_ref)
    q = _dot(h_ref[...], wq_ref[...]) * ((HEAD_DIM ** -0.5) * LOG2E)
    ones = jnp.ones((n_mem, HEAD_DIM), BF16)
    for h in range(X_HEADS):
        cols = slice(h * HEAD_DIM, (h + 1) * HEAD_DIM)
        s = _dot_nt(q[:, cols].astype(BF16), k_ref[:, cols].astype(BF16))
        e = jnp.exp2(s - jnp.max(s, axis=-1, keepdims=True))
        pv = _dot(e.astype(BF16), jnp.concatenate([v_ref[:, cols].astype(BF16), ones], axis=1))
        a_ref[:, cols] = (pv[:, 0:HEAD_DIM] / pv[:, HEAD_DIM:2 * HEAD_DIM]).astype(BF16)
    o_ref[...] = x_ref[...] + _dot(a_ref[...], wo_ref[...])


def cross_attention_block(x, g, wq, kv, wo, *, n_batch, seq, n_mem, tm):
    nq = seq // tm
    d = x.shape[1]
    return pl.pallas_call(
        _xattn_kernel,
        out_shape=jax.ShapeDtypeStruct(x.shape, F32),
        grid=(n_batch, nq),
        in_specs=[
            pl.BlockSpec((tm, d), lambda b, i: (b * nq + i, 0)),
            pl.BlockSpec((1, d), lambda b, i: (0, 0)),
            pl.BlockSpec((d, X_WIDTH), lambda b, i: (0, 0)),
            pl.BlockSpec((n_mem, X_WIDTH), lambda b, i: (b, 0)),
            pl.BlockSpec((n_mem, X_WIDTH), lambda b, i: (b, 1)),
            pl.BlockSpec((X_WIDTH, d), lambda b, i: (0, 0)),
        ],
        out_specs=pl.BlockSpec((tm, d), lambda b, i: (b * nq + i, 0)),
        scratch_shapes=[pltpu.VMEM((tm, d), BF16), pltpu.VMEM((tm, X_WIDTH), BF16)],
        compiler_params=_cparams(("parallel", "parallel")),
        name="cross_attention",
    )(x, g.reshape(1, d), wq, kv, kv, wo)


def kernel(x, mem, norm_mix, w_in, gmlp_ln_g, gmlp_ln_b, gmlp_w_s, gmlp_b_s, cmp_pos, cmp_k_w1, cmp_k_w2,
           cmp_v_w1, cmp_v_w2, w_out, norm_xattn, norm_mem, xattn_wq, xattn_wkv, xattn_wo, norm_mlp, w_up,
           w_down, final_norm):
    bsz, seq, d = x.shape
    n_mem = mem.shape[1]
    t = bsz * seq
    assert d == D_MODEL and seq // CMP_STRIDE == N_SUB and seq // SEL_LEN == N_SEL
    assert seq % (ATT_TQ * 16) == 0

    def bf16(w):
        return w.astype(BF16)

    ovt = jnp.asarray(_overlap_t()).astype(BF16)
    xf = x.reshape(t, d)
    memf = mem.reshape(bsz * n_mem, d)
    gw = C_GROUP * HEAD_DIM

    for l in range(DEPTH):
        w_in_l = bf16(jnp.pad(w_in[l], ((0, 0), (0, IN_PAD - IN_WIDTH))))
        z = norm_matmul(xf, norm_mix[l], w_in_l, tm=1024, tn=IN_TILE)

        out_a = gmlp(z, gmlp_ln_g[l], gmlp_ln_b[l], gmlp_w_s[l], gmlp_b_s[l], tm=512)

        out_b = dilated_attention(z, n_batch=bsz, seq=seq)

        cmp_w1 = bf16(jnp.stack([cmp_k_w1[l], cmp_v_w1[l]]))
        cmp_w2 = bf16(jnp.stack([cmp_k_w2[l], cmp_v_w2[l]]))
        kvc = compress(z, cmp_pos[l], cmp_w1, cmp_w2, n_batch=bsz, seq=seq)
        o_c, sel = compressed_select(z, kvc, ovt, n_batch=bsz, seq=seq)
        o_s = flash_attention(
            z, sel, n_batch=bsz, n_grp=C_KV_HEADS, n_rep=C_GROUP, seq=seq,
            q_col=lambda n: COL_CQ // gw + n,
            k_col=lambda n: COL_SEL // HEAD_DIM + n,
            v_col=lambda n: COL_SEL // HEAD_DIM + C_KV_HEADS + n,
            mode="select", branch=1)
        o_w = flash_attention(
            z, None, n_batch=bsz, n_grp=C_KV_HEADS, n_rep=C_GROUP, seq=seq,
            q_col=lambda n: COL_CQ // gw + n,
            k_col=lambda n: COL_WIN // HEAD_DIM + n,
            v_col=lambda n: COL_WIN // HEAD_DIM + C_KV_HEADS + n,
            mode="band", branch=2, max_dist=WIN_LEN - 1)

        xf = mix_out(xf, out_a, out_b, o_c, o_s, o_w, bf16(w_out[l]), tm=512)

        kv = norm_matmul(memf, norm_mem[l], bf16(xattn_wkv[l]), tm=512, tn=2 * X_WIDTH)
        xf = cross_attention_block(xf, norm_xattn[l], bf16(xattn_wq[l]), kv, bf16(xattn_wo[l]), n_batch=bsz,
                                   seq=seq, n_mem=n_mem, tm=512)

        xf = mlp(xf, norm_mlp[l], bf16(w_up[l]), bf16(w_down[l]), final_norm, tm=1024, tf=512,
                 norm_out=(l == DEPTH - 1))

    return xf.reshape(bsz, seq, d)
```

```python
import functools
import math

import numpy as np
import jax
import jax.numpy as jnp
from jax import lax
from jax.experimental import pallas as pl
from jax.experimental.pallas import tpu as pltpu

F32 = jnp.float32
BF16 = jnp.bfloat16

D_MODEL = 2048
DEPTH = 4
HEAD_DIM = 128
A_GROUPS = 4
A_WIDTH = A_GROUPS * HEAD_DIM
CHUNK = 128
B_HEADS = 4
DILATED_PAIRS = ((128, 1), (512, 4), (2048, 16))
C_HEADS = 8
C_KV_HEADS = 2
C_GROUP = C_HEADS // C_KV_HEADS
CMP_LEN = 32
CMP_STRIDE = 16
SEL_LEN = 64
SEL_TOP = 16
WIN_LEN = 512
X_HEADS = 4
X_WIDTH = X_HEADS * HEAD_DIM
EPS = 1e-6
NEG_INF = -1e30

COL_A = 0
COL_B = 2 * A_WIDTH
COL_CQ = COL_B + 3 * B_HEADS * HEAD_DIM
COL_CMP = COL_CQ + C_HEADS * HEAD_DIM
COL_SEL = COL_CMP + 2 * C_KV_HEADS * HEAD_DIM
COL_WIN = COL_SEL + 2 * C_KV_HEADS * HEAD_DIM
COL_GATE = COL_WIN + 2 * C_KV_HEADS * HEAD_DIM
IN_WIDTH = COL_GATE + 3 * C_HEADS
IN_TILE = 768
IN_PAD = 5376

LANE = 128
SUBLANE = 8
VMEM_LIMIT = 56 * 1024 * 1024

ATT_TQ = 256
ATT_TK = 256
ATT_CH = 128
SEL_UNROLL = 4
LOG2E = math.log2(math.e)
LN2 = math.log(2.0)


def _cparams(sem):
    return pltpu.CompilerParams(dimension_semantics=sem, vmem_limit_bytes=VMEM_LIMIT)


def _rms_rows(x, g):
    ms = jnp.mean(x * x, axis=-1, keepdims=True)
    return x * lax.rsqrt(ms + EPS) * g


def _gelu_tanh(x):
    c = math.sqrt(2.0 / math.pi)
    return x * (0.5 * (1.0 + jnp.tanh(c * (x + 0.044715 * (x * x * x)))))


def _dot(a, b):
    return jnp.dot(a, b, preferred_element_type=F32)


def _dot_nt(a, b):
    return lax.dot_general(a, b, (((1,), (1,)), ((), ())), preferred_element_type=F32)


NORM_ROWS = 256


def _norm_into(h_ref, x_ref, g_ref):
    rows = x_ref.shape[0]

    def body(c, carry):
        r0 = pl.multiple_of(c * NORM_ROWS, NORM_ROWS)
        x = x_ref[pl.ds(r0, NORM_ROWS), :]
        h_ref[pl.ds(r0, NORM_ROWS), :] = _rms_rows(x, g_ref[...]).astype(h_ref.dtype)
        return carry

    lax.fori_loop(0, rows // NORM_ROWS, body, 0)


def _norm_matmul_kernel(x_ref, g_ref, w_ref, o_ref, h_ref):
    @pl.when(pl.program_id(1) == 0)
    def _():
        _norm_into(h_ref, x_ref, g_ref)

    o_ref[...] = _dot(h_ref[...], w_ref[...]).astype(o_ref.dtype)


def norm_matmul(x, g, w, layer, *, tm, tn, out_dtype=F32):
    m, k = x.shape
    n = w.shape[2]
    return pl.pallas_call(
        _norm_matmul_kernel,
        out_shape=jax.ShapeDtypeStruct((m, n), out_dtype),
        grid=(m // tm, n // tn),
        in_specs=[
            pl.BlockSpec((tm, k), lambda i, j: (i, 0)),
            pl.BlockSpec((1, k), lambda i, j: (0, 0)),
            pl.BlockSpec((None, k, tn), lambda i, j: (layer, 0, j)),
        ],
        out_specs=pl.BlockSpec((tm, tn), lambda i, j: (i, j)),
        scratch_shapes=[pltpu.VMEM((tm, k), BF16)],
        compiler_params=_cparams(("parallel", "arbitrary")),
        name="norm_matmul",
    )(x, g.reshape(1, k), w)


def _res_matmul_kernel(*refs, n_parts):
    x_ref = refs[0]
    a_refs = refs[1:1 + n_parts]
    w_refs = refs[1 + n_parts:1 + 2 * n_parts]
    o_ref = refs[1 + 2 * n_parts]
    acc = x_ref[...]
    for a_ref, w_ref in zip(a_refs, w_refs):
        acc = acc + _dot(a_ref[...].astype(BF16), w_ref[...])
    o_ref[...] = acc


def res_matmul(x, parts, weights, *, tm, tn):
    m, n = x.shape
    n_parts = len(parts)
    in_specs = [pl.BlockSpec((tm, tn), lambda i, j: (i, j))]
    for a in parts:
        in_specs.append(pl.BlockSpec((tm, a.shape[1]), lambda i, j: (i, 0)))
    for w in weights:
        in_specs.append(pl.BlockSpec((w.shape[0], tn), lambda i, j: (0, j)))
    return pl.pallas_call(
        functools.partial(_res_matmul_kernel, n_parts=n_parts),
        out_shape=jax.ShapeDtypeStruct((m, n), F32),
        grid=(m // tm, n // tn),
        in_specs=in_specs,
        out_specs=pl.BlockSpec((tm, tn), lambda i, j: (i, j)),
        compiler_params=_cparams(("parallel", "parallel")),
        name="res_matmul",
    )(x, *parts, *weights)


def _mlp_kernel(x_ref, g_ref, wu_ref, wd_ref, gout_ref, o_ref, h_ref, *, norm_out):
    f = pl.program_id(1)

    @pl.when(f == 0)
    def _():
        _norm_into(h_ref, x_ref, g_ref)
        o_ref[...] = x_ref[...]

    a = _dot(h_ref[...], wu_ref[...])
    a = jnp.square(jnp.maximum(a, 0.0)).astype(BF16)
    o_ref[...] += _dot(a, wd_ref[...])

    if norm_out:
        @pl.when(f == pl.num_programs(1) - 1)
        def _():
            _norm_into(o_ref, o_ref, gout_ref)


def mlp(x, g, wu, wd, layer, g_out, *, tm, tf, norm_out):
    m, d = x.shape
    ff = wu.shape[2]
    return pl.pallas_call(
        functools.partial(_mlp_kernel, norm_out=norm_out),
        out_shape=jax.ShapeDtypeStruct((m, d), F32),
        grid=(m // tm, ff // tf),
        in_specs=[
            pl.BlockSpec((tm, d), lambda i, f: (i, 0)),
            pl.BlockSpec((1, d), lambda i, f: (0, 0)),
            pl.BlockSpec((None, d, tf), lambda i, f: (layer, 0, f)),
            pl.BlockSpec((None, tf, d), lambda i, f: (layer, f, 0)),
            pl.BlockSpec((1, d), lambda i, f: (0, 0)),
        ],
        out_specs=pl.BlockSpec((tm, d), lambda i, f: (i, 0)),
        scratch_shapes=[pltpu.VMEM((tm, d), BF16)],
        compiler_params=_cparams(("parallel", "arbitrary")),
        name="mlp",
    )(x, g.reshape(1, d), wu, wd, g_out.reshape(1, d))


def _rmsnorm_kernel(x_ref, g_ref, o_ref):
    o_ref[...] = _rms_rows(x_ref[...], g_ref[...])


def rmsnorm(x, g, *, tm):
    m, d = x.shape
    return pl.pallas_call(
        _rmsnorm_kernel,
        out_shape=jax.ShapeDtypeStruct((m, d), F32),
        grid=(m // tm,),
        in_specs=[pl.BlockSpec((tm, d), lambda i: (i, 0)),
                  pl.BlockSpec((1, d), lambda i: (0, 0))],
        out_specs=pl.BlockSpec((tm, d), lambda i: (i, 0)),
        compiler_params=_cparams(("parallel",)),
        name="final_rmsnorm",
    )(x, g.reshape(1, d))


def _gmlp_kernel(z_ref, lng_ref, lnb_ref, ws_ref, bst_ref, o_ref, *, n_chunks):
    row = lax.broadcasted_iota(jnp.int32, (CHUNK, CHUNK), 0)
    col = lax.broadcasted_iota(jnp.int32, (CHUNK, CHUNK), 1)
    causal = row >= col
    w_tril = [jnp.where(causal, ws_ref[gi], 0.0).astype(BF16) for gi in range(A_GROUPS)]
    for c in range(n_chunks):
        rows = slice(c * CHUNK, (c + 1) * CHUNK)
        u = _gelu_tanh(z_ref[rows, 0:A_WIDTH])
        v = _gelu_tanh(z_ref[rows, A_WIDTH:2 * A_WIDTH])
        vc = v - jnp.mean(v, axis=-1, keepdims=True)
        vn = vc * lax.rsqrt(jnp.mean(vc * vc, axis=-1, keepdims=True) + EPS)
        vn = vn * lng_ref[...] + lnb_ref[...]
        for gi in range(A_GROUPS):
            cols = slice(gi * HEAD_DIM, (gi + 1) * HEAD_DIM)
            sv = _dot(w_tril[gi], vn[:, cols].astype(BF16)) + bst_ref[:, gi:gi + 1]
            o_ref[rows, cols] = (u[:, cols] * sv).astype(o_ref.dtype)


def gmlp(z, ln_g, ln_b, w_s, b_s, *, tm):
    t = z.shape[0]
    return pl.pallas_call(
        functools.partial(_gmlp_kernel, n_chunks=tm // CHUNK),
        out_shape=jax.ShapeDtypeStruct((t, A_WIDTH), BF16),
        grid=(t // tm,),
        in_specs=[
            pl.BlockSpec((tm, 2 * A_WIDTH), lambda i: (i, 0)),
            pl.BlockSpec((1, A_WIDTH), lambda i: (0, 0)),
            pl.BlockSpec((1, A_WIDTH), lambda i: (0, 0)),
            pl.BlockSpec((A_GROUPS, CHUNK, CHUNK), lambda i: (0, 0, 0)),
            pl.BlockSpec((CHUNK, A_GROUPS), lambda i: (0, 0)),
        ],
        out_specs=pl.BlockSpec((tm, A_WIDTH), lambda i: (i, 0)),
        compiler_params=_cparams(("parallel",)),
        name="gmlp_gating",
    )(z, ln_g.reshape(1, A_WIDTH), ln_b.reshape(1, A_WIDTH), w_s, b_s.T)


def _gate_column(gate_ref, col):
    g = 1.0 / (1.0 + jnp.exp(-gate_ref[...]))
    lane = lax.broadcasted_iota(jnp.int32, g.shape, 1)
    return jnp.sum(jnp.where(lane == col, g, 0.0), axis=-1, keepdims=True)


def _flash_kernel(q_ref, k_ref, v_ref, sel_ref, gate_ref, o_ref, qs_ref, kb_ref, vb_ref, m_ref, accl_ref, *,
                  n_rep, mode, max_dist, branch):

    tq, tk, ch = ATT_TQ, ATT_TK, ATT_CH
    seq = k_ref.shape[0]
    i = pl.program_id(2)
    select = mode == "select"

    @pl.when(i == 0)
    def _():
        def prep(c, carry):
            r0 = pl.multiple_of(c * tk, tk)
            kb_ref[pl.ds(r0, tk), 0:HEAD_DIM] = k_ref[pl.ds(r0, tk), :].astype(BF16)
            if select:
                blk = (r0 + lax.broadcasted_iota(jnp.int32, (tk, LANE), 0)) >> 6
                lane = lax.broadcasted_iota(jnp.int32, (tk, LANE), 1)
                kb_ref[pl.ds(r0, tk), HEAD_DIM:HEAD_DIM + LANE] = jnp.where(blk == lane, 1.0, 0.0).astype(BF16)
            vb_ref[pl.ds(r0, tk), 0:HEAD_DIM] = v_ref[pl.ds(r0, tk), :].astype(BF16)
            vb_ref[pl.ds(r0, tk), HEAD_DIM:2 * HEAD_DIM] = jnp.ones((tk, HEAD_DIM), BF16)
            return carry

        lax.fori_loop(0, seq // tk, prep, 0)

    qscale = (HEAD_DIM ** -0.5) * LOG2E
    if select:
        pen = jnp.where(sel_ref[...] > 0.5, 0.0, NEG_INF).astype(BF16)
    for r in range(n_rep):
        rows = slice(r * tq, (r + 1) * tq)
        qs_ref[rows, 0:HEAD_DIM] = (q_ref[:, r * HEAD_DIM:(r + 1) * HEAD_DIM] * qscale).astype(BF16)
        if select:
            qs_ref[rows, HEAD_DIM:HEAD_DIM + LANE] = pen
    m_ref[...] = jnp.full(m_ref.shape, NEG_INF, F32)
    accl_ref[...] = jnp.zeros(accl_ref.shape, F32)

    row = lax.broadcasted_iota(jnp.int32, (tq, tk), 0)
    col = lax.broadcasted_iota(jnp.int32, (tq, tk), 1)

    def tile_bias(off):
        dist = off * tk + row - col
        if off == 0:
            ok = dist >= 0
            if not select and max_dist < tk - 1:
                ok = ok & (dist <= max_dist)
        elif not select and (off + 1) * tk - 1 > max_dist:
            ok = dist <= max_dist
        else:
            return None
        return jnp.where(ok, 0.0, NEG_INF)

    n_ch = n_rep * tq // ch

    def steps(tiles):
        scores = []
        for kt, bias in tiles:
            k0 = pl.multiple_of(kt * tk, tk)
            scores.append((_dot_nt(qs_ref[...], kb_ref[pl.ds(k0, tk), :]), k0, bias))
        for s_all, k0, bias in scores:
            ps, alphas = [], []
            for c in range(n_ch):
                rows = slice(c * ch, (c + 1) * ch)
                s = s_all[rows, :]
                if bias is not None:
                    b0 = (c % (tq // ch)) * ch
                    s = s + bias[b0:b0 + ch, :]
                m_old = m_ref[rows, :]
                m_new = jnp.maximum(m_old, jnp.max(s, axis=-1, keepdims=True))
                alphas.append(jnp.exp2(m_old - m_new))
                ps.append(jnp.exp2(s - jnp.concatenate([m_new] * (tk // LANE), axis=1)).astype(BF16))
                m_ref[rows, :] = m_new
            pv_all = _dot(jnp.concatenate(ps, axis=0), vb_ref[pl.ds(k0, tk), :])
            for c in range(n_ch):
                rows = slice(c * ch, (c + 1) * ch)
                alpha2 = jnp.concatenate([alphas[c], alphas[c]], axis=1)
                accl_ref[rows, :] = alpha2 * accl_ref[rows, :] + pv_all[rows, :]

    if select:
        def body(j, carry):
            steps([(SEL_UNROLL * j + u, None) for u in range(SEL_UNROLL)])
            return carry

        lax.fori_loop(0, i // SEL_UNROLL, body, 0)

        for rem in range(SEL_UNROLL):
            @pl.when(i % SEL_UNROLL == rem)
            def _(rem=rem):
                steps([(i - rem + u, None) for u in range(rem)] + [(i, tile_bias(0))])
    else:
        n_prev = -(-max_dist // tk)

        @pl.when(i >= n_prev)
        def _():
            steps([(i - off, tile_bias(off)) for off in range(n_prev, -1, -1)])

        for first in range(n_prev):
            @pl.when(i == first)
            def _(first=first):
                steps([(first - off, tile_bias(off)) for off in range(first, -1, -1)])

    head0 = pl.program_id(1) * n_rep
    for r in range(n_rep):
        rows = slice(r * tq, (r + 1) * tq)
        cols = slice(r * HEAD_DIM, (r + 1) * HEAD_DIM)
        gate = _gate_column(gate_ref, (head0 + r) * 3 + branch)
        o = accl_ref[rows, 0:HEAD_DIM] / accl_ref[rows, HEAD_DIM:2 * HEAD_DIM]
        o_ref[:, cols] = (gate * o).astype(o_ref.dtype)


def _flash_band_kernel(q_ref, k_ref, v_ref, gate_ref, o_ref, *scratch, **static):
    _flash_kernel(q_ref, k_ref, v_ref, None, gate_ref, o_ref, *scratch, **static)


def flash_attention(z, sel_arr, *, n_batch, n_grp, n_rep, seq, q_col, k_col, v_col, mode, branch, max_dist=0):
    tq = ATT_TQ
    nq = seq // tq
    qw = n_rep * HEAD_DIM
    kw = HEAD_DIM + (LANE if mode == "select" else 0)
    in_specs = [
        pl.BlockSpec((tq, qw), lambda b, n, i: (b * nq + i, q_col(n))),
        pl.BlockSpec((seq, HEAD_DIM), lambda b, n, i: (b, k_col(n))),
        pl.BlockSpec((seq, HEAD_DIM), lambda b, n, i: (b, v_col(n))),
    ]
    args = [z, z, z]
    if mode == "select":
        in_specs.append(pl.BlockSpec((tq, LANE), lambda b, n, i: ((b * n_grp + n) * nq + i, 0)))
        args.append(sel_arr)
    in_specs.append(pl.BlockSpec((tq, LANE), lambda b, n, i: (b * nq + i, COL_GATE // LANE)))
    args.append(z)
    body = _flash_kernel if mode == "select" else _flash_band_kernel
    return pl.pallas_call(
        functools.partial(body, n_rep=n_rep, mode=mode, max_dist=max_dist, branch=branch),
        out_shape=jax.ShapeDtypeStruct((n_batch * seq, n_grp * qw), F32),
        grid=(n_batch, n_grp, nq),
        in_specs=in_specs,
        out_specs=pl.BlockSpec((tq, qw), lambda b, n, i: (b * nq + i, n)),
        scratch_shapes=[
            pltpu.VMEM((n_rep * tq, kw), BF16),
            pltpu.VMEM((seq, kw), BF16),
            pltpu.VMEM((seq, 2 * HEAD_DIM), BF16),
            pltpu.VMEM((n_rep * tq, LANE), F32),
            pltpu.VMEM((n_rep * tq, 2 * HEAD_DIM), F32),
        ],
        compiler_params=_cparams(("parallel", "parallel", "arbitrary")),
        name="flash_" + mode,
    )(*args)


DIL_CH = 128
DIL_KW = 2 * DIL_CH
DIL_UNITS = 8


def _dilated_kernel(q_ref, k_ref, v_ref, o_ref, m_ref, acc_ref, l_ref):
    ch, kw = DIL_CH, DIL_KW
    seq = q_ref.shape[0]
    qscale = (HEAD_DIM ** -0.5) * LOG2E
    row = lax.broadcasted_iota(jnp.int32, (ch, kw), 0)
    col = lax.broadcasted_iota(jnp.int32, (ch, kw), 1)
    ones = jnp.ones((kw, HEAD_DIM), BF16)
    patterns = sorted(DILATED_PAIRS, key=lambda wd: -wd[1])

    def rows(start, n, dil):
        return pl.ds(start, n) if dil == 1 else pl.ds(start, n, stride=dil)

    def band_bias(first, span):
        dist = row - col if first else ch + row - col
        return jnp.where((dist >= 0) & (dist <= span), 0.0, NEG_INF)

    def run_units(units, dil, span, phase):
        windows = {}
        scores = []
        for q0, k0, first, wid in units:
            if wid not in windows:
                k = k_ref[rows(k0, kw, dil), :].astype(BF16)
                v = v_ref[rows(k0, kw, dil), :].astype(BF16)
                windows[wid] = (k, jnp.concatenate([v, ones], axis=1))
            q = (q_ref[rows(q0, ch, dil), :] * qscale).astype(BF16)
            scores.append(_dot_nt(q, windows[wid][0]) + band_bias(first, span))
        for (q0, k0, first, wid), s in zip(units, scores):
            qrows = rows(q0, ch, dil)
            m_cur = jnp.max(s, axis=-1, keepdims=True)
            if phase == "first":
                m_new = jnp.broadcast_to(m_cur, (ch, LANE))
            else:
                m_old = m_ref[qrows, :]
                m_new = jnp.maximum(m_old, m_cur)
                alpha = jnp.exp2(m_old - m_new)
            p = jnp.exp2(s - jnp.concatenate([m_new] * (kw // LANE), axis=1)).astype(BF16)
            pv = _dot(p, windows[wid][1])
            acc, l = pv[:, 0:HEAD_DIM], pv[:, HEAD_DIM:2 * HEAD_DIM]
            if phase != "first":
                acc = alpha * acc_ref[qrows, :] + acc
                l = alpha * l_ref[qrows, :] + l
            if phase == "last":
                o_ref[pl.ds(pl.multiple_of(q0, ch), ch), :] = (acc / l).astype(o_ref.dtype)
            else:
                m_ref[qrows, :] = m_new
                acc_ref[qrows, :] = acc
                l_ref[qrows, :] = l

    for idx, (window, dil) in enumerate(patterns):
        phase = "first" if idx == 0 else ("last" if idx == len(patterns) - 1 else "mid")
        assert phase != "last" or dil == 1
        span = window // dil
        assert span <= ch
        n_units = seq // dil // ch
        per_class = min(n_units, max(DIL_UNITS // dil, 2))
        n_classes = DIL_UNITS // per_class
        assert n_units % per_class == 0 and dil % n_classes == 0

        def group(r0, j0, head, dil=dil, span=span, phase=phase, per_class=per_class, n_classes=n_classes):
            units = []
            for rc in range(n_classes):
                r = r0 + rc
                for u in range(per_class):
                    first = head and u == 0
                    q0 = r + dil * ch * (j0 + u)
                    k0 = r if (head and u <= 1) else r + dil * ch * (j0 + u - 1)
                    wid = (rc, 0) if (head and u <= 1) else (rc, u)
                    units.append((q0, k0, first, wid))
            run_units(units, dil, span, phase)

        def class_loop(g, carry, group=group, per_class=per_class, n_classes=n_classes, n_units=n_units):
            r0 = g * n_classes
            group(r0, 0, True)
            if n_units > per_class:
                def tail(jb, c):
                    group(r0, jb * per_class, False)
                    return c
                lax.fori_loop(1, n_units // per_class, tail, 0)
            return carry

        if dil // n_classes == 1:
            class_loop(0, 0)
        else:
            lax.fori_loop(0, dil // n_classes, class_loop, 0)


def dilated_attention(z, *, n_batch, seq):
    bw = B_HEADS * HEAD_DIM
    base = COL_B // HEAD_DIM
    blk = (seq, HEAD_DIM)
    return pl.pallas_call(
        _dilated_kernel,
        out_shape=jax.ShapeDtypeStruct((n_batch * seq, bw), BF16),
        grid=(n_batch, B_HEADS),
        in_specs=[
            pl.BlockSpec(blk, lambda b, h: (b, base + h)),
            pl.BlockSpec(blk, lambda b, h: (b, base + B_HEADS + h)),
            pl.BlockSpec(blk, lambda b, h: (b, base + 2 * B_HEADS + h)),
        ],
        out_specs=pl.BlockSpec(blk, lambda b, h: (b, h)),
        scratch_shapes=[pltpu.VMEM(blk, F32)] * 3,
        compiler_params=_cparams(("parallel", "parallel")),
        name="dilated_attention",
    )(z, z, z)


N_SUB = 256
SUB_W = CMP_STRIDE * HEAD_DIM


def _compress_kernel(x_ref, pos_ref, w1_ref, w2_ref, o_ref):
    p = jnp.zeros((N_SUB, HEAD_DIM), F32)
    q = jnp.zeros((N_SUB, HEAD_DIM), F32)
    for i in range(CMP_STRIDE):
        a = x_ref[pl.ds(i, N_SUB, stride=CMP_STRIDE), :]
        top = (a + pos_ref[i:i + 1, :]).astype(BF16)
        bot = (a + pos_ref[CMP_STRIDE + i:CMP_STRIDE + i + 1, :]).astype(BF16)
        p = p + _dot(top, w1_ref[i * HEAD_DIM:(i + 1) * HEAD_DIM, :])
        q = q + _dot(bot, w1_ref[(CMP_STRIDE + i) * HEAD_DIM:(CMP_STRIDE + i + 1) * HEAD_DIM, :])
    h = _gelu_tanh(p + pltpu.roll(q, N_SUB - 1, 0))
    o_ref[...] = _dot(h.astype(BF16), w2_ref[...]).astype(o_ref.dtype)


def compress(z, pos, w1, w2, layer, *, n_batch, seq):
    return pl.pallas_call(
        _compress_kernel,
        out_shape=jax.ShapeDtypeStruct((n_batch * 4 * N_SUB, HEAD_DIM), BF16),
        grid=(n_batch, 4),
        in_specs=[
            pl.BlockSpec((seq, HEAD_DIM), lambda b, n: (b, COL_CMP // HEAD_DIM + n)),
            pl.BlockSpec((CMP_LEN, HEAD_DIM), lambda b, n: (0, 0)),
            pl.BlockSpec((None, None, CMP_LEN * HEAD_DIM, HEAD_DIM), lambda b, n: (layer, n // 2, 0, 0)),
            pl.BlockSpec((None, None, HEAD_DIM, HEAD_DIM), lambda b, n: (layer, n // 2, 0, 0)),
        ],
        out_specs=pl.BlockSpec((N_SUB, HEAD_DIM), lambda b, n: (b * 4 + n, 0)),
        compiler_params=_cparams(("parallel", "parallel")),
        name="nsa_compress",
    )(z, pos, w1, w2)


N_SEL = 64


def _csel_kernel(q_ref, kc_ref, vc_ref, ovt_ref, gate_ref, oc_ref, sel_ref, qs_ref, sc_ref):
    tq, ch = ATT_TQ, ATT_CH
    n_sub = tq // ch
    q0 = pl.program_id(2) * tq
    qscale = (HEAD_DIM ** -0.5) * LOG2E
    for r in range(C_GROUP):
        qs_ref[r * tq:(r + 1) * tq, :] = (q_ref[:, r * HEAD_DIM:(r + 1) * HEAD_DIM] * qscale).astype(BF16)
    s_all = _dot_nt(qs_ref[...], kc_ref[...])
    n_cols = lax.broadcasted_iota(jnp.int32, (ch, N_SUB), 1)
    bias, has_any = [], []
    for c in range(n_sub):
        t = q0 + c * ch + lax.broadcasted_iota(jnp.int32, (ch, N_SUB), 0)
        bias.append(jnp.where((n_cols * CMP_STRIDE + (CMP_LEN - 1)) <= t, 0.0, NEG_INF))
        has_any.append(t[:, 0:1] >= CMP_LEN - 1)
    ps = []
    psum = [jnp.zeros((ch, N_SUB), F32) for _ in range(n_sub)]
    for c in range(C_GROUP * n_sub):
        s = s_all[c * ch:(c + 1) * ch, :] + bias[c % n_sub]
        e = jnp.exp2(s - jnp.max(s, axis=-1, keepdims=True))
        den = jnp.sum(e, axis=-1, keepdims=True)
        p = e * jnp.where(has_any[c % n_sub], 1.0 / den, 0.0)
        ps.append(p.astype(BF16))
        psum[c % n_sub] = psum[c % n_sub] + p
    o_all = _dot(jnp.concatenate(ps, axis=0), vc_ref[...])
    head0 = pl.program_id(1) * C_GROUP
    for r in range(C_GROUP):
        gate = _gate_column(gate_ref, (head0 + r) * 3)
        oc_ref[:, r * HEAD_DIM:(r + 1) * HEAD_DIM] = gate * o_all[r * tq:(r + 1) * tq, :]
    psum = jnp.concatenate(psum, axis=0)
    hi = psum.astype(BF16)
    lo = (psum - hi.astype(F32)).astype(BF16)
    ovt = ovt_ref[...]
    imp_t = _dot_nt(ovt, hi) + _dot_nt(ovt, lo)
    jrow = lax.broadcasted_iota(jnp.int32, (LANE, tq), 0)
    tl = q0 + lax.broadcasted_iota(jnp.int32, (LANE, tq), 1)
    jt = tl >> 6
    forced = (jrow == 0) | (jrow == jt) | (jrow == jt - 1)
    valid_s = jrow * SEL_LEN <= tl
    score = jnp.where(forced, 1e4, jnp.where(valid_s, imp_t, -1.0))
    sc_ref[...] = score
    n_grp = N_SEL // SUBLANE
    grp = [sc_ref[g * SUBLANE:(g + 1) * SUBLANE, :] for g in range(n_grp)]
    cnt = [jnp.zeros((SUBLANE, tq), F32) for _ in range(n_grp)]
    sub = lax.broadcasted_iota(jnp.int32, (SUBLANE, tq), 0)
    for j in range(N_SEL):
        row = sc_ref[j:j + 1, :]
        for g in range(n_grp):
            if g < j // SUBLANE:
                beats = row > grp[g]
            elif g > j // SUBLANE:
                beats = row >= grp[g]
            else:
                beats = (row > grp[g]) | ((row == grp[g]) & (sub > j % SUBLANE))
            cnt[g] = cnt[g] + jnp.where(beats, 1.0, 0.0)
    sel_t = [jnp.where(c < float(SEL_TOP), 1.0, 0.0) for c in cnt]
    sel_pad = jnp.concatenate(sel_t + [jnp.zeros((LANE - N_SEL, tq), F32)], axis=0)
    sel_ref[...] = sel_pad.T


def compressed_select(z, kvc, ovt, *, n_batch, seq):
    tq = ATT_TQ
    nq = seq // tq
    qw = C_GROUP * HEAD_DIM
    return pl.pallas_call(
        _csel_kernel,
        out_shape=[jax.ShapeDtypeStruct((n_batch * seq, C_HEADS * HEAD_DIM), F32),
                   jax.ShapeDtypeStruct((n_batch * C_KV_HEADS * seq, LANE), F32)],
        grid=(n_batch, C_KV_HEADS, nq),
        in_specs=[
            pl.BlockSpec((tq, qw), lambda b, g, i: (b * nq + i, COL_CQ // qw + g)),
            pl.BlockSpec((N_SUB, HEAD_DIM), lambda b, g, i: (b * 4 + g, 0)),
            pl.BlockSpec((N_SUB, HEAD_DIM), lambda b, g, i: (b * 4 + 2 + g, 0)),
            pl.BlockSpec((LANE, N_SUB), lambda b, g, i: (0, 0)),
            pl.BlockSpec((tq, LANE), lambda b, g, i: (b * nq + i, COL_GATE // LANE)),
        ],
        out_specs=[
            pl.BlockSpec((tq, qw), lambda b, g, i: (b * nq + i, g)),
            pl.BlockSpec((tq, LANE), lambda b, g, i: ((b * C_KV_HEADS + g) * nq + i, 0)),
        ],
        scratch_shapes=[pltpu.VMEM((C_GROUP * tq, HEAD_DIM), BF16), pltpu.VMEM((LANE, tq), F32)],
        compiler_params=_cparams(("parallel", "parallel", "parallel")),
        name="nsa_compressed_select",
    )(z, kvc, kvc, ovt, z)


def _overlap_t():
    n_c = N_SUB - 1
    c_start = np.arange(n_c) * CMP_STRIDE
    s_start = np.arange(N_SEL) * SEL_LEN
    ov = ((c_start[:, None] <= s_start[None, :] + SEL_LEN - 1)
          & (c_start[:, None] + CMP_LEN - 1 >= s_start[None, :])).astype(np.float32)
    out = np.zeros((LANE, N_SUB), np.float32)
    out[:N_SEL, :n_c] = ov.T
    return out


def _mix_out_kernel(x_ref, oa_ref, ob_ref, oc_ref, os_ref, ow_ref, w_ref, o_ref):
    mix = jnp.concatenate([oa_ref[...], ob_ref[...], (oc_ref[...] + os_ref[...] + ow_ref[...]).astype(BF16)],
                          axis=1)
    o_ref[...] = x_ref[...] + _dot(mix, w_ref[...])


def mix_out(x, out_a, out_b, oc, os_, ow, w, layer, *, tm):
    m, n = x.shape
    bw = B_HEADS * HEAD_DIM
    cw = C_HEADS * HEAD_DIM
    spec_c = pl.BlockSpec((tm, cw), lambda i: (i, 0))
    return pl.pallas_call(
        _mix_out_kernel,
        out_shape=jax.ShapeDtypeStruct((m, n), F32),
        grid=(m // tm,),
        in_specs=[
            pl.BlockSpec((tm, n), lambda i: (i, 0)),
            pl.BlockSpec((tm, A_WIDTH), lambda i: (i, 0)),
            pl.BlockSpec((tm, bw), lambda i: (i, 0)),
            spec_c, spec_c, spec_c,
            pl.BlockSpec((None,) + w.shape[1:], lambda i: (layer, 0, 0), pipeline_mode=pl.Buffered(1)),
        ],
        out_specs=pl.BlockSpec((tm, n), lambda i: (i, 0)),
        compiler_params=_cparams(("parallel",)),
        name="mix_out",
    )(x, out_a, out_b, oc, os_, ow, w)


def _xattn_kernel(x_ref, g_ref, wq_ref, k_ref, v_ref, wo_ref, o_ref, h_ref, a_ref):
    n_mem = k_ref.shape[0]
    _norm_into(h_ref, x_ref, g_ref)
    q = _dot(h_ref[...], wq_ref[...]) * ((HEAD_DIM ** -0.5) * LOG2E)
    ones = jnp.ones((n_mem, HEAD_DIM), BF16)
    for h in range(X_HEADS):
        cols = slice(h * HEAD_DIM, (h + 1) * HEAD_DIM)
        s = _dot_nt(q[:, cols].astype(BF16), k_ref[:, cols].astype(BF16))
        e = jnp.exp2(s - jnp.max(s, axis=-1, keepdims=True))
        pv = _dot(e.astype(BF16), jnp.concatenate([v_ref[:, cols].astype(BF16), ones], axis=1))
        a_ref[:, cols] = (pv[:, 0:HEAD_DIM] / pv[:, HEAD_DIM:2 * HEAD_DIM]).astype(BF16)
    o_ref[...] = x_ref[...] + _dot(a_ref[...], wo_ref[...])


def cross_attention_block(x, g, wq, kv, wo, layer, *, n_batch, seq, n_mem, tm):
    nq = seq // tm
    d = x.shape[1]
    return pl.pallas_call(
        _xattn_kernel,
        out_shape=jax.ShapeDtypeStruct(x.shape, F32),
        grid=(n_batch, nq),
        in_specs=[
            pl.BlockSpec((tm, d), lambda b, i: (b * nq + i, 0)),
            pl.BlockSpec((1, d), lambda b, i: (0, 0)),
            pl.BlockSpec((None, d, X_WIDTH), lambda b, i: (layer, 0, 0), pipeline_mode=pl.Buffered(1)),
            pl.BlockSpec((n_mem, X_WIDTH), lambda b, i: (b, 0)),
            pl.BlockSpec((n_mem, X_WIDTH), lambda b, i: (b, 1)),
            pl.BlockSpec((None, X_WIDTH, d), lambda b, i: (layer, 0, 0), pipeline_mode=pl.Buffered(1)),
        ],
        out_specs=pl.BlockSpec((tm, d), lambda b, i: (b * nq + i, 0)),
        scratch_shapes=[pltpu.VMEM((tm, d), BF16), pltpu.VMEM((tm, X_WIDTH), BF16)],
        compiler_params=_cparams(("parallel", "parallel")),
        name="cross_attention",
    )(x, g.reshape(1, d), wq, kv, kv, wo)


def kernel(x, mem, norm_mix, w_in, gmlp_ln_g, gmlp_ln_b, gmlp_w_s, gmlp_b_s, cmp_pos, cmp_k_w1, cmp_k_w2,
           cmp_v_w1, cmp_v_w2, w_out, norm_xattn, norm_mem, xattn_wq, xattn_wkv, xattn_wo, norm_mlp, w_up,
           w_down, final_norm):
    bsz, seq, d = x.shape
    n_mem = mem.shape[1]
    t = bsz * seq
    assert d == D_MODEL and seq // CMP_STRIDE == N_SUB and seq // SEL_LEN == N_SEL
    assert seq % (ATT_TQ * 16) == 0

    w_in_b = jnp.pad(w_in, ((0, 0), (0, 0), (0, IN_PAD - IN_WIDTH))).astype(BF16)
    w_out_b = w_out.astype(BF16)
    wq_b = xattn_wq.astype(BF16)
    wkv_b = xattn_wkv.astype(BF16)
    wo_b = xattn_wo.astype(BF16)
    w_up_b = w_up.astype(BF16)
    w_down_b = w_down.astype(BF16)
    cmp_w1 = jnp.stack([cmp_k_w1, cmp_v_w1], axis=1).astype(BF16)
    cmp_w2 = jnp.stack([cmp_k_w2, cmp_v_w2], axis=1).astype(BF16)
    ovt = jnp.asarray(_overlap_t()).astype(BF16)
    xf = x.reshape(t, d)
    memf = mem.reshape(bsz * n_mem, d)
    gw = C_GROUP * HEAD_DIM

    for l in range(DEPTH):
        z = norm_matmul(xf, norm_mix[l], w_in_b, l, tm=1024, tn=IN_TILE)

        out_a = gmlp(z, gmlp_ln_g[l], gmlp_ln_b[l], gmlp_w_s[l], gmlp_b_s[l], tm=512)

        out_b = dilated_attention(z, n_batch=bsz, seq=seq)

        kvc = compress(z, cmp_pos[l], cmp_w1, cmp_w2, l, n_batch=bsz, seq=seq)
        o_c, sel = compressed_select(z, kvc, ovt, n_batch=bsz, seq=seq)
        o_s = flash_attention(
            z, sel, n_batch=bsz, n_grp=C_KV_HEADS, n_rep=C_GROUP, seq=seq,
            q_col=lambda n: COL_CQ // gw + n,
            k_col=lambda n: COL_SEL // HEAD_DIM + n,
            v_col=lambda n: COL_SEL // HEAD_DIM + C_KV_HEADS + n,
            mode="select", branch=1)
        o_w = flash_attention(
            z, None, n_batch=bsz, n_grp=C_KV_HEADS, n_rep=C_GROUP, seq=seq,
            q_col=lambda n: COL_CQ // gw + n,
            k_col=lambda n: COL_WIN // HEAD_DIM + n,
            v_col=lambda n: COL_WIN // HEAD_DIM + C_KV_HEADS + n,
            mode="band", branch=2, max_dist=WIN_LEN - 1)

        xf = mix_out(xf, out_a, out_b, o_c, o_s, o_w, w_out_b, l, tm=512)

        kv = norm_matmul(memf, norm_mem[l], wkv_b, l, tm=512, tn=2 * X_WIDTH)
        xf = cross_attention_block(xf, norm_xattn[l], wq_b, kv, wo_b, l, n_batch=bsz, seq=seq, n_mem=n_mem,
                                   tm=512)

        xf = mlp(xf, norm_mlp[l], w_up_b, w_down_b, l, final_norm, tm=1024, tf=512,
                 norm_out=(l == DEPTH - 1))

    return xf.reshape(bsz, seq, d)
```

```python
import functools
import math

import numpy as np
import jax
import jax.numpy as jnp
from jax import lax
from jax.experimental import pallas as pl
from jax.experimental.pallas import tpu as pltpu

F32 = jnp.float32
BF16 = jnp.bfloat16

D_MODEL = 2048
DEPTH = 4
HEAD_DIM = 128
A_GROUPS = 4
A_WIDTH = A_GROUPS * HEAD_DIM
CHUNK = 128
B_HEADS = 4
DILATED_PAIRS = ((128, 1), (512, 4), (2048, 16))
C_HEADS = 8
C_KV_HEADS = 2
C_GROUP = C_HEADS // C_KV_HEADS
CMP_LEN = 32
CMP_STRIDE = 16
SEL_LEN = 64
SEL_TOP = 16
WIN_LEN = 512
X_HEADS = 4
X_WIDTH = X_HEADS * HEAD_DIM
EPS = 1e-6
NEG_INF = -1e30

COL_A = 0
COL_B = 2 * A_WIDTH
COL_CQ = COL_B + 3 * B_HEADS * HEAD_DIM
COL_CMP = COL_CQ + C_HEADS * HEAD_DIM
COL_SEL = COL_CMP + 2 * C_KV_HEADS * HEAD_DIM
COL_WIN = COL_SEL + 2 * C_KV_HEADS * HEAD_DIM
COL_GATE = COL_WIN + 2 * C_KV_HEADS * HEAD_DIM
IN_WIDTH = COL_GATE + 3 * C_HEADS
IN_PAD = 5248

LANE = 128
SUBLANE = 8
VMEM_LIMIT = 56 * 1024 * 1024

ATT_TQ = 256
ATT_TK = 256
ATT_CH = 128
SEL_UNROLL = 4
LOG2E = math.log2(math.e)
LN2 = math.log(2.0)


def _cparams(sem):
    return pltpu.CompilerParams(dimension_semantics=sem, vmem_limit_bytes=VMEM_LIMIT)


def _rms_rows(x, g):
    ms = jnp.mean(x * x, axis=-1, keepdims=True)
    return x * lax.rsqrt(ms + EPS) * g


def _gelu_tanh(x):
    c = math.sqrt(2.0 / math.pi)
    return x * (0.5 * (1.0 + jnp.tanh(c * (x + 0.044715 * (x * x * x)))))


def _dot(a, b):
    return jnp.dot(a, b, preferred_element_type=F32)


def _dot_nt(a, b):
    return lax.dot_general(a, b, (((1,), (1,)), ((), ())), preferred_element_type=F32)


NORM_ROWS = 256


def _norm_into(h_ref, x_ref, g_ref):
    rows = x_ref.shape[0]

    def body(c, carry):
        r0 = pl.multiple_of(c * NORM_ROWS, NORM_ROWS)
        x = x_ref[pl.ds(r0, NORM_ROWS), :]
        h_ref[pl.ds(r0, NORM_ROWS), :] = _rms_rows(x, g_ref[...]).astype(h_ref.dtype)
        return carry

    lax.fori_loop(0, rows // NORM_ROWS, body, 0)


def _norm_matmul_kernel(x_ref, g_ref, w_ref, o_ref, h_ref):
    @pl.when(pl.program_id(1) == 0)
    def _():
        _norm_into(h_ref, x_ref, g_ref)

    o_ref[...] = _dot(h_ref[...], w_ref[...]).astype(o_ref.dtype)


def norm_matmul(x, g, w, layer, *, tm, tn, out_dtype=F32):
    m, k = x.shape
    n = w.shape[2]
    return pl.pallas_call(
        _norm_matmul_kernel,
        out_shape=jax.ShapeDtypeStruct((m, n), out_dtype),
        grid=(m // tm, n // tn),
        in_specs=[
            pl.BlockSpec((tm, k), lambda i, j: (i, 0)),
            pl.BlockSpec((1, k), lambda i, j: (0, 0)),
            pl.BlockSpec((None, k, tn), lambda i, j: (layer, 0, j),
                         pipeline_mode=pl.Buffered(1) if tn == n else None),
        ],
        out_specs=pl.BlockSpec((tm, tn), lambda i, j: (i, j)),
        scratch_shapes=[pltpu.VMEM((tm, k), BF16)],
        compiler_params=_cparams(("parallel", "arbitrary")),
        name="norm_matmul",
    )(x, g.reshape(1, k), w)


def _res_matmul_kernel(*refs, n_parts):
    x_ref = refs[0]
    a_refs = refs[1:1 + n_parts]
    w_refs = refs[1 + n_parts:1 + 2 * n_parts]
    o_ref = refs[1 + 2 * n_parts]
    acc = x_ref[...]
    for a_ref, w_ref in zip(a_refs, w_refs):
        acc = acc + _dot(a_ref[...].astype(BF16), w_ref[...])
    o_ref[...] = acc


def res_matmul(x, parts, weights, *, tm, tn):
    m, n = x.shape
    n_parts = len(parts)
    in_specs = [pl.BlockSpec((tm, tn), lambda i, j: (i, j))]
    for a in parts:
        in_specs.append(pl.BlockSpec((tm, a.shape[1]), lambda i, j: (i, 0)))
    for w in weights:
        in_specs.append(pl.BlockSpec((w.shape[0], tn), lambda i, j: (0, j)))
    return pl.pallas_call(
        functools.partial(_res_matmul_kernel, n_parts=n_parts),
        out_shape=jax.ShapeDtypeStruct((m, n), F32),
        grid=(m // tm, n // tn),
        in_specs=in_specs,
        out_specs=pl.BlockSpec((tm, tn), lambda i, j: (i, j)),
        compiler_params=_cparams(("parallel", "parallel")),
        name="res_matmul",
    )(x, *parts, *weights)


def _mlp_kernel(x_ref, g_ref, wu_ref, wd_ref, gout_ref, o_ref, h_ref, *, norm_out):
    f = pl.program_id(1)

    @pl.when(f == 0)
    def _():
        _norm_into(h_ref, x_ref, g_ref)
        o_ref[...] = x_ref[...]

    a = _dot(h_ref[...], wu_ref[...])
    a = jnp.square(jnp.maximum(a, 0.0)).astype(BF16)
    o_ref[...] += _dot(a, wd_ref[...])

    if norm_out:
        @pl.when(f == pl.num_programs(1) - 1)
        def _():
            _norm_into(o_ref, o_ref, gout_ref)


def mlp(x, g, wu, wd, layer, g_out, *, tm, tf, norm_out):
    m, d = x.shape
    ff = wu.shape[2]
    return pl.pallas_call(
        functools.partial(_mlp_kernel, norm_out=norm_out),
        out_shape=jax.ShapeDtypeStruct((m, d), F32),
        grid=(m // tm, ff // tf),
        in_specs=[
            pl.BlockSpec((tm, d), lambda i, f: (i, 0)),
            pl.BlockSpec((1, d), lambda i, f: (0, 0)),
            pl.BlockSpec((None, d, tf), lambda i, f: (layer, 0, f)),
            pl.BlockSpec((None, tf, d), lambda i, f: (layer, f, 0)),
            pl.BlockSpec((1, d), lambda i, f: (0, 0)),
        ],
        out_specs=pl.BlockSpec((tm, d), lambda i, f: (i, 0)),
        scratch_shapes=[pltpu.VMEM((tm, d), BF16)],
        compiler_params=_cparams(("parallel", "arbitrary")),
        name="mlp",
    )(x, g.reshape(1, d), wu, wd, g_out.reshape(1, d))


def _rmsnorm_kernel(x_ref, g_ref, o_ref):
    o_ref[...] = _rms_rows(x_ref[...], g_ref[...])


def rmsnorm(x, g, *, tm):
    m, d = x.shape
    return pl.pallas_call(
        _rmsnorm_kernel,
        out_shape=jax.ShapeDtypeStruct((m, d), F32),
        grid=(m // tm,),
        in_specs=[pl.BlockSpec((tm, d), lambda i: (i, 0)),
                  pl.BlockSpec((1, d), lambda i: (0, 0))],
        out_specs=pl.BlockSpec((tm, d), lambda i: (i, 0)),
        compiler_params=_cparams(("parallel",)),
        name="final_rmsnorm",
    )(x, g.reshape(1, d))


def _gmlp_kernel(z_ref, lng_ref, lnb_ref, ws_ref, bst_ref, o_ref, *, n_chunks):
    row = lax.broadcasted_iota(jnp.int32, (CHUNK, CHUNK), 0)
    col = lax.broadcasted_iota(jnp.int32, (CHUNK, CHUNK), 1)
    causal = row >= col
    w_tril = [jnp.where(causal, ws_ref[gi], 0.0).astype(BF16) for gi in range(A_GROUPS)]
    for c in range(n_chunks):
        rows = slice(c * CHUNK, (c + 1) * CHUNK)
        u = _gelu_tanh(z_ref[rows, 0:A_WIDTH].astype(F32))
        v = _gelu_tanh(z_ref[rows, A_WIDTH:2 * A_WIDTH].astype(F32))
        vc = v - jnp.mean(v, axis=-1, keepdims=True)
        vn = vc * lax.rsqrt(jnp.mean(vc * vc, axis=-1, keepdims=True) + EPS)
        vn = vn * lng_ref[...] + lnb_ref[...]
        for gi in range(A_GROUPS):
            cols = slice(gi * HEAD_DIM, (gi + 1) * HEAD_DIM)
            sv = _dot(w_tril[gi], vn[:, cols].astype(BF16)) + bst_ref[:, gi:gi + 1]
            o_ref[rows, cols] = (u[:, cols] * sv).astype(o_ref.dtype)


def gmlp(z, ln_g, ln_b, w_s, b_s, *, tm):
    t = z.shape[0]
    return pl.pallas_call(
        functools.partial(_gmlp_kernel, n_chunks=tm // CHUNK),
        out_shape=jax.ShapeDtypeStruct((t, A_WIDTH), BF16),
        grid=(t // tm,),
        in_specs=[
            pl.BlockSpec((tm, 2 * A_WIDTH), lambda i: (i, 0)),
            pl.BlockSpec((1, A_WIDTH), lambda i: (0, 0)),
            pl.BlockSpec((1, A_WIDTH), lambda i: (0, 0)),
            pl.BlockSpec((A_GROUPS, CHUNK, CHUNK), lambda i: (0, 0, 0)),
            pl.BlockSpec((CHUNK, A_GROUPS), lambda i: (0, 0)),
        ],
        out_specs=pl.BlockSpec((tm, A_WIDTH), lambda i: (i, 0)),
        compiler_params=_cparams(("parallel",)),
        name="gmlp_gating",
    )(z, ln_g.reshape(1, A_WIDTH), ln_b.reshape(1, A_WIDTH), w_s, b_s.T)


def _gate_column(gate_ref, col):
    g = 1.0 / (1.0 + jnp.exp(-gate_ref[...].astype(F32)))
    lane = lax.broadcasted_iota(jnp.int32, g.shape, 1)
    return jnp.sum(jnp.where(lane == col, g, 0.0), axis=-1, keepdims=True)


def _flash_kernel(q_ref, k_ref, v_ref, sel_ref, gate_ref, o_ref, qs_ref, kb_ref, vb_ref, m_ref, accl_ref, *,
                  n_rep, mode, max_dist, branch):

    tq, tk, ch = ATT_TQ, ATT_TK, ATT_CH
    seq = k_ref.shape[0]
    i = pl.program_id(2)
    select = mode == "select"

    @pl.when(i == 0)
    def _():
        def prep(c, carry):
            r0 = pl.multiple_of(c * tk, tk)
            kb_ref[pl.ds(r0, tk), 0:HEAD_DIM] = k_ref[pl.ds(r0, tk), :].astype(BF16)
            if select:
                blk = (r0 + lax.broadcasted_iota(jnp.int32, (tk, LANE), 0)) >> 6
                lane = lax.broadcasted_iota(jnp.int32, (tk, LANE), 1)
                kb_ref[pl.ds(r0, tk), HEAD_DIM:HEAD_DIM + LANE] = jnp.where(blk == lane, 1.0, 0.0).astype(BF16)
            vb_ref[pl.ds(r0, tk), 0:HEAD_DIM] = v_ref[pl.ds(r0, tk), :].astype(BF16)
            vb_ref[pl.ds(r0, tk), HEAD_DIM:2 * HEAD_DIM] = jnp.ones((tk, HEAD_DIM), BF16)
            return carry

        lax.fori_loop(0, seq // tk, prep, 0)

    qscale = (HEAD_DIM ** -0.5) * LOG2E
    if select:
        pen = jnp.where(sel_ref[...] > 0.5, 0.0, NEG_INF).astype(BF16)
    for r in range(n_rep):
        rows = slice(r * tq, (r + 1) * tq)
        qs_ref[rows, 0:HEAD_DIM] = (q_ref[:, r * HEAD_DIM:(r + 1) * HEAD_DIM].astype(F32) * qscale).astype(BF16)
        if select:
            qs_ref[rows, HEAD_DIM:HEAD_DIM + LANE] = pen
    m_ref[...] = jnp.full(m_ref.shape, NEG_INF, F32)
    accl_ref[...] = jnp.zeros(accl_ref.shape, F32)

    row = lax.broadcasted_iota(jnp.int32, (tq, tk), 0)
    col = lax.broadcasted_iota(jnp.int32, (tq, tk), 1)

    def tile_bias(off):
        dist = off * tk + row - col
        if off == 0:
            ok = dist >= 0
            if not select and max_dist < tk - 1:
                ok = ok & (dist <= max_dist)
        elif not select and (off + 1) * tk - 1 > max_dist:
            ok = dist <= max_dist
        else:
            return None
        return jnp.where(ok, 0.0, NEG_INF)

    n_ch = n_rep * tq // ch

    def steps(tiles):
        scores = []
        for kt, bias in tiles:
            k0 = pl.multiple_of(kt * tk, tk)
            scores.append((_dot_nt(qs_ref[...], kb_ref[pl.ds(k0, tk), :]), k0, bias))
        for s_all, k0, bias in scores:
            ps, alphas = [], []
            for c in range(n_ch):
                rows = slice(c * ch, (c + 1) * ch)
                s = s_all[rows, :]
                if bias is not None:
                    b0 = (c % (tq // ch)) * ch
                    s = s + bias[b0:b0 + ch, :]
                m_old = m_ref[rows, :]
                m_new = jnp.maximum(m_old, jnp.max(s, axis=-1, keepdims=True))
                alphas.append(jnp.exp2(m_old - m_new))
                ps.append(jnp.exp2(s - jnp.concatenate([m_new] * (tk // LANE), axis=1)).astype(BF16))
                m_ref[rows, :] = m_new
            pv_all = _dot(jnp.concatenate(ps, axis=0), vb_ref[pl.ds(k0, tk), :])
            for c in range(n_ch):
                rows = slice(c * ch, (c + 1) * ch)
                alpha2 = jnp.concatenate([alphas[c], alphas[c]], axis=1)
                accl_ref[rows, :] = alpha2 * accl_ref[rows, :] + pv_all[rows, :]

    if select:
        def body(j, carry):
            steps([(SEL_UNROLL * j + u, None) for u in range(SEL_UNROLL)])
            return carry

        lax.fori_loop(0, i // SEL_UNROLL, body, 0)

        for rem in range(SEL_UNROLL):
            @pl.when(i % SEL_UNROLL == rem)
            def _(rem=rem):
                steps([(i - rem + u, None) for u in range(rem)] + [(i, tile_bias(0))])
    else:
        n_prev = -(-max_dist // tk)

        @pl.when(i >= n_prev)
        def _():
            steps([(i - off, tile_bias(off)) for off in range(n_prev, -1, -1)])

        for first in range(n_prev):
            @pl.when(i == first)
            def _(first=first):
                steps([(first - off, tile_bias(off)) for off in range(first, -1, -1)])

    head0 = pl.program_id(1) * n_rep
    for r in range(n_rep):
        rows = slice(r * tq, (r + 1) * tq)
        cols = slice(r * HEAD_DIM, (r + 1) * HEAD_DIM)
        gate = _gate_column(gate_ref, (head0 + r) * 3 + branch)
        o = accl_ref[rows, 0:HEAD_DIM] / accl_ref[rows, HEAD_DIM:2 * HEAD_DIM]
        o_ref[:, cols] = (gate * o).astype(o_ref.dtype)


def _flash_band_kernel(q_ref, k_ref, v_ref, gate_ref, o_ref, *scratch, **static):
    _flash_kernel(q_ref, k_ref, v_ref, None, gate_ref, o_ref, *scratch, **static)


def flash_attention(z, sel_arr, *, n_batch, n_grp, n_rep, seq, q_col, k_col, v_col, mode, branch, max_dist=0):
    tq = ATT_TQ
    nq = seq // tq
    qw = n_rep * HEAD_DIM
    kw = HEAD_DIM + (LANE if mode == "select" else 0)
    in_specs = [
        pl.BlockSpec((tq, qw), lambda b, n, i: (b * nq + i, q_col(n))),
        pl.BlockSpec((seq, HEAD_DIM), lambda b, n, i: (b, k_col(n))),
        pl.BlockSpec((seq, HEAD_DIM), lambda b, n, i: (b, v_col(n))),
    ]
    args = [z, z, z]
    if mode == "select":
        in_specs.append(pl.BlockSpec((tq, LANE), lambda b, n, i: ((b * n_grp + n) * nq + i, 0)))
        args.append(sel_arr)
    in_specs.append(pl.BlockSpec((tq, LANE), lambda b, n, i: (b * nq + i, COL_GATE // LANE)))
    args.append(z)
    body = _flash_kernel if mode == "select" else _flash_band_kernel
    return pl.pallas_call(
        functools.partial(body, n_rep=n_rep, mode=mode, max_dist=max_dist, branch=branch),
        out_shape=jax.ShapeDtypeStruct((n_batch * seq, n_grp * qw), F32),
        grid=(n_batch, n_grp, nq),
        in_specs=in_specs,
        out_specs=pl.BlockSpec((tq, qw), lambda b, n, i: (b * nq + i, n)),
        scratch_shapes=[
            pltpu.VMEM((n_rep * tq, kw), BF16),
            pltpu.VMEM((seq, kw), BF16),
            pltpu.VMEM((seq, 2 * HEAD_DIM), BF16),
            pltpu.VMEM((n_rep * tq, LANE), F32),
            pltpu.VMEM((n_rep * tq, 2 * HEAD_DIM), F32),
        ],
        compiler_params=_cparams(("parallel", "parallel", "arbitrary")),
        name="flash_" + mode,
    )(*args)


DIL_CH = 128
DIL_KW = 2 * DIL_CH
DIL_UNITS = 8


def _dilated_kernel(q_in, k_in, v_in, o_ref, q_ref, k_ref, v_ref, m_ref, acc_ref, l_ref):
    ch, kw = DIL_CH, DIL_KW
    seq = q_in.shape[0]
    for src, dst in ((q_in, q_ref), (k_in, k_ref), (v_in, v_ref)):
        def stage(c, carry, src=src, dst=dst):
            r0 = pl.multiple_of(c * NORM_ROWS, NORM_ROWS)
            dst[pl.ds(r0, NORM_ROWS), :] = src[pl.ds(r0, NORM_ROWS), :].astype(F32)
            return carry
        lax.fori_loop(0, seq // NORM_ROWS, stage, 0)
    qscale = (HEAD_DIM ** -0.5) * LOG2E
    row = lax.broadcasted_iota(jnp.int32, (ch, kw), 0)
    col = lax.broadcasted_iota(jnp.int32, (ch, kw), 1)
    ones = jnp.ones((kw, HEAD_DIM), BF16)
    patterns = sorted(DILATED_PAIRS, key=lambda wd: -wd[1])

    def rows(start, n, dil):
        return pl.ds(start, n) if dil == 1 else pl.ds(start, n, stride=dil)

    def band_bias(first, span):
        dist = row - col if first else ch + row - col
        return jnp.where((dist >= 0) & (dist <= span), 0.0, NEG_INF)

    def run_units(units, dil, span, phase):
        windows = {}
        scores = []
        for q0, k0, first, wid in units:
            if wid not in windows:
                k = k_ref[rows(k0, kw, dil), :].astype(BF16)
                v = v_ref[rows(k0, kw, dil), :].astype(BF16)
                windows[wid] = (k, jnp.concatenate([v, ones], axis=1))
            q = (q_ref[rows(q0, ch, dil), :] * qscale).astype(BF16)
            scores.append(_dot_nt(q, windows[wid][0]) + band_bias(first, span))
        for (q0, k0, first, wid), s in zip(units, scores):
            qrows = rows(q0, ch, dil)
            m_cur = jnp.max(s, axis=-1, keepdims=True)
            if phase == "first":
                m_new = jnp.broadcast_to(m_cur, (ch, LANE))
            else:
                m_old = m_ref[qrows, :]
                m_new = jnp.maximum(m_old, m_cur)
                alpha = jnp.exp2(m_old - m_new)
            p = jnp.exp2(s - jnp.concatenate([m_new] * (kw // LANE), axis=1)).astype(BF16)
            pv = _dot(p, windows[wid][1])
            acc, l = pv[:, 0:HEAD_DIM], pv[:, HEAD_DIM:2 * HEAD_DIM]
            if phase != "first":
                acc = alpha * acc_ref[qrows, :] + acc
                l = alpha * l_ref[qrows, :] + l
            if phase == "last":
                o_ref[pl.ds(pl.multiple_of(q0, ch), ch), :] = (acc / l).astype(o_ref.dtype)
            else:
                m_ref[qrows, :] = m_new
                acc_ref[qrows, :] = acc
                l_ref[qrows, :] = l

    for idx, (window, dil) in enumerate(patterns):
        phase = "first" if idx == 0 else ("last" if idx == len(patterns) - 1 else "mid")
        assert phase != "last" or dil == 1
        span = window // dil
        assert span <= ch
        n_units = seq // dil // ch
        per_class = min(n_units, max(DIL_UNITS // dil, 2))
        n_classes = DIL_UNITS // per_class
        assert n_units % per_class == 0 and dil % n_classes == 0

        def group(r0, j0, head, dil=dil, span=span, phase=phase, per_class=per_class, n_classes=n_classes):
            units = []
            for rc in range(n_classes):
                r = r0 + rc
                for u in range(per_class):
                    first = head and u == 0
                    q0 = r + dil * ch * (j0 + u)
                    k0 = r if (head and u <= 1) else r + dil * ch * (j0 + u - 1)
                    wid = (rc, 0) if (head and u <= 1) else (rc, u)
                    units.append((q0, k0, first, wid))
            run_units(units, dil, span, phase)

        def class_loop(g, carry, group=group, per_class=per_class, n_classes=n_classes, n_units=n_units):
            r0 = g * n_classes
            group(r0, 0, True)
            if n_units > per_class:
                def tail(jb, c):
                    group(r0, jb * per_class, False)
                    return c
                lax.fori_loop(1, n_units // per_class, tail, 0)
            return carry

        if dil // n_classes == 1:
            class_loop(0, 0)
        else:
            lax.fori_loop(0, dil // n_classes, class_loop, 0)


def dilated_attention(z, *, n_batch, seq):
    bw = B_HEADS * HEAD_DIM
    base = COL_B // HEAD_DIM
    blk = (seq, HEAD_DIM)
    return pl.pallas_call(
        _dilated_kernel,
        out_shape=jax.ShapeDtypeStruct((n_batch * seq, bw), BF16),
        grid=(n_batch, B_HEADS),
        in_specs=[
            pl.BlockSpec(blk, lambda b, h: (b, base + h)),
            pl.BlockSpec(blk, lambda b, h: (b, base + B_HEADS + h)),
            pl.BlockSpec(blk, lambda b, h: (b, base + 2 * B_HEADS + h)),
        ],
        out_specs=pl.BlockSpec(blk, lambda b, h: (b, h)),
        scratch_shapes=[pltpu.VMEM(blk, F32)] * 6,
        compiler_params=_cparams(("parallel", "parallel")),
        name="dilated_attention",
    )(z, z, z)


N_SUB = 256
SUB_W = CMP_STRIDE * HEAD_DIM


def _compress_kernel(x_in, pos_ref, w1_ref, w2_ref, o_ref, x_ref):
    x_ref[...] = x_in[...].astype(F32)
    p = jnp.zeros((N_SUB, HEAD_DIM), F32)
    q = jnp.zeros((N_SUB, HEAD_DIM), F32)
    for i in range(CMP_STRIDE):
        a = x_ref[pl.ds(i, N_SUB, stride=CMP_STRIDE), :]
        top = (a + pos_ref[i:i + 1, :]).astype(BF16)
        bot = (a + pos_ref[CMP_STRIDE + i:CMP_STRIDE + i + 1, :]).astype(BF16)
        p = p + _dot(top, w1_ref[i * HEAD_DIM:(i + 1) * HEAD_DIM, :])
        q = q + _dot(bot, w1_ref[(CMP_STRIDE + i) * HEAD_DIM:(CMP_STRIDE + i + 1) * HEAD_DIM, :])
    h = _gelu_tanh(p + pltpu.roll(q, N_SUB - 1, 0))
    o_ref[...] = _dot(h.astype(BF16), w2_ref[...]).astype(o_ref.dtype)


def compress(z, pos, w1, w2, layer, *, n_batch, seq):
    return pl.pallas_call(
        _compress_kernel,
        out_shape=jax.ShapeDtypeStruct((n_batch * 4 * N_SUB, HEAD_DIM), BF16),
        grid=(n_batch, 4),
        in_specs=[
            pl.BlockSpec((seq, HEAD_DIM), lambda b, n: (b, COL_CMP // HEAD_DIM + n)),
            pl.BlockSpec((CMP_LEN, HEAD_DIM), lambda b, n: (0, 0)),
            pl.BlockSpec((None, None, CMP_LEN * HEAD_DIM, HEAD_DIM), lambda b, n: (layer, n // 2, 0, 0)),
            pl.BlockSpec((None, None, HEAD_DIM, HEAD_DIM), lambda b, n: (layer, n // 2, 0, 0)),
        ],
        out_specs=pl.BlockSpec((N_SUB, HEAD_DIM), lambda b, n: (b * 4 + n, 0)),
        scratch_shapes=[pltpu.VMEM((seq, HEAD_DIM), F32)],
        compiler_params=_cparams(("parallel", "parallel")),
        name="nsa_compress",
    )(z, pos, w1, w2)


N_SEL = 64


def _csel_kernel(q_ref, kc_ref, vc_ref, ovt_ref, gate_ref, oc_ref, sel_ref, qs_ref, sc_ref):
    tq, ch = ATT_TQ, ATT_CH
    n_sub = tq // ch
    q0 = pl.program_id(2) * tq
    qscale = (HEAD_DIM ** -0.5) * LOG2E
    for r in range(C_GROUP):
        qs_ref[r * tq:(r + 1) * tq, :] = (q_ref[:, r * HEAD_DIM:(r + 1) * HEAD_DIM].astype(F32)
                                          * qscale).astype(BF16)
    s_all = _dot_nt(qs_ref[...], kc_ref[...])
    n_cols = lax.broadcasted_iota(jnp.int32, (ch, N_SUB), 1)
    bias, has_any = [], []
    for c in range(n_sub):
        t = q0 + c * ch + lax.broadcasted_iota(jnp.int32, (ch, N_SUB), 0)
        bias.append(jnp.where((n_cols * CMP_STRIDE + (CMP_LEN - 1)) <= t, 0.0, NEG_INF))
        has_any.append(t[:, 0:1] >= CMP_LEN - 1)
    ps = []
    psum = [jnp.zeros((ch, N_SUB), F32) for _ in range(n_sub)]
    for c in range(C_GROUP * n_sub):
        s = s_all[c * ch:(c + 1) * ch, :] + bias[c % n_sub]
        e = jnp.exp2(s - jnp.max(s, axis=-1, keepdims=True))
        den = jnp.sum(e, axis=-1, keepdims=True)
        p = e * jnp.where(has_any[c % n_sub], 1.0 / den, 0.0)
        ps.append(p.astype(BF16))
        psum[c % n_sub] = psum[c % n_sub] + p
    o_all = _dot(jnp.concatenate(ps, axis=0), vc_ref[...])
    head0 = pl.program_id(1) * C_GROUP
    for r in range(C_GROUP):
        gate = _gate_column(gate_ref, (head0 + r) * 3)
        oc_ref[:, r * HEAD_DIM:(r + 1) * HEAD_DIM] = gate * o_all[r * tq:(r + 1) * tq, :]
    psum = jnp.concatenate(psum, axis=0)
    hi = psum.astype(BF16)
    lo = (psum - hi.astype(F32)).astype(BF16)
    ovt = ovt_ref[...]
    imp_t = _dot_nt(ovt, hi) + _dot_nt(ovt, lo)
    jrow = lax.broadcasted_iota(jnp.int32, (LANE, tq), 0)
    tl = q0 + lax.broadcasted_iota(jnp.int32, (LANE, tq), 1)
    jt = tl >> 6
    forced = (jrow == 0) | (jrow == jt) | (jrow == jt - 1)
    valid_s = jrow * SEL_LEN <= tl
    score = jnp.where(forced, 1e4, jnp.where(valid_s, imp_t, -1.0))
    sc_ref[...] = score
    n_grp = N_SEL // SUBLANE
    grp = [sc_ref[g * SUBLANE:(g + 1) * SUBLANE, :] for g in range(n_grp)]
    cnt = [jnp.zeros((SUBLANE, tq), F32) for _ in range(n_grp)]
    sub = lax.broadcasted_iota(jnp.int32, (SUBLANE, tq), 0)
    for j in range(N_SEL):
        row = sc_ref[j:j + 1, :]
        for g in range(n_grp):
            if g < j // SUBLANE:
                beats = row > grp[g]
            elif g > j // SUBLANE:
                beats = row >= grp[g]
            else:
                beats = (row > grp[g]) | ((row == grp[g]) & (sub > j % SUBLANE))
            cnt[g] = cnt[g] + jnp.where(beats, 1.0, 0.0)
    sel_t = [jnp.where(c < float(SEL_TOP), 1.0, 0.0) for c in cnt]
    sel_pad = jnp.concatenate(sel_t + [jnp.zeros((LANE - N_SEL, tq), F32)], axis=0)
    sel_ref[...] = sel_pad.T


def compressed_select(z, kvc, ovt, *, n_batch, seq):
    tq = ATT_TQ
    nq = seq // tq
    qw = C_GROUP * HEAD_DIM
    return pl.pallas_call(
        _csel_kernel,
        out_shape=[jax.ShapeDtypeStruct((n_batch * seq, C_HEADS * HEAD_DIM), F32),
                   jax.ShapeDtypeStruct((n_batch * C_KV_HEADS * seq, LANE), F32)],
        grid=(n_batch, C_KV_HEADS, nq),
        in_specs=[
            pl.BlockSpec((tq, qw), lambda b, g, i: (b * nq + i, COL_CQ // qw + g)),
            pl.BlockSpec((N_SUB, HEAD_DIM), lambda b, g, i: (b * 4 + g, 0)),
            pl.BlockSpec((N_SUB, HEAD_DIM), lambda b, g, i: (b * 4 + 2 + g, 0)),
            pl.BlockSpec((LANE, N_SUB), lambda b, g, i: (0, 0)),
            pl.BlockSpec((tq, LANE), lambda b, g, i: (b * nq + i, COL_GATE // LANE)),
        ],
        out_specs=[
            pl.BlockSpec((tq, qw), lambda b, g, i: (b * nq + i, g)),
            pl.BlockSpec((tq, LANE), lambda b, g, i: ((b * C_KV_HEADS + g) * nq + i, 0)),
        ],
        scratch_shapes=[pltpu.VMEM((C_GROUP * tq, HEAD_DIM), BF16), pltpu.VMEM((LANE, tq), F32)],
        compiler_params=_cparams(("parallel", "parallel", "parallel")),
        name="nsa_compressed_select",
    )(z, kvc, kvc, ovt, z)


def _overlap_t():
    n_c = N_SUB - 1
    c_start = np.arange(n_c) * CMP_STRIDE
    s_start = np.arange(N_SEL) * SEL_LEN
    ov = ((c_start[:, None] <= s_start[None, :] + SEL_LEN - 1)
          & (c_start[:, None] + CMP_LEN - 1 >= s_start[None, :])).astype(np.float32)
    out = np.zeros((LANE, N_SUB), np.float32)
    out[:N_SEL, :n_c] = ov.T
    return out


def _mix_out_kernel(x_ref, oa_ref, ob_ref, oc_ref, os_ref, ow_ref, w_ref, o_ref):
    mix = jnp.concatenate([oa_ref[...], ob_ref[...], (oc_ref[...] + os_ref[...] + ow_ref[...]).astype(BF16)],
                          axis=1)
    o_ref[...] = x_ref[...] + _dot(mix, w_ref[...])


def mix_out(x, out_a, out_b, oc, os_, ow, w, layer, *, tm):
    m, n = x.shape
    bw = B_HEADS * HEAD_DIM
    cw = C_HEADS * HEAD_DIM
    spec_c = pl.BlockSpec((tm, cw), lambda i: (i, 0))
    return pl.pallas_call(
        _mix_out_kernel,
        out_shape=jax.ShapeDtypeStruct((m, n), F32),
        grid=(m // tm,),
        in_specs=[
            pl.BlockSpec((tm, n), lambda i: (i, 0)),
            pl.BlockSpec((tm, A_WIDTH), lambda i: (i, 0)),
            pl.BlockSpec((tm, bw), lambda i: (i, 0)),
            spec_c, spec_c, spec_c,
            pl.BlockSpec((None,) + w.shape[1:], lambda i: (layer, 0, 0), pipeline_mode=pl.Buffered(1)),
        ],
        out_specs=pl.BlockSpec((tm, n), lambda i: (i, 0)),
        compiler_params=_cparams(("parallel",)),
        name="mix_out",
    )(x, out_a, out_b, oc, os_, ow, w)


def _xattn_kernel(x_ref, g_ref, wq_ref, k_ref, v_ref, wo_ref, o_ref, h_ref, a_ref):
    n_mem = k_ref.shape[0]
    _norm_into(h_ref, x_ref, g_ref)
    q = _dot(h_ref[...], wq_ref[...]) * ((HEAD_DIM ** -0.5) * LOG2E)
    ones = jnp.ones((n_mem, HEAD_DIM), BF16)
    for h in range(X_HEADS):
        cols = slice(h * HEAD_DIM, (h + 1) * HEAD_DIM)
        s = _dot_nt(q[:, cols].astype(BF16), k_ref[:, cols].astype(BF16))
        e = jnp.exp2(s - jnp.max(s, axis=-1, keepdims=True))
        pv = _dot(e.astype(BF16), jnp.concatenate([v_ref[:, cols].astype(BF16), ones], axis=1))
        a_ref[:, cols] = (pv[:, 0:HEAD_DIM] / pv[:, HEAD_DIM:2 * HEAD_DIM]).astype(BF16)
    o_ref[...] = x_ref[...] + _dot(a_ref[...], wo_ref[...])


def cross_attention_block(x, g, wq, kv, wo, layer, *, n_batch, seq, n_mem, tm):
    nq = seq // tm
    d = x.shape[1]
    return pl.pallas_call(
        _xattn_kernel,
        out_shape=jax.ShapeDtypeStruct(x.shape, F32),
        grid=(n_batch, nq),
        in_specs=[
            pl.BlockSpec((tm, d), lambda b, i: (b * nq + i, 0)),
            pl.BlockSpec((1, d), lambda b, i: (0, 0)),
            pl.BlockSpec((None, d, X_WIDTH), lambda b, i: (layer, 0, 0), pipeline_mode=pl.Buffered(1)),
            pl.BlockSpec((n_mem, X_WIDTH), lambda b, i: (b, 0)),
            pl.BlockSpec((n_mem, X_WIDTH), lambda b, i: (b, 1)),
            pl.BlockSpec((None, X_WIDTH, d), lambda b, i: (layer, 0, 0), pipeline_mode=pl.Buffered(1)),
        ],
        out_specs=pl.BlockSpec((tm, d), lambda b, i: (b * nq + i, 0)),
        scratch_shapes=[pltpu.VMEM((tm, d), BF16), pltpu.VMEM((tm, X_WIDTH), BF16)],
        compiler_params=_cparams(("parallel", "parallel")),
        name="cross_attention",
    )(x, g.reshape(1, d), wq, kv, kv, wo)


def kernel(x, mem, norm_mix, w_in, gmlp_ln_g, gmlp_ln_b, gmlp_w_s, gmlp_b_s, cmp_pos, cmp_k_w1, cmp_k_w2,
           cmp_v_w1, cmp_v_w2, w_out, norm_xattn, norm_mem, xattn_wq, xattn_wkv, xattn_wo, norm_mlp, w_up,
           w_down, final_norm):
    bsz, seq, d = x.shape
    n_mem = mem.shape[1]
    t = bsz * seq
    assert d == D_MODEL and seq // CMP_STRIDE == N_SUB and seq // SEL_LEN == N_SEL
    assert seq % (ATT_TQ * 16) == 0

    w_in_b = jnp.pad(w_in, ((0, 0), (0, 0), (0, IN_PAD - IN_WIDTH))).astype(BF16)
    w_out_b = w_out.astype(BF16)
    wq_b = xattn_wq.astype(BF16)
    wkv_b = xattn_wkv.astype(BF16)
    wo_b = xattn_wo.astype(BF16)
    w_up_b = w_up.astype(BF16)
    w_down_b = w_down.astype(BF16)
    cmp_w1 = jnp.stack([cmp_k_w1, cmp_v_w1], axis=1).astype(BF16)
    cmp_w2 = jnp.stack([cmp_k_w2, cmp_v_w2], axis=1).astype(BF16)
    ovt = jnp.asarray(_overlap_t()).astype(BF16)
    xf = x.reshape(t, d)
    memf = mem.reshape(bsz * n_mem, d)
    gw = C_GROUP * HEAD_DIM

    for l in range(DEPTH):
        z = norm_matmul(xf, norm_mix[l], w_in_b, l, tm=512, tn=IN_PAD, out_dtype=BF16)

        out_a = gmlp(z, gmlp_ln_g[l], gmlp_ln_b[l], gmlp_w_s[l], gmlp_b_s[l], tm=512)

        out_b = dilated_attention(z, n_batch=bsz, seq=seq)

        kvc = compress(z, cmp_pos[l], cmp_w1, cmp_w2, l, n_batch=bsz, seq=seq)
        o_c, sel = compressed_select(z, kvc, ovt, n_batch=bsz, seq=seq)
        o_s = flash_attention(
            z, sel, n_batch=bsz, n_grp=C_KV_HEADS, n_rep=C_GROUP, seq=seq,
            q_col=lambda n: COL_CQ // gw + n,
            k_col=lambda n: COL_SEL // HEAD_DIM + n,
            v_col=lambda n: COL_SEL // HEAD_DIM + C_KV_HEADS + n,
            mode="select", branch=1)
        o_w = flash_attention(
            z, None, n_batch=bsz, n_grp=C_KV_HEADS, n_rep=C_GROUP, seq=seq,
            q_col=lambda n: COL_CQ // gw + n,
            k_col=lambda n: COL_WIN // HEAD_DIM + n,
            v_col=lambda n: COL_WIN // HEAD_DIM + C_KV_HEADS + n,
            mode="band", branch=2, max_dist=WIN_LEN - 1)

        xf = mix_out(xf, out_a, out_b, o_c, o_s, o_w, w_out_b, l, tm=512)

        kv = norm_matmul(memf, norm_mem[l], wkv_b, l, tm=512, tn=2 * X_WIDTH)
        xf = cross_attention_block(xf, norm_xattn[l], wq_b, kv, wo_b, l, n_batch=bsz, seq=seq, n_mem=n_mem,
                                   tm=512)

        xf = mlp(xf, norm_mlp[l], w_up_b, w_down_b, l, final_norm, tm=1024, tf=512,
                 norm_out=(l == DEPTH - 1))

    return xf.reshape(bsz, seq, d)
```

```python
import functools
import math

import numpy as np
import jax
import jax.numpy as jnp
from jax import lax
from jax.experimental import pallas as pl
from jax.experimental.pallas import tpu as pltpu

F32 = jnp.float32
BF16 = jnp.bfloat16

D_MODEL = 2048
DEPTH = 4
HEAD_DIM = 128
A_GROUPS = 4
A_WIDTH = A_GROUPS * HEAD_DIM
CHUNK = 128
B_HEADS = 4
DILATED_PAIRS = ((128, 1), (512, 4), (2048, 16))
C_HEADS = 8
C_KV_HEADS = 2
C_GROUP = C_HEADS // C_KV_HEADS
CMP_LEN = 32
CMP_STRIDE = 16
SEL_LEN = 64
SEL_TOP = 16
WIN_LEN = 512
X_HEADS = 4
X_WIDTH = X_HEADS * HEAD_DIM
EPS = 1e-6
NEG_INF = -1e30

COL_A = 0
COL_B = 2 * A_WIDTH
COL_CQ = COL_B + 3 * B_HEADS * HEAD_DIM
COL_CMP = COL_CQ + C_HEADS * HEAD_DIM
COL_SEL = COL_CMP + 2 * C_KV_HEADS * HEAD_DIM
COL_WIN = COL_SEL + 2 * C_KV_HEADS * HEAD_DIM
COL_GATE = COL_WIN + 2 * C_KV_HEADS * HEAD_DIM
IN_WIDTH = COL_GATE + 3 * C_HEADS
IN_PAD = 5248

LANE = 128
SUBLANE = 8
VMEM_LIMIT = 56 * 1024 * 1024

ATT_TQ = 256
ATT_TK = 256
ATT_CH = 128
SEL_UNROLL = 4
LOG2E = math.log2(math.e)
LN2 = math.log(2.0)


def _cparams(sem):
    return pltpu.CompilerParams(dimension_semantics=sem, vmem_limit_bytes=VMEM_LIMIT)


def _rms_rows(x, g):
    ms = jnp.mean(x * x, axis=-1, keepdims=True)
    return x * lax.rsqrt(ms + EPS) * g


def _gelu_tanh(x):
    c = math.sqrt(2.0 / math.pi)
    return x * (0.5 * (1.0 + jnp.tanh(c * (x + 0.044715 * (x * x * x)))))


def _dot(a, b):
    return jnp.dot(a, b, preferred_element_type=F32)


def _dot_nt(a, b):
    return lax.dot_general(a, b, (((1,), (1,)), ((), ())), preferred_element_type=F32)


NORM_ROWS = 256


def _norm_into(h_ref, x_ref, g_ref):
    rows = x_ref.shape[0]

    def body(c, carry):
        r0 = pl.multiple_of(c * NORM_ROWS, NORM_ROWS)
        x = x_ref[pl.ds(r0, NORM_ROWS), :]
        h_ref[pl.ds(r0, NORM_ROWS), :] = _rms_rows(x, g_ref[...]).astype(h_ref.dtype)
        return carry

    lax.fori_loop(0, rows // NORM_ROWS, body, 0)


def _norm_matmul_kernel(x_ref, g_ref, w_ref, o_ref, h_ref):
    @pl.when(pl.program_id(1) == 0)
    def _():
        _norm_into(h_ref, x_ref, g_ref)

    o_ref[...] = _dot(h_ref[...], w_ref[...]).astype(o_ref.dtype)


def norm_matmul(x, g, w, layer, *, tm, tn, out_dtype=F32):
    m, k = x.shape
    n = w.shape[2]
    return pl.pallas_call(
        _norm_matmul_kernel,
        out_shape=jax.ShapeDtypeStruct((m, n), out_dtype),
        grid=(m // tm, n // tn),
        in_specs=[
            pl.BlockSpec((tm, k), lambda i, j: (i, 0)),
            pl.BlockSpec((1, k), lambda i, j: (0, 0)),
            pl.BlockSpec((None, k, tn), lambda i, j: (layer, 0, j),
                         pipeline_mode=pl.Buffered(1) if tn == n else None),
        ],
        out_specs=pl.BlockSpec((tm, tn), lambda i, j: (i, j)),
        scratch_shapes=[pltpu.VMEM((tm, k), BF16)],
        compiler_params=_cparams(("parallel", "arbitrary")),
        name="norm_matmul",
    )(x, g.reshape(1, k), w)


def _res_matmul_kernel(*refs, n_parts):
    x_ref = refs[0]
    a_refs = refs[1:1 + n_parts]
    w_refs = refs[1 + n_parts:1 + 2 * n_parts]
    o_ref = refs[1 + 2 * n_parts]
    acc = x_ref[...]
    for a_ref, w_ref in zip(a_refs, w_refs):
        acc = acc + _dot(a_ref[...].astype(BF16), w_ref[...])
    o_ref[...] = acc


def res_matmul(x, parts, weights, *, tm, tn):
    m, n = x.shape
    n_parts = len(parts)
    in_specs = [pl.BlockSpec((tm, tn), lambda i, j: (i, j))]
    for a in parts:
        in_specs.append(pl.BlockSpec((tm, a.shape[1]), lambda i, j: (i, 0)))
    for w in weights:
        in_specs.append(pl.BlockSpec((w.shape[0], tn), lambda i, j: (0, j)))
    return pl.pallas_call(
        functools.partial(_res_matmul_kernel, n_parts=n_parts),
        out_shape=jax.ShapeDtypeStruct((m, n), F32),
        grid=(m // tm, n // tn),
        in_specs=in_specs,
        out_specs=pl.BlockSpec((tm, tn), lambda i, j: (i, j)),
        compiler_params=_cparams(("parallel", "parallel")),
        name="res_matmul",
    )(x, *parts, *weights)


def _mlp_kernel(x_ref, g_ref, wu_ref, wd_ref, gout_ref, o_ref, h_ref, *, norm_out):
    f = pl.program_id(1)

    @pl.when(f == 0)
    def _():
        _norm_into(h_ref, x_ref, g_ref)
        o_ref[...] = x_ref[...]

    a = _dot(h_ref[...], wu_ref[...])
    a = jnp.square(jnp.maximum(a, 0.0)).astype(BF16)
    o_ref[...] += _dot(a, wd_ref[...])

    if norm_out:
        @pl.when(f == pl.num_programs(1) - 1)
        def _():
            _norm_into(o_ref, o_ref, gout_ref)


def mlp(x, g, wu, wd, layer, g_out, *, tm, tf, norm_out):
    m, d = x.shape
    ff = wu.shape[2]
    return pl.pallas_call(
        functools.partial(_mlp_kernel, norm_out=norm_out),
        out_shape=jax.ShapeDtypeStruct((m, d), F32),
        grid=(m // tm, ff // tf),
        in_specs=[
            pl.BlockSpec((tm, d), lambda i, f: (i, 0)),
            pl.BlockSpec((1, d), lambda i, f: (0, 0)),
            pl.BlockSpec((None, d, tf), lambda i, f: (layer, 0, f)),
            pl.BlockSpec((None, tf, d), lambda i, f: (layer, f, 0)),
            pl.BlockSpec((1, d), lambda i, f: (0, 0)),
        ],
        out_specs=pl.BlockSpec((tm, d), lambda i, f: (i, 0)),
        scratch_shapes=[pltpu.VMEM((tm, d), BF16)],
        compiler_params=_cparams(("parallel", "arbitrary")),
        name="mlp",
    )(x, g.reshape(1, d), wu, wd, g_out.reshape(1, d))


def _rmsnorm_kernel(x_ref, g_ref, o_ref):
    o_ref[...] = _rms_rows(x_ref[...], g_ref[...])


def rmsnorm(x, g, *, tm):
    m, d = x.shape
    return pl.pallas_call(
        _rmsnorm_kernel,
        out_shape=jax.ShapeDtypeStruct((m, d), F32),
        grid=(m // tm,),
        in_specs=[pl.BlockSpec((tm, d), lambda i: (i, 0)),
                  pl.BlockSpec((1, d), lambda i: (0, 0))],
        out_specs=pl.BlockSpec((tm, d), lambda i: (i, 0)),
        compiler_params=_cparams(("parallel",)),
        name="final_rmsnorm",
    )(x, g.reshape(1, d))


def _gmlp_kernel(z_ref, lng_ref, lnb_ref, ws_ref, bst_ref, o_ref, *, n_chunks):
    row = lax.broadcasted_iota(jnp.int32, (CHUNK, CHUNK), 0)
    col = lax.broadcasted_iota(jnp.int32, (CHUNK, CHUNK), 1)
    causal = row >= col
    w_tril = [jnp.where(causal, ws_ref[gi], 0.0).astype(BF16) for gi in range(A_GROUPS)]
    for c in range(n_chunks):
        rows = slice(c * CHUNK, (c + 1) * CHUNK)
        u = _gelu_tanh(z_ref[rows, 0:A_WIDTH].astype(F32))
        v = _gelu_tanh(z_ref[rows, A_WIDTH:2 * A_WIDTH].astype(F32))
        vc = v - jnp.mean(v, axis=-1, keepdims=True)
        vn = vc * lax.rsqrt(jnp.mean(vc * vc, axis=-1, keepdims=True) + EPS)
        vn = vn * lng_ref[...] + lnb_ref[...]
        for gi in range(A_GROUPS):
            cols = slice(gi * HEAD_DIM, (gi + 1) * HEAD_DIM)
            sv = _dot(w_tril[gi], vn[:, cols].astype(BF16)) + bst_ref[:, gi:gi + 1]
            o_ref[rows, cols] = (u[:, cols] * sv).astype(o_ref.dtype)


def gmlp(z, ln_g, ln_b, w_s, b_s, *, tm):
    t = z.shape[0]
    return pl.pallas_call(
        functools.partial(_gmlp_kernel, n_chunks=tm // CHUNK),
        out_shape=jax.ShapeDtypeStruct((t, A_WIDTH), BF16),
        grid=(t // tm,),
        in_specs=[
            pl.BlockSpec((tm, 2 * A_WIDTH), lambda i: (i, 0)),
            pl.BlockSpec((1, A_WIDTH), lambda i: (0, 0)),
            pl.BlockSpec((1, A_WIDTH), lambda i: (0, 0)),
            pl.BlockSpec((A_GROUPS, CHUNK, CHUNK), lambda i: (0, 0, 0)),
            pl.BlockSpec((CHUNK, A_GROUPS), lambda i: (0, 0)),
        ],
        out_specs=pl.BlockSpec((tm, A_WIDTH), lambda i: (i, 0)),
        compiler_params=_cparams(("parallel",)),
        name="gmlp_gating",
    )(z, ln_g.reshape(1, A_WIDTH), ln_b.reshape(1, A_WIDTH), w_s, b_s.T)


def _gate_column(gate_ref, col):
    g = 1.0 / (1.0 + jnp.exp(-gate_ref[...].astype(F32)))
    lane = lax.broadcasted_iota(jnp.int32, g.shape, 1)
    return jnp.sum(jnp.where(lane == col, g, 0.0), axis=-1, keepdims=True)


def _flash_kernel(q_ref, k_ref, v_ref, sel_ref, gate_ref, o_ref, qs_ref, kb_ref, vb_ref, m_ref, accl_ref, *,
                  n_rep, mode, max_dist, branch):

    tq, tk, ch = ATT_TQ, ATT_TK, ATT_CH
    seq = k_ref.shape[0]
    i = pl.program_id(2)
    select = mode == "select"

    @pl.when(i == 0)
    def _():
        def prep(c, carry):
            r0 = pl.multiple_of(c * tk, tk)
            kb_ref[pl.ds(r0, tk), 0:HEAD_DIM] = k_ref[pl.ds(r0, tk), :].astype(BF16)
            if select:
                blk = (r0 + lax.broadcasted_iota(jnp.int32, (tk, LANE), 0)) >> 6
                lane = lax.broadcasted_iota(jnp.int32, (tk, LANE), 1)
                kb_ref[pl.ds(r0, tk), HEAD_DIM:HEAD_DIM + LANE] = jnp.where(blk == lane, 1.0, 0.0).astype(BF16)
            vb_ref[pl.ds(r0, tk), 0:HEAD_DIM] = v_ref[pl.ds(r0, tk), :].astype(BF16)
            vb_ref[pl.ds(r0, tk), HEAD_DIM:2 * HEAD_DIM] = jnp.ones((tk, HEAD_DIM), BF16)
            return carry

        lax.fori_loop(0, seq // tk, prep, 0)

    qscale = (HEAD_DIM ** -0.5) * LOG2E
    if select:
        pen = jnp.where(sel_ref[...] > 0.5, 0.0, NEG_INF).astype(BF16)
    for r in range(n_rep):
        rows = slice(r * tq, (r + 1) * tq)
        qs_ref[rows, 0:HEAD_DIM] = (q_ref[:, r * HEAD_DIM:(r + 1) * HEAD_DIM].astype(F32) * qscale).astype(BF16)
        if select:
            qs_ref[rows, HEAD_DIM:HEAD_DIM + LANE] = pen
    m_ref[...] = jnp.full(m_ref.shape, NEG_INF, F32)
    accl_ref[...] = jnp.zeros(accl_ref.shape, F32)

    row = lax.broadcasted_iota(jnp.int32, (tq, tk), 0)
    col = lax.broadcasted_iota(jnp.int32, (tq, tk), 1)

    def tile_bias(off):
        dist = off * tk + row - col
        if off == 0:
            ok = dist >= 0
            if not select and max_dist < tk - 1:
                ok = ok & (dist <= max_dist)
        elif not select and (off + 1) * tk - 1 > max_dist:
            ok = dist <= max_dist
        else:
            return None
        return jnp.where(ok, 0.0, NEG_INF)

    n_ch = n_rep * tq // ch

    def steps(tiles):
        scores = []
        for kt, bias in tiles:
            k0 = pl.multiple_of(kt * tk, tk)
            scores.append((_dot_nt(qs_ref[...], kb_ref[pl.ds(k0, tk), :]), k0, bias))
        for s_all, k0, bias in scores:
            ps, alphas = [], []
            for c in range(n_ch):
                rows = slice(c * ch, (c + 1) * ch)
                s = s_all[rows, :]
                if bias is not None:
                    b0 = (c % (tq // ch)) * ch
                    s = s + bias[b0:b0 + ch, :]
                m_old = m_ref[rows, :]
                m_new = jnp.maximum(m_old, jnp.max(s, axis=-1, keepdims=True))
                alphas.append(jnp.exp2(m_old - m_new))
                ps.append(jnp.exp2((s - jnp.concatenate([m_new] * (tk // LANE), axis=1)).astype(BF16)))
                m_ref[rows, :] = m_new
            pv_all = _dot(jnp.concatenate(ps, axis=0), vb_ref[pl.ds(k0, tk), :])
            for c in range(n_ch):
                rows = slice(c * ch, (c + 1) * ch)
                alpha2 = jnp.concatenate([alphas[c], alphas[c]], axis=1)
                accl_ref[rows, :] = alpha2 * accl_ref[rows, :] + pv_all[rows, :]

    if select:
        def body(j, carry):
            steps([(SEL_UNROLL * j + u, None) for u in range(SEL_UNROLL)])
            return carry

        lax.fori_loop(0, i // SEL_UNROLL, body, 0)

        for rem in range(SEL_UNROLL):
            @pl.when(i % SEL_UNROLL == rem)
            def _(rem=rem):
                steps([(i - rem + u, None) for u in range(rem)] + [(i, tile_bias(0))])
    else:
        n_prev = -(-max_dist // tk)

        @pl.when(i >= n_prev)
        def _():
            steps([(i - off, tile_bias(off)) for off in range(n_prev, -1, -1)])

        for first in range(n_prev):
            @pl.when(i == first)
            def _(first=first):
                steps([(first - off, tile_bias(off)) for off in range(first, -1, -1)])

    head0 = pl.program_id(1) * n_rep
    for r in range(n_rep):
        rows = slice(r * tq, (r + 1) * tq)
        cols = slice(r * HEAD_DIM, (r + 1) * HEAD_DIM)
        gate = _gate_column(gate_ref, (head0 + r) * 3 + branch)
        o = accl_ref[rows, 0:HEAD_DIM] / accl_ref[rows, HEAD_DIM:2 * HEAD_DIM]
        o_ref[:, cols] = (gate * o).astype(o_ref.dtype)


def _flash_band_kernel(q_ref, k_ref, v_ref, gate_ref, o_ref, *scratch, **static):
    _flash_kernel(q_ref, k_ref, v_ref, None, gate_ref, o_ref, *scratch, **static)


def flash_attention(z, sel_arr, *, n_batch, n_grp, n_rep, seq, q_col, k_col, v_col, mode, branch, max_dist=0):
    tq = ATT_TQ
    nq = seq // tq
    qw = n_rep * HEAD_DIM
    kw = HEAD_DIM + (LANE if mode == "select" else 0)
    in_specs = [
        pl.BlockSpec((tq, qw), lambda b, n, i: (b * nq + i, q_col(n))),
        pl.BlockSpec((seq, HEAD_DIM), lambda b, n, i: (b, k_col(n))),
        pl.BlockSpec((seq, HEAD_DIM), lambda b, n, i: (b, v_col(n))),
    ]
    args = [z, z, z]
    if mode == "select":
        in_specs.append(pl.BlockSpec((tq, LANE), lambda b, n, i: ((b * n_grp + n) * nq + i, 0)))
        args.append(sel_arr)
    in_specs.append(pl.BlockSpec((tq, LANE), lambda b, n, i: (b * nq + i, COL_GATE // LANE)))
    args.append(z)
    body = _flash_kernel if mode == "select" else _flash_band_kernel
    return pl.pallas_call(
        functools.partial(body, n_rep=n_rep, mode=mode, max_dist=max_dist, branch=branch),
        out_shape=jax.ShapeDtypeStruct((n_batch * seq, n_grp * qw), F32),
        grid=(n_batch, n_grp, nq),
        in_specs=in_specs,
        out_specs=pl.BlockSpec((tq, qw), lambda b, n, i: (b * nq + i, n)),
        scratch_shapes=[
            pltpu.VMEM((n_rep * tq, kw), BF16),
            pltpu.VMEM((seq, kw), BF16),
            pltpu.VMEM((seq, 2 * HEAD_DIM), BF16),
            pltpu.VMEM((n_rep * tq, LANE), F32),
            pltpu.VMEM((n_rep * tq, 2 * HEAD_DIM), F32),
        ],
        compiler_params=_cparams(("parallel", "parallel", "arbitrary")),
        name="flash_" + mode,
    )(*args)


DIL_CH = 128
DIL_KW = 2 * DIL_CH
DIL_UNITS = 8


def _dilated_kernel(q_in, k_in, v_in, o_ref, q_ref, k_ref, v_ref, m_ref, acc_ref, l_ref):
    ch, kw = DIL_CH, DIL_KW
    seq = q_in.shape[0]
    for src, dst in ((q_in, q_ref), (k_in, k_ref), (v_in, v_ref)):
        def stage(c, carry, src=src, dst=dst):
            r0 = pl.multiple_of(c * NORM_ROWS, NORM_ROWS)
            dst[pl.ds(r0, NORM_ROWS), :] = src[pl.ds(r0, NORM_ROWS), :].astype(F32)
            return carry
        lax.fori_loop(0, seq // NORM_ROWS, stage, 0)
    qscale = (HEAD_DIM ** -0.5) * LOG2E
    row = lax.broadcasted_iota(jnp.int32, (ch, kw), 0)
    col = lax.broadcasted_iota(jnp.int32, (ch, kw), 1)
    ones = jnp.ones((kw, HEAD_DIM), BF16)
    patterns = sorted(DILATED_PAIRS, key=lambda wd: -wd[1])

    def rows(start, n, dil):
        return pl.ds(start, n) if dil == 1 else pl.ds(start, n, stride=dil)

    def band_bias(first, span):
        dist = row - col if first else ch + row - col
        return jnp.where((dist >= 0) & (dist <= span), 0.0, NEG_INF)

    def run_units(units, dil, span, phase):
        windows = {}
        scores = []
        for q0, k0, first, wid in units:
            if wid not in windows:
                k = k_ref[rows(k0, kw, dil), :].astype(BF16)
                v = v_ref[rows(k0, kw, dil), :].astype(BF16)
                windows[wid] = (k, jnp.concatenate([v, ones], axis=1))
            q = (q_ref[rows(q0, ch, dil), :] * qscale).astype(BF16)
            scores.append(_dot_nt(q, windows[wid][0]) + band_bias(first, span))
        for (q0, k0, first, wid), s in zip(units, scores):
            qrows = rows(q0, ch, dil)
            m_cur = jnp.max(s, axis=-1, keepdims=True)
            if phase == "first":
                m_new = jnp.broadcast_to(m_cur, (ch, LANE))
            else:
                m_old = m_ref[qrows, :]
                m_new = jnp.maximum(m_old, m_cur)
                alpha = jnp.exp2(m_old - m_new)
            p = jnp.exp2((s - jnp.concatenate([m_new] * (kw // LANE), axis=1)).astype(BF16))
            pv = _dot(p, windows[wid][1])
            acc, l = pv[:, 0:HEAD_DIM], pv[:, HEAD_DIM:2 * HEAD_DIM]
            if phase != "first":
                acc = alpha * acc_ref[qrows, :] + acc
                l = alpha * l_ref[qrows, :] + l
            if phase == "last":
                o_ref[pl.ds(pl.multiple_of(q0, ch), ch), :] = (acc / l).astype(o_ref.dtype)
            else:
                m_ref[qrows, :] = m_new
                acc_ref[qrows, :] = acc
                l_ref[qrows, :] = l

    for idx, (window, dil) in enumerate(patterns):
        phase = "first" if idx == 0 else ("last" if idx == len(patterns) - 1 else "mid")
        assert phase != "last" or dil == 1
        span = window // dil
        assert span <= ch
        n_units = seq // dil // ch
        per_class = min(n_units, max(DIL_UNITS // dil, 2))
        n_classes = DIL_UNITS // per_class
        assert n_units % per_class == 0 and dil % n_classes == 0

        def group(r0, j0, head, dil=dil, span=span, phase=phase, per_class=per_class, n_classes=n_classes):
            units = []
            for rc in range(n_classes):
                r = r0 + rc
                for u in range(per_class):
                    first = head and u == 0
                    q0 = r + dil * ch * (j0 + u)
                    k0 = r if (head and u <= 1) else r + dil * ch * (j0 + u - 1)
                    wid = (rc, 0) if (head and u <= 1) else (rc, u)
                    units.append((q0, k0, first, wid))
            run_units(units, dil, span, phase)

        def class_loop(g, carry, group=group, per_class=per_class, n_classes=n_classes, n_units=n_units):
            r0 = g * n_classes
            group(r0, 0, True)
            if n_units > per_class:
                def tail(jb, c):
                    group(r0, jb * per_class, False)
                    return c
                lax.fori_loop(1, n_units // per_class, tail, 0)
            return carry

        if dil // n_classes == 1:
            class_loop(0, 0)
        else:
            lax.fori_loop(0, dil // n_classes, class_loop, 0)


def dilated_attention(z, *, n_batch, seq):
    bw = B_HEADS * HEAD_DIM
    base = COL_B // HEAD_DIM
    blk = (seq, HEAD_DIM)
    return pl.pallas_call(
        _dilated_kernel,
        out_shape=jax.ShapeDtypeStruct((n_batch * seq, bw), BF16),
        grid=(n_batch, B_HEADS),
        in_specs=[
            pl.BlockSpec(blk, lambda b, h: (b, base + h)),
            pl.BlockSpec(blk, lambda b, h: (b, base + B_HEADS + h)),
            pl.BlockSpec(blk, lambda b, h: (b, base + 2 * B_HEADS + h)),
        ],
        out_specs=pl.BlockSpec(blk, lambda b, h: (b, h)),
        scratch_shapes=[pltpu.VMEM(blk, F32)] * 6,
        compiler_params=_cparams(("parallel", "parallel")),
        name="dilated_attention",
    )(z, z, z)


N_SUB = 256
SUB_W = CMP_STRIDE * HEAD_DIM


def _compress_kernel(x_in, pos_ref, w1_ref, w2_ref, o_ref, x_ref):
    x_ref[...] = x_in[...].astype(F32)
    p = jnp.zeros((N_SUB, HEAD_DIM), F32)
    q = jnp.zeros((N_SUB, HEAD_DIM), F32)
    for i in range(CMP_STRIDE):
        a = x_ref[pl.ds(i, N_SUB, stride=CMP_STRIDE), :]
        top = (a + pos_ref[i:i + 1, :]).astype(BF16)
        bot = (a + pos_ref[CMP_STRIDE + i:CMP_STRIDE + i + 1, :]).astype(BF16)
        p = p + _dot(top, w1_ref[i * HEAD_DIM:(i + 1) * HEAD_DIM, :])
        q = q + _dot(bot, w1_ref[(CMP_STRIDE + i) * HEAD_DIM:(CMP_STRIDE + i + 1) * HEAD_DIM, :])
    h = _gelu_tanh(p + pltpu.roll(q, N_SUB - 1, 0))
    o_ref[...] = _dot(h.astype(BF16), w2_ref[...]).astype(o_ref.dtype)


def compress(z, pos, w1, w2, layer, *, n_batch, seq):
    return pl.pallas_call(
        _compress_kernel,
        out_shape=jax.ShapeDtypeStruct((n_batch * 4 * N_SUB, HEAD_DIM), BF16),
        grid=(n_batch, 4),
        in_specs=[
            pl.BlockSpec((seq, HEAD_DIM), lambda b, n: (b, COL_CMP // HEAD_DIM + n)),
            pl.BlockSpec((CMP_LEN, HEAD_DIM), lambda b, n: (0, 0)),
            pl.BlockSpec((None, None, CMP_LEN * HEAD_DIM, HEAD_DIM), lambda b, n: (layer, n // 2, 0, 0)),
            pl.BlockSpec((None, None, HEAD_DIM, HEAD_DIM), lambda b, n: (layer, n // 2, 0, 0)),
        ],
        out_specs=pl.BlockSpec((N_SUB, HEAD_DIM), lambda b, n: (b * 4 + n, 0)),
        scratch_shapes=[pltpu.VMEM((seq, HEAD_DIM), F32)],
        compiler_params=_cparams(("parallel", "parallel")),
        name="nsa_compress",
    )(z, pos, w1, w2)


N_SEL = 64


def _csel_kernel(q_ref, kc_ref, vc_ref, ovt_ref, gate_ref, oc_ref, sel_ref, qs_ref, sc_ref):
    tq, ch = ATT_TQ, ATT_CH
    n_sub = tq // ch
    q0 = pl.program_id(2) * tq
    qscale = (HEAD_DIM ** -0.5) * LOG2E
    for r in range(C_GROUP):
        qs_ref[r * tq:(r + 1) * tq, :] = (q_ref[:, r * HEAD_DIM:(r + 1) * HEAD_DIM].astype(F32)
                                          * qscale).astype(BF16)
    s_all = _dot_nt(qs_ref[...], kc_ref[...])
    n_cols = lax.broadcasted_iota(jnp.int32, (ch, N_SUB), 1)
    bias, has_any = [], []
    for c in range(n_sub):
        t = q0 + c * ch + lax.broadcasted_iota(jnp.int32, (ch, N_SUB), 0)
        bias.append(jnp.where((n_cols * CMP_STRIDE + (CMP_LEN - 1)) <= t, 0.0, NEG_INF))
        has_any.append(t[:, 0:1] >= CMP_LEN - 1)
    ps = []
    psum = [jnp.zeros((ch, N_SUB), F32) for _ in range(n_sub)]
    for c in range(C_GROUP * n_sub):
        s = s_all[c * ch:(c + 1) * ch, :] + bias[c % n_sub]
        e = jnp.exp2(s - jnp.max(s, axis=-1, keepdims=True))
        den = jnp.sum(e, axis=-1, keepdims=True)
        p = e * jnp.where(has_any[c % n_sub], 1.0 / den, 0.0)
        ps.append(p.astype(BF16))
        psum[c % n_sub] = psum[c % n_sub] + p
    o_all = _dot(jnp.concatenate(ps, axis=0), vc_ref[...])
    head0 = pl.program_id(1) * C_GROUP
    for r in range(C_GROUP):
        gate = _gate_column(gate_ref, (head0 + r) * 3)
        oc_ref[:, r * HEAD_DIM:(r + 1) * HEAD_DIM] = gate * o_all[r * tq:(r + 1) * tq, :]
    psum = jnp.concatenate(psum, axis=0)
    hi = psum.astype(BF16)
    lo = (psum - hi.astype(F32)).astype(BF16)
    ovt = ovt_ref[...]
    imp_t = _dot_nt(ovt, hi) + _dot_nt(ovt, lo)
    jrow = lax.broadcasted_iota(jnp.int32, (LANE, tq), 0)
    tl = q0 + lax.broadcasted_iota(jnp.int32, (LANE, tq), 1)
    jt = tl >> 6
    forced = (jrow == 0) | (jrow == jt) | (jrow == jt - 1)
    valid_s = jrow * SEL_LEN <= tl
    score = jnp.where(forced, 1e4, jnp.where(valid_s, imp_t, -1.0))
    sc_ref[...] = score
    n_grp = N_SEL // SUBLANE
    grp = [sc_ref[g * SUBLANE:(g + 1) * SUBLANE, :] for g in range(n_grp)]
    cnt = [jnp.zeros((SUBLANE, tq), F32) for _ in range(n_grp)]
    sub = lax.broadcasted_iota(jnp.int32, (SUBLANE, tq), 0)
    for j in range(N_SEL):
        row = sc_ref[j:j + 1, :]
        for g in range(n_grp):
            if g < j // SUBLANE:
                beats = row > grp[g]
            elif g > j // SUBLANE:
                beats = row >= grp[g]
            else:
                beats = (row > grp[g]) | ((row == grp[g]) & (sub > j % SUBLANE))
            cnt[g] = cnt[g] + jnp.where(beats, 1.0, 0.0)
    sel_t = [jnp.where(c < float(SEL_TOP), 1.0, 0.0) for c in cnt]
    sel_pad = jnp.concatenate(sel_t + [jnp.zeros((LANE - N_SEL, tq), F32)], axis=0)
    sel_ref[...] = sel_pad.T


def compressed_select(z, kvc, ovt, *, n_batch, seq):
    tq = ATT_TQ
    nq = seq // tq
    qw = C_GROUP * HEAD_DIM
    return pl.pallas_call(
        _csel_kernel,
        out_shape=[jax.ShapeDtypeStruct((n_batch * seq, C_HEADS * HEAD_DIM), F32),
                   jax.ShapeDtypeStruct((n_batch * C_KV_HEADS * seq, LANE), F32)],
        grid=(n_batch, C_KV_HEADS, nq),
        in_specs=[
            pl.BlockSpec((tq, qw), lambda b, g, i: (b * nq + i, COL_CQ // qw + g)),
            pl.BlockSpec((N_SUB, HEAD_DIM), lambda b, g, i: (b * 4 + g, 0)),
            pl.BlockSpec((N_SUB, HEAD_DIM), lambda b, g, i: (b * 4 + 2 + g, 0)),
            pl.BlockSpec((LANE, N_SUB), lambda b, g, i: (0, 0)),
            pl.BlockSpec((tq, LANE), lambda b, g, i: (b * nq + i, COL_GATE // LANE)),
        ],
        out_specs=[
            pl.BlockSpec((tq, qw), lambda b, g, i: (b * nq + i, g)),
            pl.BlockSpec((tq, LANE), lambda b, g, i: ((b * C_KV_HEADS + g) * nq + i, 0)),
        ],
        scratch_shapes=[pltpu.VMEM((C_GROUP * tq, HEAD_DIM), BF16), pltpu.VMEM((LANE, tq), F32)],
        compiler_params=_cparams(("parallel", "parallel", "parallel")),
        name="nsa_compressed_select",
    )(z, kvc, kvc, ovt, z)


def _nsa_kernel(q_ref, kc_ref, vc_ref, ovt_ref, ks_ref, vs_ref, kw_ref, vw_ref, gate_ref, o_ref,
                qs_ref, kbs_ref, vbs_ref, kbw_ref, vbw_ref, ms_ref, as_ref, mw_ref, aw_ref, oc_ref, sc_ref):
    tq, tk, ch = ATT_TQ, ATT_TK, ATT_CH
    n_rep = C_GROUP
    n_sub = tq // ch
    n_ch = n_rep * n_sub
    seq = ks_ref.shape[0]
    i = pl.program_id(2)
    q0 = i * tq

    @pl.when(i == 0)
    def _():
        def prep(c, carry):
            rows = pl.ds(pl.multiple_of(c * tk, tk), tk)
            blk = (c * tk + lax.broadcasted_iota(jnp.int32, (tk, LANE), 0)) >> 6
            lane = lax.broadcasted_iota(jnp.int32, (tk, LANE), 1)
            ones = jnp.ones((tk, HEAD_DIM), BF16)
            kbs_ref[rows, 0:HEAD_DIM] = ks_ref[rows, :]
            kbs_ref[rows, HEAD_DIM:HEAD_DIM + LANE] = jnp.where(blk == lane, 1.0, 0.0).astype(BF16)
            vbs_ref[rows, 0:HEAD_DIM] = vs_ref[rows, :]
            vbs_ref[rows, HEAD_DIM:2 * HEAD_DIM] = ones
            kbw_ref[rows, :] = kw_ref[rows, :]
            vbw_ref[rows, 0:HEAD_DIM] = vw_ref[rows, :]
            vbw_ref[rows, HEAD_DIM:2 * HEAD_DIM] = ones
            return carry

        lax.fori_loop(0, seq // tk, prep, 0)

    qscale = (HEAD_DIM ** -0.5) * LOG2E
    for r in range(n_rep):
        qs_ref[r * tq:(r + 1) * tq, 0:HEAD_DIM] = (q_ref[:, r * HEAD_DIM:(r + 1) * HEAD_DIM].astype(F32)
                                                   * qscale).astype(BF16)

    s_all = _dot_nt(qs_ref[:, 0:HEAD_DIM], kc_ref[...])
    n_cols = lax.broadcasted_iota(jnp.int32, (ch, N_SUB), 1)
    bias, has_any = [], []
    for c in range(n_sub):
        t = q0 + c * ch + lax.broadcasted_iota(jnp.int32, (ch, N_SUB), 0)
        bias.append(jnp.where((n_cols * CMP_STRIDE + (CMP_LEN - 1)) <= t, 0.0, NEG_INF))
        has_any.append(t[:, 0:1] >= CMP_LEN - 1)
    ps = []
    psum = [jnp.zeros((ch, N_SUB), F32) for _ in range(n_sub)]
    for c in range(n_ch):
        s = s_all[c * ch:(c + 1) * ch, :] + bias[c % n_sub]
        e = jnp.exp2(s - jnp.max(s, axis=-1, keepdims=True))
        den = jnp.sum(e, axis=-1, keepdims=True)
        p = e * jnp.where(has_any[c % n_sub], 1.0 / den, 0.0)
        ps.append(p.astype(BF16))
        psum[c % n_sub] = psum[c % n_sub] + p
    oc_ref[...] = _dot(jnp.concatenate(ps, axis=0), vc_ref[...])
    psum = jnp.concatenate(psum, axis=0)

    hi = psum.astype(BF16)
    lo = (psum - hi.astype(F32)).astype(BF16)
    ovt = ovt_ref[...]
    imp_t = _dot_nt(ovt, hi) + _dot_nt(ovt, lo)
    jrow = lax.broadcasted_iota(jnp.int32, (LANE, tq), 0)
    tl = q0 + lax.broadcasted_iota(jnp.int32, (LANE, tq), 1)
    jt = tl >> 6
    forced = (jrow == 0) | (jrow == jt) | (jrow == jt - 1)
    valid_s = jrow * SEL_LEN <= tl
    sc_ref[...] = jnp.where(forced, 1e4, jnp.where(valid_s, imp_t, -1.0))
    n_grp = N_SEL // SUBLANE
    grp = [sc_ref[g * SUBLANE:(g + 1) * SUBLANE, :] for g in range(n_grp)]
    cnt = [jnp.zeros((SUBLANE, tq), F32) for _ in range(n_grp)]
    sub = lax.broadcasted_iota(jnp.int32, (SUBLANE, tq), 0)
    for j in range(N_SEL):
        row = sc_ref[j:j + 1, :]
        for g in range(n_grp):
            if g < j // SUBLANE:
                beats = row > grp[g]
            elif g > j // SUBLANE:
                beats = row >= grp[g]
            else:
                beats = (row > grp[g]) | ((row == grp[g]) & (sub > j % SUBLANE))
            cnt[g] = cnt[g] + jnp.where(beats, 1.0, 0.0)
    pen_t = [jnp.where(c < float(SEL_TOP), 0.0, NEG_INF) for c in cnt]
    pen_t = jnp.concatenate(pen_t + [jnp.full((LANE - N_SEL, tq), NEG_INF, F32)], axis=0)
    pen = pen_t.T.astype(BF16)
    for r in range(n_rep):
        qs_ref[r * tq:(r + 1) * tq, HEAD_DIM:HEAD_DIM + LANE] = pen

    for m_ref, accl_ref in ((ms_ref, as_ref), (mw_ref, aw_ref)):
        m_ref[...] = jnp.full(m_ref.shape, NEG_INF, F32)
        accl_ref[...] = jnp.zeros(accl_ref.shape, F32)

    row = lax.broadcasted_iota(jnp.int32, (tq, tk), 0)
    col = lax.broadcasted_iota(jnp.int32, (tq, tk), 1)

    def tile_bias(off, max_dist):
        dist = off * tk + row - col
        if off == 0:
            ok = dist >= 0
        elif max_dist is not None and (off + 1) * tk - 1 > max_dist:
            ok = dist <= max_dist
        else:
            return None
        return jnp.where(ok, 0.0, NEG_INF)

    def steps(tiles, load_q, kb_ref, vb_ref, m_ref, accl_ref):
        scores = []
        for kt, tbias in tiles:
            k0 = pl.multiple_of(kt * tk, tk)
            scores.append((_dot_nt(load_q(), kb_ref[pl.ds(k0, tk), :]), k0, tbias))
        for s_tile, k0, tbias in scores:
            pts, alphas = [], []
            for c in range(n_ch):
                rows = slice(c * ch, (c + 1) * ch)
                s = s_tile[rows, :]
                if tbias is not None:
                    b0 = (c % n_sub) * ch
                    s = s + tbias[b0:b0 + ch, :]
                m_old = m_ref[rows, :]
                m_new = jnp.maximum(m_old, jnp.max(s, axis=-1, keepdims=True))
                alphas.append(jnp.exp2(m_old - m_new))
                pts.append(jnp.exp2((s - jnp.concatenate([m_new] * (tk // LANE), axis=1)).astype(BF16)))
                m_ref[rows, :] = m_new
            pv = _dot(jnp.concatenate(pts, axis=0), vb_ref[pl.ds(k0, tk), :])
            for c in range(n_ch):
                rows = slice(c * ch, (c + 1) * ch)
                alpha2 = jnp.concatenate([alphas[c], alphas[c]], axis=1)
                accl_ref[rows, :] = alpha2 * accl_ref[rows, :] + pv[rows, :]

    def sel_steps(tiles):
        steps(tiles, lambda: qs_ref[...], kbs_ref, vbs_ref, ms_ref, as_ref)

    def win_steps(tiles):
        steps(tiles, lambda: qs_ref[:, 0:HEAD_DIM], kbw_ref, vbw_ref, mw_ref, aw_ref)

    def sel_body(j, carry):
        sel_steps([(SEL_UNROLL * j + u, None) for u in range(SEL_UNROLL)])
        return carry

    lax.fori_loop(0, i // SEL_UNROLL, sel_body, 0)

    for rem in range(SEL_UNROLL):
        @pl.when(i % SEL_UNROLL == rem)
        def _(rem=rem):
            sel_steps([(i - rem + u, None) for u in range(rem)] + [(i, tile_bias(0, None))])

    max_dist = WIN_LEN - 1
    n_prev = -(-max_dist // tk)

    @pl.when(i >= n_prev)
    def _():
        win_steps([(i - off, tile_bias(off, max_dist)) for off in range(n_prev, -1, -1)])

    for first in range(n_prev):
        @pl.when(i == first)
        def _(first=first):
            win_steps([(first - off, tile_bias(off, max_dist)) for off in range(first, -1, -1)])

    g = 1.0 / (1.0 + jnp.exp(-gate_ref[...].astype(F32)))
    lane = lax.broadcasted_iota(jnp.int32, g.shape, 1)
    head0 = pl.program_id(1) * n_rep

    def gate(head, branch):
        return jnp.sum(jnp.where(lane == head * 3 + branch, g, 0.0), axis=-1, keepdims=True)

    for r in range(n_rep):
        rows = slice(r * tq, (r + 1) * tq)
        o = (gate(head0 + r, 0) * oc_ref[rows, :]
             + gate(head0 + r, 1) * (as_ref[rows, 0:HEAD_DIM] / as_ref[rows, HEAD_DIM:2 * HEAD_DIM])
             + gate(head0 + r, 2) * (aw_ref[rows, 0:HEAD_DIM] / aw_ref[rows, HEAD_DIM:2 * HEAD_DIM]))
        o_ref[:, r * HEAD_DIM:(r + 1) * HEAD_DIM] = o.astype(o_ref.dtype)


def nsa_attention(z, kvc, ovt, *, n_batch, seq):
    tq = ATT_TQ
    nq = seq // tq
    n_rep = C_GROUP
    qw = n_rep * HEAD_DIM
    kv_blk = (seq, HEAD_DIM)
    sel0, win0 = COL_SEL // HEAD_DIM, COL_WIN // HEAD_DIM
    return pl.pallas_call(
        _nsa_kernel,
        out_shape=jax.ShapeDtypeStruct((n_batch * seq, C_HEADS * HEAD_DIM), BF16),
        grid=(n_batch, C_KV_HEADS, nq),
        in_specs=[
            pl.BlockSpec((tq, qw), lambda b, g, i: (b * nq + i, COL_CQ // qw + g)),
            pl.BlockSpec((N_SUB, HEAD_DIM), lambda b, g, i: (b * 4 + g, 0)),
            pl.BlockSpec((N_SUB, HEAD_DIM), lambda b, g, i: (b * 4 + 2 + g, 0)),
            pl.BlockSpec((LANE, N_SUB), lambda b, g, i: (0, 0)),
            pl.BlockSpec(kv_blk, lambda b, g, i: (b, sel0 + g)),
            pl.BlockSpec(kv_blk, lambda b, g, i: (b, sel0 + C_KV_HEADS + g)),
            pl.BlockSpec(kv_blk, lambda b, g, i: (b, win0 + g)),
            pl.BlockSpec(kv_blk, lambda b, g, i: (b, win0 + C_KV_HEADS + g)),
            pl.BlockSpec((tq, LANE), lambda b, g, i: (b * nq + i, COL_GATE // LANE)),
        ],
        out_specs=pl.BlockSpec((tq, qw), lambda b, g, i: (b * nq + i, g)),
        scratch_shapes=[
            pltpu.VMEM((n_rep * tq, HEAD_DIM + LANE), BF16),
            pltpu.VMEM((seq, HEAD_DIM + LANE), BF16),
            pltpu.VMEM((seq, 2 * HEAD_DIM), BF16),
            pltpu.VMEM((seq, HEAD_DIM), BF16),
            pltpu.VMEM((seq, 2 * HEAD_DIM), BF16),
            pltpu.VMEM((n_rep * tq, LANE), F32),
            pltpu.VMEM((n_rep * tq, 2 * HEAD_DIM), F32),
            pltpu.VMEM((n_rep * tq, LANE), F32),
            pltpu.VMEM((n_rep * tq, 2 * HEAD_DIM), F32),
            pltpu.VMEM((n_rep * tq, HEAD_DIM), F32),
            pltpu.VMEM((LANE, tq), F32),
        ],
        compiler_params=_cparams(("parallel", "parallel", "arbitrary")),
        name="nsa_attention",
    )(z, kvc, kvc, ovt, z, z, z, z, z)


def _overlap_t():
    n_c = N_SUB - 1
    c_start = np.arange(n_c) * CMP_STRIDE
    s_start = np.arange(N_SEL) * SEL_LEN
    ov = ((c_start[:, None] <= s_start[None, :] + SEL_LEN - 1)
          & (c_start[:, None] + CMP_LEN - 1 >= s_start[None, :])).astype(np.float32)
    out = np.zeros((LANE, N_SUB), np.float32)
    out[:N_SEL, :n_c] = ov.T
    return out


def _mix_out_kernel(x_ref, oa_ref, ob_ref, oc_ref, w_ref, o_ref):
    mix = jnp.concatenate([oa_ref[...], ob_ref[...], oc_ref[...]], axis=1)
    o_ref[...] = x_ref[...] + _dot(mix, w_ref[...])


def mix_out(x, out_a, out_b, out_c, w, layer, *, tm):
    m, n = x.shape
    return pl.pallas_call(
        _mix_out_kernel,
        out_shape=jax.ShapeDtypeStruct((m, n), F32),
        grid=(m // tm,),
        in_specs=[
            pl.BlockSpec((tm, n), lambda i: (i, 0)),
            pl.BlockSpec((tm, out_a.shape[1]), lambda i: (i, 0)),
            pl.BlockSpec((tm, out_b.shape[1]), lambda i: (i, 0)),
            pl.BlockSpec((tm, out_c.shape[1]), lambda i: (i, 0)),
            pl.BlockSpec((None,) + w.shape[1:], lambda i: (layer, 0, 0), pipeline_mode=pl.Buffered(1)),
        ],
        out_specs=pl.BlockSpec((tm, n), lambda i: (i, 0)),
        compiler_params=_cparams(("parallel",)),
        name="mix_out",
    )(x, out_a, out_b, out_c, w)


def _xattn_kernel(x_ref, g_ref, wq_ref, k_ref, v_ref, wo_ref, o_ref, h_ref, a_ref):
    n_mem = k_ref.shape[0]
    _norm_into(h_ref, x_ref, g_ref)
    q = _dot(h_ref[...], wq_ref[...]) * ((HEAD_DIM ** -0.5) * LOG2E)
    ones = jnp.ones((n_mem, HEAD_DIM), BF16)
    for h in range(X_HEADS):
        cols = slice(h * HEAD_DIM, (h + 1) * HEAD_DIM)
        s = _dot_nt(q[:, cols].astype(BF16), k_ref[:, cols].astype(BF16))
        e = jnp.exp2((s - jnp.max(s, axis=-1, keepdims=True)).astype(BF16))
        pv = _dot(e, jnp.concatenate([v_ref[:, cols].astype(BF16), ones], axis=1))
        a_ref[:, cols] = (pv[:, 0:HEAD_DIM] / pv[:, HEAD_DIM:2 * HEAD_DIM]).astype(BF16)
    o_ref[...] = x_ref[...] + _dot(a_ref[...], wo_ref[...])


def cross_attention_block(x, g, wq, kv, wo, layer, *, n_batch, seq, n_mem, tm):
    nq = seq // tm
    d = x.shape[1]
    return pl.pallas_call(
        _xattn_kernel,
        out_shape=jax.ShapeDtypeStruct(x.shape, F32),
        grid=(n_batch, nq),
        in_specs=[
            pl.BlockSpec((tm, d), lambda b, i: (b * nq + i, 0)),
            pl.BlockSpec((1, d), lambda b, i: (0, 0)),
            pl.BlockSpec((None, d, X_WIDTH), lambda b, i: (layer, 0, 0), pipeline_mode=pl.Buffered(1)),
            pl.BlockSpec((n_mem, X_WIDTH), lambda b, i: (b, 0)),
            pl.BlockSpec((n_mem, X_WIDTH), lambda b, i: (b, 1)),
            pl.BlockSpec((None, X_WIDTH, d), lambda b, i: (layer, 0, 0), pipeline_mode=pl.Buffered(1)),
        ],
        out_specs=pl.BlockSpec((tm, d), lambda b, i: (b * nq + i, 0)),
        scratch_shapes=[pltpu.VMEM((tm, d), BF16), pltpu.VMEM((tm, X_WIDTH), BF16)],
        compiler_params=_cparams(("parallel", "parallel")),
        name="cross_attention",
    )(x, g.reshape(1, d), wq, kv, kv, wo)


def kernel(x, mem, norm_mix, w_in, gmlp_ln_g, gmlp_ln_b, gmlp_w_s, gmlp_b_s, cmp_pos, cmp_k_w1, cmp_k_w2,
           cmp_v_w1, cmp_v_w2, w_out, norm_xattn, norm_mem, xattn_wq, xattn_wkv, xattn_wo, norm_mlp, w_up,
           w_down, final_norm):
    bsz, seq, d = x.shape
    n_mem = mem.shape[1]
    t = bsz * seq
    assert d == D_MODEL and seq // CMP_STRIDE == N_SUB and seq // SEL_LEN == N_SEL
    assert seq % (ATT_TQ * 16) == 0

    w_in_b = jnp.pad(w_in, ((0, 0), (0, 0), (0, IN_PAD - IN_WIDTH))).astype(BF16)
    w_out_b = w_out.astype(BF16)
    wq_b = xattn_wq.astype(BF16)
    wkv_b = xattn_wkv.astype(BF16)
    wo_b = xattn_wo.astype(BF16)
    w_up_b = w_up.astype(BF16)
    w_down_b = w_down.astype(BF16)
    cmp_w1 = jnp.stack([cmp_k_w1, cmp_v_w1], axis=1).astype(BF16)
    cmp_w2 = jnp.stack([cmp_k_w2, cmp_v_w2], axis=1).astype(BF16)
    ovt = jnp.asarray(_overlap_t()).astype(BF16)
    xf = x.reshape(t, d)
    memf = mem.reshape(bsz * n_mem, d)
    gw = C_GROUP * HEAD_DIM

    for l in range(DEPTH):
        z = norm_matmul(xf, norm_mix[l], w_in_b, l, tm=512, tn=IN_PAD, out_dtype=BF16)

        out_a = gmlp(z, gmlp_ln_g[l], gmlp_ln_b[l], gmlp_w_s[l], gmlp_b_s[l], tm=512)

        out_b = dilated_attention(z, n_batch=bsz, seq=seq)

        kvc = compress(z, cmp_pos[l], cmp_w1, cmp_w2, l, n_batch=bsz, seq=seq)
        out_c = nsa_attention(z, kvc, ovt, n_batch=bsz, seq=seq)

        xf = mix_out(xf, out_a, out_b, out_c, w_out_b, l, tm=512)

        kv = norm_matmul(memf, norm_mem[l], wkv_b, l, tm=512, tn=2 * X_WIDTH)
        xf = cross_attention_block(xf, norm_xattn[l], wq_b, kv, wo_b, l, n_batch=bsz, seq=seq, n_mem=n_mem,
                                   tm=512)

        xf = mlp(xf, norm_mlp[l], w_up_b, w_down_b, l, final_norm, tm=1024, tf=512,
                 norm_out=(l == DEPTH - 1))

    return xf.reshape(bsz, seq, d)
```

```python
import functools
import math

import numpy as np
import jax
import jax.numpy as jnp
from jax import lax
from jax.experimental import pallas as pl
from jax.experimental.pallas import tpu as pltpu

F32 = jnp.float32
BF16 = jnp.bfloat16

D_MODEL = 2048
DEPTH = 4
HEAD_DIM = 128
A_GROUPS = 4
A_WIDTH = A_GROUPS * HEAD_DIM
CHUNK = 128
B_HEADS = 4
DILATED_PAIRS = ((128, 1), (512, 4), (2048, 16))
C_HEADS = 8
C_KV_HEADS = 2
C_GROUP = C_HEADS // C_KV_HEADS
CMP_LEN = 32
CMP_STRIDE = 16
SEL_LEN = 64
SEL_TOP = 16
WIN_LEN = 512
X_HEADS = 4
X_WIDTH = X_HEADS * HEAD_DIM
EPS = 1e-6
NEG_INF = -1e30

COL_A = 0
COL_B = 2 * A_WIDTH
COL_CQ = COL_B + 3 * B_HEADS * HEAD_DIM
COL_CMP = COL_CQ + C_HEADS * HEAD_DIM
COL_SEL = COL_CMP + 2 * C_KV_HEADS * HEAD_DIM
COL_WIN = COL_SEL + 2 * C_KV_HEADS * HEAD_DIM
COL_GATE = COL_WIN + 2 * C_KV_HEADS * HEAD_DIM
IN_WIDTH = COL_GATE + 3 * C_HEADS
IN_PAD = 5248

LANE = 128
SUBLANE = 8
VMEM_LIMIT = 56 * 1024 * 1024

ATT_TQ = 256
ATT_TK = 256
ATT_CH = 128
SEL_UNROLL = 4
LOG2E = math.log2(math.e)
LN2 = math.log(2.0)


def _cparams(sem):
    return pltpu.CompilerParams(dimension_semantics=sem, vmem_limit_bytes=VMEM_LIMIT)


def _rms_rows(x, g):
    ms = jnp.mean(x * x, axis=-1, keepdims=True)
    return x * lax.rsqrt(ms + EPS) * g


def _gelu_tanh(x):
    c = math.sqrt(2.0 / math.pi)
    return x * (0.5 * (1.0 + jnp.tanh(c * (x + 0.044715 * (x * x * x)))))


def _dot(a, b):
    return jnp.dot(a, b, preferred_element_type=F32)


def _dot_nt(a, b):
    return lax.dot_general(a, b, (((1,), (1,)), ((), ())), preferred_element_type=F32)


NORM_ROWS = 256


def _norm_into(h_ref, x_ref, g_ref):
    rows = x_ref.shape[0]

    def body(c, carry):
        r0 = pl.multiple_of(c * NORM_ROWS, NORM_ROWS)
        x = x_ref[pl.ds(r0, NORM_ROWS), :]
        h_ref[pl.ds(r0, NORM_ROWS), :] = _rms_rows(x, g_ref[...]).astype(h_ref.dtype)
        return carry

    lax.fori_loop(0, rows // NORM_ROWS, body, 0)


def _norm_matmul_kernel(x_ref, g_ref, w_ref, o_ref, h_ref):
    @pl.when(pl.program_id(1) == 0)
    def _():
        _norm_into(h_ref, x_ref, g_ref)

    o_ref[...] = _dot(h_ref[...], w_ref[...]).astype(o_ref.dtype)


def norm_matmul(x, g, w, layer, *, tm, tn, out_dtype=F32):
    m, k = x.shape
    n = w.shape[2]
    return pl.pallas_call(
        _norm_matmul_kernel,
        out_shape=jax.ShapeDtypeStruct((m, n), out_dtype),
        grid=(m // tm, n // tn),
        in_specs=[
            pl.BlockSpec((tm, k), lambda i, j: (i, 0)),
            pl.BlockSpec((1, k), lambda i, j: (0, 0)),
            pl.BlockSpec((None, k, tn), lambda i, j: (layer, 0, j),
                         pipeline_mode=pl.Buffered(1) if tn == n else None),
        ],
        out_specs=pl.BlockSpec((tm, tn), lambda i, j: (i, j)),
        scratch_shapes=[pltpu.VMEM((tm, k), BF16)],
        compiler_params=_cparams(("parallel", "arbitrary")),
        name="norm_matmul",
    )(x, g.reshape(1, k), w)


def _res_matmul_kernel(*refs, n_parts):
    x_ref = refs[0]
    a_refs = refs[1:1 + n_parts]
    w_refs = refs[1 + n_parts:1 + 2 * n_parts]
    o_ref = refs[1 + 2 * n_parts]
    acc = x_ref[...]
    for a_ref, w_ref in zip(a_refs, w_refs):
        acc = acc + _dot(a_ref[...].astype(BF16), w_ref[...])
    o_ref[...] = acc


def res_matmul(x, parts, weights, *, tm, tn):
    m, n = x.shape
    n_parts = len(parts)
    in_specs = [pl.BlockSpec((tm, tn), lambda i, j: (i, j))]
    for a in parts:
        in_specs.append(pl.BlockSpec((tm, a.shape[1]), lambda i, j: (i, 0)))
    for w in weights:
        in_specs.append(pl.BlockSpec((w.shape[0], tn), lambda i, j: (0, j)))
    return pl.pallas_call(
        functools.partial(_res_matmul_kernel, n_parts=n_parts),
        out_shape=jax.ShapeDtypeStruct((m, n), F32),
        grid=(m // tm, n // tn),
        in_specs=in_specs,
        out_specs=pl.BlockSpec((tm, tn), lambda i, j: (i, j)),
        compiler_params=_cparams(("parallel", "parallel")),
        name="res_matmul",
    )(x, *parts, *weights)


def _mlp_kernel(x_ref, g_ref, wu_ref, wd_ref, gout_ref, o_ref, h_ref, *, norm_out):
    f = pl.program_id(1)

    @pl.when(f == 0)
    def _():
        _norm_into(h_ref, x_ref, g_ref)
        o_ref[...] = x_ref[...]

    a = _dot(h_ref[...], wu_ref[...])
    a = jnp.square(jnp.maximum(a, 0.0)).astype(BF16)
    o_ref[...] += _dot(a, wd_ref[...])

    if norm_out:
        @pl.when(f == pl.num_programs(1) - 1)
        def _():
            _norm_into(o_ref, o_ref, gout_ref)


def mlp(x, g, wu, wd, layer, g_out, *, tm, tf, norm_out):
    m, d = x.shape
    ff = wu.shape[2]
    return pl.pallas_call(
        functools.partial(_mlp_kernel, norm_out=norm_out),
        out_shape=jax.ShapeDtypeStruct((m, d), F32),
        grid=(m // tm, ff // tf),
        in_specs=[
            pl.BlockSpec((tm, d), lambda i, f: (i, 0)),
            pl.BlockSpec((1, d), lambda i, f: (0, 0)),
            pl.BlockSpec((None, d, tf), lambda i, f: (layer, 0, f)),
            pl.BlockSpec((None, tf, d), lambda i, f: (layer, f, 0)),
            pl.BlockSpec((1, d), lambda i, f: (0, 0)),
        ],
        out_specs=pl.BlockSpec((tm, d), lambda i, f: (i, 0)),
        scratch_shapes=[pltpu.VMEM((tm, d), BF16)],
        compiler_params=_cparams(("parallel", "arbitrary")),
        name="mlp",
    )(x, g.reshape(1, d), wu, wd, g_out.reshape(1, d))


def _rmsnorm_kernel(x_ref, g_ref, o_ref):
    o_ref[...] = _rms_rows(x_ref[...], g_ref[...])


def rmsnorm(x, g, *, tm):
    m, d = x.shape
    return pl.pallas_call(
        _rmsnorm_kernel,
        out_shape=jax.ShapeDtypeStruct((m, d), F32),
        grid=(m // tm,),
        in_specs=[pl.BlockSpec((tm, d), lambda i: (i, 0)),
                  pl.BlockSpec((1, d), lambda i: (0, 0))],
        out_specs=pl.BlockSpec((tm, d), lambda i: (i, 0)),
        compiler_params=_cparams(("parallel",)),
        name="final_rmsnorm",
    )(x, g.reshape(1, d))


def _gmlp_kernel(z_ref, lng_ref, lnb_ref, ws_ref, bst_ref, o_ref, *, n_chunks):
    row = lax.broadcasted_iota(jnp.int32, (CHUNK, CHUNK), 0)
    col = lax.broadcasted_iota(jnp.int32, (CHUNK, CHUNK), 1)
    causal = row >= col
    w_tril = [jnp.where(causal, ws_ref[gi], 0.0).astype(BF16) for gi in range(A_GROUPS)]
    for c in range(n_chunks):
        rows = slice(c * CHUNK, (c + 1) * CHUNK)
        u = _gelu_tanh(z_ref[rows, 0:A_WIDTH].astype(F32))
        v = _gelu_tanh(z_ref[rows, A_WIDTH:2 * A_WIDTH].astype(F32))
        vc = v - jnp.mean(v, axis=-1, keepdims=True)
        vn = vc * lax.rsqrt(jnp.mean(vc * vc, axis=-1, keepdims=True) + EPS)
        vn = vn * lng_ref[...] + lnb_ref[...]
        for gi in range(A_GROUPS):
            cols = slice(gi * HEAD_DIM, (gi + 1) * HEAD_DIM)
            sv = _dot(w_tril[gi], vn[:, cols].astype(BF16)) + bst_ref[:, gi:gi + 1]
            o_ref[rows, cols] = (u[:, cols] * sv).astype(o_ref.dtype)


def gmlp(z, ln_g, ln_b, w_s, b_s, *, tm):
    t = z.shape[0]
    return pl.pallas_call(
        functools.partial(_gmlp_kernel, n_chunks=tm // CHUNK),
        out_shape=jax.ShapeDtypeStruct((t, A_WIDTH), BF16),
        grid=(t // tm,),
        in_specs=[
            pl.BlockSpec((tm, 2 * A_WIDTH), lambda i: (i, 0)),
            pl.BlockSpec((1, A_WIDTH), lambda i: (0, 0)),
            pl.BlockSpec((1, A_WIDTH), lambda i: (0, 0)),
            pl.BlockSpec((A_GROUPS, CHUNK, CHUNK), lambda i: (0, 0, 0)),
            pl.BlockSpec((CHUNK, A_GROUPS), lambda i: (0, 0)),
        ],
        out_specs=pl.BlockSpec((tm, A_WIDTH), lambda i: (i, 0)),
        compiler_params=_cparams(("parallel",)),
        name="gmlp_gating",
    )(z, ln_g.reshape(1, A_WIDTH), ln_b.reshape(1, A_WIDTH), w_s, b_s.T)


def _gate_column(gate_ref, col):
    g = 1.0 / (1.0 + jnp.exp(-gate_ref[...].astype(F32)))
    lane = lax.broadcasted_iota(jnp.int32, g.shape, 1)
    return jnp.sum(jnp.where(lane == col, g, 0.0), axis=-1, keepdims=True)


def _flash_kernel(q_ref, k_ref, v_ref, sel_ref, gate_ref, o_ref, qs_ref, kb_ref, vb_ref, m_ref, accl_ref, *,
                  n_rep, mode, max_dist, branch):

    tq, tk, ch = ATT_TQ, ATT_TK, ATT_CH
    seq = k_ref.shape[0]
    i = pl.program_id(2)
    select = mode == "select"

    @pl.when(i == 0)
    def _():
        def prep(c, carry):
            r0 = pl.multiple_of(c * tk, tk)
            kb_ref[pl.ds(r0, tk), 0:HEAD_DIM] = k_ref[pl.ds(r0, tk), :].astype(BF16)
            if select:
                blk = (r0 + lax.broadcasted_iota(jnp.int32, (tk, LANE), 0)) >> 6
                lane = lax.broadcasted_iota(jnp.int32, (tk, LANE), 1)
                kb_ref[pl.ds(r0, tk), HEAD_DIM:HEAD_DIM + LANE] = jnp.where(blk == lane, 1.0, 0.0).astype(BF16)
            vb_ref[pl.ds(r0, tk), 0:HEAD_DIM] = v_ref[pl.ds(r0, tk), :].astype(BF16)
            vb_ref[pl.ds(r0, tk), HEAD_DIM:2 * HEAD_DIM] = jnp.ones((tk, HEAD_DIM), BF16)
            return carry

        lax.fori_loop(0, seq // tk, prep, 0)

    qscale = (HEAD_DIM ** -0.5) * LOG2E
    if select:
        pen = jnp.where(sel_ref[...] > 0.5, 0.0, NEG_INF).astype(BF16)
    for r in range(n_rep):
        rows = slice(r * tq, (r + 1) * tq)
        qs_ref[rows, 0:HEAD_DIM] = (q_ref[:, r * HEAD_DIM:(r + 1) * HEAD_DIM].astype(F32) * qscale).astype(BF16)
        if select:
            qs_ref[rows, HEAD_DIM:HEAD_DIM + LANE] = pen
    m_ref[...] = jnp.full(m_ref.shape, NEG_INF, F32)
    accl_ref[...] = jnp.zeros(accl_ref.shape, F32)

    row = lax.broadcasted_iota(jnp.int32, (tq, tk), 0)
    col = lax.broadcasted_iota(jnp.int32, (tq, tk), 1)

    def tile_bias(off):
        dist = off * tk + row - col
        if off == 0:
            ok = dist >= 0
            if not select and max_dist < tk - 1:
                ok = ok & (dist <= max_dist)
        elif not select and (off + 1) * tk - 1 > max_dist:
            ok = dist <= max_dist
        else:
            return None
        return jnp.where(ok, 0.0, NEG_INF)

    n_ch = n_rep * tq // ch

    def steps(tiles):
        scores = []
        for kt, bias in tiles:
            k0 = pl.multiple_of(kt * tk, tk)
            scores.append((_dot_nt(qs_ref[...], kb_ref[pl.ds(k0, tk), :]), k0, bias))
        for s_all, k0, bias in scores:
            ps, alphas = [], []
            for c in range(n_ch):
                rows = slice(c * ch, (c + 1) * ch)
                s = s_all[rows, :]
                if bias is not None:
                    b0 = (c % (tq // ch)) * ch
                    s = s + bias[b0:b0 + ch, :]
                m_old = m_ref[rows, :]
                m_new = jnp.maximum(m_old, jnp.max(s, axis=-1, keepdims=True))
                alphas.append(jnp.exp2(m_old - m_new))
                ps.append(jnp.exp2((s - jnp.concatenate([m_new] * (tk // LANE), axis=1)).astype(BF16)))
                m_ref[rows, :] = m_new
            pv_all = _dot(jnp.concatenate(ps, axis=0), vb_ref[pl.ds(k0, tk), :])
            for c in range(n_ch):
                rows = slice(c * ch, (c + 1) * ch)
                alpha2 = jnp.concatenate([alphas[c], alphas[c]], axis=1)
                accl_ref[rows, :] = alpha2 * accl_ref[rows, :] + pv_all[rows, :]

    if select:
        def body(j, carry):
            steps([(SEL_UNROLL * j + u, None) for u in range(SEL_UNROLL)])
            return carry

        lax.fori_loop(0, i // SEL_UNROLL, body, 0)

        for rem in range(SEL_UNROLL):
            @pl.when(i % SEL_UNROLL == rem)
            def _(rem=rem):
                steps([(i - rem + u, None) for u in range(rem)] + [(i, tile_bias(0))])
    else:
        n_prev = -(-max_dist // tk)

        @pl.when(i >= n_prev)
        def _():
            steps([(i - off, tile_bias(off)) for off in range(n_prev, -1, -1)])

        for first in range(n_prev):
            @pl.when(i == first)
            def _(first=first):
                steps([(first - off, tile_bias(off)) for off in range(first, -1, -1)])

    head0 = pl.program_id(1) * n_rep
    for r in range(n_rep):
        rows = slice(r * tq, (r + 1) * tq)
        cols = slice(r * HEAD_DIM, (r + 1) * HEAD_DIM)
        gate = _gate_column(gate_ref, (head0 + r) * 3 + branch)
        o = accl_ref[rows, 0:HEAD_DIM] / accl_ref[rows, HEAD_DIM:2 * HEAD_DIM]
        o_ref[:, cols] = (gate * o).astype(o_ref.dtype)


def _flash_band_kernel(q_ref, k_ref, v_ref, gate_ref, o_ref, *scratch, **static):
    _flash_kernel(q_ref, k_ref, v_ref, None, gate_ref, o_ref, *scratch, **static)


def flash_attention(z, sel_arr, *, n_batch, n_grp, n_rep, seq, q_col, k_col, v_col, mode, branch, max_dist=0):
    tq = ATT_TQ
    nq = seq // tq
    qw = n_rep * HEAD_DIM
    kw = HEAD_DIM + (LANE if mode == "select" else 0)
    in_specs = [
        pl.BlockSpec((tq, qw), lambda b, n, i: (b * nq + i, q_col(n))),
        pl.BlockSpec((seq, HEAD_DIM), lambda b, n, i: (b, k_col(n))),
        pl.BlockSpec((seq, HEAD_DIM), lambda b, n, i: (b, v_col(n))),
    ]
    args = [z, z, z]
    if mode == "select":
        in_specs.append(pl.BlockSpec((tq, LANE), lambda b, n, i: ((b * n_grp + n) * nq + i, 0)))
        args.append(sel_arr)
    in_specs.append(pl.BlockSpec((tq, LANE), lambda b, n, i: (b * nq + i, COL_GATE // LANE)))
    args.append(z)
    body = _flash_kernel if mode == "select" else _flash_band_kernel
    return pl.pallas_call(
        functools.partial(body, n_rep=n_rep, mode=mode, max_dist=max_dist, branch=branch),
        out_shape=jax.ShapeDtypeStruct((n_batch * seq, n_grp * qw), F32),
        grid=(n_batch, n_grp, nq),
        in_specs=in_specs,
        out_specs=pl.BlockSpec((tq, qw), lambda b, n, i: (b * nq + i, n)),
        scratch_shapes=[
            pltpu.VMEM((n_rep * tq, kw), BF16),
            pltpu.VMEM((seq, kw), BF16),
            pltpu.VMEM((seq, 2 * HEAD_DIM), BF16),
            pltpu.VMEM((n_rep * tq, LANE), F32),
            pltpu.VMEM((n_rep * tq, 2 * HEAD_DIM), F32),
        ],
        compiler_params=_cparams(("parallel", "parallel", "arbitrary")),
        name="flash_" + mode,
    )(*args)


DIL_CH = 128
DIL_KW = 2 * DIL_CH
DIL_UNITS = 8


def _dilated_kernel(q_in, k_in, v_in, o_ref, q_ref, k_ref, v_ref, m_ref, acc_ref, l_ref):
    ch, kw = DIL_CH, DIL_KW
    seq = q_in.shape[0]
    for src, dst in ((q_in, q_ref), (k_in, k_ref), (v_in, v_ref)):
        def stage(c, carry, src=src, dst=dst):
            r0 = pl.multiple_of(c * NORM_ROWS, NORM_ROWS)
            dst[pl.ds(r0, NORM_ROWS), :] = src[pl.ds(r0, NORM_ROWS), :].astype(F32)
            return carry
        lax.fori_loop(0, seq // NORM_ROWS, stage, 0)
    qscale = (HEAD_DIM ** -0.5) * LOG2E
    row = lax.broadcasted_iota(jnp.int32, (ch, kw), 0)
    col = lax.broadcasted_iota(jnp.int32, (ch, kw), 1)
    ones = jnp.ones((kw, HEAD_DIM), BF16)
    patterns = sorted(DILATED_PAIRS, key=lambda wd: -wd[1])

    def rows(start, n, dil):
        return pl.ds(start, n) if dil == 1 else pl.ds(start, n, stride=dil)

    def band_bias(first, span):
        dist = row - col if first else ch + row - col
        return jnp.where((dist >= 0) & (dist <= span), 0.0, NEG_INF)

    def run_units(units, dil, span, phase):
        windows = {}
        scores = []
        for q0, k0, first, wid in units:
            if wid not in windows:
                k = k_ref[rows(k0, kw, dil), :].astype(BF16)
                v = v_ref[rows(k0, kw, dil), :].astype(BF16)
                windows[wid] = (k, jnp.concatenate([v, ones], axis=1))
            q = (q_ref[rows(q0, ch, dil), :] * qscale).astype(BF16)
            scores.append(_dot_nt(q, windows[wid][0]) + band_bias(first, span))
        for (q0, k0, first, wid), s in zip(units, scores):
            qrows = rows(q0, ch, dil)
            m_cur = jnp.max(s, axis=-1, keepdims=True)
            if phase == "first":
                m_new = jnp.broadcast_to(m_cur, (ch, LANE))
            else:
                m_old = m_ref[qrows, :]
                m_new = jnp.maximum(m_old, m_cur)
                alpha = jnp.exp2(m_old - m_new)
            p = jnp.exp2((s - jnp.concatenate([m_new] * (kw // LANE), axis=1)).astype(BF16))
            pv = _dot(p, windows[wid][1])
            acc, l = pv[:, 0:HEAD_DIM], pv[:, HEAD_DIM:2 * HEAD_DIM]
            if phase != "first":
                acc = alpha * acc_ref[qrows, :] + acc
                l = alpha * l_ref[qrows, :] + l
            if phase == "last":
                o_ref[pl.ds(pl.multiple_of(q0, ch), ch), :] = (acc / l).astype(o_ref.dtype)
            else:
                m_ref[qrows, :] = m_new
                acc_ref[qrows, :] = acc
                l_ref[qrows, :] = l

    for idx, (window, dil) in enumerate(patterns):
        phase = "first" if idx == 0 else ("last" if idx == len(patterns) - 1 else "mid")
        assert phase != "last" or dil == 1
        span = window // dil
        assert span <= ch
        n_units = seq // dil // ch
        per_class = min(n_units, max(DIL_UNITS // dil, 2))
        n_classes = DIL_UNITS // per_class
        assert n_units % per_class == 0 and dil % n_classes == 0

        def group(r0, j0, head, dil=dil, span=span, phase=phase, per_class=per_class, n_classes=n_classes):
            units = []
            for rc in range(n_classes):
                r = r0 + rc
                for u in range(per_class):
                    first = head and u == 0
                    q0 = r + dil * ch * (j0 + u)
                    k0 = r if (head and u <= 1) else r + dil * ch * (j0 + u - 1)
                    wid = (rc, 0) if (head and u <= 1) else (rc, u)
                    units.append((q0, k0, first, wid))
            run_units(units, dil, span, phase)

        def class_loop(g, carry, group=group, per_class=per_class, n_classes=n_classes, n_units=n_units):
            r0 = g * n_classes
            group(r0, 0, True)
            if n_units > per_class:
                def tail(jb, c):
                    group(r0, jb * per_class, False)
                    return c
                lax.fori_loop(1, n_units // per_class, tail, 0)
            return carry

        if dil // n_classes == 1:
            class_loop(0, 0)
        else:
            lax.fori_loop(0, dil // n_classes, class_loop, 0)


def dilated_attention(z, *, n_batch, seq):
    bw = B_HEADS * HEAD_DIM
    base = COL_B // HEAD_DIM
    blk = (seq, HEAD_DIM)
    return pl.pallas_call(
        _dilated_kernel,
        out_shape=jax.ShapeDtypeStruct((n_batch * seq, bw), BF16),
        grid=(n_batch, B_HEADS),
        in_specs=[
            pl.BlockSpec(blk, lambda b, h: (b, base + h)),
            pl.BlockSpec(blk, lambda b, h: (b, base + B_HEADS + h)),
            pl.BlockSpec(blk, lambda b, h: (b, base + 2 * B_HEADS + h)),
        ],
        out_specs=pl.BlockSpec(blk, lambda b, h: (b, h)),
        scratch_shapes=[pltpu.VMEM(blk, F32)] * 6,
        compiler_params=_cparams(("parallel", "parallel")),
        name="dilated_attention",
    )(z, z, z)


N_SUB = 256
SUB_W = CMP_STRIDE * HEAD_DIM


def _compress_kernel(x_in, pos_ref, w1_ref, w2_ref, o_ref, x_ref):
    x_ref[...] = x_in[...].astype(F32)
    p = jnp.zeros((N_SUB, HEAD_DIM), F32)
    q = jnp.zeros((N_SUB, HEAD_DIM), F32)
    for i in range(CMP_STRIDE):
        a = x_ref[pl.ds(i, N_SUB, stride=CMP_STRIDE), :]
        top = (a + pos_ref[i:i + 1, :]).astype(BF16)
        bot = (a + pos_ref[CMP_STRIDE + i:CMP_STRIDE + i + 1, :]).astype(BF16)
        p = p + _dot(top, w1_ref[i * HEAD_DIM:(i + 1) * HEAD_DIM, :])
        q = q + _dot(bot, w1_ref[(CMP_STRIDE + i) * HEAD_DIM:(CMP_STRIDE + i + 1) * HEAD_DIM, :])
    h = _gelu_tanh(p + pltpu.roll(q, N_SUB - 1, 0))
    o_ref[...] = _dot(h.astype(BF16), w2_ref[...]).astype(o_ref.dtype)


def compress(z, pos, w1, w2, layer, *, n_batch, seq):
    return pl.pallas_call(
        _compress_kernel,
        out_shape=jax.ShapeDtypeStruct((n_batch * 4 * N_SUB, HEAD_DIM), BF16),
        grid=(n_batch, 4),
        in_specs=[
            pl.BlockSpec((seq, HEAD_DIM), lambda b, n: (b, COL_CMP // HEAD_DIM + n)),
            pl.BlockSpec((CMP_LEN, HEAD_DIM), lambda b, n: (0, 0)),
            pl.BlockSpec((None, None, CMP_LEN * HEAD_DIM, HEAD_DIM), lambda b, n: (layer, n // 2, 0, 0)),
            pl.BlockSpec((None, None, HEAD_DIM, HEAD_DIM), lambda b, n: (layer, n // 2, 0, 0)),
        ],
        out_specs=pl.BlockSpec((N_SUB, HEAD_DIM), lambda b, n: (b * 4 + n, 0)),
        scratch_shapes=[pltpu.VMEM((seq, HEAD_DIM), F32)],
        compiler_params=_cparams(("parallel", "parallel")),
        name="nsa_compress",
    )(z, pos, w1, w2)


N_SEL = 64


def _csel_kernel(q_ref, kc_ref, vc_ref, ovt_ref, gate_ref, oc_ref, sel_ref, qs_ref, sc_ref):
    tq, ch = ATT_TQ, ATT_CH
    n_sub = tq // ch
    q0 = pl.program_id(2) * tq
    qscale = (HEAD_DIM ** -0.5) * LOG2E
    for r in range(C_GROUP):
        qs_ref[r * tq:(r + 1) * tq, :] = (q_ref[:, r * HEAD_DIM:(r + 1) * HEAD_DIM].astype(F32)
                                          * qscale).astype(BF16)
    s_all = _dot_nt(qs_ref[...], kc_ref[...])
    n_cols = lax.broadcasted_iota(jnp.int32, (ch, N_SUB), 1)
    bias, has_any = [], []
    for c in range(n_sub):
        t = q0 + c * ch + lax.broadcasted_iota(jnp.int32, (ch, N_SUB), 0)
        bias.append(jnp.where((n_cols * CMP_STRIDE + (CMP_LEN - 1)) <= t, 0.0, NEG_INF))
        has_any.append(t[:, 0:1] >= CMP_LEN - 1)
    ps = []
    psum = [jnp.zeros((ch, N_SUB), F32) for _ in range(n_sub)]
    for c in range(C_GROUP * n_sub):
        s = s_all[c * ch:(c + 1) * ch, :] + bias[c % n_sub]
        e = jnp.exp2(s - jnp.max(s, axis=-1, keepdims=True))
        den = jnp.sum(e, axis=-1, keepdims=True)
        p = e * jnp.where(has_any[c % n_sub], 1.0 / den, 0.0)
        ps.append(p.astype(BF16))
        psum[c % n_sub] = psum[c % n_sub] + p
    o_all = _dot(jnp.concatenate(ps, axis=0), vc_ref[...])
    head0 = pl.program_id(1) * C_GROUP
    for r in range(C_GROUP):
        gate = _gate_column(gate_ref, (head0 + r) * 3)
        oc_ref[:, r * HEAD_DIM:(r + 1) * HEAD_DIM] = gate * o_all[r * tq:(r + 1) * tq, :]
    psum = jnp.concatenate(psum, axis=0)
    hi = psum.astype(BF16)
    lo = (psum - hi.astype(F32)).astype(BF16)
    ovt = ovt_ref[...]
    imp_t = _dot_nt(ovt, hi) + _dot_nt(ovt, lo)
    jrow = lax.broadcasted_iota(jnp.int32, (LANE, tq), 0)
    tl = q0 + lax.broadcasted_iota(jnp.int32, (LANE, tq), 1)
    jt = tl >> 6
    forced = (jrow == 0) | (jrow == jt) | (jrow == jt - 1)
    valid_s = jrow * SEL_LEN <= tl
    score = jnp.where(forced, 1e4, jnp.where(valid_s, imp_t, -1.0))
    sc_ref[...] = score
    n_grp = N_SEL // SUBLANE
    grp = [sc_ref[g * SUBLANE:(g + 1) * SUBLANE, :] for g in range(n_grp)]
    cnt = [jnp.zeros((SUBLANE, tq), F32) for _ in range(n_grp)]
    sub = lax.broadcasted_iota(jnp.int32, (SUBLANE, tq), 0)
    for j in range(N_SEL):
        row = sc_ref[j:j + 1, :]
        for g in range(n_grp):
            if g < j // SUBLANE:
                beats = row > grp[g]
            elif g > j // SUBLANE:
                beats = row >= grp[g]
            else:
                beats = (row > grp[g]) | ((row == grp[g]) & (sub > j % SUBLANE))
            cnt[g] = cnt[g] + jnp.where(beats, 1.0, 0.0)
    sel_t = [jnp.where(c < float(SEL_TOP), 1.0, 0.0) for c in cnt]
    sel_pad = jnp.concatenate(sel_t + [jnp.zeros((LANE - N_SEL, tq), F32)], axis=0)
    sel_ref[...] = sel_pad.T


def compressed_select(z, kvc, ovt, *, n_batch, seq):
    tq = ATT_TQ
    nq = seq // tq
    qw = C_GROUP * HEAD_DIM
    return pl.pallas_call(
        _csel_kernel,
        out_shape=[jax.ShapeDtypeStruct((n_batch * seq, C_HEADS * HEAD_DIM), F32),
                   jax.ShapeDtypeStruct((n_batch * C_KV_HEADS * seq, LANE), F32)],
        grid=(n_batch, C_KV_HEADS, nq),
        in_specs=[
            pl.BlockSpec((tq, qw), lambda b, g, i: (b * nq + i, COL_CQ // qw + g)),
            pl.BlockSpec((N_SUB, HEAD_DIM), lambda b, g, i: (b * 4 + g, 0)),
            pl.BlockSpec((N_SUB, HEAD_DIM), lambda b, g, i: (b * 4 + 2 + g, 0)),
            pl.BlockSpec((LANE, N_SUB), lambda b, g, i: (0, 0)),
            pl.BlockSpec((tq, LANE), lambda b, g, i: (b * nq + i, COL_GATE // LANE)),
        ],
        out_specs=[
            pl.BlockSpec((tq, qw), lambda b, g, i: (b * nq + i, g)),
            pl.BlockSpec((tq, LANE), lambda b, g, i: ((b * C_KV_HEADS + g) * nq + i, 0)),
        ],
        scratch_shapes=[pltpu.VMEM((C_GROUP * tq, HEAD_DIM), BF16), pltpu.VMEM((LANE, tq), F32)],
        compiler_params=_cparams(("parallel", "parallel", "parallel")),
        name="nsa_compressed_select",
    )(z, kvc, kvc, ovt, z)


def _nsa_kernel(q_ref, kc_ref, vc_ref, ovt_ref, ks_ref, vs_ref, kw_ref, vw_ref, gate_ref, o_ref,
                qs_ref, kbs_ref, vbs_ref, kbw_ref, vbw_ref, ms_ref, as_ref, mw_ref, aw_ref, oc_ref, sc_ref):
    tq, tk, ch = ATT_TQ, ATT_TK, ATT_CH
    n_rep = C_GROUP
    n_sub = tq // ch
    n_ch = n_rep * n_sub
    seq = ks_ref.shape[0]
    i = pl.program_id(2)
    q0 = i * tq

    @pl.when(i == 0)
    def _():
        def prep(c, carry):
            rows = pl.ds(pl.multiple_of(c * tk, tk), tk)
            blk = (c * tk + lax.broadcasted_iota(jnp.int32, (tk, LANE), 0)) >> 6
            lane = lax.broadcasted_iota(jnp.int32, (tk, LANE), 1)
            ones = jnp.ones((tk, HEAD_DIM), BF16)
            kbs_ref[rows, 0:HEAD_DIM] = ks_ref[rows, :]
            kbs_ref[rows, HEAD_DIM:HEAD_DIM + LANE] = jnp.where(blk == lane, 1.0, 0.0).astype(BF16)
            vbs_ref[rows, 0:HEAD_DIM] = vs_ref[rows, :]
            vbs_ref[rows, HEAD_DIM:2 * HEAD_DIM] = ones
            kbw_ref[rows, :] = kw_ref[rows, :]
            vbw_ref[rows, 0:HEAD_DIM] = vw_ref[rows, :]
            vbw_ref[rows, HEAD_DIM:2 * HEAD_DIM] = ones
            return carry

        lax.fori_loop(0, seq // tk, prep, 0)

    qscale = (HEAD_DIM ** -0.5) * LOG2E
    for r in range(n_rep):
        qs_ref[r * tq:(r + 1) * tq, 0:HEAD_DIM] = (q_ref[:, r * HEAD_DIM:(r + 1) * HEAD_DIM].astype(F32)
                                                   * qscale).astype(BF16)

    for m_ref, accl_ref in ((ms_ref, as_ref), (mw_ref, aw_ref)):
        m_ref[...] = jnp.full(m_ref.shape, NEG_INF, F32)
        accl_ref[...] = jnp.zeros(accl_ref.shape, F32)

    q_idx = lax.broadcasted_iota(jnp.int32, (tq, tk), 0)
    k_idx = lax.broadcasted_iota(jnp.int32, (tq, tk), 1)

    def tile_bias(off, max_dist):
        dist = off * tk + q_idx - k_idx
        if off == 0:
            ok = dist >= 0
        elif max_dist is not None and (off + 1) * tk - 1 > max_dist:
            ok = dist <= max_dist
        else:
            return None
        return jnp.where(ok, 0.0, NEG_INF)

    def issue(tiles, load_q, kb_ref):
        scores = []
        for kt, tbias in tiles:
            k0 = pl.multiple_of(kt * tk, tk)
            scores.append((_dot_nt(load_q(), kb_ref[pl.ds(k0, tk), :]), k0, tbias))
        return scores

    def finish(scores, vb_ref, m_ref, accl_ref):
        for s_tile, k0, tbias in scores:
            pts, alphas = [], []
            for c in range(n_ch):
                rows = slice(c * ch, (c + 1) * ch)
                s = s_tile[rows, :]
                if tbias is not None:
                    b0 = (c % n_sub) * ch
                    s = s + tbias[b0:b0 + ch, :]
                m_old = m_ref[rows, :]
                m_new = jnp.maximum(m_old, jnp.max(s, axis=-1, keepdims=True))
                alphas.append(jnp.exp2(m_old - m_new))
                pts.append(jnp.exp2((s - jnp.concatenate([m_new] * (tk // LANE), axis=1)).astype(BF16)))
                m_ref[rows, :] = m_new
            pv = _dot(jnp.concatenate(pts, axis=0), vb_ref[pl.ds(k0, tk), :])
            for c in range(n_ch):
                rows = slice(c * ch, (c + 1) * ch)
                alpha2 = jnp.concatenate([alphas[c], alphas[c]], axis=1)
                accl_ref[rows, :] = alpha2 * accl_ref[rows, :] + pv[rows, :]

    max_dist = WIN_LEN - 1
    win_tiles = []
    for off in range(-(-max_dist // tk), -1, -1):
        tbias = tile_bias(off, max_dist)
        if off > 0:
            before_start = jnp.where(i >= off, 0.0, NEG_INF)
            if tbias is None:
                tbias = jnp.where(q_idx + k_idx >= 0, 0.0, NEG_INF)
            tbias = tbias + before_start
        win_tiles.append((jnp.maximum(i - off, 0), tbias))
    win_scores = issue(win_tiles, lambda: qs_ref[:, 0:HEAD_DIM], kbw_ref)

    s_all = _dot_nt(qs_ref[:, 0:HEAD_DIM], kc_ref[...])
    n_cols = lax.broadcasted_iota(jnp.int32, (ch, N_SUB), 1)
    bias, has_any = [], []
    for c in range(n_sub):
        t = q0 + c * ch + lax.broadcasted_iota(jnp.int32, (ch, N_SUB), 0)
        bias.append(jnp.where((n_cols * CMP_STRIDE + (CMP_LEN - 1)) <= t, 0.0, NEG_INF))
        has_any.append(t[:, 0:1] >= CMP_LEN - 1)
    ps = []
    psum = [jnp.zeros((ch, N_SUB), F32) for _ in range(n_sub)]
    for c in range(n_ch):
        s = s_all[c * ch:(c + 1) * ch, :] + bias[c % n_sub]
        e = jnp.exp2(s - jnp.max(s, axis=-1, keepdims=True))
        den = jnp.sum(e, axis=-1, keepdims=True)
        p = e * jnp.where(has_any[c % n_sub], 1.0 / den, 0.0)
        ps.append(p.astype(BF16))
        psum[c % n_sub] = psum[c % n_sub] + p
    oc_ref[...] = _dot(jnp.concatenate(ps, axis=0), vc_ref[...])
    psum = jnp.concatenate(psum, axis=0)

    hi = psum.astype(BF16)
    lo = (psum - hi.astype(F32)).astype(BF16)
    ovt = ovt_ref[...]
    imp_t = _dot_nt(ovt, hi) + _dot_nt(ovt, lo)
    jrow = lax.broadcasted_iota(jnp.int32, (LANE, tq), 0)
    tl = q0 + lax.broadcasted_iota(jnp.int32, (LANE, tq), 1)
    jt = tl >> 6
    forced = (jrow == 0) | (jrow == jt) | (jrow == jt - 1)
    valid_s = jrow * SEL_LEN <= tl
    sc_ref[...] = jnp.where(forced, 1e4, jnp.where(valid_s, imp_t, -1.0))
    finish(win_scores, vbw_ref, mw_ref, aw_ref)
    n_grp = N_SEL // SUBLANE
    grp = [sc_ref[g * SUBLANE:(g + 1) * SUBLANE, :] for g in range(n_grp)]
    cnt = [jnp.zeros((SUBLANE, tq), F32) for _ in range(n_grp)]
    sub = lax.broadcasted_iota(jnp.int32, (SUBLANE, tq), 0)
    for j in range(N_SEL):
        row = sc_ref[j:j + 1, :]
        for g in range(n_grp):
            if g < j // SUBLANE:
                beats = row > grp[g]
            elif g > j // SUBLANE:
                beats = row >= grp[g]
            else:
                beats = (row > grp[g]) | ((row == grp[g]) & (sub > j % SUBLANE))
            cnt[g] = cnt[g] + jnp.where(beats, 1.0, 0.0)
    pen_t = [jnp.where(c < float(SEL_TOP), 0.0, NEG_INF) for c in cnt]
    pen_t = jnp.concatenate(pen_t + [jnp.full((LANE - N_SEL, tq), NEG_INF, F32)], axis=0)
    pen = pen_t.T.astype(BF16)
    for r in range(n_rep):
        qs_ref[r * tq:(r + 1) * tq, HEAD_DIM:HEAD_DIM + LANE] = pen

    def sel_steps(tiles):
        finish(issue(tiles, lambda: qs_ref[...], kbs_ref), vbs_ref, ms_ref, as_ref)

    def sel_body(j, carry):
        sel_steps([(SEL_UNROLL * j + u, None) for u in range(SEL_UNROLL)])
        return carry

    lax.fori_loop(0, i // SEL_UNROLL, sel_body, 0)

    for rem in range(SEL_UNROLL):
        @pl.when(i % SEL_UNROLL == rem)
        def _(rem=rem):
            sel_steps([(i - rem + u, None) for u in range(rem)] + [(i, tile_bias(0, None))])

    g = 1.0 / (1.0 + jnp.exp(-gate_ref[...].astype(F32)))
    lane = lax.broadcasted_iota(jnp.int32, g.shape, 1)
    head0 = pl.program_id(1) * n_rep

    def gate(head, branch):
        return jnp.sum(jnp.where(lane == head * 3 + branch, g, 0.0), axis=-1, keepdims=True)

    for r in range(n_rep):
        rows = slice(r * tq, (r + 1) * tq)
        o = (gate(head0 + r, 0) * oc_ref[rows, :]
             + gate(head0 + r, 1) * (as_ref[rows, 0:HEAD_DIM] / as_ref[rows, HEAD_DIM:2 * HEAD_DIM])
             + gate(head0 + r, 2) * (aw_ref[rows, 0:HEAD_DIM] / aw_ref[rows, HEAD_DIM:2 * HEAD_DIM]))
        o_ref[:, r * HEAD_DIM:(r + 1) * HEAD_DIM] = o.astype(o_ref.dtype)


def nsa_attention(z, kvc, ovt, *, n_batch, seq):
    tq = ATT_TQ
    nq = seq // tq
    n_rep = C_GROUP
    qw = n_rep * HEAD_DIM
    kv_blk = (seq, HEAD_DIM)
    sel0, win0 = COL_SEL // HEAD_DIM, COL_WIN // HEAD_DIM
    return pl.pallas_call(
        _nsa_kernel,
        out_shape=jax.ShapeDtypeStruct((n_batch * seq, C_HEADS * HEAD_DIM), BF16),
        grid=(n_batch, C_KV_HEADS, nq),
        in_specs=[
            pl.BlockSpec((tq, qw), lambda b, g, i: (b * nq + i, COL_CQ // qw + g)),
            pl.BlockSpec((N_SUB, HEAD_DIM), lambda b, g, i: (b * 4 + g, 0)),
            pl.BlockSpec((N_SUB, HEAD_DIM), lambda b, g, i: (b * 4 + 2 + g, 0)),
            pl.BlockSpec((LANE, N_SUB), lambda b, g, i: (0, 0)),
            pl.BlockSpec(kv_blk, lambda b, g, i: (b, sel0 + g)),
            pl.BlockSpec(kv_blk, lambda b, g, i: (b, sel0 + C_KV_HEADS + g)),
            pl.BlockSpec(kv_blk, lambda b, g, i: (b, win0 + g)),
            pl.BlockSpec(kv_blk, lambda b, g, i: (b, win0 + C_KV_HEADS + g)),
            pl.BlockSpec((tq, LANE), lambda b, g, i: (b * nq + i, COL_GATE // LANE)),
        ],
        out_specs=pl.BlockSpec((tq, qw), lambda b, g, i: (b * nq + i, g)),
        scratch_shapes=[
            pltpu.VMEM((n_rep * tq, HEAD_DIM + LANE), BF16),
            pltpu.VMEM((seq, HEAD_DIM + LANE), BF16),
            pltpu.VMEM((seq, 2 * HEAD_DIM), BF16),
            pltpu.VMEM((seq, HEAD_DIM), BF16),
            pltpu.VMEM((seq, 2 * HEAD_DIM), BF16),
            pltpu.VMEM((n_rep * tq, LANE), F32),
            pltpu.VMEM((n_rep * tq, 2 * HEAD_DIM), F32),
            pltpu.VMEM((n_rep * tq, LANE), F32),
            pltpu.VMEM((n_rep * tq, 2 * HEAD_DIM), F32),
            pltpu.VMEM((n_rep * tq, HEAD_DIM), F32),
            pltpu.VMEM((LANE, tq), F32),
        ],
        compiler_params=_cparams(("parallel", "parallel", "arbitrary")),
        name="nsa_attention",
    )(z, kvc, kvc, ovt, z, z, z, z, z)


def _overlap_t():
    n_c = N_SUB - 1
    c_start = np.arange(n_c) * CMP_STRIDE
    s_start = np.arange(N_SEL) * SEL_LEN
    ov = ((c_start[:, None] <= s_start[None, :] + SEL_LEN - 1)
          & (c_start[:, None] + CMP_LEN - 1 >= s_start[None, :])).astype(np.float32)
    out = np.zeros((LANE, N_SUB), np.float32)
    out[:N_SEL, :n_c] = ov.T
    return out


def _mix_out_kernel(x_ref, oa_ref, ob_ref, oc_ref, w_ref, o_ref):
    mix = jnp.concatenate([oa_ref[...], ob_ref[...], oc_ref[...]], axis=1)
    o_ref[...] = x_ref[...] + _dot(mix, w_ref[...])


def mix_out(x, out_a, out_b, out_c, w, layer, *, tm):
    m, n = x.shape
    return pl.pallas_call(
        _mix_out_kernel,
        out_shape=jax.ShapeDtypeStruct((m, n), F32),
        grid=(m // tm,),
        in_specs=[
            pl.BlockSpec((tm, n), lambda i: (i, 0)),
            pl.BlockSpec((tm, out_a.shape[1]), lambda i: (i, 0)),
            pl.BlockSpec((tm, out_b.shape[1]), lambda i: (i, 0)),
            pl.BlockSpec((tm, out_c.shape[1]), lambda i: (i, 0)),
            pl.BlockSpec((None,) + w.shape[1:], lambda i: (layer, 0, 0), pipeline_mode=pl.Buffered(1)),
        ],
        out_specs=pl.BlockSpec((tm, n), lambda i: (i, 0)),
        compiler_params=_cparams(("parallel",)),
        name="mix_out",
    )(x, out_a, out_b, out_c, w)


def _xattn_kernel(x_ref, g_ref, wq_ref, k_ref, v_ref, wo_ref, o_ref, h_ref, a_ref):
    n_mem = k_ref.shape[0]
    _norm_into(h_ref, x_ref, g_ref)
    q = _dot(h_ref[...], wq_ref[...]) * ((HEAD_DIM ** -0.5) * LOG2E)
    ones = jnp.ones((n_mem, HEAD_DIM), BF16)
    for h in range(X_HEADS):
        cols = slice(h * HEAD_DIM, (h + 1) * HEAD_DIM)
        s = _dot_nt(q[:, cols].astype(BF16), k_ref[:, cols].astype(BF16))
        e = jnp.exp2((s - jnp.max(s, axis=-1, keepdims=True)).astype(BF16))
        pv = _dot(e, jnp.concatenate([v_ref[:, cols].astype(BF16), ones], axis=1))
        a_ref[:, cols] = (pv[:, 0:HEAD_DIM] / pv[:, HEAD_DIM:2 * HEAD_DIM]).astype(BF16)
    o_ref[...] = x_ref[...] + _dot(a_ref[...], wo_ref[...])


def cross_attention_block(x, g, wq, kv, wo, layer, *, n_batch, seq, n_mem, tm):
    nq = seq // tm
    d = x.shape[1]
    return pl.pallas_call(
        _xattn_kernel,
        out_shape=jax.ShapeDtypeStruct(x.shape, F32),
        grid=(n_batch, nq),
        in_specs=[
            pl.BlockSpec((tm, d), lambda b, i: (b * nq + i, 0)),
            pl.BlockSpec((1, d), lambda b, i: (0, 0)),
            pl.BlockSpec((None, d, X_WIDTH), lambda b, i: (layer, 0, 0), pipeline_mode=pl.Buffered(1)),
            pl.BlockSpec((n_mem, X_WIDTH), lambda b, i: (b, 0)),
            pl.BlockSpec((n_mem, X_WIDTH), lambda b, i: (b, 1)),
            pl.BlockSpec((None, X_WIDTH, d), lambda b, i: (layer, 0, 0), pipeline_mode=pl.Buffered(1)),
        ],
        out_specs=pl.BlockSpec((tm, d), lambda b, i: (b * nq + i, 0)),
        scratch_shapes=[pltpu.VMEM((tm, d), BF16), pltpu.VMEM((tm, X_WIDTH), BF16)],
        compiler_params=_cparams(("parallel", "parallel")),
        name="cross_attention",
    )(x, g.reshape(1, d), wq, kv, kv, wo)


def kernel(x, mem, norm_mix, w_in, gmlp_ln_g, gmlp_ln_b, gmlp_w_s, gmlp_b_s, cmp_pos, cmp_k_w1, cmp_k_w2,
           cmp_v_w1, cmp_v_w2, w_out, norm_xattn, norm_mem, xattn_wq, xattn_wkv, xattn_wo, norm_mlp, w_up,
           w_down, final_norm):
    bsz, seq, d = x.shape
    n_mem = mem.shape[1]
    t = bsz * seq
    assert d == D_MODEL and seq // CMP_STRIDE == N_SUB and seq // SEL_LEN == N_SEL
    assert seq % (ATT_TQ * 16) == 0

    w_in_b = jnp.pad(w_in, ((0, 0), (0, 0), (0, IN_PAD - IN_WIDTH))).astype(BF16)
    w_out_b = w_out.astype(BF16)
    wq_b = xattn_wq.astype(BF16)
    wkv_b = xattn_wkv.astype(BF16)
    wo_b = xattn_wo.astype(BF16)
    w_up_b = w_up.astype(BF16)
    w_down_b = w_down.astype(BF16)
    cmp_w1 = jnp.stack([cmp_k_w1, cmp_v_w1], axis=1).astype(BF16)
    cmp_w2 = jnp.stack([cmp_k_w2, cmp_v_w2], axis=1).astype(BF16)
    ovt = jnp.asarray(_overlap_t()).astype(BF16)
    xf = x.reshape(t, d)
    memf = mem.reshape(bsz * n_mem, d)
    gw = C_GROUP * HEAD_DIM

    for l in range(DEPTH):
        z = norm_matmul(xf, norm_mix[l], w_in_b, l, tm=512, tn=IN_PAD, out_dtype=BF16)

        out_a = gmlp(z, gmlp_ln_g[l], gmlp_ln_b[l], gmlp_w_s[l], gmlp_b_s[l], tm=512)

        out_b = dilated_attention(z, n_batch=bsz, seq=seq)

        kvc = compress(z, cmp_pos[l], cmp_w1, cmp_w2, l, n_batch=bsz, seq=seq)
        out_c = nsa_attention(z, kvc, ovt, n_batch=bsz, seq=seq)

        xf = mix_out(xf, out_a, out_b, out_c, w_out_b, l, tm=512)

        kv = norm_matmul(memf, norm_mem[l], wkv_b, l, tm=512, tn=2 * X_WIDTH)
        xf = cross_attention_block(xf, norm_xattn[l], wq_b, kv, wo_b, l, n_batch=bsz, seq=seq, n_mem=n_mem,
                                   tm=512)

        xf = mlp(xf, norm_mlp[l], w_up_b, w_down_b, l, final_norm, tm=1024, tf=512,
                 norm_out=(l == DEPTH - 1))

    return xf.reshape(bsz, seq, d)
```

```python
import functools
import math

import numpy as np
import jax
import jax.numpy as jnp
from jax import lax
from jax.experimental import pallas as pl
from jax.experimental.pallas import tpu as pltpu

F32 = jnp.float32
BF16 = jnp.bfloat16

D_MODEL = 2048
DEPTH = 4
HEAD_DIM = 128
A_GROUPS = 4
A_WIDTH = A_GROUPS * HEAD_DIM
CHUNK = 128
B_HEADS = 4
DILATED_PAIRS = ((128, 1), (512, 4), (2048, 16))
C_HEADS = 8
C_KV_HEADS = 2
C_GROUP = C_HEADS // C_KV_HEADS
CMP_LEN = 32
CMP_STRIDE = 16
SEL_LEN = 64
SEL_TOP = 16
WIN_LEN = 512
X_HEADS = 4
X_WIDTH = X_HEADS * HEAD_DIM
EPS = 1e-6
NEG_INF = -1e30

COL_A = 0
COL_B = 2 * A_WIDTH
COL_CQ = COL_B + 3 * B_HEADS * HEAD_DIM
COL_CMP = COL_CQ + C_HEADS * HEAD_DIM
COL_SEL = COL_CMP + 2 * C_KV_HEADS * HEAD_DIM
COL_WIN = COL_SEL + 2 * C_KV_HEADS * HEAD_DIM
COL_GATE = COL_WIN + 2 * C_KV_HEADS * HEAD_DIM
IN_WIDTH = COL_GATE + 3 * C_HEADS
IN_PAD = 5248

LANE = 128
SUBLANE = 8
VMEM_LIMIT = 56 * 1024 * 1024

ATT_TQ = 256
ATT_TK = 256
ATT_CH = 128
SEL_UNROLL = 4
LOG2E = math.log2(math.e)
LN2 = math.log(2.0)


def _cparams(sem):
    return pltpu.CompilerParams(dimension_semantics=sem, vmem_limit_bytes=VMEM_LIMIT)


def _rms_rows(x, g):
    ms = jnp.mean(x * x, axis=-1, keepdims=True)
    return x * lax.rsqrt(ms + EPS) * g


def _gelu_tanh(x):
    c = math.sqrt(2.0 / math.pi)
    return x * (0.5 * (1.0 + jnp.tanh(c * (x + 0.044715 * (x * x * x)))))


def _dot(a, b):
    return jnp.dot(a, b, preferred_element_type=F32)


def _dot_nt(a, b):
    return lax.dot_general(a, b, (((1,), (1,)), ((), ())), preferred_element_type=F32)


NORM_ROWS = 256


def _norm_into(h_ref, x_ref, g_ref, copy_ref=None):
    rows = x_ref.shape[0]

    def body(c, carry):
        r0 = pl.multiple_of(c * NORM_ROWS, NORM_ROWS)
        x = x_ref[pl.ds(r0, NORM_ROWS), :]
        h_ref[pl.ds(r0, NORM_ROWS), :] = _rms_rows(x, g_ref[...]).astype(h_ref.dtype)
        if copy_ref is not None:
            copy_ref[pl.ds(r0, NORM_ROWS), :] = x
        return carry

    lax.fori_loop(0, rows // NORM_ROWS, body, 0)


def _norm_matmul_kernel(x_ref, g_ref, w_ref, o_ref):
    half = x_ref.shape[0] // 2
    for c in range(2):
        r = slice(c * half, (c + 1) * half)
        hn = _rms_rows(x_ref[r, :], g_ref[...]).astype(BF16)
        o_ref[r, :] = _dot(hn, w_ref[...]).astype(o_ref.dtype)


def norm_matmul(x, g, w, layer, *, tm, out_dtype=F32):
    m, k = x.shape
    n = w.shape[2]
    return pl.pallas_call(
        _norm_matmul_kernel,
        out_shape=jax.ShapeDtypeStruct((m, n), out_dtype),
        grid=(m // tm,),
        in_specs=[
            pl.BlockSpec((tm, k), lambda i: (i, 0)),
            pl.BlockSpec((1, k), lambda i: (0, 0)),
            pl.BlockSpec((None, k, n), lambda i: (layer, 0, 0), pipeline_mode=pl.Buffered(1)),
        ],
        out_specs=pl.BlockSpec((tm, n), lambda i: (i, 0)),
        compiler_params=_cparams(("parallel",)),
        name="norm_matmul",
    )(x, g.reshape(1, k), w)


def _res_matmul_kernel(*refs, n_parts):
    x_ref = refs[0]
    a_refs = refs[1:1 + n_parts]
    w_refs = refs[1 + n_parts:1 + 2 * n_parts]
    o_ref = refs[1 + 2 * n_parts]
    acc = x_ref[...]
    for a_ref, w_ref in zip(a_refs, w_refs):
        acc = acc + _dot(a_ref[...].astype(BF16), w_ref[...])
    o_ref[...] = acc


def res_matmul(x, parts, weights, *, tm, tn):
    m, n = x.shape
    n_parts = len(parts)
    in_specs = [pl.BlockSpec((tm, tn), lambda i, j: (i, j))]
    for a in parts:
        in_specs.append(pl.BlockSpec((tm, a.shape[1]), lambda i, j: (i, 0)))
    for w in weights:
        in_specs.append(pl.BlockSpec((w.shape[0], tn), lambda i, j: (0, j)))
    return pl.pallas_call(
        functools.partial(_res_matmul_kernel, n_parts=n_parts),
        out_shape=jax.ShapeDtypeStruct((m, n), F32),
        grid=(m // tm, n // tn),
        in_specs=in_specs,
        out_specs=pl.BlockSpec((tm, tn), lambda i, j: (i, j)),
        compiler_params=_cparams(("parallel", "parallel")),
        name="res_matmul",
    )(x, *parts, *weights)


def _mlp_kernel(x_ref, g_ref, wu_ref, wd_ref, gout_ref, o_ref, h_ref, *, norm_out):
    f = pl.program_id(1)

    @pl.when(f == 0)
    def _():
        _norm_into(h_ref, x_ref, g_ref, copy_ref=o_ref)

    a = _dot(h_ref[...], wu_ref[...])
    a = jnp.square(jnp.maximum(a, 0.0)).astype(BF16)
    o_ref[...] += _dot(a, wd_ref[...])

    if norm_out:
        @pl.when(f == pl.num_programs(1) - 1)
        def _():
            _norm_into(o_ref, o_ref, gout_ref)


def mlp(x, g, wu, wd, layer, g_out, *, tm, tf, norm_out):
    m, d = x.shape
    ff = wu.shape[2]
    return pl.pallas_call(
        functools.partial(_mlp_kernel, norm_out=norm_out),
        out_shape=jax.ShapeDtypeStruct((m, d), F32),
        grid=(m // tm, ff // tf),
        in_specs=[
            pl.BlockSpec((tm, d), lambda i, f: (i, 0)),
            pl.BlockSpec((1, d), lambda i, f: (0, 0)),
            pl.BlockSpec((None, d, tf), lambda i, f: (layer, 0, f)),
            pl.BlockSpec((None, tf, d), lambda i, f: (layer, f, 0)),
            pl.BlockSpec((1, d), lambda i, f: (0, 0)),
        ],
        out_specs=pl.BlockSpec((tm, d), lambda i, f: (i, 0)),
        scratch_shapes=[pltpu.VMEM((tm, d), BF16)],
        compiler_params=_cparams(("parallel", "arbitrary")),
        name="mlp",
    )(x, g.reshape(1, d), wu, wd, g_out.reshape(1, d))


def _rmsnorm_kernel(x_ref, g_ref, o_ref):
    o_ref[...] = _rms_rows(x_ref[...], g_ref[...])


def rmsnorm(x, g, *, tm):
    m, d = x.shape
    return pl.pallas_call(
        _rmsnorm_kernel,
        out_shape=jax.ShapeDtypeStruct((m, d), F32),
        grid=(m // tm,),
        in_specs=[pl.BlockSpec((tm, d), lambda i: (i, 0)),
                  pl.BlockSpec((1, d), lambda i: (0, 0))],
        out_specs=pl.BlockSpec((tm, d), lambda i: (i, 0)),
        compiler_params=_cparams(("parallel",)),
        name="final_rmsnorm",
    )(x, g.reshape(1, d))


def _gmlp_kernel(z_ref, lng_ref, lnb_ref, ws_ref, bst_ref, o_ref, *, n_chunks):
    row = lax.broadcasted_iota(jnp.int32, (CHUNK, CHUNK), 0)
    col = lax.broadcasted_iota(jnp.int32, (CHUNK, CHUNK), 1)
    causal = row >= col
    w_tril = [jnp.where(causal, ws_ref[gi], 0.0).astype(BF16) for gi in range(A_GROUPS)]
    for c in range(n_chunks):
        rows = slice(c * CHUNK, (c + 1) * CHUNK)
        u = _gelu_tanh(z_ref[rows, 0:A_WIDTH].astype(F32))
        v = _gelu_tanh(z_ref[rows, A_WIDTH:2 * A_WIDTH].astype(F32))
        vc = v - jnp.mean(v, axis=-1, keepdims=True)
        vn = vc * lax.rsqrt(jnp.mean(vc * vc, axis=-1, keepdims=True) + EPS)
        vn = vn * lng_ref[...] + lnb_ref[...]
        for gi in range(A_GROUPS):
            cols = slice(gi * HEAD_DIM, (gi + 1) * HEAD_DIM)
            sv = _dot(w_tril[gi], vn[:, cols].astype(BF16)) + bst_ref[:, gi:gi + 1]
            o_ref[rows, cols] = (u[:, cols] * sv).astype(o_ref.dtype)


def gmlp(z, ln_g, ln_b, w_s, b_s, *, tm):
    t = z.shape[0]
    return pl.pallas_call(
        functools.partial(_gmlp_kernel, n_chunks=tm // CHUNK),
        out_shape=jax.ShapeDtypeStruct((t, A_WIDTH), BF16),
        grid=(t // tm,),
        in_specs=[
            pl.BlockSpec((tm, 2 * A_WIDTH), lambda i: (i, 0)),
            pl.BlockSpec((1, A_WIDTH), lambda i: (0, 0)),
            pl.BlockSpec((1, A_WIDTH), lambda i: (0, 0)),
            pl.BlockSpec((A_GROUPS, CHUNK, CHUNK), lambda i: (0, 0, 0)),
            pl.BlockSpec((CHUNK, A_GROUPS), lambda i: (0, 0)),
        ],
        out_specs=pl.BlockSpec((tm, A_WIDTH), lambda i: (i, 0)),
        compiler_params=_cparams(("parallel",)),
        name="gmlp_gating",
    )(z, ln_g.reshape(1, A_WIDTH), ln_b.reshape(1, A_WIDTH), w_s, b_s.T)


def _gate_column(gate_ref, col):
    g = 1.0 / (1.0 + jnp.exp(-gate_ref[...].astype(F32)))
    lane = lax.broadcasted_iota(jnp.int32, g.shape, 1)
    return jnp.sum(jnp.where(lane == col, g, 0.0), axis=-1, keepdims=True)


def _flash_kernel(q_ref, k_ref, v_ref, sel_ref, gate_ref, o_ref, qs_ref, kb_ref, vb_ref, m_ref, accl_ref, *,
                  n_rep, mode, max_dist, branch):

    tq, tk, ch = ATT_TQ, ATT_TK, ATT_CH
    seq = k_ref.shape[0]
    i = pl.program_id(2)
    select = mode == "select"

    @pl.when(i == 0)
    def _():
        def prep(c, carry):
            r0 = pl.multiple_of(c * tk, tk)
            kb_ref[pl.ds(r0, tk), 0:HEAD_DIM] = k_ref[pl.ds(r0, tk), :].astype(BF16)
            if select:
                blk = (r0 + lax.broadcasted_iota(jnp.int32, (tk, LANE), 0)) >> 6
                lane = lax.broadcasted_iota(jnp.int32, (tk, LANE), 1)
                kb_ref[pl.ds(r0, tk), HEAD_DIM:HEAD_DIM + LANE] = jnp.where(blk == lane, 1.0, 0.0).astype(BF16)
            vb_ref[pl.ds(r0, tk), 0:HEAD_DIM] = v_ref[pl.ds(r0, tk), :].astype(BF16)
            vb_ref[pl.ds(r0, tk), HEAD_DIM:2 * HEAD_DIM] = jnp.ones((tk, HEAD_DIM), BF16)
            return carry

        lax.fori_loop(0, seq // tk, prep, 0)

    qscale = (HEAD_DIM ** -0.5) * LOG2E
    if select:
        pen = jnp.where(sel_ref[...] > 0.5, 0.0, NEG_INF).astype(BF16)
    for r in range(n_rep):
        rows = slice(r * tq, (r + 1) * tq)
        qs_ref[rows, 0:HEAD_DIM] = (q_ref[:, r * HEAD_DIM:(r + 1) * HEAD_DIM].astype(F32) * qscale).astype(BF16)
        if select:
            qs_ref[rows, HEAD_DIM:HEAD_DIM + LANE] = pen
    m_ref[...] = jnp.full(m_ref.shape, NEG_INF, F32)
    accl_ref[...] = jnp.zeros(accl_ref.shape, F32)

    row = lax.broadcasted_iota(jnp.int32, (tq, tk), 0)
    col = lax.broadcasted_iota(jnp.int32, (tq, tk), 1)

    def tile_bias(off):
        dist = off * tk + row - col
        if off == 0:
            ok = dist >= 0
            if not select and max_dist < tk - 1:
                ok = ok & (dist <= max_dist)
        elif not select and (off + 1) * tk - 1 > max_dist:
            ok = dist <= max_dist
        else:
            return None
        return jnp.where(ok, 0.0, NEG_INF)

    n_ch = n_rep * tq // ch

    def steps(tiles):
        scores = []
        for kt, bias in tiles:
            k0 = pl.multiple_of(kt * tk, tk)
            scores.append((_dot_nt(qs_ref[...], kb_ref[pl.ds(k0, tk), :]), k0, bias))
        for s_all, k0, bias in scores:
            ps, alphas = [], []
            for c in range(n_ch):
                rows = slice(c * ch, (c + 1) * ch)
                s = s_all[rows, :]
                if bias is not None:
                    b0 = (c % (tq // ch)) * ch
                    s = s + bias[b0:b0 + ch, :]
                m_old = m_ref[rows, :]
                m_new = jnp.maximum(m_old, jnp.max(s, axis=-1, keepdims=True))
                alphas.append(jnp.exp2(m_old - m_new))
                ps.append(jnp.exp2((s - jnp.concatenate([m_new] * (tk // LANE), axis=1)).astype(BF16)))
                m_ref[rows, :] = m_new
            pv_all = _dot(jnp.concatenate(ps, axis=0), vb_ref[pl.ds(k0, tk), :])
            for c in range(n_ch):
                rows = slice(c * ch, (c + 1) * ch)
                alpha2 = jnp.concatenate([alphas[c], alphas[c]], axis=1)
                accl_ref[rows, :] = alpha2 * accl_ref[rows, :] + pv_all[rows, :]

    if select:
        def body(j, carry):
            steps([(SEL_UNROLL * j + u, None) for u in range(SEL_UNROLL)])
            return carry

        lax.fori_loop(0, i // SEL_UNROLL, body, 0)

        for rem in range(SEL_UNROLL):
            @pl.when(i % SEL_UNROLL == rem)
            def _(rem=rem):
                steps([(i - rem + u, None) for u in range(rem)] + [(i, tile_bias(0))])
    else:
        n_prev = -(-max_dist // tk)

        @pl.when(i >= n_prev)
        def _():
            steps([(i - off, tile_bias(off)) for off in range(n_prev, -1, -1)])

        for first in range(n_prev):
            @pl.when(i == first)
            def _(first=first):
                steps([(first - off, tile_bias(off)) for off in range(first, -1, -1)])

    head0 = pl.program_id(1) * n_rep
    for r in range(n_rep):
        rows = slice(r * tq, (r + 1) * tq)
        cols = slice(r * HEAD_DIM, (r + 1) * HEAD_DIM)
        gate = _gate_column(gate_ref, (head0 + r) * 3 + branch)
        o = accl_ref[rows, 0:HEAD_DIM] / accl_ref[rows, HEAD_DIM:2 * HEAD_DIM]
        o_ref[:, cols] = (gate * o).astype(o_ref.dtype)


def _flash_band_kernel(q_ref, k_ref, v_ref, gate_ref, o_ref, *scratch, **static):
    _flash_kernel(q_ref, k_ref, v_ref, None, gate_ref, o_ref, *scratch, **static)


def flash_attention(z, sel_arr, *, n_batch, n_grp, n_rep, seq, q_col, k_col, v_col, mode, branch, max_dist=0):
    tq = ATT_TQ
    nq = seq // tq
    qw = n_rep * HEAD_DIM
    kw = HEAD_DIM + (LANE if mode == "select" else 0)
    in_specs = [
        pl.BlockSpec((tq, qw), lambda b, n, i: (b * nq + i, q_col(n))),
        pl.BlockSpec((seq, HEAD_DIM), lambda b, n, i: (b, k_col(n))),
        pl.BlockSpec((seq, HEAD_DIM), lambda b, n, i: (b, v_col(n))),
    ]
    args = [z, z, z]
    if mode == "select":
        in_specs.append(pl.BlockSpec((tq, LANE), lambda b, n, i: ((b * n_grp + n) * nq + i, 0)))
        args.append(sel_arr)
    in_specs.append(pl.BlockSpec((tq, LANE), lambda b, n, i: (b * nq + i, COL_GATE // LANE)))
    args.append(z)
    body = _flash_kernel if mode == "select" else _flash_band_kernel
    return pl.pallas_call(
        functools.partial(body, n_rep=n_rep, mode=mode, max_dist=max_dist, branch=branch),
        out_shape=jax.ShapeDtypeStruct((n_batch * seq, n_grp * qw), F32),
        grid=(n_batch, n_grp, nq),
        in_specs=in_specs,
        out_specs=pl.BlockSpec((tq, qw), lambda b, n, i: (b * nq + i, n)),
        scratch_shapes=[
            pltpu.VMEM((n_rep * tq, kw), BF16),
            pltpu.VMEM((seq, kw), BF16),
            pltpu.VMEM((seq, 2 * HEAD_DIM), BF16),
            pltpu.VMEM((n_rep * tq, LANE), F32),
            pltpu.VMEM((n_rep * tq, 2 * HEAD_DIM), F32),
        ],
        compiler_params=_cparams(("parallel", "parallel", "arbitrary")),
        name="flash_" + mode,
    )(*args)


DIL_CH = 128
DIL_KW = 2 * DIL_CH
DIL_UNITS = 8


def _dilated_kernel(q_in, k_in, v_in, o_ref, q_ref, k_ref, v_ref, m_ref, acc_ref, l_ref):
    ch, kw = DIL_CH, DIL_KW
    seq = q_in.shape[0]
    for src, dst in ((q_in, q_ref), (k_in, k_ref), (v_in, v_ref)):
        def stage(c, carry, src=src, dst=dst):
            r0 = pl.multiple_of(c * NORM_ROWS, NORM_ROWS)
            dst[pl.ds(r0, NORM_ROWS), :] = src[pl.ds(r0, NORM_ROWS), :].astype(F32)
            return carry
        lax.fori_loop(0, seq // NORM_ROWS, stage, 0)
    qscale = (HEAD_DIM ** -0.5) * LOG2E
    row = lax.broadcasted_iota(jnp.int32, (ch, kw), 0)
    col = lax.broadcasted_iota(jnp.int32, (ch, kw), 1)
    ones = jnp.ones((kw, HEAD_DIM), BF16)
    patterns = sorted(DILATED_PAIRS, key=lambda wd: -wd[1])

    def rows(start, n, dil):
        return pl.ds(start, n) if dil == 1 else pl.ds(start, n, stride=dil)

    def band_bias(first, span):
        dist = row - col if first else ch + row - col
        return jnp.where((dist >= 0) & (dist <= span), 0.0, NEG_INF)

    def run_units(units, dil, span, phase):
        windows = {}
        scores = []
        for q0, k0, first, wid in units:
            if wid not in windows:
                k = k_ref[rows(k0, kw, dil), :].astype(BF16)
                v = v_ref[rows(k0, kw, dil), :].astype(BF16)
                windows[wid] = (k, jnp.concatenate([v, ones], axis=1))
            q = (q_ref[rows(q0, ch, dil), :] * qscale).astype(BF16)
            scores.append(_dot_nt(q, windows[wid][0]) + band_bias(first, span))
        for (q0, k0, first, wid), s in zip(units, scores):
            qrows = rows(q0, ch, dil)
            m_cur = jnp.max(s, axis=-1, keepdims=True)
            if phase == "first":
                m_new = jnp.broadcast_to(m_cur, (ch, LANE))
            else:
                m_old = m_ref[qrows, :]
                m_new = jnp.maximum(m_old, m_cur)
                alpha = jnp.exp2(m_old - m_new)
            p = jnp.exp2((s - jnp.concatenate([m_new] * (kw // LANE), axis=1)).astype(BF16))
            pv = _dot(p, windows[wid][1])
            acc, l = pv[:, 0:HEAD_DIM], pv[:, HEAD_DIM:2 * HEAD_DIM]
            if phase != "first":
                acc = alpha * acc_ref[qrows, :] + acc
                l = alpha * l_ref[qrows, :] + l
            if phase == "last":
                o_ref[pl.ds(pl.multiple_of(q0, ch), ch), :] = (acc / l).astype(o_ref.dtype)
            else:
                m_ref[qrows, :] = m_new
                acc_ref[qrows, :] = acc
                l_ref[qrows, :] = l

    for idx, (window, dil) in enumerate(patterns):
        phase = "first" if idx == 0 else ("last" if idx == len(patterns) - 1 else "mid")
        assert phase != "last" or dil == 1
        span = window // dil
        assert span <= ch
        n_units = seq // dil // ch
        per_class = min(n_units, max(DIL_UNITS // dil, 2))
        n_classes = DIL_UNITS // per_class
        assert n_units % per_class == 0 and dil % n_classes == 0

        def group(r0, j0, head, dil=dil, span=span, phase=phase, per_class=per_class, n_classes=n_classes):
            units = []
            for rc in range(n_classes):
                r = r0 + rc
                for u in range(per_class):
                    first = head and u == 0
                    q0 = r + dil * ch * (j0 + u)
                    k0 = r if (head and u <= 1) else r + dil * ch * (j0 + u - 1)
                    wid = (rc, 0) if (head and u <= 1) else (rc, u)
                    units.append((q0, k0, first, wid))
            run_units(units, dil, span, phase)

        def class_loop(g, carry, group=group, per_class=per_class, n_classes=n_classes, n_units=n_units):
            r0 = g * n_classes
            group(r0, 0, True)
            if n_units > per_class:
                def tail(jb, c):
                    group(r0, jb * per_class, False)
                    return c
                lax.fori_loop(1, n_units // per_class, tail, 0)
            return carry

        if dil // n_classes == 1:
            class_loop(0, 0)
        else:
            lax.fori_loop(0, dil // n_classes, class_loop, 0)


def dilated_attention(z, *, n_batch, seq):
    bw = B_HEADS * HEAD_DIM
    base = COL_B // HEAD_DIM
    blk = (seq, HEAD_DIM)
    return pl.pallas_call(
        _dilated_kernel,
        out_shape=jax.ShapeDtypeStruct((n_batch * seq, bw), BF16),
        grid=(n_batch, B_HEADS),
        in_specs=[
            pl.BlockSpec(blk, lambda b, h: (b, base + h)),
            pl.BlockSpec(blk, lambda b, h: (b, base + B_HEADS + h)),
            pl.BlockSpec(blk, lambda b, h: (b, base + 2 * B_HEADS + h)),
        ],
        out_specs=pl.BlockSpec(blk, lambda b, h: (b, h)),
        scratch_shapes=[pltpu.VMEM(blk, F32)] * 6,
        compiler_params=_cparams(("parallel", "parallel")),
        name="dilated_attention",
    )(z, z, z)


N_SUB = 256
SUB_W = CMP_STRIDE * HEAD_DIM


def _compress_kernel(x_in, pos_ref, w1_ref, w2_ref, o_ref, x_ref):
    x_ref[...] = x_in[...].astype(F32)
    p = jnp.zeros((N_SUB, HEAD_DIM), F32)
    q = jnp.zeros((N_SUB, HEAD_DIM), F32)
    for i in range(CMP_STRIDE):
        a = x_ref[pl.ds(i, N_SUB, stride=CMP_STRIDE), :]
        top = (a + pos_ref[i:i + 1, :]).astype(BF16)
        bot = (a + pos_ref[CMP_STRIDE + i:CMP_STRIDE + i + 1, :]).astype(BF16)
        p = p + _dot(top, w1_ref[i * HEAD_DIM:(i + 1) * HEAD_DIM, :])
        q = q + _dot(bot, w1_ref[(CMP_STRIDE + i) * HEAD_DIM:(CMP_STRIDE + i + 1) * HEAD_DIM, :])
    h = _gelu_tanh(p + pltpu.roll(q, N_SUB - 1, 0))
    o_ref[...] = _dot(h.astype(BF16), w2_ref[...]).astype(o_ref.dtype)


def compress(z, pos, w1, w2, layer, *, n_batch, seq):
    return pl.pallas_call(
        _compress_kernel,
        out_shape=jax.ShapeDtypeStruct((n_batch * 4 * N_SUB, HEAD_DIM), BF16),
        grid=(n_batch, 4),
        in_specs=[
            pl.BlockSpec((seq, HEAD_DIM), lambda b, n: (b, COL_CMP // HEAD_DIM + n)),
            pl.BlockSpec((CMP_LEN, HEAD_DIM), lambda b, n: (0, 0)),
            pl.BlockSpec((None, None, CMP_LEN * HEAD_DIM, HEAD_DIM), lambda b, n: (layer, n // 2, 0, 0)),
            pl.BlockSpec((None, None, HEAD_DIM, HEAD_DIM), lambda b, n: (layer, n // 2, 0, 0)),
        ],
        out_specs=pl.BlockSpec((N_SUB, HEAD_DIM), lambda b, n: (b * 4 + n, 0)),
        scratch_shapes=[pltpu.VMEM((seq, HEAD_DIM), F32)],
        compiler_params=_cparams(("parallel", "parallel")),
        name="nsa_compress",
    )(z, pos, w1, w2)


N_SEL = 64


def _csel_kernel(q_ref, kc_ref, vc_ref, ovt_ref, gate_ref, oc_ref, sel_ref, qs_ref, sc_ref):
    tq, ch = ATT_TQ, ATT_CH
    n_sub = tq // ch
    q0 = pl.program_id(2) * tq
    qscale = (HEAD_DIM ** -0.5) * LOG2E
    for r in range(C_GROUP):
        qs_ref[r * tq:(r + 1) * tq, :] = (q_ref[:, r * HEAD_DIM:(r + 1) * HEAD_DIM].astype(F32)
                                          * qscale).astype(BF16)
    s_all = _dot_nt(qs_ref[...], kc_ref[...])
    n_cols = lax.broadcasted_iota(jnp.int32, (ch, N_SUB), 1)
    bias, has_any = [], []
    for c in range(n_sub):
        t = q0 + c * ch + lax.broadcasted_iota(jnp.int32, (ch, N_SUB), 0)
        bias.append(jnp.where((n_cols * CMP_STRIDE + (CMP_LEN - 1)) <= t, 0.0, NEG_INF))
        has_any.append(t[:, 0:1] >= CMP_LEN - 1)
    ps = []
    psum = [jnp.zeros((ch, N_SUB), F32) for _ in range(n_sub)]
    for c in range(C_GROUP * n_sub):
        s = s_all[c * ch:(c + 1) * ch, :] + bias[c % n_sub]
        e = jnp.exp2(s - jnp.max(s, axis=-1, keepdims=True))
        den = jnp.sum(e, axis=-1, keepdims=True)
        p = e * jnp.where(has_any[c % n_sub], 1.0 / den, 0.0)
        ps.append(p.astype(BF16))
        psum[c % n_sub] = psum[c % n_sub] + p
    o_all = _dot(jnp.concatenate(ps, axis=0), vc_ref[...])
    head0 = pl.program_id(1) * C_GROUP
    for r in range(C_GROUP):
        gate = _gate_column(gate_ref, (head0 + r) * 3)
        oc_ref[:, r * HEAD_DIM:(r + 1) * HEAD_DIM] = gate * o_all[r * tq:(r + 1) * tq, :]
    psum = jnp.concatenate(psum, axis=0)
    hi = psum.astype(BF16)
    lo = (psum - hi.astype(F32)).astype(BF16)
    ovt = ovt_ref[...]
    imp_t = _dot_nt(ovt, hi) + _dot_nt(ovt, lo)
    jrow = lax.broadcasted_iota(jnp.int32, (LANE, tq), 0)
    tl = q0 + lax.broadcasted_iota(jnp.int32, (LANE, tq), 1)
    jt = tl >> 6
    forced = (jrow == 0) | (jrow == jt) | (jrow == jt - 1)
    valid_s = jrow * SEL_LEN <= tl
    score = jnp.where(forced, 1e4, jnp.where(valid_s, imp_t, -1.0))
    sc_ref[...] = score
    n_grp = N_SEL // SUBLANE
    grp = [sc_ref[g * SUBLANE:(g + 1) * SUBLANE, :] for g in range(n_grp)]
    cnt = [jnp.zeros((SUBLANE, tq), F32) for _ in range(n_grp)]
    sub = lax.broadcasted_iota(jnp.int32, (SUBLANE, tq), 0)
    for j in range(N_SEL):
        row = sc_ref[j:j + 1, :]
        for g in range(n_grp):
            if g < j // SUBLANE:
                beats = row > grp[g]
            elif g > j // SUBLANE:
                beats = row >= grp[g]
            else:
                beats = (row > grp[g]) | ((row == grp[g]) & (sub > j % SUBLANE))
            cnt[g] = cnt[g] + jnp.where(beats, 1.0, 0.0)
    sel_t = [jnp.where(c < float(SEL_TOP), 1.0, 0.0) for c in cnt]
    sel_pad = jnp.concatenate(sel_t + [jnp.zeros((LANE - N_SEL, tq), F32)], axis=0)
    sel_ref[...] = sel_pad.T


def compressed_select(z, kvc, ovt, *, n_batch, seq):
    tq = ATT_TQ
    nq = seq // tq
    qw = C_GROUP * HEAD_DIM
    return pl.pallas_call(
        _csel_kernel,
        out_shape=[jax.ShapeDtypeStruct((n_batch * seq, C_HEADS * HEAD_DIM), F32),
                   jax.ShapeDtypeStruct((n_batch * C_KV_HEADS * seq, LANE), F32)],
        grid=(n_batch, C_KV_HEADS, nq),
        in_specs=[
            pl.BlockSpec((tq, qw), lambda b, g, i: (b * nq + i, COL_CQ // qw + g)),
            pl.BlockSpec((N_SUB, HEAD_DIM), lambda b, g, i: (b * 4 + g, 0)),
            pl.BlockSpec((N_SUB, HEAD_DIM), lambda b, g, i: (b * 4 + 2 + g, 0)),
            pl.BlockSpec((LANE, N_SUB), lambda b, g, i: (0, 0)),
            pl.BlockSpec((tq, LANE), lambda b, g, i: (b * nq + i, COL_GATE // LANE)),
        ],
        out_specs=[
            pl.BlockSpec((tq, qw), lambda b, g, i: (b * nq + i, g)),
            pl.BlockSpec((tq, LANE), lambda b, g, i: ((b * C_KV_HEADS + g) * nq + i, 0)),
        ],
        scratch_shapes=[pltpu.VMEM((C_GROUP * tq, HEAD_DIM), BF16), pltpu.VMEM((LANE, tq), F32)],
        compiler_params=_cparams(("parallel", "parallel", "parallel")),
        name="nsa_compressed_select",
    )(z, kvc, kvc, ovt, z)


def _nsa_kernel(q_ref, kc_ref, vc_ref, ovt_ref, ks_ref, vs_ref, kw_ref, vw_ref, gate_ref, o_ref,
                qs_ref, kbs_ref, vbs_ref, kbw_ref, vbw_ref, ms_ref, as_ref, mw_ref, aw_ref, oc_ref, sc_ref):
    tq, tk, ch = ATT_TQ, ATT_TK, ATT_CH
    n_rep = C_GROUP
    n_sub = tq // ch
    n_ch = n_rep * n_sub
    seq = ks_ref.shape[0]
    i = pl.program_id(2)
    q0 = i * tq

    @pl.when(i == 0)
    def _():
        def prep(c, carry):
            rows = pl.ds(pl.multiple_of(c * tk, tk), tk)
            blk = (c * tk + lax.broadcasted_iota(jnp.int32, (tk, LANE), 0)) >> 6
            lane = lax.broadcasted_iota(jnp.int32, (tk, LANE), 1)
            ones = jnp.ones((tk, HEAD_DIM), BF16)
            kbs_ref[rows, 0:HEAD_DIM] = ks_ref[rows, :]
            kbs_ref[rows, HEAD_DIM:HEAD_DIM + LANE] = jnp.where(blk == lane, 1.0, 0.0).astype(BF16)
            vbs_ref[rows, 0:HEAD_DIM] = vs_ref[rows, :]
            vbs_ref[rows, HEAD_DIM:2 * HEAD_DIM] = ones
            kbw_ref[rows, :] = kw_ref[rows, :]
            vbw_ref[rows, 0:HEAD_DIM] = vw_ref[rows, :]
            vbw_ref[rows, HEAD_DIM:2 * HEAD_DIM] = ones
            return carry

        lax.fori_loop(0, seq // tk, prep, 0)

    qscale = (HEAD_DIM ** -0.5) * LOG2E
    for r in range(n_rep):
        qs_ref[r * tq:(r + 1) * tq, 0:HEAD_DIM] = (q_ref[:, r * HEAD_DIM:(r + 1) * HEAD_DIM].astype(F32)
                                                   * qscale).astype(BF16)

    for m_ref, accl_ref in ((ms_ref, as_ref), (mw_ref, aw_ref)):
        m_ref[...] = jnp.full(m_ref.shape, NEG_INF, F32)
        accl_ref[...] = jnp.zeros(accl_ref.shape, F32)

    q_idx = lax.broadcasted_iota(jnp.int32, (tq, tk), 0)
    k_idx = lax.broadcasted_iota(jnp.int32, (tq, tk), 1)

    def tile_bias(off, max_dist):
        dist = off * tk + q_idx - k_idx
        if off == 0:
            ok = dist >= 0
        elif max_dist is not None and (off + 1) * tk - 1 > max_dist:
            ok = dist <= max_dist
        else:
            return None
        return jnp.where(ok, 0.0, NEG_INF)

    def issue(tiles, load_q, kb_ref):
        scores = []
        for kt, tbias in tiles:
            k0 = pl.multiple_of(kt * tk, tk)
            scores.append((_dot_nt(load_q(), kb_ref[pl.ds(k0, tk), :]), k0, tbias))
        return scores

    def finish(scores, vb_ref, m_ref, accl_ref):
        for s_tile, k0, tbias in scores:
            pts, alphas = [], []
            for c in range(n_ch):
                rows = slice(c * ch, (c + 1) * ch)
                s = s_tile[rows, :]
                if tbias is not None:
                    b0 = (c % n_sub) * ch
                    s = s + tbias[b0:b0 + ch, :]
                m_old = m_ref[rows, :]
                m_new = jnp.maximum(m_old, jnp.max(s, axis=-1, keepdims=True))
                alphas.append(jnp.exp2(m_old - m_new))
                pts.append(jnp.exp2((s - jnp.concatenate([m_new] * (tk // LANE), axis=1)).astype(BF16)))
                m_ref[rows, :] = m_new
            pv = _dot(jnp.concatenate(pts, axis=0), vb_ref[pl.ds(k0, tk), :])
            for c in range(n_ch):
                rows = slice(c * ch, (c + 1) * ch)
                alpha2 = jnp.concatenate([alphas[c], alphas[c]], axis=1)
                accl_ref[rows, :] = alpha2 * accl_ref[rows, :] + pv[rows, :]

    max_dist = WIN_LEN - 1
    win_tiles = []
    for off in range(-(-max_dist // tk), -1, -1):
        tbias = tile_bias(off, max_dist)
        if off > 0:
            before_start = jnp.where(i >= off, 0.0, NEG_INF)
            if tbias is None:
                tbias = jnp.where(q_idx + k_idx >= 0, 0.0, NEG_INF)
            tbias = tbias + before_start
        win_tiles.append((jnp.maximum(i - off, 0), tbias))
    win_scores = issue(win_tiles, lambda: qs_ref[:, 0:HEAD_DIM], kbw_ref)

    s_all = _dot_nt(qs_ref[:, 0:HEAD_DIM], kc_ref[...])
    n_cols = lax.broadcasted_iota(jnp.int32, (ch, N_SUB), 1)
    bias, has_any = [], []
    for c in range(n_sub):
        t = q0 + c * ch + lax.broadcasted_iota(jnp.int32, (ch, N_SUB), 0)
        bias.append(jnp.where((n_cols * CMP_STRIDE + (CMP_LEN - 1)) <= t, 0.0, NEG_INF))
        has_any.append(t[:, 0:1] >= CMP_LEN - 1)
    ps = []
    psum = [jnp.zeros((ch, N_SUB), F32) for _ in range(n_sub)]
    for c in range(n_ch):
        s = s_all[c * ch:(c + 1) * ch, :] + bias[c % n_sub]
        e = jnp.exp2(s - jnp.max(s, axis=-1, keepdims=True))
        den = jnp.sum(e, axis=-1, keepdims=True)
        p = e * jnp.where(has_any[c % n_sub], 1.0 / den, 0.0)
        ps.append(p.astype(BF16))
        psum[c % n_sub] = psum[c % n_sub] + p
    oc_ref[...] = _dot(jnp.concatenate(ps, axis=0), vc_ref[...])
    psum = jnp.concatenate(psum, axis=0)

    hi = psum.astype(BF16)
    lo = (psum - hi.astype(F32)).astype(BF16)
    ovt = ovt_ref[...]
    imp_t = _dot_nt(ovt, hi) + _dot_nt(ovt, lo)
    jrow = lax.broadcasted_iota(jnp.int32, (LANE, tq), 0)
    tl = q0 + lax.broadcasted_iota(jnp.int32, (LANE, tq), 1)
    jt = tl >> 6
    forced = (jrow == 0) | (jrow == jt) | (jrow == jt - 1)
    valid_s = jrow * SEL_LEN <= tl
    sc_ref[...] = jnp.where(forced, 1e4, jnp.where(valid_s, imp_t, -1.0))
    finish(win_scores, vbw_ref, mw_ref, aw_ref)
    n_grp = N_SEL // SUBLANE
    grp = [sc_ref[g * SUBLANE:(g + 1) * SUBLANE, :] for g in range(n_grp)]
    cnt = [jnp.zeros((SUBLANE, tq), F32) for _ in range(n_grp)]
    sub = lax.broadcasted_iota(jnp.int32, (SUBLANE, tq), 0)
    for j in range(N_SEL):
        row = sc_ref[j:j + 1, :]
        for g in range(n_grp):
            if g < j // SUBLANE:
                beats = row > grp[g]
            elif g > j // SUBLANE:
                beats = row >= grp[g]
            else:
                beats = (row > grp[g]) | ((row == grp[g]) & (sub > j % SUBLANE))
            cnt[g] = cnt[g] + jnp.where(beats, 1.0, 0.0)
    pen_t = [jnp.where(c < float(SEL_TOP), 0.0, NEG_INF) for c in cnt]
    pen_t = jnp.concatenate(pen_t + [jnp.full((LANE - N_SEL, tq), NEG_INF, F32)], axis=0)
    pen = pen_t.T.astype(BF16)
    for r in range(n_rep):
        qs_ref[r * tq:(r + 1) * tq, HEAD_DIM:HEAD_DIM + LANE] = pen

    def sel_steps(tiles):
        finish(issue(tiles, lambda: qs_ref[...], kbs_ref), vbs_ref, ms_ref, as_ref)

    def sel_body(j, carry):
        sel_steps([(SEL_UNROLL * j + u, None) for u in range(SEL_UNROLL)])
        return carry

    lax.fori_loop(0, i // SEL_UNROLL, sel_body, 0)

    for rem in range(SEL_UNROLL):
        @pl.when(i % SEL_UNROLL == rem)
        def _(rem=rem):
            sel_steps([(i - rem + u, None) for u in range(rem)] + [(i, tile_bias(0, None))])

    g = 1.0 / (1.0 + jnp.exp(-gate_ref[...].astype(F32)))
    lane = lax.broadcasted_iota(jnp.int32, g.shape, 1)
    head0 = pl.program_id(1) * n_rep

    def gate(head, branch):
        return jnp.sum(jnp.where(lane == head * 3 + branch, g, 0.0), axis=-1, keepdims=True)

    for r in range(n_rep):
        rows = slice(r * tq, (r + 1) * tq)
        o = (gate(head0 + r, 0) * oc_ref[rows, :]
             + gate(head0 + r, 1) * (as_ref[rows, 0:HEAD_DIM] / as_ref[rows, HEAD_DIM:2 * HEAD_DIM])
             + gate(head0 + r, 2) * (aw_ref[rows, 0:HEAD_DIM] / aw_ref[rows, HEAD_DIM:2 * HEAD_DIM]))
        o_ref[:, r * HEAD_DIM:(r + 1) * HEAD_DIM] = o.astype(o_ref.dtype)


def nsa_attention(z, kvc, ovt, *, n_batch, seq):
    tq = ATT_TQ
    nq = seq // tq
    n_rep = C_GROUP
    qw = n_rep * HEAD_DIM
    kv_blk = (seq, HEAD_DIM)
    sel0, win0 = COL_SEL // HEAD_DIM, COL_WIN // HEAD_DIM
    return pl.pallas_call(
        _nsa_kernel,
        out_shape=jax.ShapeDtypeStruct((n_batch * seq, C_HEADS * HEAD_DIM), BF16),
        grid=(n_batch, C_KV_HEADS, nq),
        in_specs=[
            pl.BlockSpec((tq, qw), lambda b, g, i: (b * nq + i, COL_CQ // qw + g)),
            pl.BlockSpec((N_SUB, HEAD_DIM), lambda b, g, i: (b * 4 + g, 0)),
            pl.BlockSpec((N_SUB, HEAD_DIM), lambda b, g, i: (b * 4 + 2 + g, 0)),
            pl.BlockSpec((LANE, N_SUB), lambda b, g, i: (0, 0)),
            pl.BlockSpec(kv_blk, lambda b, g, i: (b, sel0 + g)),
            pl.BlockSpec(kv_blk, lambda b, g, i: (b, sel0 + C_KV_HEADS + g)),
            pl.BlockSpec(kv_blk, lambda b, g, i: (b, win0 + g)),
            pl.BlockSpec(kv_blk, lambda b, g, i: (b, win0 + C_KV_HEADS + g)),
            pl.BlockSpec((tq, LANE), lambda b, g, i: (b * nq + i, COL_GATE // LANE)),
        ],
        out_specs=pl.BlockSpec((tq, qw), lambda b, g, i: (b * nq + i, g)),
        scratch_shapes=[
            pltpu.VMEM((n_rep * tq, HEAD_DIM + LANE), BF16),
            pltpu.VMEM((seq, HEAD_DIM + LANE), BF16),
            pltpu.VMEM((seq, 2 * HEAD_DIM), BF16),
            pltpu.VMEM((seq, HEAD_DIM), BF16),
            pltpu.VMEM((seq, 2 * HEAD_DIM), BF16),
            pltpu.VMEM((n_rep * tq, LANE), F32),
            pltpu.VMEM((n_rep * tq, 2 * HEAD_DIM), F32),
            pltpu.VMEM((n_rep * tq, LANE), F32),
            pltpu.VMEM((n_rep * tq, 2 * HEAD_DIM), F32),
            pltpu.VMEM((n_rep * tq, HEAD_DIM), F32),
            pltpu.VMEM((LANE, tq), F32),
        ],
        compiler_params=_cparams(("parallel", "parallel", "arbitrary")),
        name="nsa_attention",
    )(z, kvc, kvc, ovt, z, z, z, z, z)


def _overlap_t():
    n_c = N_SUB - 1
    c_start = np.arange(n_c) * CMP_STRIDE
    s_start = np.arange(N_SEL) * SEL_LEN
    ov = ((c_start[:, None] <= s_start[None, :] + SEL_LEN - 1)
          & (c_start[:, None] + CMP_LEN - 1 >= s_start[None, :])).astype(np.float32)
    out = np.zeros((LANE, N_SUB), np.float32)
    out[:N_SEL, :n_c] = ov.T
    return out


def _mix_out_kernel(x_ref, oa_ref, ob_ref, oc_ref, w_ref, o_ref):
    mix = jnp.concatenate([oa_ref[...], ob_ref[...], oc_ref[...]], axis=1)
    o_ref[...] = x_ref[...] + _dot(mix, w_ref[...])


def mix_out(x, out_a, out_b, out_c, w, layer, *, tm):
    m, n = x.shape
    return pl.pallas_call(
        _mix_out_kernel,
        out_shape=jax.ShapeDtypeStruct((m, n), F32),
        grid=(m // tm,),
        in_specs=[
            pl.BlockSpec((tm, n), lambda i: (i, 0)),
            pl.BlockSpec((tm, out_a.shape[1]), lambda i: (i, 0)),
            pl.BlockSpec((tm, out_b.shape[1]), lambda i: (i, 0)),
            pl.BlockSpec((tm, out_c.shape[1]), lambda i: (i, 0)),
            pl.BlockSpec((None,) + w.shape[1:], lambda i: (layer, 0, 0), pipeline_mode=pl.Buffered(1)),
        ],
        out_specs=pl.BlockSpec((tm, n), lambda i: (i, 0)),
        compiler_params=_cparams(("parallel",)),
        name="mix_out",
    )(x, out_a, out_b, out_c, w)


def _xattn_kernel(x_ref, g_ref, wq_ref, k_ref, v_ref, wo_ref, o_ref):
    n_mem = k_ref.shape[0]
    rows = x_ref.shape[0]
    ones = jnp.ones((n_mem, HEAD_DIM), BF16)
    kv = [(k_ref[:, h * HEAD_DIM:(h + 1) * HEAD_DIM].astype(BF16),
           jnp.concatenate([v_ref[:, h * HEAD_DIM:(h + 1) * HEAD_DIM].astype(BF16), ones], axis=1))
          for h in range(X_HEADS)]
    for c in range(rows // NORM_ROWS):
        r = slice(c * NORM_ROWS, (c + 1) * NORM_ROWS)
        x = x_ref[r, :]
        hn = _rms_rows(x, g_ref[...]).astype(BF16)
        q = _dot(hn, wq_ref[...]) * ((HEAD_DIM ** -0.5) * LOG2E)
        outs = []
        for h in range(X_HEADS):
            s = _dot_nt(q[:, h * HEAD_DIM:(h + 1) * HEAD_DIM].astype(BF16), kv[h][0])
            e = jnp.exp2((s - jnp.max(s, axis=-1, keepdims=True)).astype(BF16))
            pv = _dot(e, kv[h][1])
            outs.append((pv[:, 0:HEAD_DIM] / pv[:, HEAD_DIM:2 * HEAD_DIM]).astype(BF16))
        o_ref[r, :] = x + _dot(jnp.concatenate(outs, axis=1), wo_ref[...])


def cross_attention_block(x, g, wq, kv, wo, layer, *, n_batch, seq, n_mem, tm):
    nq = seq // tm
    d = x.shape[1]
    return pl.pallas_call(
        _xattn_kernel,
        out_shape=jax.ShapeDtypeStruct(x.shape, F32),
        grid=(n_batch, nq),
        in_specs=[
            pl.BlockSpec((tm, d), lambda b, i: (b * nq + i, 0)),
            pl.BlockSpec((1, d), lambda b, i: (0, 0)),
            pl.BlockSpec((None, d, X_WIDTH), lambda b, i: (layer, 0, 0), pipeline_mode=pl.Buffered(1)),
            pl.BlockSpec((n_mem, X_WIDTH), lambda b, i: (b, 0)),
            pl.BlockSpec((n_mem, X_WIDTH), lambda b, i: (b, 1)),
            pl.BlockSpec((None, X_WIDTH, d), lambda b, i: (layer, 0, 0), pipeline_mode=pl.Buffered(1)),
        ],
        out_specs=pl.BlockSpec((tm, d), lambda b, i: (b * nq + i, 0)),
        compiler_params=_cparams(("parallel", "parallel")),
        name="cross_attention",
    )(x, g.reshape(1, d), wq, kv, kv, wo)


def kernel(x, mem, norm_mix, w_in, gmlp_ln_g, gmlp_ln_b, gmlp_w_s, gmlp_b_s, cmp_pos, cmp_k_w1, cmp_k_w2,
           cmp_v_w1, cmp_v_w2, w_out, norm_xattn, norm_mem, xattn_wq, xattn_wkv, xattn_wo, norm_mlp, w_up,
           w_down, final_norm):
    bsz, seq, d = x.shape
    n_mem = mem.shape[1]
    t = bsz * seq
    assert d == D_MODEL and seq // CMP_STRIDE == N_SUB and seq // SEL_LEN == N_SEL
    assert seq % (ATT_TQ * 16) == 0

    w_in_b = jnp.pad(w_in, ((0, 0), (0, 0), (0, IN_PAD - IN_WIDTH))).astype(BF16)
    w_out_b = w_out.astype(BF16)
    wq_b = xattn_wq.astype(BF16)
    wkv_b = xattn_wkv.astype(BF16)
    wo_b = xattn_wo.astype(BF16)
    w_up_b = w_up.astype(BF16)
    w_down_b = w_down.astype(BF16)
    cmp_w1 = jnp.stack([cmp_k_w1, cmp_v_w1], axis=1).astype(BF16)
    cmp_w2 = jnp.stack([cmp_k_w2, cmp_v_w2], axis=1).astype(BF16)
    ovt = jnp.asarray(_overlap_t()).astype(BF16)
    xf = x.reshape(t, d)
    memf = mem.reshape(bsz * n_mem, d)
    gw = C_GROUP * HEAD_DIM

    for l in range(DEPTH):
        z = norm_matmul(xf, norm_mix[l], w_in_b, l, tm=512, out_dtype=BF16)

        out_a = gmlp(z, gmlp_ln_g[l], gmlp_ln_b[l], gmlp_w_s[l], gmlp_b_s[l], tm=512)

        out_b = dilated_attention(z, n_batch=bsz, seq=seq)

        kvc = compress(z, cmp_pos[l], cmp_w1, cmp_w2, l, n_batch=bsz, seq=seq)
        out_c = nsa_attention(z, kvc, ovt, n_batch=bsz, seq=seq)

        xf = mix_out(xf, out_a, out_b, out_c, w_out_b, l, tm=512)

        kv = norm_matmul(memf, norm_mem[l], wkv_b, l, tm=512)
        xf = cross_attention_block(xf, norm_xattn[l], wq_b, kv, wo_b, l, n_batch=bsz, seq=seq, n_mem=n_mem,
                                   tm=1024)

        xf = mlp(xf, norm_mlp[l], w_up_b, w_down_b, l, final_norm, tm=1024, tf=512,
                 norm_out=(l == DEPTH - 1))

    return xf.reshape(bsz, seq, d)
```

```python
import functools
import math

import numpy as np
import jax
import jax.numpy as jnp
from jax import lax
from jax.experimental import pallas as pl
from jax.experimental.pallas import tpu as pltpu

F32 = jnp.float32
BF16 = jnp.bfloat16

D_MODEL = 2048
DEPTH = 4
HEAD_DIM = 128
A_GROUPS = 4
A_WIDTH = A_GROUPS * HEAD_DIM
CHUNK = 128
B_HEADS = 4
DILATED_PAIRS = ((128, 1), (512, 4), (2048, 16))
C_HEADS = 8
C_KV_HEADS = 2
C_GROUP = C_HEADS // C_KV_HEADS
CMP_LEN = 32
CMP_STRIDE = 16
SEL_LEN = 64
SEL_SHIFT = SEL_LEN.bit_length() - 1
SEL_TOP = 16
WIN_LEN = 512
X_HEADS = 4
X_WIDTH = X_HEADS * HEAD_DIM
EPS = 1e-6
NEG_INF = -1e30
LOG2E = math.log2(math.e)

COL_B = 2 * A_WIDTH
COL_CQ = COL_B + 3 * B_HEADS * HEAD_DIM
COL_CMP = COL_CQ + C_HEADS * HEAD_DIM
COL_SEL = COL_CMP + 2 * C_KV_HEADS * HEAD_DIM
COL_WIN = COL_SEL + 2 * C_KV_HEADS * HEAD_DIM
COL_GATE = COL_WIN + 2 * C_KV_HEADS * HEAD_DIM
IN_WIDTH = COL_GATE + 3 * C_HEADS
IN_PAD = 5248

LANE = 128
SUBLANE = 8
VMEM_LIMIT = 56 * 1024 * 1024

TM_PROJ = 512
TM_XATTN = 1024
TM_MLP = 1024
TF_MLP = 512
NORM_ROWS = 256
ATT_TQ = 256
ATT_TK = 256
ATT_CH = 128
SEL_UNROLL = 4


def _cparams(sem):
    return pltpu.CompilerParams(dimension_semantics=sem, vmem_limit_bytes=VMEM_LIMIT)


def _rms_rows(x, g):
    ms = jnp.mean(x * x, axis=-1, keepdims=True)
    return x * lax.rsqrt(ms + EPS) * g


def _gelu_tanh(x):
    c = math.sqrt(2.0 / math.pi)
    return x * (0.5 * (1.0 + jnp.tanh(c * (x + 0.044715 * (x * x * x)))))


def _dot(a, b):
    return jnp.dot(a, b, preferred_element_type=F32)


def _dot_nt(a, b):
    return lax.dot_general(a, b, (((1,), (1,)), ((), ())), preferred_element_type=F32)


def _norm_into(h_ref, x_ref, g_ref, copy_ref=None):
    rows = x_ref.shape[0]

    def body(c, carry):
        r0 = pl.multiple_of(c * NORM_ROWS, NORM_ROWS)
        x = x_ref[pl.ds(r0, NORM_ROWS), :]
        h_ref[pl.ds(r0, NORM_ROWS), :] = _rms_rows(x, g_ref[...]).astype(h_ref.dtype)
        if copy_ref is not None:
            copy_ref[pl.ds(r0, NORM_ROWS), :] = x
        return carry

    lax.fori_loop(0, rows // NORM_ROWS, body, 0)


def _norm_matmul_kernel(x_ref, g_ref, w_ref, o_ref):
    half = x_ref.shape[0] // 2
    for c in range(2):
        r = slice(c * half, (c + 1) * half)
        hn = _rms_rows(x_ref[r, :], g_ref[...]).astype(BF16)
        o_ref[r, :] = _dot(hn, w_ref[...]).astype(o_ref.dtype)


def norm_matmul(x, g, w, layer, *, tm, out_dtype=F32):
    m, k = x.shape
    n = w.shape[2]
    return pl.pallas_call(
        _norm_matmul_kernel,
        out_shape=jax.ShapeDtypeStruct((m, n), out_dtype),
        grid=(m // tm,),
        in_specs=[
            pl.BlockSpec((tm, k), lambda i: (i, 0)),
            pl.BlockSpec((1, k), lambda i: (0, 0)),
            pl.BlockSpec((None, k, n), lambda i: (layer, 0, 0), pipeline_mode=pl.Buffered(1)),
        ],
        out_specs=pl.BlockSpec((tm, n), lambda i: (i, 0)),
        compiler_params=_cparams(("parallel",)),
        name="norm_matmul",
    )(x, g.reshape(1, k), w)


def _mlp_kernel(x_ref, g_ref, wu_ref, wd_ref, gout_ref, o_ref, h_ref, *, norm_out):
    f = pl.program_id(1)

    @pl.when(f == 0)
    def _():
        _norm_into(h_ref, x_ref, g_ref, copy_ref=o_ref)

    a = _dot(h_ref[...], wu_ref[...])
    a = jnp.square(jnp.maximum(a, 0.0)).astype(BF16)
    o_ref[...] += _dot(a, wd_ref[...])

    if norm_out:
        @pl.when(f == pl.num_programs(1) - 1)
        def _():
            _norm_into(o_ref, o_ref, gout_ref)


def mlp(x, g, wu, wd, layer, g_out, *, tm, tf, norm_out):
    m, d = x.shape
    ff = wu.shape[2]
    return pl.pallas_call(
        functools.partial(_mlp_kernel, norm_out=norm_out),
        out_shape=jax.ShapeDtypeStruct((m, d), F32),
        grid=(m // tm, ff // tf),
        in_specs=[
            pl.BlockSpec((tm, d), lambda i, f: (i, 0)),
            pl.BlockSpec((1, d), lambda i, f: (0, 0)),
            pl.BlockSpec((None, d, tf), lambda i, f: (layer, 0, f)),
            pl.BlockSpec((None, tf, d), lambda i, f: (layer, f, 0)),
            pl.BlockSpec((1, d), lambda i, f: (0, 0)),
        ],
        out_specs=pl.BlockSpec((tm, d), lambda i, f: (i, 0)),
        scratch_shapes=[pltpu.VMEM((tm, d), BF16)],
        compiler_params=_cparams(("parallel", "arbitrary")),
        name="mlp",
    )(x, g.reshape(1, d), wu, wd, g_out.reshape(1, d))


def _gmlp_kernel(z_ref, lng_ref, lnb_ref, ws_ref, bst_ref, o_ref, *, n_chunks):
    row = lax.broadcasted_iota(jnp.int32, (CHUNK, CHUNK), 0)
    col = lax.broadcasted_iota(jnp.int32, (CHUNK, CHUNK), 1)
    causal = row >= col
    w_tril = [jnp.where(causal, ws_ref[gi], 0.0).astype(BF16) for gi in range(A_GROUPS)]
    for c in range(n_chunks):
        rows = slice(c * CHUNK, (c + 1) * CHUNK)
        u = _gelu_tanh(z_ref[rows, 0:A_WIDTH].astype(F32))
        v = _gelu_tanh(z_ref[rows, A_WIDTH:2 * A_WIDTH].astype(F32))
        vc = v - jnp.mean(v, axis=-1, keepdims=True)
        vn = vc * lax.rsqrt(jnp.mean(vc * vc, axis=-1, keepdims=True) + EPS)
        vn = vn * lng_ref[...] + lnb_ref[...]
        for gi in range(A_GROUPS):
            cols = slice(gi * HEAD_DIM, (gi + 1) * HEAD_DIM)
            sv = _dot(w_tril[gi], vn[:, cols].astype(BF16)) + bst_ref[:, gi:gi + 1]
            o_ref[rows, cols] = (u[:, cols] * sv).astype(o_ref.dtype)


def gmlp(z, ln_g, ln_b, w_s, b_s, *, tm):
    t = z.shape[0]
    return pl.pallas_call(
        functools.partial(_gmlp_kernel, n_chunks=tm // CHUNK),
        out_shape=jax.ShapeDtypeStruct((t, A_WIDTH), BF16),
        grid=(t // tm,),
        in_specs=[
            pl.BlockSpec((tm, 2 * A_WIDTH), lambda i: (i, 0)),
            pl.BlockSpec((1, A_WIDTH), lambda i: (0, 0)),
            pl.BlockSpec((1, A_WIDTH), lambda i: (0, 0)),
            pl.BlockSpec((A_GROUPS, CHUNK, CHUNK), lambda i: (0, 0, 0)),
            pl.BlockSpec((CHUNK, A_GROUPS), lambda i: (0, 0)),
        ],
        out_specs=pl.BlockSpec((tm, A_WIDTH), lambda i: (i, 0)),
        compiler_params=_cparams(("parallel",)),
        name="gmlp_gating",
    )(z, ln_g.reshape(1, A_WIDTH), ln_b.reshape(1, A_WIDTH), w_s, b_s.T)


DIL_CH = 128
DIL_KW = 2 * DIL_CH
DIL_UNITS = 8


def _dilated_kernel(q_in, k_in, v_in, o_ref, q_ref, k_ref, v_ref, m_ref, acc_ref, l_ref):
    ch, kw = DIL_CH, DIL_KW
    seq = q_in.shape[0]
    for src, dst in ((q_in, q_ref), (k_in, k_ref), (v_in, v_ref)):
        def stage(c, carry, src=src, dst=dst):
            r0 = pl.multiple_of(c * NORM_ROWS, NORM_ROWS)
            dst[pl.ds(r0, NORM_ROWS), :] = src[pl.ds(r0, NORM_ROWS), :].astype(F32)
            return carry
        lax.fori_loop(0, seq // NORM_ROWS, stage, 0)
    qscale = (HEAD_DIM ** -0.5) * LOG2E
    row = lax.broadcasted_iota(jnp.int32, (ch, kw), 0)
    col = lax.broadcasted_iota(jnp.int32, (ch, kw), 1)
    ones = jnp.ones((kw, HEAD_DIM), BF16)
    patterns = sorted(DILATED_PAIRS, key=lambda wd: -wd[1])

    def rows(start, n, dil):
        return pl.ds(start, n) if dil == 1 else pl.ds(start, n, stride=dil)

    def band_bias(first, span):
        dist = row - col if first else ch + row - col
        return jnp.where((dist >= 0) & (dist <= span), 0.0, NEG_INF)

    def run_units(units, dil, span, phase):
        windows = {}
        scores = []
        for q0, k0, first, wid in units:
            if wid not in windows:
                k = k_ref[rows(k0, kw, dil), :].astype(BF16)
                v = v_ref[rows(k0, kw, dil), :].astype(BF16)
                windows[wid] = (k, jnp.concatenate([v, ones], axis=1))
            q = (q_ref[rows(q0, ch, dil), :] * qscale).astype(BF16)
            scores.append(_dot_nt(q, windows[wid][0]) + band_bias(first, span))
        for (q0, k0, first, wid), s in zip(units, scores):
            qrows = rows(q0, ch, dil)
            m_cur = jnp.max(s, axis=-1, keepdims=True)
            if phase == "first":
                m_new = jnp.broadcast_to(m_cur, (ch, LANE))
            else:
                m_old = m_ref[qrows, :]
                m_new = jnp.maximum(m_old, m_cur)
                alpha = jnp.exp2(m_old - m_new)
            p = jnp.exp2((s - jnp.concatenate([m_new] * (kw // LANE), axis=1)).astype(BF16))
            pv = _dot(p, windows[wid][1])
            acc, l = pv[:, 0:HEAD_DIM], pv[:, HEAD_DIM:2 * HEAD_DIM]
            if phase != "first":
                acc = alpha * acc_ref[qrows, :] + acc
                l = alpha * l_ref[qrows, :] + l
            if phase == "last":
                o_ref[pl.ds(pl.multiple_of(q0, ch), ch), :] = (acc / l).astype(o_ref.dtype)
            else:
                m_ref[qrows, :] = m_new
                acc_ref[qrows, :] = acc
                l_ref[qrows, :] = l

    for idx, (window, dil) in enumerate(patterns):
        phase = "first" if idx == 0 else ("last" if idx == len(patterns) - 1 else "mid")
        assert phase != "last" or dil == 1
        span = window // dil
        assert span <= ch
        n_units = seq // dil // ch
        per_class = min(n_units, max(DIL_UNITS // dil, 2))
        n_classes = DIL_UNITS // per_class
        assert n_units % per_class == 0 and dil % n_classes == 0

        def group(r0, j0, head, dil=dil, span=span, phase=phase, per_class=per_class, n_classes=n_classes):
            units = []
            for rc in range(n_classes):
                r = r0 + rc
                for u in range(per_class):
                    first = head and u == 0
                    q0 = r + dil * ch * (j0 + u)
                    k0 = r if (head and u <= 1) else r + dil * ch * (j0 + u - 1)
                    wid = (rc, 0) if (head and u <= 1) else (rc, u)
                    units.append((q0, k0, first, wid))
            run_units(units, dil, span, phase)

        def class_loop(g, carry, group=group, per_class=per_class, n_classes=n_classes, n_units=n_units):
            r0 = g * n_classes
            group(r0, 0, True)
            if n_units > per_class:
                def tail(jb, c):
                    group(r0, jb * per_class, False)
                    return c
                lax.fori_loop(1, n_units // per_class, tail, 0)
            return carry

        if dil // n_classes == 1:
            class_loop(0, 0)
        else:
            lax.fori_loop(0, dil // n_classes, class_loop, 0)


def dilated_attention(z, *, n_batch, seq):
    bw = B_HEADS * HEAD_DIM
    base = COL_B // HEAD_DIM
    blk = (seq, HEAD_DIM)
    return pl.pallas_call(
        _dilated_kernel,
        out_shape=jax.ShapeDtypeStruct((n_batch * seq, bw), BF16),
        grid=(n_batch, B_HEADS),
        in_specs=[
            pl.BlockSpec(blk, lambda b, h: (b, base + h)),
            pl.BlockSpec(blk, lambda b, h: (b, base + B_HEADS + h)),
            pl.BlockSpec(blk, lambda b, h: (b, base + 2 * B_HEADS + h)),
        ],
        out_specs=pl.BlockSpec(blk, lambda b, h: (b, h)),
        scratch_shapes=[pltpu.VMEM(blk, F32)] * 6,
        compiler_params=_cparams(("parallel", "parallel")),
        name="dilated_attention",
    )(z, z, z)


N_SUB = 256
N_CMP = 2 * C_KV_HEADS


def _compress_kernel(x_in, pos_ref, w1_ref, w2_ref, o_ref, x_ref):
    x_ref[...] = x_in[...].astype(F32)
    p = jnp.zeros((N_SUB, HEAD_DIM), F32)
    q = jnp.zeros((N_SUB, HEAD_DIM), F32)
    for i in range(CMP_STRIDE):
        a = x_ref[pl.ds(i, N_SUB, stride=CMP_STRIDE), :]
        top = (a + pos_ref[i:i + 1, :]).astype(BF16)
        bot = (a + pos_ref[CMP_STRIDE + i:CMP_STRIDE + i + 1, :]).astype(BF16)
        p = p + _dot(top, w1_ref[i * HEAD_DIM:(i + 1) * HEAD_DIM, :])
        q = q + _dot(bot, w1_ref[(CMP_STRIDE + i) * HEAD_DIM:(CMP_STRIDE + i + 1) * HEAD_DIM, :])
    h = _gelu_tanh(p + pltpu.roll(q, N_SUB - 1, 0))
    o_ref[...] = _dot(h.astype(BF16), w2_ref[...]).astype(o_ref.dtype)


def compress(z, pos, w1, w2, layer, *, n_batch, seq):
    return pl.pallas_call(
        _compress_kernel,
        out_shape=jax.ShapeDtypeStruct((n_batch * N_CMP * N_SUB, HEAD_DIM), BF16),
        grid=(n_batch, N_CMP),
        in_specs=[
            pl.BlockSpec((seq, HEAD_DIM), lambda b, n: (b, COL_CMP // HEAD_DIM + n)),
            pl.BlockSpec((CMP_LEN, HEAD_DIM), lambda b, n: (0, 0)),
            pl.BlockSpec((None, None, CMP_LEN * HEAD_DIM, HEAD_DIM), lambda b, n: (layer, n // C_KV_HEADS, 0, 0)),
            pl.BlockSpec((None, None, HEAD_DIM, HEAD_DIM), lambda b, n: (layer, n // C_KV_HEADS, 0, 0)),
        ],
        out_specs=pl.BlockSpec((N_SUB, HEAD_DIM), lambda b, n: (b * N_CMP + n, 0)),
        scratch_shapes=[pltpu.VMEM((seq, HEAD_DIM), F32)],
        compiler_params=_cparams(("parallel", "parallel")),
        name="nsa_compress",
    )(z, pos, w1, w2)


N_SEL = 64


def _nsa_kernel(q_ref, kc_ref, vc_ref, ovt_ref, ks_ref, vs_ref, kw_ref, vw_ref, gate_ref, o_ref,
                qs_ref, kbs_ref, vbs_ref, kbw_ref, vbw_ref, ms_ref, as_ref, mw_ref, aw_ref, oc_ref, sc_ref):
    tq, tk, ch = ATT_TQ, ATT_TK, ATT_CH
    n_rep = C_GROUP
    n_sub = tq // ch
    n_ch = n_rep * n_sub
    seq = ks_ref.shape[0]
    i = pl.program_id(2)
    q0 = i * tq

    @pl.when(i == 0)
    def _():
        def prep(c, carry):
            rows = pl.ds(pl.multiple_of(c * tk, tk), tk)
            blk = (c * tk + lax.broadcasted_iota(jnp.int32, (tk, LANE), 0)) >> SEL_SHIFT
            lane = lax.broadcasted_iota(jnp.int32, (tk, LANE), 1)
            ones = jnp.ones((tk, HEAD_DIM), BF16)
            kbs_ref[rows, 0:HEAD_DIM] = ks_ref[rows, :]
            kbs_ref[rows, HEAD_DIM:HEAD_DIM + LANE] = jnp.where(blk == lane, 1.0, 0.0).astype(BF16)
            vbs_ref[rows, 0:HEAD_DIM] = vs_ref[rows, :]
            vbs_ref[rows, HEAD_DIM:2 * HEAD_DIM] = ones
            kbw_ref[rows, :] = kw_ref[rows, :]
            vbw_ref[rows, 0:HEAD_DIM] = vw_ref[rows, :]
            vbw_ref[rows, HEAD_DIM:2 * HEAD_DIM] = ones
            return carry

        lax.fori_loop(0, seq // tk, prep, 0)

    qscale = (HEAD_DIM ** -0.5) * LOG2E
    for r in range(n_rep):
        qs_ref[r * tq:(r + 1) * tq, 0:HEAD_DIM] = (q_ref[:, r * HEAD_DIM:(r + 1) * HEAD_DIM].astype(F32)
                                                   * qscale).astype(BF16)

    for m_ref, accl_ref in ((ms_ref, as_ref), (mw_ref, aw_ref)):
        m_ref[...] = jnp.full(m_ref.shape, NEG_INF, F32)
        accl_ref[...] = jnp.zeros(accl_ref.shape, F32)

    q_idx = lax.broadcasted_iota(jnp.int32, (tq, tk), 0)
    k_idx = lax.broadcasted_iota(jnp.int32, (tq, tk), 1)

    def tile_bias(off, max_dist):
        dist = off * tk + q_idx - k_idx
        if off == 0:
            ok = dist >= 0
        elif max_dist is not None and (off + 1) * tk - 1 > max_dist:
            ok = dist <= max_dist
        else:
            return None
        return jnp.where(ok, 0.0, NEG_INF)

    def issue(tiles, load_q, kb_ref):
        scores = []
        for kt, tbias in tiles:
            k0 = pl.multiple_of(kt * tk, tk)
            scores.append((_dot_nt(load_q(), kb_ref[pl.ds(k0, tk), :]), k0, tbias))
        return scores

    def finish(scores, vb_ref, m_ref, accl_ref):
        for s_tile, k0, tbias in scores:
            pts, alphas = [], []
            for c in range(n_ch):
                rows = slice(c * ch, (c + 1) * ch)
                s = s_tile[rows, :]
                if tbias is not None:
                    b0 = (c % n_sub) * ch
                    s = s + tbias[b0:b0 + ch, :]
                m_old = m_ref[rows, :]
                m_new = jnp.maximum(m_old, jnp.max(s, axis=-1, keepdims=True))
                alphas.append(jnp.exp2(m_old - m_new))
                pts.append(jnp.exp2((s - jnp.concatenate([m_new] * (tk // LANE), axis=1)).astype(BF16)))
                m_ref[rows, :] = m_new
            pv = _dot(jnp.concatenate(pts, axis=0), vb_ref[pl.ds(k0, tk), :])
            for c in range(n_ch):
                rows = slice(c * ch, (c + 1) * ch)
                alpha2 = jnp.concatenate([alphas[c], alphas[c]], axis=1)
                accl_ref[rows, :] = alpha2 * accl_ref[rows, :] + pv[rows, :]

    max_dist = WIN_LEN - 1
    win_tiles = []
    for off in range(-(-max_dist // tk), -1, -1):
        tbias = tile_bias(off, max_dist)
        if off > 0:
            before_start = jnp.where(i >= off, 0.0, NEG_INF)
            if tbias is None:
                tbias = jnp.where(q_idx + k_idx >= 0, 0.0, NEG_INF)
            tbias = tbias + before_start
        win_tiles.append((jnp.maximum(i - off, 0), tbias))
    win_scores = issue(win_tiles, lambda: qs_ref[:, 0:HEAD_DIM], kbw_ref)

    s_all = _dot_nt(qs_ref[:, 0:HEAD_DIM], kc_ref[...])
    n_cols = lax.broadcasted_iota(jnp.int32, (ch, N_SUB), 1)
    bias, has_any = [], []
    for c in range(n_sub):
        t = q0 + c * ch + lax.broadcasted_iota(jnp.int32, (ch, N_SUB), 0)
        bias.append(jnp.where((n_cols * CMP_STRIDE + (CMP_LEN - 1)) <= t, 0.0, NEG_INF))
        has_any.append(t[:, 0:1] >= CMP_LEN - 1)
    ps = []
    psum = [jnp.zeros((ch, N_SUB), F32) for _ in range(n_sub)]
    for c in range(n_ch):
        s = s_all[c * ch:(c + 1) * ch, :] + bias[c % n_sub]
        e = jnp.exp2(s - jnp.max(s, axis=-1, keepdims=True))
        den = jnp.sum(e, axis=-1, keepdims=True)
        p = e * jnp.where(has_any[c % n_sub], 1.0 / den, 0.0)
        ps.append(p.astype(BF16))
        psum[c % n_sub] = psum[c % n_sub] + p
    oc_ref[...] = _dot(jnp.concatenate(ps, axis=0), vc_ref[...])
    psum = jnp.concatenate(psum, axis=0)

    hi = psum.astype(BF16)
    lo = (psum - hi.astype(F32)).astype(BF16)
    ovt = ovt_ref[...]
    imp_t = _dot_nt(ovt, hi) + _dot_nt(ovt, lo)
    jrow = lax.broadcasted_iota(jnp.int32, (LANE, tq), 0)
    tl = q0 + lax.broadcasted_iota(jnp.int32, (LANE, tq), 1)
    jt = tl >> SEL_SHIFT
    forced = (jrow == 0) | (jrow == jt) | (jrow == jt - 1)
    valid_s = jrow * SEL_LEN <= tl
    sc_ref[...] = jnp.where(forced, 1e4, jnp.where(valid_s, imp_t, -1.0))
    finish(win_scores, vbw_ref, mw_ref, aw_ref)
    n_grp = N_SEL // SUBLANE
    grp = [sc_ref[g * SUBLANE:(g + 1) * SUBLANE, :] for g in range(n_grp)]
    cnt = [jnp.zeros((SUBLANE, tq), F32) for _ in range(n_grp)]
    sub = lax.broadcasted_iota(jnp.int32, (SUBLANE, tq), 0)
    for j in range(N_SEL):
        row = sc_ref[j:j + 1, :]
        for g in range(n_grp):
            if g < j // SUBLANE:
                beats = row > grp[g]
            elif g > j // SUBLANE:
                beats = row >= grp[g]
            else:
                beats = (row > grp[g]) | ((row == grp[g]) & (sub > j % SUBLANE))
            cnt[g] = cnt[g] + jnp.where(beats, 1.0, 0.0)
    pen_t = [jnp.where(c < float(SEL_TOP), 0.0, NEG_INF) for c in cnt]
    pen_t = jnp.concatenate(pen_t + [jnp.full((LANE - N_SEL, tq), NEG_INF, F32)], axis=0)
    pen = pen_t.T.astype(BF16)
    for r in range(n_rep):
        qs_ref[r * tq:(r + 1) * tq, HEAD_DIM:HEAD_DIM + LANE] = pen

    def sel_steps(tiles):
        finish(issue(tiles, lambda: qs_ref[...], kbs_ref), vbs_ref, ms_ref, as_ref)

    def sel_body(j, carry):
        sel_steps([(SEL_UNROLL * j + u, None) for u in range(SEL_UNROLL)])
        return carry

    lax.fori_loop(0, i // SEL_UNROLL, sel_body, 0)

    for rem in range(SEL_UNROLL):
        @pl.when(i % SEL_UNROLL == rem)
        def _(rem=rem):
            sel_steps([(i - rem + u, None) for u in range(rem)] + [(i, tile_bias(0, None))])

    g = 1.0 / (1.0 + jnp.exp(-gate_ref[...].astype(F32)))
    lane = lax.broadcasted_iota(jnp.int32, g.shape, 1)
    head0 = pl.program_id(1) * n_rep

    def gate(head, branch):
        return jnp.sum(jnp.where(lane == head * 3 + branch, g, 0.0), axis=-1, keepdims=True)

    for r in range(n_rep):
        rows = slice(r * tq, (r + 1) * tq)
        o = (gate(head0 + r, 0) * oc_ref[rows, :]
             + gate(head0 + r, 1) * (as_ref[rows, 0:HEAD_DIM] / as_ref[rows, HEAD_DIM:2 * HEAD_DIM])
             + gate(head0 + r, 2) * (aw_ref[rows, 0:HEAD_DIM] / aw_ref[rows, HEAD_DIM:2 * HEAD_DIM]))
        o_ref[:, r * HEAD_DIM:(r + 1) * HEAD_DIM] = o.astype(o_ref.dtype)


def nsa_attention(z, kvc, ovt, *, n_batch, seq):
    tq = ATT_TQ
    nq = seq // tq
    n_rep = C_GROUP
    qw = n_rep * HEAD_DIM
    kv_blk = (seq, HEAD_DIM)
    sel0, win0 = COL_SEL // HEAD_DIM, COL_WIN // HEAD_DIM
    return pl.pallas_call(
        _nsa_kernel,
        out_shape=jax.ShapeDtypeStruct((n_batch * seq, C_HEADS * HEAD_DIM), BF16),
        grid=(n_batch, C_KV_HEADS, nq),
        in_specs=[
            pl.BlockSpec((tq, qw), lambda b, g, i: (b * nq + i, COL_CQ // qw + g)),
            pl.BlockSpec((N_SUB, HEAD_DIM), lambda b, g, i: (b * N_CMP + g, 0)),
            pl.BlockSpec((N_SUB, HEAD_DIM), lambda b, g, i: (b * N_CMP + C_KV_HEADS + g, 0)),
            pl.BlockSpec((LANE, N_SUB), lambda b, g, i: (0, 0)),
            pl.BlockSpec(kv_blk, lambda b, g, i: (b, sel0 + g)),
            pl.BlockSpec(kv_blk, lambda b, g, i: (b, sel0 + C_KV_HEADS + g)),
            pl.BlockSpec(kv_blk, lambda b, g, i: (b, win0 + g)),
            pl.BlockSpec(kv_blk, lambda b, g, i: (b, win0 + C_KV_HEADS + g)),
            pl.BlockSpec((tq, LANE), lambda b, g, i: (b * nq + i, COL_GATE // LANE)),
        ],
        out_specs=pl.BlockSpec((tq, qw), lambda b, g, i: (b * nq + i, g)),
        scratch_shapes=[
            pltpu.VMEM((n_rep * tq, HEAD_DIM + LANE), BF16),
            pltpu.VMEM((seq, HEAD_DIM + LANE), BF16),
            pltpu.VMEM((seq, 2 * HEAD_DIM), BF16),
            pltpu.VMEM((seq, HEAD_DIM), BF16),
            pltpu.VMEM((seq, 2 * HEAD_DIM), BF16),
            pltpu.VMEM((n_rep * tq, LANE), F32),
            pltpu.VMEM((n_rep * tq, 2 * HEAD_DIM), F32),
            pltpu.VMEM((n_rep * tq, LANE), F32),
            pltpu.VMEM((n_rep * tq, 2 * HEAD_DIM), F32),
            pltpu.VMEM((n_rep * tq, HEAD_DIM), F32),
            pltpu.VMEM((LANE, tq), F32),
        ],
        compiler_params=_cparams(("parallel", "parallel", "arbitrary")),
        name="nsa_attention",
    )(z, kvc, kvc, ovt, z, z, z, z, z)


def _overlap_t():
    n_c = N_SUB - 1
    c_start = np.arange(n_c) * CMP_STRIDE
    s_start = np.arange(N_SEL) * SEL_LEN
    ov = ((c_start[:, None] <= s_start[None, :] + SEL_LEN - 1)
          & (c_start[:, None] + CMP_LEN - 1 >= s_start[None, :])).astype(np.float32)
    out = np.zeros((LANE, N_SUB), np.float32)
    out[:N_SEL, :n_c] = ov.T
    return out


def _mix_out_kernel(x_ref, oa_ref, ob_ref, oc_ref, w_ref, o_ref):
    mix = jnp.concatenate([oa_ref[...], ob_ref[...], oc_ref[...]], axis=1)
    o_ref[...] = x_ref[...] + _dot(mix, w_ref[...])


def mix_out(x, out_a, out_b, out_c, w, layer, *, tm):
    m, n = x.shape
    return pl.pallas_call(
        _mix_out_kernel,
        out_shape=jax.ShapeDtypeStruct((m, n), F32),
        grid=(m // tm,),
        in_specs=[
            pl.BlockSpec((tm, n), lambda i: (i, 0)),
            pl.BlockSpec((tm, out_a.shape[1]), lambda i: (i, 0)),
            pl.BlockSpec((tm, out_b.shape[1]), lambda i: (i, 0)),
            pl.BlockSpec((tm, out_c.shape[1]), lambda i: (i, 0)),
            pl.BlockSpec((None,) + w.shape[1:], lambda i: (layer, 0, 0), pipeline_mode=pl.Buffered(1)),
        ],
        out_specs=pl.BlockSpec((tm, n), lambda i: (i, 0)),
        compiler_params=_cparams(("parallel",)),
        name="mix_out",
    )(x, out_a, out_b, out_c, w)


def _xattn_kernel(x_ref, g_ref, wq_ref, k_ref, v_ref, wo_ref, o_ref):
    n_mem = k_ref.shape[0]
    rows = x_ref.shape[0]
    ones = jnp.ones((n_mem, HEAD_DIM), BF16)
    kv = [(k_ref[:, h * HEAD_DIM:(h + 1) * HEAD_DIM].astype(BF16),
           jnp.concatenate([v_ref[:, h * HEAD_DIM:(h + 1) * HEAD_DIM].astype(BF16), ones], axis=1))
          for h in range(X_HEADS)]
    for c in range(rows // NORM_ROWS):
        r = slice(c * NORM_ROWS, (c + 1) * NORM_ROWS)
        x = x_ref[r, :]
        hn = _rms_rows(x, g_ref[...]).astype(BF16)
        q = _dot(hn, wq_ref[...]) * ((HEAD_DIM ** -0.5) * LOG2E)
        outs = []
        for h in range(X_HEADS):
            s = _dot_nt(q[:, h * HEAD_DIM:(h + 1) * HEAD_DIM].astype(BF16), kv[h][0])
            e = jnp.exp2((s - jnp.max(s, axis=-1, keepdims=True)).astype(BF16))
            pv = _dot(e, kv[h][1])
            outs.append((pv[:, 0:HEAD_DIM] / pv[:, HEAD_DIM:2 * HEAD_DIM]).astype(BF16))
        o_ref[r, :] = x + _dot(jnp.concatenate(outs, axis=1), wo_ref[...])


def cross_attention_block(x, g, wq, kv, wo, layer, *, n_batch, seq, n_mem, tm):
    nq = seq // tm
    d = x.shape[1]
    return pl.pallas_call(
        _xattn_kernel,
        out_shape=jax.ShapeDtypeStruct(x.shape, F32),
        grid=(n_batch, nq),
        in_specs=[
            pl.BlockSpec((tm, d), lambda b, i: (b * nq + i, 0)),
            pl.BlockSpec((1, d), lambda b, i: (0, 0)),
            pl.BlockSpec((None, d, X_WIDTH), lambda b, i: (layer, 0, 0), pipeline_mode=pl.Buffered(1)),
            pl.BlockSpec((n_mem, X_WIDTH), lambda b, i: (b, 0)),
            pl.BlockSpec((n_mem, X_WIDTH), lambda b, i: (b, 1)),
            pl.BlockSpec((None, X_WIDTH, d), lambda b, i: (layer, 0, 0), pipeline_mode=pl.Buffered(1)),
        ],
        out_specs=pl.BlockSpec((tm, d), lambda b, i: (b * nq + i, 0)),
        compiler_params=_cparams(("parallel", "parallel")),
        name="cross_attention",
    )(x, g.reshape(1, d), wq, kv, kv, wo)


def kernel(x, mem, norm_mix, w_in, gmlp_ln_g, gmlp_ln_b, gmlp_w_s, gmlp_b_s, cmp_pos, cmp_k_w1, cmp_k_w2,
           cmp_v_w1, cmp_v_w2, w_out, norm_xattn, norm_mem, xattn_wq, xattn_wkv, xattn_wo, norm_mlp, w_up,
           w_down, final_norm):
    bsz, seq, d = x.shape
    n_mem = mem.shape[1]
    t = bsz * seq
    assert d == D_MODEL and seq // CMP_STRIDE == N_SUB and seq // SEL_LEN == N_SEL
    assert seq % (DIL_KW * max(dil for _, dil in DILATED_PAIRS)) == 0 and seq % ATT_TQ == 0

    w_in_b = jnp.pad(w_in, ((0, 0), (0, 0), (0, IN_PAD - IN_WIDTH))).astype(BF16)
    w_out_b = w_out.astype(BF16)
    wq_b = xattn_wq.astype(BF16)
    wkv_b = xattn_wkv.astype(BF16)
    wo_b = xattn_wo.astype(BF16)
    w_up_b = w_up.astype(BF16)
    w_down_b = w_down.astype(BF16)
    cmp_w1 = jnp.stack([cmp_k_w1, cmp_v_w1], axis=1).astype(BF16)
    cmp_w2 = jnp.stack([cmp_k_w2, cmp_v_w2], axis=1).astype(BF16)
    ovt = jnp.asarray(_overlap_t()).astype(BF16)
    xf = x.reshape(t, d)
    memf = mem.reshape(bsz * n_mem, d)

    for l in range(DEPTH):
        z = norm_matmul(xf, norm_mix[l], w_in_b, l, tm=TM_PROJ, out_dtype=BF16)

        out_a = gmlp(z, gmlp_ln_g[l], gmlp_ln_b[l], gmlp_w_s[l], gmlp_b_s[l], tm=TM_PROJ)

        out_b = dilated_attention(z, n_batch=bsz, seq=seq)

        kvc = compress(z, cmp_pos[l], cmp_w1, cmp_w2, l, n_batch=bsz, seq=seq)
        out_c = nsa_attention(z, kvc, ovt, n_batch=bsz, seq=seq)

        xf = mix_out(xf, out_a, out_b, out_c, w_out_b, l, tm=TM_PROJ)

        kv = norm_matmul(memf, norm_mem[l], wkv_b, l, tm=TM_PROJ)
        xf = cross_attention_block(xf, norm_xattn[l], wq_b, kv, wo_b, l, n_batch=bsz, seq=seq, n_mem=n_mem,
                                   tm=TM_XATTN)

        xf = mlp(xf, norm_mlp[l], w_up_b, w_down_b, l, final_norm, tm=TM_MLP, tf=TF_MLP,
                 norm_out=(l == DEPTH - 1))

    return xf.reshape(bsz, seq, d)
```

```python
import functools
import math

import numpy as np
import jax
import jax.numpy as jnp
from jax import lax
from jax.experimental import pallas as pl
from jax.experimental.pallas import tpu as pltpu

F32 = jnp.float32
BF16 = jnp.bfloat16

D_MODEL = 2048
DEPTH = 4
HEAD_DIM = 128
A_GROUPS = 4
A_WIDTH = A_GROUPS * HEAD_DIM
CHUNK = 128
B_HEADS = 4
DILATED_PAIRS = ((128, 1), (512, 4), (2048, 16))
C_HEADS = 8
C_KV_HEADS = 2
C_GROUP = C_HEADS // C_KV_HEADS
CMP_LEN = 32
CMP_STRIDE = 16
SEL_LEN = 64
SEL_SHIFT = SEL_LEN.bit_length() - 1
SEL_TOP = 16
WIN_LEN = 512
X_HEADS = 4
X_WIDTH = X_HEADS * HEAD_DIM
EPS = 1e-6
NEG_INF = -1e30
LOG2E = math.log2(math.e)

COL_B = 2 * A_WIDTH
COL_CQ = COL_B + 3 * B_HEADS * HEAD_DIM
COL_CMP = COL_CQ + C_HEADS * HEAD_DIM
COL_SEL = COL_CMP + 2 * C_KV_HEADS * HEAD_DIM
COL_WIN = COL_SEL + 2 * C_KV_HEADS * HEAD_DIM
COL_GATE = COL_WIN + 2 * C_KV_HEADS * HEAD_DIM
IN_WIDTH = COL_GATE + 3 * C_HEADS
IN_PAD = 5248
Z_OFF = COL_B
Z_WIDTH = IN_PAD - Z_OFF

LANE = 128
SUBLANE = 8
VMEM_LIMIT = 56 * 1024 * 1024

TM_PROJ = 512
TM_XATTN = 1024
TM_MLP = 1024
TF_MLP = 512
NORM_ROWS = 256
ATT_TQ = 256
ATT_TK = 256
ATT_CH = 128
SEL_UNROLL = 4


def _cparams(sem):
    return pltpu.CompilerParams(dimension_semantics=sem, vmem_limit_bytes=VMEM_LIMIT)


def _rms_rows(x, g):
    ms = jnp.mean(x * x, axis=-1, keepdims=True)
    return x * lax.rsqrt(ms + EPS) * g


def _gelu_tanh(x):
    c = math.sqrt(2.0 / math.pi)
    return x * (0.5 * (1.0 + jnp.tanh(c * (x + 0.044715 * (x * x * x)))))


def _dot(a, b):
    return jnp.dot(a, b, preferred_element_type=F32)


def _dot_nt(a, b):
    return lax.dot_general(a, b, (((1,), (1,)), ((), ())), preferred_element_type=F32)


def _norm_into(h_ref, x_ref, g_ref, copy_ref=None):
    rows = x_ref.shape[0]

    def body(c, carry):
        r0 = pl.multiple_of(c * NORM_ROWS, NORM_ROWS)
        x = x_ref[pl.ds(r0, NORM_ROWS), :]
        h_ref[pl.ds(r0, NORM_ROWS), :] = _rms_rows(x, g_ref[...]).astype(h_ref.dtype)
        if copy_ref is not None:
            copy_ref[pl.ds(r0, NORM_ROWS), :] = x
        return carry

    lax.fori_loop(0, rows // NORM_ROWS, body, 0)


def _norm_matmul_kernel(x_ref, g_ref, w_ref, o_ref):
    half = x_ref.shape[0] // 2
    for c in range(2):
        r = slice(c * half, (c + 1) * half)
        hn = _rms_rows(x_ref[r, :], g_ref[...]).astype(BF16)
        o_ref[r, :] = _dot(hn, w_ref[...]).astype(o_ref.dtype)


def norm_matmul(x, g, w, layer, *, tm, out_dtype=F32):
    m, k = x.shape
    n = w.shape[2]
    return pl.pallas_call(
        _norm_matmul_kernel,
        out_shape=jax.ShapeDtypeStruct((m, n), out_dtype),
        grid=(m // tm,),
        in_specs=[
            pl.BlockSpec((tm, k), lambda i: (i, 0)),
            pl.BlockSpec((1, k), lambda i: (0, 0)),
            pl.BlockSpec((None, k, n), lambda i: (layer, 0, 0), pipeline_mode=pl.Buffered(1)),
        ],
        out_specs=pl.BlockSpec((tm, n), lambda i: (i, 0)),
        compiler_params=_cparams(("parallel",)),
        name="norm_matmul",
    )(x, g.reshape(1, k), w)


def _mlp_kernel(x_ref, g_ref, wu_ref, wd_ref, gout_ref, o_ref, h_ref, *, norm_out):
    f = pl.program_id(1)

    @pl.when(f == 0)
    def _():
        _norm_into(h_ref, x_ref, g_ref, copy_ref=o_ref)

    a = _dot(h_ref[...], wu_ref[...])
    a = jnp.square(jnp.maximum(a, 0.0)).astype(BF16)
    o_ref[...] += _dot(a, wd_ref[...])

    if norm_out:
        @pl.when(f == pl.num_programs(1) - 1)
        def _():
            _norm_into(o_ref, o_ref, gout_ref)


def mlp(x, g, wu, wd, layer, g_out, *, tm, tf, norm_out):
    m, d = x.shape
    ff = wu.shape[2]
    return pl.pallas_call(
        functools.partial(_mlp_kernel, norm_out=norm_out),
        out_shape=jax.ShapeDtypeStruct((m, d), F32),
        grid=(m // tm, ff // tf),
        in_specs=[
            pl.BlockSpec((tm, d), lambda i, f: (i, 0)),
            pl.BlockSpec((1, d), lambda i, f: (0, 0)),
            pl.BlockSpec((None, d, tf), lambda i, f: (layer, 0, f)),
            pl.BlockSpec((None, tf, d), lambda i, f: (layer, f, 0)),
            pl.BlockSpec((1, d), lambda i, f: (0, 0)),
        ],
        out_specs=pl.BlockSpec((tm, d), lambda i, f: (i, 0)),
        scratch_shapes=[pltpu.VMEM((tm, d), BF16)],
        compiler_params=_cparams(("parallel", "arbitrary")),
        name="mlp",
    )(x, g.reshape(1, d), wu, wd, g_out.reshape(1, d))


def _in_proj_kernel(x_ref, g_ref, w_ref, lng_ref, lnb_ref, ws_ref, bst_ref, z_ref, oa_ref):
    row = lax.broadcasted_iota(jnp.int32, (CHUNK, CHUNK), 0)
    col = lax.broadcasted_iota(jnp.int32, (CHUNK, CHUNK), 1)
    causal = row >= col
    w_tril = [jnp.where(causal, ws_ref[gi], 0.0).astype(BF16) for gi in range(A_GROUPS)]

    def gate_inputs(za):
        u = _gelu_tanh(za[:, 0:A_WIDTH])
        v = _gelu_tanh(za[:, A_WIDTH:2 * A_WIDTH])
        vc = v - jnp.mean(v, axis=-1, keepdims=True)
        vn = vc * lax.rsqrt(jnp.mean(vc * vc, axis=-1, keepdims=True) + EPS)
        return u, (vn * lng_ref[...] + lnb_ref[...]).astype(BF16)

    def gate_store(r0, u, vn):
        for gi in range(A_GROUPS):
            cols = slice(gi * HEAD_DIM, (gi + 1) * HEAD_DIM)
            sv = _dot(w_tril[gi], vn[:, cols]) + bst_ref[:, gi:gi + 1]
            oa_ref[r0:r0 + CHUNK, cols] = (u[:, cols] * sv).astype(oa_ref.dtype)

    half = x_ref.shape[0] // 2
    pending = []
    for c in range(2):
        r = slice(c * half, (c + 1) * half)
        hn = _rms_rows(x_ref[r, :], g_ref[...]).astype(BF16)
        za = _dot(hn, w_ref[:, 0:Z_OFF])
        for r0, u, vn in pending:
            gate_store(r0, u, vn)
        z_ref[r, :] = _dot(hn, w_ref[:, Z_OFF:IN_PAD]).astype(z_ref.dtype)
        pending = [(c * half + k * CHUNK,) + gate_inputs(za[k * CHUNK:(k + 1) * CHUNK, :])
                   for k in range(half // CHUNK)]
    for r0, u, vn in pending:
        gate_store(r0, u, vn)


def in_proj(x, g, w, layer, ln_g, ln_b, w_s, b_s, *, tm):
    m, k = x.shape
    return pl.pallas_call(
        _in_proj_kernel,
        out_shape=[jax.ShapeDtypeStruct((m, Z_WIDTH), BF16), jax.ShapeDtypeStruct((m, A_WIDTH), BF16)],
        grid=(m // tm,),
        in_specs=[
            pl.BlockSpec((tm, k), lambda i: (i, 0)),
            pl.BlockSpec((1, k), lambda i: (0, 0)),
            pl.BlockSpec((None, k, IN_PAD), lambda i: (layer, 0, 0), pipeline_mode=pl.Buffered(1)),
            pl.BlockSpec((1, A_WIDTH), lambda i: (0, 0)),
            pl.BlockSpec((1, A_WIDTH), lambda i: (0, 0)),
            pl.BlockSpec((A_GROUPS, CHUNK, CHUNK), lambda i: (0, 0, 0)),
            pl.BlockSpec((CHUNK, A_GROUPS), lambda i: (0, 0)),
        ],
        out_specs=[pl.BlockSpec((tm, Z_WIDTH), lambda i: (i, 0)), pl.BlockSpec((tm, A_WIDTH), lambda i: (i, 0))],
        compiler_params=_cparams(("parallel",)),
        name="in_proj",
    )(x, g.reshape(1, k), w, ln_g.reshape(1, A_WIDTH), ln_b.reshape(1, A_WIDTH), w_s, b_s.T)


DIL_CH = 128
DIL_KW = 2 * DIL_CH
DIL_UNITS = 8


def _dilated_kernel(q_in, k_in, v_in, o_ref, q_ref, k_ref, v_ref, m_ref, acc_ref, l_ref):
    ch, kw = DIL_CH, DIL_KW
    seq = q_in.shape[0]
    for src, dst in ((q_in, q_ref), (k_in, k_ref), (v_in, v_ref)):
        def stage(c, carry, src=src, dst=dst):
            r0 = pl.multiple_of(c * NORM_ROWS, NORM_ROWS)
            dst[pl.ds(r0, NORM_ROWS), :] = src[pl.ds(r0, NORM_ROWS), :].astype(F32)
            return carry
        lax.fori_loop(0, seq // NORM_ROWS, stage, 0)
    qscale = (HEAD_DIM ** -0.5) * LOG2E
    row = lax.broadcasted_iota(jnp.int32, (ch, kw), 0)
    col = lax.broadcasted_iota(jnp.int32, (ch, kw), 1)
    ones = jnp.ones((kw, HEAD_DIM), BF16)
    patterns = sorted(DILATED_PAIRS, key=lambda wd: -wd[1])

    def rows(start, n, dil):
        return pl.ds(start, n) if dil == 1 else pl.ds(start, n, stride=dil)

    def band_bias(first, span):
        dist = row - col if first else ch + row - col
        return jnp.where((dist >= 0) & (dist <= span), 0.0, NEG_INF)

    def run_units(units, dil, span, phase):
        windows = {}
        scores = []
        for q0, k0, first, wid in units:
            if wid not in windows:
                k = k_ref[rows(k0, kw, dil), :].astype(BF16)
                v = v_ref[rows(k0, kw, dil), :].astype(BF16)
                windows[wid] = (k, jnp.concatenate([v, ones], axis=1))
            q = (q_ref[rows(q0, ch, dil), :] * qscale).astype(BF16)
            scores.append(_dot_nt(q, windows[wid][0]) + band_bias(first, span))
        for (q0, k0, first, wid), s in zip(units, scores):
            qrows = rows(q0, ch, dil)
            m_cur = jnp.max(s, axis=-1, keepdims=True)
            if phase == "first":
                m_new = jnp.broadcast_to(m_cur, (ch, LANE))
            else:
                m_old = m_ref[qrows, :]
                m_new = jnp.maximum(m_old, m_cur)
                alpha = jnp.exp2(m_old - m_new)
            p = jnp.exp2((s - jnp.concatenate([m_new] * (kw // LANE), axis=1)).astype(BF16))
            pv = _dot(p, windows[wid][1])
            acc, l = pv[:, 0:HEAD_DIM], pv[:, HEAD_DIM:2 * HEAD_DIM]
            if phase != "first":
                acc = alpha * acc_ref[qrows, :] + acc
                l = alpha * l_ref[qrows, :] + l
            if phase == "last":
                o_ref[pl.ds(pl.multiple_of(q0, ch), ch), :] = (acc / l).astype(o_ref.dtype)
            else:
                m_ref[qrows, :] = m_new
                acc_ref[qrows, :] = acc
                l_ref[qrows, :] = l

    for idx, (window, dil) in enumerate(patterns):
        phase = "first" if idx == 0 else ("last" if idx == len(patterns) - 1 else "mid")
        assert phase != "last" or dil == 1
        span = window // dil
        assert span <= ch
        n_units = seq // dil // ch
        per_class = min(n_units, max(DIL_UNITS // dil, 2))
        n_classes = DIL_UNITS // per_class
        assert n_units % per_class == 0 and dil % n_classes == 0

        def group(r0, j0, head, dil=dil, span=span, phase=phase, per_class=per_class, n_classes=n_classes):
            units = []
            for rc in range(n_classes):
                r = r0 + rc
                for u in range(per_class):
                    first = head and u == 0
                    q0 = r + dil * ch * (j0 + u)
                    k0 = r if (head and u <= 1) else r + dil * ch * (j0 + u - 1)
                    wid = (rc, 0) if (head and u <= 1) else (rc, u)
                    units.append((q0, k0, first, wid))
            run_units(units, dil, span, phase)

        def class_loop(g, carry, group=group, per_class=per_class, n_classes=n_classes, n_units=n_units):
            r0 = g * n_classes
            group(r0, 0, True)
            if n_units > per_class:
                def tail(jb, c):
                    group(r0, jb * per_class, False)
                    return c
                lax.fori_loop(1, n_units // per_class, tail, 0)
            return carry

        if dil // n_classes == 1:
            class_loop(0, 0)
        else:
            lax.fori_loop(0, dil // n_classes, class_loop, 0)


def dilated_attention(z, *, n_batch, seq):
    bw = B_HEADS * HEAD_DIM
    base = (COL_B - Z_OFF) // HEAD_DIM
    blk = (seq, HEAD_DIM)
    return pl.pallas_call(
        _dilated_kernel,
        out_shape=jax.ShapeDtypeStruct((n_batch * seq, bw), BF16),
        grid=(n_batch, B_HEADS),
        in_specs=[
            pl.BlockSpec(blk, lambda b, h: (b, base + h)),
            pl.BlockSpec(blk, lambda b, h: (b, base + B_HEADS + h)),
            pl.BlockSpec(blk, lambda b, h: (b, base + 2 * B_HEADS + h)),
        ],
        out_specs=pl.BlockSpec(blk, lambda b, h: (b, h)),
        scratch_shapes=[pltpu.VMEM(blk, F32)] * 6,
        compiler_params=_cparams(("parallel", "parallel")),
        name="dilated_attention",
    )(z, z, z)


N_SUB = 256
N_CMP = 2 * C_KV_HEADS


def _compress_kernel(x_in, pos_ref, w1_ref, w2_ref, o_ref, x_ref):
    x_ref[...] = x_in[...].astype(F32)
    p = jnp.zeros((N_SUB, HEAD_DIM), F32)
    q = jnp.zeros((N_SUB, HEAD_DIM), F32)
    for i in range(CMP_STRIDE):
        a = x_ref[pl.ds(i, N_SUB, stride=CMP_STRIDE), :]
        top = (a + pos_ref[i:i + 1, :]).astype(BF16)
        bot = (a + pos_ref[CMP_STRIDE + i:CMP_STRIDE + i + 1, :]).astype(BF16)
        p = p + _dot(top, w1_ref[i * HEAD_DIM:(i + 1) * HEAD_DIM, :])
        q = q + _dot(bot, w1_ref[(CMP_STRIDE + i) * HEAD_DIM:(CMP_STRIDE + i + 1) * HEAD_DIM, :])
    h = _gelu_tanh(p + pltpu.roll(q, N_SUB - 1, 0))
    o_ref[...] = _dot(h.astype(BF16), w2_ref[...]).astype(o_ref.dtype)


def compress(z, pos, w1, w2, layer, *, n_batch, seq):
    return pl.pallas_call(
        _compress_kernel,
        out_shape=jax.ShapeDtypeStruct((n_batch * N_CMP * N_SUB, HEAD_DIM), BF16),
        grid=(n_batch, N_CMP),
        in_specs=[
            pl.BlockSpec((seq, HEAD_DIM), lambda b, n: (b, (COL_CMP - Z_OFF) // HEAD_DIM + n)),
            pl.BlockSpec((CMP_LEN, HEAD_DIM), lambda b, n: (0, 0)),
            pl.BlockSpec((None, None, CMP_LEN * HEAD_DIM, HEAD_DIM), lambda b, n: (layer, n // C_KV_HEADS, 0, 0)),
            pl.BlockSpec((None, None, HEAD_DIM, HEAD_DIM), lambda b, n: (layer, n // C_KV_HEADS, 0, 0)),
        ],
        out_specs=pl.BlockSpec((N_SUB, HEAD_DIM), lambda b, n: (b * N_CMP + n, 0)),
        scratch_shapes=[pltpu.VMEM((seq, HEAD_DIM), F32)],
        compiler_params=_cparams(("parallel", "parallel")),
        name="nsa_compress",
    )(z, pos, w1, w2)


N_SEL = 64


def _nsa_kernel(q_ref, kc_ref, vc_ref, ovt_ref, ks_ref, vs_ref, kw_ref, vw_ref, gate_ref, o_ref,
                qs_ref, kbs_ref, vbs_ref, kbw_ref, vbw_ref, ms_ref, as_ref, mw_ref, aw_ref, oc_ref, sc_ref):
    tq, tk, ch = ATT_TQ, ATT_TK, ATT_CH
    n_rep = C_GROUP
    n_sub = tq // ch
    n_ch = n_rep * n_sub
    seq = ks_ref.shape[0]
    i = pl.program_id(2)
    q0 = i * tq

    @pl.when(i == 0)
    def _():
        def prep(c, carry):
            rows = pl.ds(pl.multiple_of(c * tk, tk), tk)
            blk = (c * tk + lax.broadcasted_iota(jnp.int32, (tk, LANE), 0)) >> SEL_SHIFT
            lane = lax.broadcasted_iota(jnp.int32, (tk, LANE), 1)
            ones = jnp.ones((tk, HEAD_DIM), BF16)
            kbs_ref[rows, 0:HEAD_DIM] = ks_ref[rows, :]
            kbs_ref[rows, HEAD_DIM:HEAD_DIM + LANE] = jnp.where(blk == lane, 1.0, 0.0).astype(BF16)
            vbs_ref[rows, 0:HEAD_DIM] = vs_ref[rows, :]
            vbs_ref[rows, HEAD_DIM:2 * HEAD_DIM] = ones
            kbw_ref[rows, :] = kw_ref[rows, :]
            vbw_ref[rows, 0:HEAD_DIM] = vw_ref[rows, :]
            vbw_ref[rows, HEAD_DIM:2 * HEAD_DIM] = ones
            return carry

        lax.fori_loop(0, seq // tk, prep, 0)

    qscale = (HEAD_DIM ** -0.5) * LOG2E
    for r in range(n_rep):
        qs_ref[r * tq:(r + 1) * tq, 0:HEAD_DIM] = (q_ref[:, r * HEAD_DIM:(r + 1) * HEAD_DIM].astype(F32)
                                                   * qscale).astype(BF16)

    for m_ref, accl_ref in ((ms_ref, as_ref), (mw_ref, aw_ref)):
        m_ref[...] = jnp.full(m_ref.shape, NEG_INF, F32)
        accl_ref[...] = jnp.zeros(accl_ref.shape, F32)

    q_idx = lax.broadcasted_iota(jnp.int32, (tq, tk), 0)
    k_idx = lax.broadcasted_iota(jnp.int32, (tq, tk), 1)

    def tile_bias(off, max_dist):
        dist = off * tk + q_idx - k_idx
        if off == 0:
            ok = dist >= 0
        elif max_dist is not None and (off + 1) * tk - 1 > max_dist:
            ok = dist <= max_dist
        else:
            return None
        return jnp.where(ok, 0.0, NEG_INF)

    def issue(tiles, load_q, kb_ref):
        scores = []
        for kt, tbias in tiles:
            k0 = pl.multiple_of(kt * tk, tk)
            scores.append((_dot_nt(load_q(), kb_ref[pl.ds(k0, tk), :]), k0, tbias))
        return scores

    def finish(scores, vb_ref, m_ref, accl_ref):
        for s_tile, k0, tbias in scores:
            pts, alphas = [], []
            for c in range(n_ch):
                rows = slice(c * ch, (c + 1) * ch)
                s = s_tile[rows, :]
                if tbias is not None:
                    b0 = (c % n_sub) * ch
                    s = s + tbias[b0:b0 + ch, :]
                m_old = m_ref[rows, :]
                m_new = jnp.maximum(m_old, jnp.max(s, axis=-1, keepdims=True))
                alphas.append(jnp.exp2(m_old - m_new))
                pts.append(jnp.exp2((s - jnp.concatenate([m_new] * (tk // LANE), axis=1)).astype(BF16)))
                m_ref[rows, :] = m_new
            pv = _dot(jnp.concatenate(pts, axis=0), vb_ref[pl.ds(k0, tk), :])
            for c in range(n_ch):
                rows = slice(c * ch, (c + 1) * ch)
                alpha2 = jnp.concatenate([alphas[c], alphas[c]], axis=1)
                accl_ref[rows, :] = alpha2 * accl_ref[rows, :] + pv[rows, :]

    max_dist = WIN_LEN - 1
    win_tiles = []
    for off in range(-(-max_dist // tk), -1, -1):
        tbias = tile_bias(off, max_dist)
        if off > 0:
            before_start = jnp.where(i >= off, 0.0, NEG_INF)
            if tbias is None:
                tbias = jnp.where(q_idx + k_idx >= 0, 0.0, NEG_INF)
            tbias = tbias + before_start
        win_tiles.append((jnp.maximum(i - off, 0), tbias))
    win_scores = issue(win_tiles, lambda: qs_ref[:, 0:HEAD_DIM], kbw_ref)

    s_all = _dot_nt(qs_ref[:, 0:HEAD_DIM], kc_ref[...])
    n_cols = lax.broadcasted_iota(jnp.int32, (ch, N_SUB), 1)
    bias, has_any = [], []
    for c in range(n_sub):
        t = q0 + c * ch + lax.broadcasted_iota(jnp.int32, (ch, N_SUB), 0)
        bias.append(jnp.where((n_cols * CMP_STRIDE + (CMP_LEN - 1)) <= t, 0.0, NEG_INF))
        has_any.append(t[:, 0:1] >= CMP_LEN - 1)
    ps = []
    psum = [jnp.zeros((ch, N_SUB), F32) for _ in range(n_sub)]
    for c in range(n_ch):
        s = s_all[c * ch:(c + 1) * ch, :] + bias[c % n_sub]
        e = jnp.exp2(s - jnp.max(s, axis=-1, keepdims=True))
        den = jnp.sum(e, axis=-1, keepdims=True)
        p = e * jnp.where(has_any[c % n_sub], 1.0 / den, 0.0)
        ps.append(p.astype(BF16))
        psum[c % n_sub] = psum[c % n_sub] + p
    oc_ref[...] = _dot(jnp.concatenate(ps, axis=0), vc_ref[...])
    psum = jnp.concatenate(psum, axis=0)

    hi = psum.astype(BF16)
    lo = (psum - hi.astype(F32)).astype(BF16)
    ovt = ovt_ref[...]
    imp_t = _dot_nt(ovt, hi) + _dot_nt(ovt, lo)
    jrow = lax.broadcasted_iota(jnp.int32, (LANE, tq), 0)
    tl = q0 + lax.broadcasted_iota(jnp.int32, (LANE, tq), 1)
    jt = tl >> SEL_SHIFT
    forced = (jrow == 0) | (jrow == jt) | (jrow == jt - 1)
    valid_s = jrow * SEL_LEN <= tl
    sc_ref[...] = jnp.where(forced, 1e4, jnp.where(valid_s, imp_t, -1.0))
    finish(win_scores, vbw_ref, mw_ref, aw_ref)
    n_grp = N_SEL // SUBLANE
    grp = [sc_ref[g * SUBLANE:(g + 1) * SUBLANE, :] for g in range(n_grp)]
    cnt = [jnp.zeros((SUBLANE, tq), F32) for _ in range(n_grp)]
    sub = lax.broadcasted_iota(jnp.int32, (SUBLANE, tq), 0)
    for j in range(N_SEL):
        row = sc_ref[j:j + 1, :]
        for g in range(n_grp):
            if g < j // SUBLANE:
                beats = row > grp[g]
            elif g > j // SUBLANE:
                beats = row >= grp[g]
            else:
                beats = (row > grp[g]) | ((row == grp[g]) & (sub > j % SUBLANE))
            cnt[g] = cnt[g] + jnp.where(beats, 1.0, 0.0)
    pen_t = [jnp.where(c < float(SEL_TOP), 0.0, NEG_INF) for c in cnt]
    pen_t = jnp.concatenate(pen_t + [jnp.full((LANE - N_SEL, tq), NEG_INF, F32)], axis=0)
    pen = pen_t.T.astype(BF16)
    for r in range(n_rep):
        qs_ref[r * tq:(r + 1) * tq, HEAD_DIM:HEAD_DIM + LANE] = pen

    def sel_steps(tiles):
        finish(issue(tiles, lambda: qs_ref[...], kbs_ref), vbs_ref, ms_ref, as_ref)

    def sel_body(j, carry):
        sel_steps([(SEL_UNROLL * j + u, None) for u in range(SEL_UNROLL)])
        return carry

    lax.fori_loop(0, i // SEL_UNROLL, sel_body, 0)

    for rem in range(SEL_UNROLL):
        @pl.when(i % SEL_UNROLL == rem)
        def _(rem=rem):
            sel_steps([(i - rem + u, None) for u in range(rem)] + [(i, tile_bias(0, None))])

    g = 1.0 / (1.0 + jnp.exp(-gate_ref[...].astype(F32)))
    lane = lax.broadcasted_iota(jnp.int32, g.shape, 1)
    head0 = pl.program_id(1) * n_rep

    def gate(head, branch):
        return jnp.sum(jnp.where(lane == head * 3 + branch, g, 0.0), axis=-1, keepdims=True)

    for r in range(n_rep):
        rows = slice(r * tq, (r + 1) * tq)
        o = (gate(head0 + r, 0) * oc_ref[rows, :]
             + gate(head0 + r, 1) * (as_ref[rows, 0:HEAD_DIM] / as_ref[rows, HEAD_DIM:2 * HEAD_DIM])
             + gate(head0 + r, 2) * (aw_ref[rows, 0:HEAD_DIM] / aw_ref[rows, HEAD_DIM:2 * HEAD_DIM]))
        o_ref[:, r * HEAD_DIM:(r + 1) * HEAD_DIM] = o.astype(o_ref.dtype)


def nsa_attention(z, kvc, ovt, *, n_batch, seq):
    tq = ATT_TQ
    nq = seq // tq
    n_rep = C_GROUP
    qw = n_rep * HEAD_DIM
    kv_blk = (seq, HEAD_DIM)
    sel0, win0 = (COL_SEL - Z_OFF) // HEAD_DIM, (COL_WIN - Z_OFF) // HEAD_DIM
    return pl.pallas_call(
        _nsa_kernel,
        out_shape=jax.ShapeDtypeStruct((n_batch * seq, C_HEADS * HEAD_DIM), BF16),
        grid=(n_batch, C_KV_HEADS, nq),
        in_specs=[
            pl.BlockSpec((tq, qw), lambda b, g, i: (b * nq + i, (COL_CQ - Z_OFF) // qw + g)),
            pl.BlockSpec((N_SUB, HEAD_DIM), lambda b, g, i: (b * N_CMP + g, 0)),
            pl.BlockSpec((N_SUB, HEAD_DIM), lambda b, g, i: (b * N_CMP + C_KV_HEADS + g, 0)),
            pl.BlockSpec((LANE, N_SUB), lambda b, g, i: (0, 0)),
            pl.BlockSpec(kv_blk, lambda b, g, i: (b, sel0 + g)),
            pl.BlockSpec(kv_blk, lambda b, g, i: (b, sel0 + C_KV_HEADS + g)),
            pl.BlockSpec(kv_blk, lambda b, g, i: (b, win0 + g)),
            pl.BlockSpec(kv_blk, lambda b, g, i: (b, win0 + C_KV_HEADS + g)),
            pl.BlockSpec((tq, LANE), lambda b, g, i: (b * nq + i, (COL_GATE - Z_OFF) // LANE)),
        ],
        out_specs=pl.BlockSpec((tq, qw), lambda b, g, i: (b * nq + i, g)),
        scratch_shapes=[
            pltpu.VMEM((n_rep * tq, HEAD_DIM + LANE), BF16),
            pltpu.VMEM((seq, HEAD_DIM + LANE), BF16),
            pltpu.VMEM((seq, 2 * HEAD_DIM), BF16),
            pltpu.VMEM((seq, HEAD_DIM), BF16),
            pltpu.VMEM((seq, 2 * HEAD_DIM), BF16),
            pltpu.VMEM((n_rep * tq, LANE), F32),
            pltpu.VMEM((n_rep * tq, 2 * HEAD_DIM), F32),
            pltpu.VMEM((n_rep * tq, LANE), F32),
            pltpu.VMEM((n_rep * tq, 2 * HEAD_DIM), F32),
            pltpu.VMEM((n_rep * tq, HEAD_DIM), F32),
            pltpu.VMEM((LANE, tq), F32),
        ],
        compiler_params=_cparams(("parallel", "parallel", "arbitrary")),
        name="nsa_attention",
    )(z, kvc, kvc, ovt, z, z, z, z, z)


def _overlap_t():
    n_c = N_SUB - 1
    c_start = np.arange(n_c) * CMP_STRIDE
    s_start = np.arange(N_SEL) * SEL_LEN
    ov = ((c_start[:, None] <= s_start[None, :] + SEL_LEN - 1)
          & (c_start[:, None] + CMP_LEN - 1 >= s_start[None, :])).astype(np.float32)
    out = np.zeros((LANE, N_SUB), np.float32)
    out[:N_SEL, :n_c] = ov.T
    return out


def _mix_out_kernel(x_ref, oa_ref, ob_ref, oc_ref, w_ref, o_ref):
    mix = jnp.concatenate([oa_ref[...], ob_ref[...], oc_ref[...]], axis=1)
    o_ref[...] = x_ref[...] + _dot(mix, w_ref[...])


def mix_out(x, out_a, out_b, out_c, w, layer, *, tm):
    m, n = x.shape
    return pl.pallas_call(
        _mix_out_kernel,
        out_shape=jax.ShapeDtypeStruct((m, n), F32),
        grid=(m // tm,),
        in_specs=[
            pl.BlockSpec((tm, n), lambda i: (i, 0)),
            pl.BlockSpec((tm, out_a.shape[1]), lambda i: (i, 0)),
            pl.BlockSpec((tm, out_b.shape[1]), lambda i: (i, 0)),
            pl.BlockSpec((tm, out_c.shape[1]), lambda i: (i, 0)),
            pl.BlockSpec((None,) + w.shape[1:], lambda i: (layer, 0, 0), pipeline_mode=pl.Buffered(1)),
        ],
        out_specs=pl.BlockSpec((tm, n), lambda i: (i, 0)),
        compiler_params=_cparams(("parallel",)),
        name="mix_out",
    )(x, out_a, out_b, out_c, w)


def _xattn_kernel(x_ref, g_ref, wq_ref, k_ref, v_ref, wo_ref, o_ref):
    n_mem = k_ref.shape[0]
    rows = x_ref.shape[0]
    ones = jnp.ones((n_mem, HEAD_DIM), BF16)
    kv = [(k_ref[:, h * HEAD_DIM:(h + 1) * HEAD_DIM].astype(BF16),
           jnp.concatenate([v_ref[:, h * HEAD_DIM:(h + 1) * HEAD_DIM].astype(BF16), ones], axis=1))
          for h in range(X_HEADS)]
    for c in range(rows // NORM_ROWS):
        r = slice(c * NORM_ROWS, (c + 1) * NORM_ROWS)
        x = x_ref[r, :]
        hn = _rms_rows(x, g_ref[...]).astype(BF16)
        q = _dot(hn, wq_ref[...]) * ((HEAD_DIM ** -0.5) * LOG2E)
        outs = []
        for h in range(X_HEADS):
            s = _dot_nt(q[:, h * HEAD_DIM:(h + 1) * HEAD_DIM].astype(BF16), kv[h][0])
            e = jnp.exp2((s - jnp.max(s, axis=-1, keepdims=True)).astype(BF16))
            pv = _dot(e, kv[h][1])
            outs.append((pv[:, 0:HEAD_DIM] / pv[:, HEAD_DIM:2 * HEAD_DIM]).astype(BF16))
        o_ref[r, :] = x + _dot(jnp.concatenate(outs, axis=1), wo_ref[...])


def cross_attention_block(x, g, wq, kv, wo, layer, *, n_batch, seq, n_mem, tm):
    nq = seq // tm
    d = x.shape[1]
    return pl.pallas_call(
        _xattn_kernel,
        out_shape=jax.ShapeDtypeStruct(x.shape, F32),
        grid=(n_batch, nq),
        in_specs=[
            pl.BlockSpec((tm, d), lambda b, i: (b * nq + i, 0)),
            pl.BlockSpec((1, d), lambda b, i: (0, 0)),
            pl.BlockSpec((None, d, X_WIDTH), lambda b, i: (layer, 0, 0), pipeline_mode=pl.Buffered(1)),
            pl.BlockSpec((n_mem, X_WIDTH), lambda b, i: (b, 0)),
            pl.BlockSpec((n_mem, X_WIDTH), lambda b, i: (b, 1)),
            pl.BlockSpec((None, X_WIDTH, d), lambda b, i: (layer, 0, 0), pipeline_mode=pl.Buffered(1)),
        ],
        out_specs=pl.BlockSpec((tm, d), lambda b, i: (b * nq + i, 0)),
        compiler_params=_cparams(("parallel", "parallel")),
        name="cross_attention",
    )(x, g.reshape(1, d), wq, kv, kv, wo)


def kernel(x, mem, norm_mix, w_in, gmlp_ln_g, gmlp_ln_b, gmlp_w_s, gmlp_b_s, cmp_pos, cmp_k_w1, cmp_k_w2,
           cmp_v_w1, cmp_v_w2, w_out, norm_xattn, norm_mem, xattn_wq, xattn_wkv, xattn_wo, norm_mlp, w_up,
           w_down, final_norm):
    bsz, seq, d = x.shape
    n_mem = mem.shape[1]
    t = bsz * seq
    assert d == D_MODEL and seq // CMP_STRIDE == N_SUB and seq // SEL_LEN == N_SEL
    assert seq % (DIL_KW * max(dil for _, dil in DILATED_PAIRS)) == 0 and seq % ATT_TQ == 0

    w_in_b = jnp.pad(w_in, ((0, 0), (0, 0), (0, IN_PAD - IN_WIDTH))).astype(BF16)
    w_out_b = w_out.astype(BF16)
    wq_b = xattn_wq.astype(BF16)
    wkv_b = xattn_wkv.astype(BF16)
    wo_b = xattn_wo.astype(BF16)
    w_up_b = w_up.astype(BF16)
    w_down_b = w_down.astype(BF16)
    cmp_w1 = jnp.stack([cmp_k_w1, cmp_v_w1], axis=1).astype(BF16)
    cmp_w2 = jnp.stack([cmp_k_w2, cmp_v_w2], axis=1).astype(BF16)
    ovt = jnp.asarray(_overlap_t()).astype(BF16)
    xf = x.reshape(t, d)
    memf = mem.reshape(bsz * n_mem, d)

    for l in range(DEPTH):
        z, out_a = in_proj(xf, norm_mix[l], w_in_b, l, gmlp_ln_g[l], gmlp_ln_b[l], gmlp_w_s[l], gmlp_b_s[l],
                           tm=TM_PROJ)

        out_b = dilated_attention(z, n_batch=bsz, seq=seq)

        kvc = compress(z, cmp_pos[l], cmp_w1, cmp_w2, l, n_batch=bsz, seq=seq)
        out_c = nsa_attention(z, kvc, ovt, n_batch=bsz, seq=seq)

        xf = mix_out(xf, out_a, out_b, out_c, w_out_b, l, tm=TM_PROJ)

        kv = norm_matmul(memf, norm_mem[l], wkv_b, l, tm=TM_PROJ)
        xf = cross_attention_block(xf, norm_xattn[l], wq_b, kv, wo_b, l, n_batch=bsz, seq=seq, n_mem=n_mem,
                                   tm=TM_XATTN)

        xf = mlp(xf, norm_mlp[l], w_up_b, w_down_b, l, final_norm, tm=TM_MLP, tf=TF_MLP,
                 norm_out=(l == DEPTH - 1))

    return xf.reshape(bsz, seq, d)
```

```python
import functools
import math

import numpy as np
import jax
import jax.numpy as jnp
from jax import lax
from jax.experimental import pallas as pl
from jax.experimental.pallas import tpu as pltpu

F32 = jnp.float32
BF16 = jnp.bfloat16

D_MODEL = 2048
DEPTH = 4
HEAD_DIM = 128
A_GROUPS = 4
A_WIDTH = A_GROUPS * HEAD_DIM
CHUNK = 128
B_HEADS = 4
DILATED_PAIRS = ((128, 1), (512, 4), (2048, 16))
C_HEADS = 8
C_KV_HEADS = 2
C_GROUP = C_HEADS // C_KV_HEADS
CMP_LEN = 32
CMP_STRIDE = 16
SEL_LEN = 64
SEL_SHIFT = SEL_LEN.bit_length() - 1
SEL_TOP = 16
WIN_LEN = 512
X_HEADS = 4
X_WIDTH = X_HEADS * HEAD_DIM
EPS = 1e-6
NEG_INF = -1e30
LOG2E = math.log2(math.e)

COL_B = 2 * A_WIDTH
COL_CQ = COL_B + 3 * B_HEADS * HEAD_DIM
COL_CMP = COL_CQ + C_HEADS * HEAD_DIM
COL_SEL = COL_CMP + 2 * C_KV_HEADS * HEAD_DIM
COL_WIN = COL_SEL + 2 * C_KV_HEADS * HEAD_DIM
COL_GATE = COL_WIN + 2 * C_KV_HEADS * HEAD_DIM
IN_WIDTH = COL_GATE + 3 * C_HEADS
IN_PAD = 5248
Z_OFF = COL_B
Z_WIDTH = IN_PAD - Z_OFF

LANE = 128
SUBLANE = 8
VMEM_LIMIT = 56 * 1024 * 1024

TM_PROJ = 512
TM_XATTN = 1024
TM_MLP = 1024
TF_MLP = 512
NORM_ROWS = 256
ATT_TQ = 256
ATT_TK = 256
ATT_CH = 128
SEL_UNROLL = 4
CAST_CHUNKS = 128


def _cparams(sem):
    return pltpu.CompilerParams(dimension_semantics=sem, vmem_limit_bytes=VMEM_LIMIT)


def _rms_rows(x, g):
    ms = jnp.mean(x * x, axis=-1, keepdims=True)
    return x * lax.rsqrt(ms + EPS) * g


def _gelu_tanh(x):
    c = math.sqrt(2.0 / math.pi)
    return x * (0.5 * (1.0 + jnp.tanh(c * (x + 0.044715 * (x * x * x)))))


def _dot(a, b):
    return jnp.dot(a, b, preferred_element_type=F32)


def _dot_nt(a, b):
    return lax.dot_general(a, b, (((1,), (1,)), ((), ())), preferred_element_type=F32)


def _norm_into(h_ref, x_ref, g_ref, copy_ref=None):
    rows = x_ref.shape[0]

    def body(c, carry):
        r0 = pl.multiple_of(c * NORM_ROWS, NORM_ROWS)
        x = x_ref[pl.ds(r0, NORM_ROWS), :]
        h_ref[pl.ds(r0, NORM_ROWS), :] = _rms_rows(x, g_ref[...]).astype(h_ref.dtype)
        if copy_ref is not None:
            copy_ref[pl.ds(r0, NORM_ROWS), :] = x
        return carry

    lax.fori_loop(0, rows // NORM_ROWS, body, 0)


def _norm_matmul_kernel(x_ref, g_ref, w_ref, o_ref):
    half = x_ref.shape[0] // 2
    for c in range(2):
        r = slice(c * half, (c + 1) * half)
        hn = _rms_rows(x_ref[r, :], g_ref[...]).astype(BF16)
        o_ref[r, :] = _dot(hn, w_ref[...]).astype(o_ref.dtype)


def norm_matmul(x, g, w, layer, *, tm, out_dtype=F32):
    m, k = x.shape
    n = w.shape[2]
    return pl.pallas_call(
        _norm_matmul_kernel,
        out_shape=jax.ShapeDtypeStruct((m, n), out_dtype),
        grid=(m // tm,),
        in_specs=[
            pl.BlockSpec((tm, k), lambda i: (i, 0)),
            pl.BlockSpec((1, k), lambda i: (0, 0)),
            pl.BlockSpec((None, k, n), lambda i: (layer, 0, 0), pipeline_mode=pl.Buffered(1)),
        ],
        out_specs=pl.BlockSpec((tm, n), lambda i: (i, 0)),
        compiler_params=_cparams(("parallel",)),
        name="norm_matmul",
    )(x, g.reshape(1, k), w)


def _mlp_kernel(*refs, norm_out, n_cast):
    x_ref, g_ref, wu_ref, wd_ref, gout_ref = refs[:5]
    cast_in = refs[5:5 + n_cast]
    o_ref = refs[5 + n_cast]
    cast_out = refs[6 + n_cast:6 + 2 * n_cast]
    h_ref = refs[6 + 2 * n_cast]
    f = pl.program_id(1)

    @pl.when(f == 0)
    def _():
        _norm_into(h_ref, x_ref, g_ref, copy_ref=o_ref)

    a = _dot(h_ref[...], wu_ref[...])
    a = jnp.square(jnp.maximum(a, 0.0)).astype(BF16)
    o_ref[...] += _dot(a, wd_ref[...])

    for src, dst in zip(cast_in, cast_out):
        dst[...] = src[...].astype(dst.dtype)

    if norm_out:
        @pl.when(f == pl.num_programs(1) - 1)
        def _():
            _norm_into(o_ref, o_ref, gout_ref)


def mlp(x, g, wu, wd, layer, g_out, cast=(), cast_layer=0, *, tm, tf, norm_out):
    m, d = x.shape
    ff = wu.shape[2]
    n_f = ff // tf
    steps_per_chunk = (m // tm) * n_f // CAST_CHUNKS

    def chunk_in(w):
        return pl.BlockSpec((None, w.shape[1] // CAST_CHUNKS, w.shape[2]),
                            lambda i, f: (cast_layer, (i * n_f + f) // steps_per_chunk, 0))

    def chunk_out(w):
        return pl.BlockSpec((w.shape[1] // CAST_CHUNKS, w.shape[2]),
                            lambda i, f: ((i * n_f + f) // steps_per_chunk, 0))

    res = pl.pallas_call(
        functools.partial(_mlp_kernel, norm_out=norm_out, n_cast=len(cast)),
        out_shape=[jax.ShapeDtypeStruct((m, d), F32)] + [jax.ShapeDtypeStruct(w.shape[1:], BF16) for w in cast],
        grid=(m // tm, n_f),
        in_specs=[
            pl.BlockSpec((tm, d), lambda i, f: (i, 0)),
            pl.BlockSpec((1, d), lambda i, f: (0, 0)),
            pl.BlockSpec((None, d, tf), lambda i, f: (layer, 0, f)),
            pl.BlockSpec((None, tf, d), lambda i, f: (layer, f, 0)),
            pl.BlockSpec((1, d), lambda i, f: (0, 0)),
        ] + [chunk_in(w) for w in cast],
        out_specs=[pl.BlockSpec((tm, d), lambda i, f: (i, 0))] + [chunk_out(w) for w in cast],
        scratch_shapes=[pltpu.VMEM((tm, d), BF16)],
        compiler_params=_cparams(("parallel", "arbitrary")),
        name="mlp",
    )(x, g.reshape(1, d), wu, wd, g_out.reshape(1, d), *cast)
    return res[0], res[1:]


def _in_proj_kernel(x_ref, g_ref, w_ref, lng_ref, lnb_ref, ws_ref, bst_ref, z_ref, oa_ref):
    row = lax.broadcasted_iota(jnp.int32, (CHUNK, CHUNK), 0)
    col = lax.broadcasted_iota(jnp.int32, (CHUNK, CHUNK), 1)
    causal = row >= col
    w_tril = [jnp.where(causal, ws_ref[gi], 0.0).astype(BF16) for gi in range(A_GROUPS)]

    def gate_inputs(za):
        u = _gelu_tanh(za[:, 0:A_WIDTH])
        v = _gelu_tanh(za[:, A_WIDTH:2 * A_WIDTH])
        vc = v - jnp.mean(v, axis=-1, keepdims=True)
        vn = vc * lax.rsqrt(jnp.mean(vc * vc, axis=-1, keepdims=True) + EPS)
        return u, (vn * lng_ref[...] + lnb_ref[...]).astype(BF16)

    def gate_store(r0, u, vn):
        for gi in range(A_GROUPS):
            cols = slice(gi * HEAD_DIM, (gi + 1) * HEAD_DIM)
            sv = _dot(w_tril[gi], vn[:, cols]) + bst_ref[:, gi:gi + 1]
            oa_ref[r0:r0 + CHUNK, cols] = (u[:, cols] * sv).astype(oa_ref.dtype)

    half = x_ref.shape[0] // 2
    pending = []
    for c in range(2):
        r = slice(c * half, (c + 1) * half)
        hn = _rms_rows(x_ref[r, :], g_ref[...]).astype(BF16)
        za = _dot(hn, w_ref[:, 0:Z_OFF])
        for r0, u, vn in pending:
            gate_store(r0, u, vn)
        z_ref[r, :] = _dot(hn, w_ref[:, Z_OFF:IN_PAD]).astype(z_ref.dtype)
        pending = [(c * half + k * CHUNK,) + gate_inputs(za[k * CHUNK:(k + 1) * CHUNK, :])
                   for k in range(half // CHUNK)]
    for r0, u, vn in pending:
        gate_store(r0, u, vn)


def in_proj(x, g, w, layer, ln_g, ln_b, w_s, b_s, *, tm):
    m, k = x.shape
    return pl.pallas_call(
        _in_proj_kernel,
        out_shape=[jax.ShapeDtypeStruct((m, Z_WIDTH), BF16), jax.ShapeDtypeStruct((m, A_WIDTH), BF16)],
        grid=(m // tm,),
        in_specs=[
            pl.BlockSpec((tm, k), lambda i: (i, 0)),
            pl.BlockSpec((1, k), lambda i: (0, 0)),
            pl.BlockSpec((None, k, IN_PAD), lambda i: (layer, 0, 0), pipeline_mode=pl.Buffered(1)),
            pl.BlockSpec((1, A_WIDTH), lambda i: (0, 0)),
            pl.BlockSpec((1, A_WIDTH), lambda i: (0, 0)),
            pl.BlockSpec((A_GROUPS, CHUNK, CHUNK), lambda i: (0, 0, 0)),
            pl.BlockSpec((CHUNK, A_GROUPS), lambda i: (0, 0)),
        ],
        out_specs=[pl.BlockSpec((tm, Z_WIDTH), lambda i: (i, 0)), pl.BlockSpec((tm, A_WIDTH), lambda i: (i, 0))],
        compiler_params=_cparams(("parallel",)),
        name="in_proj",
    )(x, g.reshape(1, k), w, ln_g.reshape(1, A_WIDTH), ln_b.reshape(1, A_WIDTH), w_s, b_s.T)


DIL_CH = 128
DIL_KW = 2 * DIL_CH
DIL_UNITS = 8


def _dilated_kernel(q_in, k_in, v_in, o_ref, q_ref, k_ref, v_ref, m_ref, acc_ref, l_ref):
    ch, kw = DIL_CH, DIL_KW
    seq = q_in.shape[0]
    for src, dst in ((q_in, q_ref), (k_in, k_ref), (v_in, v_ref)):
        def stage(c, carry, src=src, dst=dst):
            r0 = pl.multiple_of(c * NORM_ROWS, NORM_ROWS)
            dst[pl.ds(r0, NORM_ROWS), :] = src[pl.ds(r0, NORM_ROWS), :].astype(F32)
            return carry
        lax.fori_loop(0, seq // NORM_ROWS, stage, 0)
    qscale = (HEAD_DIM ** -0.5) * LOG2E
    row = lax.broadcasted_iota(jnp.int32, (ch, kw), 0)
    col = lax.broadcasted_iota(jnp.int32, (ch, kw), 1)
    ones = jnp.ones((kw, HEAD_DIM), BF16)
    patterns = sorted(DILATED_PAIRS, key=lambda wd: -wd[1])

    def rows(start, n, dil):
        return pl.ds(start, n) if dil == 1 else pl.ds(start, n, stride=dil)

    def band_bias(first, span):
        dist = row - col if first else ch + row - col
        return jnp.where((dist >= 0) & (dist <= span), 0.0, NEG_INF)

    def run_units(units, dil, span, phase):
        windows = {}
        scores = []
        for q0, k0, first, wid in units:
            if wid not in windows:
                k = k_ref[rows(k0, kw, dil), :].astype(BF16)
                v = v_ref[rows(k0, kw, dil), :].astype(BF16)
                windows[wid] = (k, jnp.concatenate([v, ones], axis=1))
            q = (q_ref[rows(q0, ch, dil), :] * qscale).astype(BF16)
            scores.append(_dot_nt(q, windows[wid][0]) + band_bias(first, span))
        for (q0, k0, first, wid), s in zip(units, scores):
            qrows = rows(q0, ch, dil)
            m_cur = jnp.max(s, axis=-1, keepdims=True)
            if phase == "first":
                m_new = jnp.broadcast_to(m_cur, (ch, LANE))
            else:
                m_old = m_ref[qrows, :]
                m_new = jnp.maximum(m_old, m_cur)
                alpha = jnp.exp2(m_old - m_new)
            p = jnp.exp2((s - jnp.concatenate([m_new] * (kw // LANE), axis=1)).astype(BF16))
            pv = _dot(p, windows[wid][1])
            acc, l = pv[:, 0:HEAD_DIM], pv[:, HEAD_DIM:2 * HEAD_DIM]
            if phase != "first":
                acc = alpha * acc_ref[qrows, :] + acc
                l = alpha * l_ref[qrows, :] + l
            if phase == "last":
                o_ref[pl.ds(pl.multiple_of(q0, ch), ch), :] = (acc / l).astype(o_ref.dtype)
            else:
                m_ref[qrows, :] = m_new
                acc_ref[qrows, :] = acc
                l_ref[qrows, :] = l

    for idx, (window, dil) in enumerate(patterns):
        phase = "first" if idx == 0 else ("last" if idx == len(patterns) - 1 else "mid")
        assert phase != "last" or dil == 1
        span = window // dil
        assert span <= ch
        n_units = seq // dil // ch
        per_class = min(n_units, max(DIL_UNITS // dil, 2))
        n_classes = DIL_UNITS // per_class
        assert n_units % per_class == 0 and dil % n_classes == 0

        def group(r0, j0, head, dil=dil, span=span, phase=phase, per_class=per_class, n_classes=n_classes):
            units = []
            for rc in range(n_classes):
                r = r0 + rc
                for u in range(per_class):
                    first = head and u == 0
                    q0 = r + dil * ch * (j0 + u)
                    k0 = r if (head and u <= 1) else r + dil * ch * (j0 + u - 1)
                    wid = (rc, 0) if (head and u <= 1) else (rc, u)
                    units.append((q0, k0, first, wid))
            run_units(units, dil, span, phase)

        def class_loop(g, carry, group=group, per_class=per_class, n_classes=n_classes, n_units=n_units):
            r0 = g * n_classes
            group(r0, 0, True)
            if n_units > per_class:
                def tail(jb, c):
                    group(r0, jb * per_class, False)
                    return c
                lax.fori_loop(1, n_units // per_class, tail, 0)
            return carry

        if dil // n_classes == 1:
            class_loop(0, 0)
        else:
            lax.fori_loop(0, dil // n_classes, class_loop, 0)


def dilated_attention(z, *, n_batch, seq):
    bw = B_HEADS * HEAD_DIM
    base = (COL_B - Z_OFF) // HEAD_DIM
    blk = (seq, HEAD_DIM)
    return pl.pallas_call(
        _dilated_kernel,
        out_shape=jax.ShapeDtypeStruct((n_batch * seq, bw), BF16),
        grid=(n_batch, B_HEADS),
        in_specs=[
            pl.BlockSpec(blk, lambda b, h: (b, base + h)),
            pl.BlockSpec(blk, lambda b, h: (b, base + B_HEADS + h)),
            pl.BlockSpec(blk, lambda b, h: (b, base + 2 * B_HEADS + h)),
        ],
        out_specs=pl.BlockSpec(blk, lambda b, h: (b, h)),
        scratch_shapes=[pltpu.VMEM(blk, F32)] * 6,
        compiler_params=_cparams(("parallel", "parallel")),
        name="dilated_attention",
    )(z, z, z)


N_SUB = 256
N_CMP = 2 * C_KV_HEADS


def _compress_kernel(x_in, pos_ref, w1_ref, w2_ref, o_ref, x_ref):
    x_ref[...] = x_in[...].astype(F32)
    p = jnp.zeros((N_SUB, HEAD_DIM), F32)
    q = jnp.zeros((N_SUB, HEAD_DIM), F32)
    for i in range(CMP_STRIDE):
        a = x_ref[pl.ds(i, N_SUB, stride=CMP_STRIDE), :]
        top = (a + pos_ref[i:i + 1, :]).astype(BF16)
        bot = (a + pos_ref[CMP_STRIDE + i:CMP_STRIDE + i + 1, :]).astype(BF16)
        p = p + _dot(top, w1_ref[i * HEAD_DIM:(i + 1) * HEAD_DIM, :])
        q = q + _dot(bot, w1_ref[(CMP_STRIDE + i) * HEAD_DIM:(CMP_STRIDE + i + 1) * HEAD_DIM, :])
    h = _gelu_tanh(p + pltpu.roll(q, N_SUB - 1, 0))
    o_ref[...] = _dot(h.astype(BF16), w2_ref[...]).astype(o_ref.dtype)


def compress(z, pos, w1, w2, layer, *, n_batch, seq):
    return pl.pallas_call(
        _compress_kernel,
        out_shape=jax.ShapeDtypeStruct((n_batch * N_CMP * N_SUB, HEAD_DIM), BF16),
        grid=(n_batch, N_CMP),
        in_specs=[
            pl.BlockSpec((seq, HEAD_DIM), lambda b, n: (b, (COL_CMP - Z_OFF) // HEAD_DIM + n)),
            pl.BlockSpec((CMP_LEN, HEAD_DIM), lambda b, n: (0, 0)),
            pl.BlockSpec((None, None, CMP_LEN * HEAD_DIM, HEAD_DIM), lambda b, n: (layer, n // C_KV_HEADS, 0, 0)),
            pl.BlockSpec((None, None, HEAD_DIM, HEAD_DIM), lambda b, n: (layer, n // C_KV_HEADS, 0, 0)),
        ],
        out_specs=pl.BlockSpec((N_SUB, HEAD_DIM), lambda b, n: (b * N_CMP + n, 0)),
        scratch_shapes=[pltpu.VMEM((seq, HEAD_DIM), F32)],
        compiler_params=_cparams(("parallel", "parallel")),
        name="nsa_compress",
    )(z, pos, w1, w2)


N_SEL = 64


def _nsa_kernel(q_ref, kc_ref, vc_ref, ovt_ref, ks_ref, vs_ref, kw_ref, vw_ref, gate_ref, o_ref,
                qs_ref, kbs_ref, vbs_ref, kbw_ref, vbw_ref, ms_ref, as_ref, mw_ref, aw_ref, oc_ref, sc_ref):
    tq, tk, ch = ATT_TQ, ATT_TK, ATT_CH
    n_rep = C_GROUP
    n_sub = tq // ch
    n_ch = n_rep * n_sub
    seq = ks_ref.shape[0]
    i = pl.program_id(2)
    q0 = i * tq

    @pl.when(i == 0)
    def _():
        def prep(c, carry):
            rows = pl.ds(pl.multiple_of(c * tk, tk), tk)
            blk = (c * tk + lax.broadcasted_iota(jnp.int32, (tk, LANE), 0)) >> SEL_SHIFT
            lane = lax.broadcasted_iota(jnp.int32, (tk, LANE), 1)
            ones = jnp.ones((tk, HEAD_DIM), BF16)
            kbs_ref[rows, 0:HEAD_DIM] = ks_ref[rows, :]
            kbs_ref[rows, HEAD_DIM:HEAD_DIM + LANE] = jnp.where(blk == lane, 1.0, 0.0).astype(BF16)
            vbs_ref[rows, 0:HEAD_DIM] = vs_ref[rows, :]
            vbs_ref[rows, HEAD_DIM:2 * HEAD_DIM] = ones
            kbw_ref[rows, :] = kw_ref[rows, :]
            vbw_ref[rows, 0:HEAD_DIM] = vw_ref[rows, :]
            vbw_ref[rows, HEAD_DIM:2 * HEAD_DIM] = ones
            return carry

        lax.fori_loop(0, seq // tk, prep, 0)

    qscale = (HEAD_DIM ** -0.5) * LOG2E
    for r in range(n_rep):
        qs_ref[r * tq:(r + 1) * tq, 0:HEAD_DIM] = (q_ref[:, r * HEAD_DIM:(r + 1) * HEAD_DIM].astype(F32)
                                                   * qscale).astype(BF16)

    for m_ref, accl_ref in ((ms_ref, as_ref), (mw_ref, aw_ref)):
        m_ref[...] = jnp.full(m_ref.shape, NEG_INF, F32)
        accl_ref[...] = jnp.zeros(accl_ref.shape, F32)

    q_idx = lax.broadcasted_iota(jnp.int32, (tq, tk), 0)
    k_idx = lax.broadcasted_iota(jnp.int32, (tq, tk), 1)

    def tile_bias(off, max_dist):
        dist = off * tk + q_idx - k_idx
        if off == 0:
            ok = dist >= 0
        elif max_dist is not None and (off + 1) * tk - 1 > max_dist:
            ok = dist <= max_dist
        else:
            return None
        return jnp.where(ok, 0.0, NEG_INF)

    def issue(tiles, load_q, kb_ref):
        scores = []
        for kt, tbias in tiles:
            k0 = pl.multiple_of(kt * tk, tk)
            scores.append((_dot_nt(load_q(), kb_ref[pl.ds(k0, tk), :]), k0, tbias))
        return scores

    def finish(scores, vb_ref, m_ref, accl_ref):
        for s_tile, k0, tbias in scores:
            pts, alphas = [], []
            for c in range(n_ch):
                rows = slice(c * ch, (c + 1) * ch)
                s = s_tile[rows, :]
                if tbias is not None:
                    b0 = (c % n_sub) * ch
                    s = s + tbias[b0:b0 + ch, :]
                m_old = m_ref[rows, :]
                m_new = jnp.maximum(m_old, jnp.max(s, axis=-1, keepdims=True))
                alphas.append(jnp.exp2(m_old - m_new))
                pts.append(jnp.exp2((s - jnp.concatenate([m_new] * (tk // LANE), axis=1)).astype(BF16)))
                m_ref[rows, :] = m_new
            pv = _dot(jnp.concatenate(pts, axis=0), vb_ref[pl.ds(k0, tk), :])
            for c in range(n_ch):
                rows = slice(c * ch, (c + 1) * ch)
                alpha2 = jnp.concatenate([alphas[c], alphas[c]], axis=1)
                accl_ref[rows, :] = alpha2 * accl_ref[rows, :] + pv[rows, :]

    max_dist = WIN_LEN - 1
    win_tiles = []
    for off in range(-(-max_dist // tk), -1, -1):
        tbias = tile_bias(off, max_dist)
        if off > 0:
            before_start = jnp.where(i >= off, 0.0, NEG_INF)
            if tbias is None:
                tbias = jnp.where(q_idx + k_idx >= 0, 0.0, NEG_INF)
            tbias = tbias + before_start
        win_tiles.append((jnp.maximum(i - off, 0), tbias))
    win_scores = issue(win_tiles, lambda: qs_ref[:, 0:HEAD_DIM], kbw_ref)

    s_all = _dot_nt(qs_ref[:, 0:HEAD_DIM], kc_ref[...])
    n_cols = lax.broadcasted_iota(jnp.int32, (ch, N_SUB), 1)
    bias, has_any = [], []
    for c in range(n_sub):
        t = q0 + c * ch + lax.broadcasted_iota(jnp.int32, (ch, N_SUB), 0)
        bias.append(jnp.where((n_cols * CMP_STRIDE + (CMP_LEN - 1)) <= t, 0.0, NEG_INF))
        has_any.append(t[:, 0:1] >= CMP_LEN - 1)
    ps = []
    psum = [jnp.zeros((ch, N_SUB), F32) for _ in range(n_sub)]
    for c in range(n_ch):
        s = s_all[c * ch:(c + 1) * ch, :] + bias[c % n_sub]
        e = jnp.exp2(s - jnp.max(s, axis=-1, keepdims=True))
        den = jnp.sum(e, axis=-1, keepdims=True)
        p = e * jnp.where(has_any[c % n_sub], 1.0 / den, 0.0)
        ps.append(p.astype(BF16))
        psum[c % n_sub] = psum[c % n_sub] + p
    oc_ref[...] = _dot(jnp.concatenate(ps, axis=0), vc_ref[...])
    psum = jnp.concatenate(psum, axis=0)

    hi = psum.astype(BF16)
    lo = (psum - hi.astype(F32)).astype(BF16)
    ovt = ovt_ref[...]
    imp_t = _dot_nt(ovt, hi) + _dot_nt(ovt, lo)
    jrow = lax.broadcasted_iota(jnp.int32, (LANE, tq), 0)
    tl = q0 + lax.broadcasted_iota(jnp.int32, (LANE, tq), 1)
    jt = tl >> SEL_SHIFT
    forced = (jrow == 0) | (jrow == jt) | (jrow == jt - 1)
    valid_s = jrow * SEL_LEN <= tl
    sc_ref[...] = jnp.where(forced, 1e4, jnp.where(valid_s, imp_t, -1.0))
    finish(win_scores, vbw_ref, mw_ref, aw_ref)
    n_grp = N_SEL // SUBLANE
    grp = [sc_ref[g * SUBLANE:(g + 1) * SUBLANE, :] for g in range(n_grp)]
    cnt = [jnp.zeros((SUBLANE, tq), F32) for _ in range(n_grp)]
    sub = lax.broadcasted_iota(jnp.int32, (SUBLANE, tq), 0)
    for j in range(N_SEL):
        row = sc_ref[j:j + 1, :]
        for g in range(n_grp):
            if g < j // SUBLANE:
                beats = row > grp[g]
            elif g > j // SUBLANE:
                beats = row >= grp[g]
            else:
                beats = (row > grp[g]) | ((row == grp[g]) & (sub > j % SUBLANE))
            cnt[g] = cnt[g] + jnp.where(beats, 1.0, 0.0)
    pen_t = [jnp.where(c < float(SEL_TOP), 0.0, NEG_INF) for c in cnt]
    pen_t = jnp.concatenate(pen_t + [jnp.full((LANE - N_SEL, tq), NEG_INF, F32)], axis=0)
    pen = pen_t.T.astype(BF16)
    for r in range(n_rep):
        qs_ref[r * tq:(r + 1) * tq, HEAD_DIM:HEAD_DIM + LANE] = pen

    def sel_steps(tiles):
        finish(issue(tiles, lambda: qs_ref[...], kbs_ref), vbs_ref, ms_ref, as_ref)

    def sel_body(j, carry):
        sel_steps([(SEL_UNROLL * j + u, None) for u in range(SEL_UNROLL)])
        return carry

    lax.fori_loop(0, i // SEL_UNROLL, sel_body, 0)

    for rem in range(SEL_UNROLL):
        @pl.when(i % SEL_UNROLL == rem)
        def _(rem=rem):
            sel_steps([(i - rem + u, None) for u in range(rem)] + [(i, tile_bias(0, None))])

    g = 1.0 / (1.0 + jnp.exp(-gate_ref[...].astype(F32)))
    lane = lax.broadcasted_iota(jnp.int32, g.shape, 1)
    head0 = pl.program_id(1) * n_rep

    def gate(head, branch):
        return jnp.sum(jnp.where(lane == head * 3 + branch, g, 0.0), axis=-1, keepdims=True)

    for r in range(n_rep):
        rows = slice(r * tq, (r + 1) * tq)
        o = (gate(head0 + r, 0) * oc_ref[rows, :]
             + gate(head0 + r, 1) * (as_ref[rows, 0:HEAD_DIM] / as_ref[rows, HEAD_DIM:2 * HEAD_DIM])
             + gate(head0 + r, 2) * (aw_ref[rows, 0:HEAD_DIM] / aw_ref[rows, HEAD_DIM:2 * HEAD_DIM]))
        o_ref[:, r * HEAD_DIM:(r + 1) * HEAD_DIM] = o.astype(o_ref.dtype)


def nsa_attention(z, kvc, ovt, *, n_batch, seq):
    tq = ATT_TQ
    nq = seq // tq
    n_rep = C_GROUP
    qw = n_rep * HEAD_DIM
    kv_blk = (seq, HEAD_DIM)
    sel0, win0 = (COL_SEL - Z_OFF) // HEAD_DIM, (COL_WIN - Z_OFF) // HEAD_DIM
    return pl.pallas_call(
        _nsa_kernel,
        out_shape=jax.ShapeDtypeStruct((n_batch * seq, C_HEADS * HEAD_DIM), BF16),
        grid=(n_batch, C_KV_HEADS, nq),
        in_specs=[
            pl.BlockSpec((tq, qw), lambda b, g, i: (b * nq + i, (COL_CQ - Z_OFF) // qw + g)),
            pl.BlockSpec((N_SUB, HEAD_DIM), lambda b, g, i: (b * N_CMP + g, 0)),
            pl.BlockSpec((N_SUB, HEAD_DIM), lambda b, g, i: (b * N_CMP + C_KV_HEADS + g, 0)),
            pl.BlockSpec((LANE, N_SUB), lambda b, g, i: (0, 0)),
            pl.BlockSpec(kv_blk, lambda b, g, i: (b, sel0 + g)),
            pl.BlockSpec(kv_blk, lambda b, g, i: (b, sel0 + C_KV_HEADS + g)),
            pl.BlockSpec(kv_blk, lambda b, g, i: (b, win0 + g)),
            pl.BlockSpec(kv_blk, lambda b, g, i: (b, win0 + C_KV_HEADS + g)),
            pl.BlockSpec((tq, LANE), lambda b, g, i: (b * nq + i, (COL_GATE - Z_OFF) // LANE)),
        ],
        out_specs=pl.BlockSpec((tq, qw), lambda b, g, i: (b * nq + i, g)),
        scratch_shapes=[
            pltpu.VMEM((n_rep * tq, HEAD_DIM + LANE), BF16),
            pltpu.VMEM((seq, HEAD_DIM + LANE), BF16),
            pltpu.VMEM((seq, 2 * HEAD_DIM), BF16),
            pltpu.VMEM((seq, HEAD_DIM), BF16),
            pltpu.VMEM((seq, 2 * HEAD_DIM), BF16),
            pltpu.VMEM((n_rep * tq, LANE), F32),
            pltpu.VMEM((n_rep * tq, 2 * HEAD_DIM), F32),
            pltpu.VMEM((n_rep * tq, LANE), F32),
            pltpu.VMEM((n_rep * tq, 2 * HEAD_DIM), F32),
            pltpu.VMEM((n_rep * tq, HEAD_DIM), F32),
            pltpu.VMEM((LANE, tq), F32),
        ],
        compiler_params=_cparams(("parallel", "parallel", "arbitrary")),
        name="nsa_attention",
    )(z, kvc, kvc, ovt, z, z, z, z, z)


def _overlap_t():
    n_c = N_SUB - 1
    c_start = np.arange(n_c) * CMP_STRIDE
    s_start = np.arange(N_SEL) * SEL_LEN
    ov = ((c_start[:, None] <= s_start[None, :] + SEL_LEN - 1)
          & (c_start[:, None] + CMP_LEN - 1 >= s_start[None, :])).astype(np.float32)
    out = np.zeros((LANE, N_SUB), np.float32)
    out[:N_SEL, :n_c] = ov.T
    return out


def _mix_out_kernel(x_ref, oa_ref, ob_ref, oc_ref, w_ref, o_ref):
    mix = jnp.concatenate([oa_ref[...], ob_ref[...], oc_ref[...]], axis=1)
    o_ref[...] = x_ref[...] + _dot(mix, w_ref[...])


def mix_out(x, out_a, out_b, out_c, w, layer, *, tm):
    m, n = x.shape
    return pl.pallas_call(
        _mix_out_kernel,
        out_shape=jax.ShapeDtypeStruct((m, n), F32),
        grid=(m // tm,),
        in_specs=[
            pl.BlockSpec((tm, n), lambda i: (i, 0)),
            pl.BlockSpec((tm, out_a.shape[1]), lambda i: (i, 0)),
            pl.BlockSpec((tm, out_b.shape[1]), lambda i: (i, 0)),
            pl.BlockSpec((tm, out_c.shape[1]), lambda i: (i, 0)),
            pl.BlockSpec((None,) + w.shape[1:], lambda i: (layer, 0, 0), pipeline_mode=pl.Buffered(1)),
        ],
        out_specs=pl.BlockSpec((tm, n), lambda i: (i, 0)),
        compiler_params=_cparams(("parallel",)),
        name="mix_out",
    )(x, out_a, out_b, out_c, w)


def _xattn_kernel(x_ref, g_ref, wq_ref, k_ref, v_ref, wo_ref, o_ref):
    n_mem = k_ref.shape[0]
    rows = x_ref.shape[0]
    ones = jnp.ones((n_mem, HEAD_DIM), BF16)
    kv = [(k_ref[:, h * HEAD_DIM:(h + 1) * HEAD_DIM].astype(BF16),
           jnp.concatenate([v_ref[:, h * HEAD_DIM:(h + 1) * HEAD_DIM].astype(BF16), ones], axis=1))
          for h in range(X_HEADS)]
    for c in range(rows // NORM_ROWS):
        r = slice(c * NORM_ROWS, (c + 1) * NORM_ROWS)
        x = x_ref[r, :]
        hn = _rms_rows(x, g_ref[...]).astype(BF16)
        q = _dot(hn, wq_ref[...]) * ((HEAD_DIM ** -0.5) * LOG2E)
        outs = []
        for h in range(X_HEADS):
            s = _dot_nt(q[:, h * HEAD_DIM:(h + 1) * HEAD_DIM].astype(BF16), kv[h][0])
            e = jnp.exp2((s - jnp.max(s, axis=-1, keepdims=True)).astype(BF16))
            pv = _dot(e, kv[h][1])
            outs.append((pv[:, 0:HEAD_DIM] / pv[:, HEAD_DIM:2 * HEAD_DIM]).astype(BF16))
        o_ref[r, :] = x + _dot(jnp.concatenate(outs, axis=1), wo_ref[...])


def cross_attention_block(x, g, wq, kv, wo, layer, *, n_batch, seq, n_mem, tm):
    nq = seq // tm
    d = x.shape[1]
    return pl.pallas_call(
        _xattn_kernel,
        out_shape=jax.ShapeDtypeStruct(x.shape, F32),
        grid=(n_batch, nq),
        in_specs=[
            pl.BlockSpec((tm, d), lambda b, i: (b * nq + i, 0)),
            pl.BlockSpec((1, d), lambda b, i: (0, 0)),
            pl.BlockSpec((None, d, X_WIDTH), lambda b, i: (layer, 0, 0), pipeline_mode=pl.Buffered(1)),
            pl.BlockSpec((n_mem, X_WIDTH), lambda b, i: (b, 0)),
            pl.BlockSpec((n_mem, X_WIDTH), lambda b, i: (b, 1)),
            pl.BlockSpec((None, X_WIDTH, d), lambda b, i: (layer, 0, 0), pipeline_mode=pl.Buffered(1)),
        ],
        out_specs=pl.BlockSpec((tm, d), lambda b, i: (b * nq + i, 0)),
        compiler_params=_cparams(("parallel", "parallel")),
        name="cross_attention",
    )(x, g.reshape(1, d), wq, kv, kv, wo)


def kernel(x, mem, norm_mix, w_in, gmlp_ln_g, gmlp_ln_b, gmlp_w_s, gmlp_b_s, cmp_pos, cmp_k_w1, cmp_k_w2,
           cmp_v_w1, cmp_v_w2, w_out, norm_xattn, norm_mem, xattn_wq, xattn_wkv, xattn_wo, norm_mlp, w_up,
           w_down, final_norm):
    bsz, seq, d = x.shape
    n_mem = mem.shape[1]
    t = bsz * seq
    assert d == D_MODEL and seq // CMP_STRIDE == N_SUB and seq // SEL_LEN == N_SEL
    assert seq % (DIL_KW * max(dil for _, dil in DILATED_PAIRS)) == 0 and seq % ATT_TQ == 0

    w_in_b = jnp.pad(w_in, ((0, 0), (0, 0), (0, IN_PAD - IN_WIDTH))).astype(BF16)
    wq_b = xattn_wq.astype(BF16)
    wkv_b = xattn_wkv.astype(BF16)
    wo_b = xattn_wo.astype(BF16)
    w_out_l, w_up_l, w_down_l = (w[0].astype(BF16)[None] for w in (w_out, w_up, w_down))
    cmp_w1 = jnp.stack([cmp_k_w1, cmp_v_w1], axis=1).astype(BF16)
    cmp_w2 = jnp.stack([cmp_k_w2, cmp_v_w2], axis=1).astype(BF16)
    ovt = jnp.asarray(_overlap_t()).astype(BF16)
    xf = x.reshape(t, d)
    memf = mem.reshape(bsz * n_mem, d)

    for l in range(DEPTH):
        z, out_a = in_proj(xf, norm_mix[l], w_in_b, l, gmlp_ln_g[l], gmlp_ln_b[l], gmlp_w_s[l], gmlp_b_s[l],
                           tm=TM_PROJ)

        out_b = dilated_attention(z, n_batch=bsz, seq=seq)

        kvc = compress(z, cmp_pos[l], cmp_w1, cmp_w2, l, n_batch=bsz, seq=seq)
        out_c = nsa_attention(z, kvc, ovt, n_batch=bsz, seq=seq)

        xf = mix_out(xf, out_a, out_b, out_c, w_out_l, 0, tm=TM_PROJ)

        kv = norm_matmul(memf, norm_mem[l], wkv_b, l, tm=TM_PROJ)
        xf = cross_attention_block(xf, norm_xattn[l], wq_b, kv, wo_b, l, n_batch=bsz, seq=seq, n_mem=n_mem,
                                   tm=TM_XATTN)

        nxt = () if l == DEPTH - 1 else (w_out, w_up, w_down)
        xf, nxt_b = mlp(xf, norm_mlp[l], w_up_l, w_down_l, 0, final_norm, nxt, l + 1, tm=TM_MLP, tf=TF_MLP,
                        norm_out=(l == DEPTH - 1))
        if nxt_b:
            w_out_l, w_up_l, w_down_l = (w[None] for w in nxt_b)

    return xf.reshape(bsz, seq, d)
```

```python
import functools
import math

import numpy as np
import jax
import jax.numpy as jnp
from jax import lax
from jax.experimental import pallas as pl
from jax.experimental.pallas import tpu as pltpu

F32 = jnp.float32
BF16 = jnp.bfloat16

D_MODEL = 2048
DEPTH = 4
HEAD_DIM = 128
A_GROUPS = 4
A_WIDTH = A_GROUPS * HEAD_DIM
CHUNK = 128
B_HEADS = 4
DILATED_PAIRS = ((128, 1), (512, 4), (2048, 16))
C_HEADS = 8
C_KV_HEADS = 2
C_GROUP = C_HEADS // C_KV_HEADS
CMP_LEN = 32
CMP_STRIDE = 16
SEL_LEN = 64
SEL_SHIFT = SEL_LEN.bit_length() - 1
SEL_TOP = 16
WIN_LEN = 512
X_HEADS = 4
X_WIDTH = X_HEADS * HEAD_DIM
EPS = 1e-6
NEG_INF = -1e30
LOG2E = math.log2(math.e)

COL_B = 2 * A_WIDTH
COL_CQ = COL_B + 3 * B_HEADS * HEAD_DIM
COL_CMP = COL_CQ + C_HEADS * HEAD_DIM
COL_SEL = COL_CMP + 2 * C_KV_HEADS * HEAD_DIM
COL_WIN = COL_SEL + 2 * C_KV_HEADS * HEAD_DIM
COL_GATE = COL_WIN + 2 * C_KV_HEADS * HEAD_DIM
IN_WIDTH = COL_GATE + 3 * C_HEADS
IN_PAD = 5248
Z_OFF = COL_B
Z_WIDTH = IN_PAD - Z_OFF

LANE = 128
SUBLANE = 8
VMEM_LIMIT = 56 * 1024 * 1024

TM_PROJ = 512
TM_XATTN = 1024
TM_MLP = 1024
TF_MLP = 512
NORM_ROWS = 256
ATT_TQ = 256
ATT_TK = 256
ATT_CH = 128
SEL_UNROLL = 4
CAST_CHUNKS = 128


def _cparams(sem):
    return pltpu.CompilerParams(dimension_semantics=sem, vmem_limit_bytes=VMEM_LIMIT)


def _rms_rows(x, g):
    ms = jnp.mean(x * x, axis=-1, keepdims=True)
    return x * lax.rsqrt(ms + EPS) * g


def _gelu_tanh(x):
    c = math.sqrt(2.0 / math.pi)
    return x * (0.5 * (1.0 + jnp.tanh(c * (x + 0.044715 * (x * x * x)))))


def _dot(a, b):
    return jnp.dot(a, b, preferred_element_type=F32)


def _dot_nt(a, b):
    return lax.dot_general(a, b, (((1,), (1,)), ((), ())), preferred_element_type=F32)


def _norm_into(h_ref, x_ref, g_ref, copy_ref=None):
    rows = x_ref.shape[0]

    def body(c, carry):
        r0 = pl.multiple_of(c * NORM_ROWS, NORM_ROWS)
        x = x_ref[pl.ds(r0, NORM_ROWS), :]
        h_ref[pl.ds(r0, NORM_ROWS), :] = _rms_rows(x, g_ref[...]).astype(h_ref.dtype)
        if copy_ref is not None:
            copy_ref[pl.ds(r0, NORM_ROWS), :] = x
        return carry

    lax.fori_loop(0, rows // NORM_ROWS, body, 0)


def _norm_matmul_kernel(x_ref, g_ref, w_ref, o_ref):
    half = x_ref.shape[0] // 2
    for c in range(2):
        r = slice(c * half, (c + 1) * half)
        hn = _rms_rows(x_ref[r, :], g_ref[...]).astype(BF16)
        o_ref[r, :] = _dot(hn, w_ref[...]).astype(o_ref.dtype)


def norm_matmul(x, g, w, layer, *, tm, out_dtype=F32):
    m, k = x.shape
    n = w.shape[2]
    return pl.pallas_call(
        _norm_matmul_kernel,
        out_shape=jax.ShapeDtypeStruct((m, n), out_dtype),
        grid=(m // tm,),
        in_specs=[
            pl.BlockSpec((tm, k), lambda i: (i, 0)),
            pl.BlockSpec((1, k), lambda i: (0, 0)),
            pl.BlockSpec((None, k, n), lambda i: (layer, 0, 0), pipeline_mode=pl.Buffered(1)),
        ],
        out_specs=pl.BlockSpec((tm, n), lambda i: (i, 0)),
        compiler_params=_cparams(("parallel",)),
        name="norm_matmul",
    )(x, g.reshape(1, k), w)


def _cast_specs(cast, cast_layer, step_of, n_steps):
    per_chunk = n_steps // CAST_CHUNKS
    in_specs, out_specs, out_shapes = [], [], []
    for w, width in cast:
        rows = w.shape[1] // CAST_CHUNKS
        in_specs.append(pl.BlockSpec((None, rows, w.shape[2]),
                                     lambda *ids: (cast_layer, step_of(*ids) // per_chunk, 0)))
        out_specs.append(pl.BlockSpec((rows, width), lambda *ids: (step_of(*ids) // per_chunk, 0)))
        out_shapes.append(jax.ShapeDtypeStruct((w.shape[1], width), BF16))
    return in_specs, out_specs, out_shapes


def _cast_slices(cast_in, cast_out):
    for src, dst in zip(cast_in, cast_out):
        width = src.shape[1]
        if dst.shape[1] > width:
            dst[...] = jnp.zeros(dst.shape, dst.dtype)
        dst[:, 0:width] = src[...].astype(dst.dtype)


def _mlp_kernel(*refs, norm_out, n_cast):
    x_ref, g_ref, wu_ref, wd_ref, gout_ref = refs[:5]
    cast_in = refs[5:5 + n_cast]
    o_ref = refs[5 + n_cast]
    cast_out = refs[6 + n_cast:6 + 2 * n_cast]
    h_ref = refs[6 + 2 * n_cast]
    f = pl.program_id(1)

    @pl.when(f == 0)
    def _():
        _norm_into(h_ref, x_ref, g_ref, copy_ref=o_ref)

    a = _dot(h_ref[...], wu_ref[...])
    a = jnp.square(jnp.maximum(a, 0.0)).astype(BF16)
    o_ref[...] += _dot(a, wd_ref[...])

    _cast_slices(cast_in, cast_out)

    if norm_out:
        @pl.when(f == pl.num_programs(1) - 1)
        def _():
            _norm_into(o_ref, o_ref, gout_ref)


def mlp(x, g, wu, wd, layer, g_out, cast=(), cast_layer=0, *, tm, tf, norm_out):
    m, d = x.shape
    ff = wu.shape[2]
    n_f = ff // tf
    cast_in, cast_out, cast_shapes = _cast_specs(cast, cast_layer, lambda i, f: i * n_f + f, (m // tm) * n_f)
    res = pl.pallas_call(
        functools.partial(_mlp_kernel, norm_out=norm_out, n_cast=len(cast)),
        out_shape=[jax.ShapeDtypeStruct((m, d), F32)] + cast_shapes,
        grid=(m // tm, n_f),
        in_specs=[
            pl.BlockSpec((tm, d), lambda i, f: (i, 0)),
            pl.BlockSpec((1, d), lambda i, f: (0, 0)),
            pl.BlockSpec((None, d, tf), lambda i, f: (layer, 0, f)),
            pl.BlockSpec((None, tf, d), lambda i, f: (layer, f, 0)),
            pl.BlockSpec((1, d), lambda i, f: (0, 0)),
        ] + cast_in,
        out_specs=[pl.BlockSpec((tm, d), lambda i, f: (i, 0))] + cast_out,
        scratch_shapes=[pltpu.VMEM((tm, d), BF16)],
        compiler_params=_cparams(("parallel", "arbitrary")),
        name="mlp",
    )(x, g.reshape(1, d), wu, wd, g_out.reshape(1, d), *[w for w, _ in cast])
    return res[0], res[1:]


def _in_proj_kernel(x_ref, g_ref, w_ref, lng_ref, lnb_ref, ws_ref, bst_ref, z_ref, oa_ref):
    row = lax.broadcasted_iota(jnp.int32, (CHUNK, CHUNK), 0)
    col = lax.broadcasted_iota(jnp.int32, (CHUNK, CHUNK), 1)
    causal = row >= col
    w_tril = [jnp.where(causal, ws_ref[gi], 0.0).astype(BF16) for gi in range(A_GROUPS)]

    def gate_inputs(za):
        u = _gelu_tanh(za[:, 0:A_WIDTH])
        v = _gelu_tanh(za[:, A_WIDTH:2 * A_WIDTH])
        vc = v - jnp.mean(v, axis=-1, keepdims=True)
        vn = vc * lax.rsqrt(jnp.mean(vc * vc, axis=-1, keepdims=True) + EPS)
        return u, (vn * lng_ref[...] + lnb_ref[...]).astype(BF16)

    def gate_store(r0, u, vn):
        for gi in range(A_GROUPS):
            cols = slice(gi * HEAD_DIM, (gi + 1) * HEAD_DIM)
            sv = _dot(w_tril[gi], vn[:, cols]) + bst_ref[:, gi:gi + 1]
            oa_ref[r0:r0 + CHUNK, cols] = (u[:, cols] * sv).astype(oa_ref.dtype)

    half = x_ref.shape[0] // 2
    pending = []
    for c in range(2):
        r = slice(c * half, (c + 1) * half)
        hn = _rms_rows(x_ref[r, :], g_ref[...]).astype(BF16)
        za = _dot(hn, w_ref[:, 0:Z_OFF])
        for r0, u, vn in pending:
            gate_store(r0, u, vn)
        z_ref[r, :] = _dot(hn, w_ref[:, Z_OFF:IN_PAD]).astype(z_ref.dtype)
        pending = [(c * half + k * CHUNK,) + gate_inputs(za[k * CHUNK:(k + 1) * CHUNK, :])
                   for k in range(half // CHUNK)]
    for r0, u, vn in pending:
        gate_store(r0, u, vn)


def in_proj(x, g, w, layer, ln_g, ln_b, w_s, b_s, *, tm):
    m, k = x.shape
    return pl.pallas_call(
        _in_proj_kernel,
        out_shape=[jax.ShapeDtypeStruct((m, Z_WIDTH), BF16), jax.ShapeDtypeStruct((m, A_WIDTH), BF16)],
        grid=(m // tm,),
        in_specs=[
            pl.BlockSpec((tm, k), lambda i: (i, 0)),
            pl.BlockSpec((1, k), lambda i: (0, 0)),
            pl.BlockSpec((None, k, IN_PAD), lambda i: (layer, 0, 0), pipeline_mode=pl.Buffered(1)),
            pl.BlockSpec((1, A_WIDTH), lambda i: (0, 0)),
            pl.BlockSpec((1, A_WIDTH), lambda i: (0, 0)),
            pl.BlockSpec((A_GROUPS, CHUNK, CHUNK), lambda i: (0, 0, 0)),
            pl.BlockSpec((CHUNK, A_GROUPS), lambda i: (0, 0)),
        ],
        out_specs=[pl.BlockSpec((tm, Z_WIDTH), lambda i: (i, 0)), pl.BlockSpec((tm, A_WIDTH), lambda i: (i, 0))],
        compiler_params=_cparams(("parallel",)),
        name="in_proj",
    )(x, g.reshape(1, k), w, ln_g.reshape(1, A_WIDTH), ln_b.reshape(1, A_WIDTH), w_s, b_s.T)


DIL_CH = 128
DIL_KW = 2 * DIL_CH
DIL_UNITS = 8


def _dilated_kernel(q_in, k_in, v_in, o_ref, q_ref, k_ref, v_ref, m_ref, acc_ref, l_ref):
    ch, kw = DIL_CH, DIL_KW
    seq = q_in.shape[0]
    for src, dst in ((q_in, q_ref), (k_in, k_ref), (v_in, v_ref)):
        def stage(c, carry, src=src, dst=dst):
            r0 = pl.multiple_of(c * NORM_ROWS, NORM_ROWS)
            dst[pl.ds(r0, NORM_ROWS), :] = src[pl.ds(r0, NORM_ROWS), :].astype(F32)
            return carry
        lax.fori_loop(0, seq // NORM_ROWS, stage, 0)
    qscale = (HEAD_DIM ** -0.5) * LOG2E
    row = lax.broadcasted_iota(jnp.int32, (ch, kw), 0)
    col = lax.broadcasted_iota(jnp.int32, (ch, kw), 1)
    ones = jnp.ones((kw, HEAD_DIM), BF16)
    patterns = sorted(DILATED_PAIRS, key=lambda wd: -wd[1])

    def rows(start, n, dil):
        return pl.ds(start, n) if dil == 1 else pl.ds(start, n, stride=dil)

    def band_bias(first, span):
        dist = row - col if first else ch + row - col
        return jnp.where((dist >= 0) & (dist <= span), 0.0, NEG_INF)

    def run_units(units, dil, span, phase):
        windows = {}
        scores = []
        for q0, k0, first, wid in units:
            if wid not in windows:
                k = k_ref[rows(k0, kw, dil), :].astype(BF16)
                v = v_ref[rows(k0, kw, dil), :].astype(BF16)
                windows[wid] = (k, jnp.concatenate([v, ones], axis=1))
            q = (q_ref[rows(q0, ch, dil), :] * qscale).astype(BF16)
            scores.append(_dot_nt(q, windows[wid][0]) + band_bias(first, span))
        for (q0, k0, first, wid), s in zip(units, scores):
            qrows = rows(q0, ch, dil)
            m_cur = jnp.max(s, axis=-1, keepdims=True)
            if phase == "first":
                m_new = jnp.broadcast_to(m_cur, (ch, LANE))
            else:
                m_old = m_ref[qrows, :]
                m_new = jnp.maximum(m_old, m_cur)
                alpha = jnp.exp2(m_old - m_new)
            p = jnp.exp2((s - jnp.concatenate([m_new] * (kw // LANE), axis=1)).astype(BF16))
            pv = _dot(p, windows[wid][1])
            acc, l = pv[:, 0:HEAD_DIM], pv[:, HEAD_DIM:2 * HEAD_DIM]
            if phase != "first":
                acc = alpha * acc_ref[qrows, :] + acc
                l = alpha * l_ref[qrows, :] + l
            if phase == "last":
                o_ref[pl.ds(pl.multiple_of(q0, ch), ch), :] = (acc / l).astype(o_ref.dtype)
            else:
                m_ref[qrows, :] = m_new
                acc_ref[qrows, :] = acc
                l_ref[qrows, :] = l

    for idx, (window, dil) in enumerate(patterns):
        phase = "first" if idx == 0 else ("last" if idx == len(patterns) - 1 else "mid")
        assert phase != "last" or dil == 1
        span = window // dil
        assert span <= ch
        n_units = seq // dil // ch
        per_class = min(n_units, max(DIL_UNITS // dil, 2))
        n_classes = DIL_UNITS // per_class
        assert n_units % per_class == 0 and dil % n_classes == 0

        def group(r0, j0, head, dil=dil, span=span, phase=phase, per_class=per_class, n_classes=n_classes):
            units = []
            for rc in range(n_classes):
                r = r0 + rc
                for u in range(per_class):
                    first = head and u == 0
                    q0 = r + dil * ch * (j0 + u)
                    k0 = r if (head and u <= 1) else r + dil * ch * (j0 + u - 1)
                    wid = (rc, 0) if (head and u <= 1) else (rc, u)
                    units.append((q0, k0, first, wid))
            run_units(units, dil, span, phase)

        def class_loop(g, carry, group=group, per_class=per_class, n_classes=n_classes, n_units=n_units):
            r0 = g * n_classes
            group(r0, 0, True)
            if n_units > per_class:
                def tail(jb, c):
                    group(r0, jb * per_class, False)
                    return c
                lax.fori_loop(1, n_units // per_class, tail, 0)
            return carry

        if dil // n_classes == 1:
            class_loop(0, 0)
        else:
            lax.fori_loop(0, dil // n_classes, class_loop, 0)


def dilated_attention(z, *, n_batch, seq):
    bw = B_HEADS * HEAD_DIM
    base = (COL_B - Z_OFF) // HEAD_DIM
    blk = (seq, HEAD_DIM)
    return pl.pallas_call(
        _dilated_kernel,
        out_shape=jax.ShapeDtypeStruct((n_batch * seq, bw), BF16),
        grid=(n_batch, B_HEADS),
        in_specs=[
            pl.BlockSpec(blk, lambda b, h: (b, base + h)),
            pl.BlockSpec(blk, lambda b, h: (b, base + B_HEADS + h)),
            pl.BlockSpec(blk, lambda b, h: (b, base + 2 * B_HEADS + h)),
        ],
        out_specs=pl.BlockSpec(blk, lambda b, h: (b, h)),
        scratch_shapes=[pltpu.VMEM(blk, F32)] * 6,
        compiler_params=_cparams(("parallel", "parallel")),
        name="dilated_attention",
    )(z, z, z)


N_SUB = 256
N_CMP = 2 * C_KV_HEADS


def _compress_kernel(x_in, pos_ref, w1_ref, w2_ref, o_ref, x_ref):
    x_ref[...] = x_in[...].astype(F32)
    p = jnp.zeros((N_SUB, HEAD_DIM), F32)
    q = jnp.zeros((N_SUB, HEAD_DIM), F32)
    for i in range(CMP_STRIDE):
        a = x_ref[pl.ds(i, N_SUB, stride=CMP_STRIDE), :]
        top = (a + pos_ref[i:i + 1, :]).astype(BF16)
        bot = (a + pos_ref[CMP_STRIDE + i:CMP_STRIDE + i + 1, :]).astype(BF16)
        p = p + _dot(top, w1_ref[i * HEAD_DIM:(i + 1) * HEAD_DIM, :])
        q = q + _dot(bot, w1_ref[(CMP_STRIDE + i) * HEAD_DIM:(CMP_STRIDE + i + 1) * HEAD_DIM, :])
    h = _gelu_tanh(p + pltpu.roll(q, N_SUB - 1, 0))
    o_ref[...] = _dot(h.astype(BF16), w2_ref[...]).astype(o_ref.dtype)


def compress(z, pos, w1, w2, layer, *, n_batch, seq):
    return pl.pallas_call(
        _compress_kernel,
        out_shape=jax.ShapeDtypeStruct((n_batch * N_CMP * N_SUB, HEAD_DIM), BF16),
        grid=(n_batch, N_CMP),
        in_specs=[
            pl.BlockSpec((seq, HEAD_DIM), lambda b, n: (b, (COL_CMP - Z_OFF) // HEAD_DIM + n)),
            pl.BlockSpec((CMP_LEN, HEAD_DIM), lambda b, n: (0, 0)),
            pl.BlockSpec((None, None, CMP_LEN * HEAD_DIM, HEAD_DIM), lambda b, n: (layer, n // C_KV_HEADS, 0, 0)),
            pl.BlockSpec((None, None, HEAD_DIM, HEAD_DIM), lambda b, n: (layer, n // C_KV_HEADS, 0, 0)),
        ],
        out_specs=pl.BlockSpec((N_SUB, HEAD_DIM), lambda b, n: (b * N_CMP + n, 0)),
        scratch_shapes=[pltpu.VMEM((seq, HEAD_DIM), F32)],
        compiler_params=_cparams(("parallel", "parallel")),
        name="nsa_compress",
    )(z, pos, w1, w2)


N_SEL = 64


def _nsa_kernel(*refs, n_cast):
    q_ref, kc_ref, vc_ref, ovt_ref, ks_ref, vs_ref, kw_ref, vw_ref, gate_ref = refs[:9]
    cast_in = refs[9:9 + n_cast]
    o_ref = refs[9 + n_cast]
    cast_out = refs[10 + n_cast:10 + 2 * n_cast]
    (qs_ref, kbs_ref, vbs_ref, kbw_ref, vbw_ref, ms_ref, as_ref, mw_ref, aw_ref, oc_ref,
     sc_ref) = refs[10 + 2 * n_cast:]
    _cast_slices(cast_in, cast_out)
    tq, tk, ch = ATT_TQ, ATT_TK, ATT_CH
    n_rep = C_GROUP
    n_sub = tq // ch
    n_ch = n_rep * n_sub
    seq = ks_ref.shape[0]
    i = pl.program_id(2)
    q0 = i * tq

    @pl.when(i == 0)
    def _():
        def prep(c, carry):
            rows = pl.ds(pl.multiple_of(c * tk, tk), tk)
            blk = (c * tk + lax.broadcasted_iota(jnp.int32, (tk, LANE), 0)) >> SEL_SHIFT
            lane = lax.broadcasted_iota(jnp.int32, (tk, LANE), 1)
            ones = jnp.ones((tk, HEAD_DIM), BF16)
            kbs_ref[rows, 0:HEAD_DIM] = ks_ref[rows, :]
            kbs_ref[rows, HEAD_DIM:HEAD_DIM + LANE] = jnp.where(blk == lane, 1.0, 0.0).astype(BF16)
            vbs_ref[rows, 0:HEAD_DIM] = vs_ref[rows, :]
            vbs_ref[rows, HEAD_DIM:2 * HEAD_DIM] = ones
            kbw_ref[rows, :] = kw_ref[rows, :]
            vbw_ref[rows, 0:HEAD_DIM] = vw_ref[rows, :]
            vbw_ref[rows, HEAD_DIM:2 * HEAD_DIM] = ones
            return carry

        lax.fori_loop(0, seq // tk, prep, 0)

    qscale = (HEAD_DIM ** -0.5) * LOG2E
    for r in range(n_rep):
        qs_ref[r * tq:(r + 1) * tq, 0:HEAD_DIM] = (q_ref[:, r * HEAD_DIM:(r + 1) * HEAD_DIM].astype(F32)
                                                   * qscale).astype(BF16)

    for m_ref, accl_ref in ((ms_ref, as_ref), (mw_ref, aw_ref)):
        m_ref[...] = jnp.full(m_ref.shape, NEG_INF, F32)
        accl_ref[...] = jnp.zeros(accl_ref.shape, F32)

    q_idx = lax.broadcasted_iota(jnp.int32, (tq, tk), 0)
    k_idx = lax.broadcasted_iota(jnp.int32, (tq, tk), 1)

    def tile_bias(off, max_dist):
        dist = off * tk + q_idx - k_idx
        if off == 0:
            ok = dist >= 0
        elif max_dist is not None and (off + 1) * tk - 1 > max_dist:
            ok = dist <= max_dist
        else:
            return None
        return jnp.where(ok, 0.0, NEG_INF)

    def issue(tiles, load_q, kb_ref):
        scores = []
        for kt, tbias in tiles:
            k0 = pl.multiple_of(kt * tk, tk)
            scores.append((_dot_nt(load_q(), kb_ref[pl.ds(k0, tk), :]), k0, tbias))
        return scores

    def finish(scores, vb_ref, m_ref, accl_ref):
        for s_tile, k0, tbias in scores:
            pts, alphas = [], []
            for c in range(n_ch):
                rows = slice(c * ch, (c + 1) * ch)
                s = s_tile[rows, :]
                if tbias is not None:
                    b0 = (c % n_sub) * ch
                    s = s + tbias[b0:b0 + ch, :]
                m_old = m_ref[rows, :]
                m_new = jnp.maximum(m_old, jnp.max(s, axis=-1, keepdims=True))
                alphas.append(jnp.exp2(m_old - m_new))
                pts.append(jnp.exp2((s - jnp.concatenate([m_new] * (tk // LANE), axis=1)).astype(BF16)))
                m_ref[rows, :] = m_new
            pv = _dot(jnp.concatenate(pts, axis=0), vb_ref[pl.ds(k0, tk), :])
            for c in range(n_ch):
                rows = slice(c * ch, (c + 1) * ch)
                alpha2 = jnp.concatenate([alphas[c], alphas[c]], axis=1)
                accl_ref[rows, :] = alpha2 * accl_ref[rows, :] + pv[rows, :]

    max_dist = WIN_LEN - 1
    win_tiles = []
    for off in range(-(-max_dist // tk), -1, -1):
        tbias = tile_bias(off, max_dist)
        if off > 0:
            before_start = jnp.where(i >= off, 0.0, NEG_INF)
            if tbias is None:
                tbias = jnp.where(q_idx + k_idx >= 0, 0.0, NEG_INF)
            tbias = tbias + before_start
        win_tiles.append((jnp.maximum(i - off, 0), tbias))
    win_scores = issue(win_tiles, lambda: qs_ref[:, 0:HEAD_DIM], kbw_ref)

    s_all = _dot_nt(qs_ref[:, 0:HEAD_DIM], kc_ref[...])
    n_cols = lax.broadcasted_iota(jnp.int32, (ch, N_SUB), 1)
    bias, has_any = [], []
    for c in range(n_sub):
        t = q0 + c * ch + lax.broadcasted_iota(jnp.int32, (ch, N_SUB), 0)
        bias.append(jnp.where((n_cols * CMP_STRIDE + (CMP_LEN - 1)) <= t, 0.0, NEG_INF))
        has_any.append(t[:, 0:1] >= CMP_LEN - 1)
    ps = []
    psum = [jnp.zeros((ch, N_SUB), F32) for _ in range(n_sub)]
    for c in range(n_ch):
        s = s_all[c * ch:(c + 1) * ch, :] + bias[c % n_sub]
        e = jnp.exp2(s - jnp.max(s, axis=-1, keepdims=True))
        den = jnp.sum(e, axis=-1, keepdims=True)
        p = e * jnp.where(has_any[c % n_sub], 1.0 / den, 0.0)
        ps.append(p.astype(BF16))
        psum[c % n_sub] = psum[c % n_sub] + p
    oc_ref[...] = _dot(jnp.concatenate(ps, axis=0), vc_ref[...])
    psum = jnp.concatenate(psum, axis=0)

    hi = psum.astype(BF16)
    lo = (psum - hi.astype(F32)).astype(BF16)
    ovt = ovt_ref[...]
    imp_t = _dot_nt(ovt, hi) + _dot_nt(ovt, lo)
    jrow = lax.broadcasted_iota(jnp.int32, (LANE, tq), 0)
    tl = q0 + lax.broadcasted_iota(jnp.int32, (LANE, tq), 1)
    jt = tl >> SEL_SHIFT
    forced = (jrow == 0) | (jrow == jt) | (jrow == jt - 1)
    valid_s = jrow * SEL_LEN <= tl
    sc_ref[...] = jnp.where(forced, 1e4, jnp.where(valid_s, imp_t, -1.0))
    finish(win_scores, vbw_ref, mw_ref, aw_ref)
    n_grp = N_SEL // SUBLANE
    grp = [sc_ref[g * SUBLANE:(g + 1) * SUBLANE, :] for g in range(n_grp)]
    cnt = [jnp.zeros((SUBLANE, tq), F32) for _ in range(n_grp)]
    sub = lax.broadcasted_iota(jnp.int32, (SUBLANE, tq), 0)
    for j in range(N_SEL):
        row = sc_ref[j:j + 1, :]
        for g in range(n_grp):
            if g < j // SUBLANE:
                beats = row > grp[g]
            elif g > j // SUBLANE:
                beats = row >= grp[g]
            else:
                beats = (row > grp[g]) | ((row == grp[g]) & (sub > j % SUBLANE))
            cnt[g] = cnt[g] + jnp.where(beats, 1.0, 0.0)
    pen_t = [jnp.where(c < float(SEL_TOP), 0.0, NEG_INF) for c in cnt]
    pen_t = jnp.concatenate(pen_t + [jnp.full((LANE - N_SEL, tq), NEG_INF, F32)], axis=0)
    pen = pen_t.T.astype(BF16)
    for r in range(n_rep):
        qs_ref[r * tq:(r + 1) * tq, HEAD_DIM:HEAD_DIM + LANE] = pen

    def sel_steps(tiles):
        finish(issue(tiles, lambda: qs_ref[...], kbs_ref), vbs_ref, ms_ref, as_ref)

    def sel_body(j, carry):
        sel_steps([(SEL_UNROLL * j + u, None) for u in range(SEL_UNROLL)])
        return carry

    lax.fori_loop(0, i // SEL_UNROLL, sel_body, 0)

    for rem in range(SEL_UNROLL):
        @pl.when(i % SEL_UNROLL == rem)
        def _(rem=rem):
            sel_steps([(i - rem + u, None) for u in range(rem)] + [(i, tile_bias(0, None))])

    g = 1.0 / (1.0 + jnp.exp(-gate_ref[...].astype(F32)))
    lane = lax.broadcasted_iota(jnp.int32, g.shape, 1)
    head0 = pl.program_id(1) * n_rep

    def gate(head, branch):
        return jnp.sum(jnp.where(lane == head * 3 + branch, g, 0.0), axis=-1, keepdims=True)

    for r in range(n_rep):
        rows = slice(r * tq, (r + 1) * tq)
        o = (gate(head0 + r, 0) * oc_ref[rows, :]
             + gate(head0 + r, 1) * (as_ref[rows, 0:HEAD_DIM] / as_ref[rows, HEAD_DIM:2 * HEAD_DIM])
             + gate(head0 + r, 2) * (aw_ref[rows, 0:HEAD_DIM] / aw_ref[rows, HEAD_DIM:2 * HEAD_DIM]))
        o_ref[:, r * HEAD_DIM:(r + 1) * HEAD_DIM] = o.astype(o_ref.dtype)


def nsa_attention(z, kvc, ovt, cast=(), cast_layer=0, *, n_batch, seq):
    tq = ATT_TQ
    nq = seq // tq
    n_rep = C_GROUP
    qw = n_rep * HEAD_DIM
    kv_blk = (seq, HEAD_DIM)
    sel0, win0 = (COL_SEL - Z_OFF) // HEAD_DIM, (COL_WIN - Z_OFF) // HEAD_DIM
    cast_in, cast_out, cast_shapes = _cast_specs(
        cast, cast_layer, lambda b, g, i: (b * C_KV_HEADS + g) * nq + i, n_batch * C_KV_HEADS * nq)
    res = pl.pallas_call(
        functools.partial(_nsa_kernel, n_cast=len(cast)),
        out_shape=[jax.ShapeDtypeStruct((n_batch * seq, C_HEADS * HEAD_DIM), BF16)] + cast_shapes,
        grid=(n_batch, C_KV_HEADS, nq),
        in_specs=[
            pl.BlockSpec((tq, qw), lambda b, g, i: (b * nq + i, (COL_CQ - Z_OFF) // qw + g)),
            pl.BlockSpec((N_SUB, HEAD_DIM), lambda b, g, i: (b * N_CMP + g, 0)),
            pl.BlockSpec((N_SUB, HEAD_DIM), lambda b, g, i: (b * N_CMP + C_KV_HEADS + g, 0)),
            pl.BlockSpec((LANE, N_SUB), lambda b, g, i: (0, 0)),
            pl.BlockSpec(kv_blk, lambda b, g, i: (b, sel0 + g)),
            pl.BlockSpec(kv_blk, lambda b, g, i: (b, sel0 + C_KV_HEADS + g)),
            pl.BlockSpec(kv_blk, lambda b, g, i: (b, win0 + g)),
            pl.BlockSpec(kv_blk, lambda b, g, i: (b, win0 + C_KV_HEADS + g)),
            pl.BlockSpec((tq, LANE), lambda b, g, i: (b * nq + i, (COL_GATE - Z_OFF) // LANE)),
        ] + cast_in,
        out_specs=[pl.BlockSpec((tq, qw), lambda b, g, i: (b * nq + i, g))] + cast_out,
        scratch_shapes=[
            pltpu.VMEM((n_rep * tq, HEAD_DIM + LANE), BF16),
            pltpu.VMEM((seq, HEAD_DIM + LANE), BF16),
            pltpu.VMEM((seq, 2 * HEAD_DIM), BF16),
            pltpu.VMEM((seq, HEAD_DIM), BF16),
            pltpu.VMEM((seq, 2 * HEAD_DIM), BF16),
            pltpu.VMEM((n_rep * tq, LANE), F32),
            pltpu.VMEM((n_rep * tq, 2 * HEAD_DIM), F32),
            pltpu.VMEM((n_rep * tq, LANE), F32),
            pltpu.VMEM((n_rep * tq, 2 * HEAD_DIM), F32),
            pltpu.VMEM((n_rep * tq, HEAD_DIM), F32),
            pltpu.VMEM((LANE, tq), F32),
        ],
        compiler_params=_cparams(("parallel", "parallel", "arbitrary")),
        name="nsa_attention",
    )(z, kvc, kvc, ovt, z, z, z, z, z, *[w for w, _ in cast])
    return res[0], res[1:]


def _overlap_t():
    n_c = N_SUB - 1
    c_start = np.arange(n_c) * CMP_STRIDE
    s_start = np.arange(N_SEL) * SEL_LEN
    ov = ((c_start[:, None] <= s_start[None, :] + SEL_LEN - 1)
          & (c_start[:, None] + CMP_LEN - 1 >= s_start[None, :])).astype(np.float32)
    out = np.zeros((LANE, N_SUB), np.float32)
    out[:N_SEL, :n_c] = ov.T
    return out


def _mix_out_kernel(x_ref, oa_ref, ob_ref, oc_ref, w_ref, o_ref):
    mix = jnp.concatenate([oa_ref[...], ob_ref[...], oc_ref[...]], axis=1)
    o_ref[...] = x_ref[...] + _dot(mix, w_ref[...])


def mix_out(x, out_a, out_b, out_c, w, layer, *, tm):
    m, n = x.shape
    return pl.pallas_call(
        _mix_out_kernel,
        out_shape=jax.ShapeDtypeStruct((m, n), F32),
        grid=(m // tm,),
        in_specs=[
            pl.BlockSpec((tm, n), lambda i: (i, 0)),
            pl.BlockSpec((tm, out_a.shape[1]), lambda i: (i, 0)),
            pl.BlockSpec((tm, out_b.shape[1]), lambda i: (i, 0)),
            pl.BlockSpec((tm, out_c.shape[1]), lambda i: (i, 0)),
            pl.BlockSpec((None,) + w.shape[1:], lambda i: (layer, 0, 0), pipeline_mode=pl.Buffered(1)),
        ],
        out_specs=pl.BlockSpec((tm, n), lambda i: (i, 0)),
        compiler_params=_cparams(("parallel",)),
        name="mix_out",
    )(x, out_a, out_b, out_c, w)


def _xattn_kernel(x_ref, g_ref, wq_ref, k_ref, v_ref, wo_ref, o_ref):
    n_mem = k_ref.shape[0]
    rows = x_ref.shape[0]
    ones = jnp.ones((n_mem, HEAD_DIM), BF16)
    kv = [(k_ref[:, h * HEAD_DIM:(h + 1) * HEAD_DIM].astype(BF16),
           jnp.concatenate([v_ref[:, h * HEAD_DIM:(h + 1) * HEAD_DIM].astype(BF16), ones], axis=1))
          for h in range(X_HEADS)]
    for c in range(rows // NORM_ROWS):
        r = slice(c * NORM_ROWS, (c + 1) * NORM_ROWS)
        x = x_ref[r, :]
        hn = _rms_rows(x, g_ref[...]).astype(BF16)
        q = _dot(hn, wq_ref[...]) * ((HEAD_DIM ** -0.5) * LOG2E)
        outs = []
        for h in range(X_HEADS):
            s = _dot_nt(q[:, h * HEAD_DIM:(h + 1) * HEAD_DIM].astype(BF16), kv[h][0])
            e = jnp.exp2((s - jnp.max(s, axis=-1, keepdims=True)).astype(BF16))
            pv = _dot(e, kv[h][1])
            outs.append((pv[:, 0:HEAD_DIM] / pv[:, HEAD_DIM:2 * HEAD_DIM]).astype(BF16))
        o_ref[r, :] = x + _dot(jnp.concatenate(outs, axis=1), wo_ref[...])


def cross_attention_block(x, g, wq, kv, wo, layer, *, n_batch, seq, n_mem, tm):
    nq = seq // tm
    d = x.shape[1]
    return pl.pallas_call(
        _xattn_kernel,
        out_shape=jax.ShapeDtypeStruct(x.shape, F32),
        grid=(n_batch, nq),
        in_specs=[
            pl.BlockSpec((tm, d), lambda b, i: (b * nq + i, 0)),
            pl.BlockSpec((1, d), lambda b, i: (0, 0)),
            pl.BlockSpec((None, d, X_WIDTH), lambda b, i: (layer, 0, 0), pipeline_mode=pl.Buffered(1)),
            pl.BlockSpec((n_mem, X_WIDTH), lambda b, i: (b, 0)),
            pl.BlockSpec((n_mem, X_WIDTH), lambda b, i: (b, 1)),
            pl.BlockSpec((None, X_WIDTH, d), lambda b, i: (layer, 0, 0), pipeline_mode=pl.Buffered(1)),
        ],
        out_specs=pl.BlockSpec((tm, d), lambda b, i: (b * nq + i, 0)),
        compiler_params=_cparams(("parallel", "parallel")),
        name="cross_attention",
    )(x, g.reshape(1, d), wq, kv, kv, wo)


def kernel(x, mem, norm_mix, w_in, gmlp_ln_g, gmlp_ln_b, gmlp_w_s, gmlp_b_s, cmp_pos, cmp_k_w1, cmp_k_w2,
           cmp_v_w1, cmp_v_w2, w_out, norm_xattn, norm_mem, xattn_wq, xattn_wkv, xattn_wo, norm_mlp, w_up,
           w_down, final_norm):
    bsz, seq, d = x.shape
    n_mem = mem.shape[1]
    t = bsz * seq
    assert d == D_MODEL and seq // CMP_STRIDE == N_SUB and seq // SEL_LEN == N_SEL
    assert seq % (DIL_KW * max(dil for _, dil in DILATED_PAIRS)) == 0 and seq % ATT_TQ == 0

    w_in_l = jnp.pad(w_in[0], ((0, 0), (0, IN_PAD - IN_WIDTH))).astype(BF16)[None]
    big = ((w_out, w_out.shape[2]), (w_up, w_up.shape[2]), (w_down, w_down.shape[2]))
    wq_b = xattn_wq.astype(BF16)
    wkv_b = xattn_wkv.astype(BF16)
    wo_b = xattn_wo.astype(BF16)
    cmp_w1 = jnp.stack([cmp_k_w1, cmp_v_w1], axis=1).astype(BF16)
    cmp_w2 = jnp.stack([cmp_k_w2, cmp_v_w2], axis=1).astype(BF16)
    ovt = jnp.asarray(_overlap_t()).astype(BF16)
    xf = x.reshape(t, d)
    memf = mem.reshape(bsz * n_mem, d)

    for l in range(DEPTH):
        z, out_a = in_proj(xf, norm_mix[l], w_in_l, 0, gmlp_ln_g[l], gmlp_ln_b[l], gmlp_w_s[l], gmlp_b_s[l],
                           tm=TM_PROJ)

        out_b = dilated_attention(z, n_batch=bsz, seq=seq)

        kvc = compress(z, cmp_pos[l], cmp_w1, cmp_w2, l, n_batch=bsz, seq=seq)
        out_c, cast0 = nsa_attention(z, kvc, ovt, big if l == 0 else (), 0, n_batch=bsz, seq=seq)
        if l == 0:
            w_out_l, w_up_l, w_down_l = (w[None] for w in cast0)

        xf = mix_out(xf, out_a, out_b, out_c, w_out_l, 0, tm=TM_PROJ)

        kv = norm_matmul(memf, norm_mem[l], wkv_b, l, tm=TM_PROJ)
        xf = cross_attention_block(xf, norm_xattn[l], wq_b, kv, wo_b, l, n_batch=bsz, seq=seq, n_mem=n_mem,
                                   tm=TM_XATTN)

        nxt = () if l == DEPTH - 1 else ((w_in, IN_PAD),) + big
        xf, nxt_b = mlp(xf, norm_mlp[l], w_up_l, w_down_l, 0, final_norm, nxt, l + 1, tm=TM_MLP, tf=TF_MLP,
                        norm_out=(l == DEPTH - 1))
        if nxt_b:
            w_in_l, w_out_l, w_up_l, w_down_l = (w[None] for w in nxt_b)

    return xf.reshape(bsz, seq, d)
```

```python
import functools
import math

import numpy as np
import jax
import jax.numpy as jnp
from jax import lax
from jax.experimental import pallas as pl
from jax.experimental.pallas import tpu as pltpu

F32 = jnp.float32
BF16 = jnp.bfloat16

D_MODEL = 2048
DEPTH = 4
HEAD_DIM = 128
A_GROUPS = 4
A_WIDTH = A_GROUPS * HEAD_DIM
CHUNK = 128
B_HEADS = 4
DILATED_PAIRS = ((128, 1), (512, 4), (2048, 16))
C_HEADS = 8
C_KV_HEADS = 2
C_GROUP = C_HEADS // C_KV_HEADS
CMP_LEN = 32
CMP_STRIDE = 16
SEL_LEN = 64
SEL_SHIFT = SEL_LEN.bit_length() - 1
SEL_TOP = 16
WIN_LEN = 512
X_HEADS = 4
X_WIDTH = X_HEADS * HEAD_DIM
EPS = 1e-6
NEG_INF = -1e30
LOG2E = math.log2(math.e)

COL_B = 2 * A_WIDTH
COL_CQ = COL_B + 3 * B_HEADS * HEAD_DIM
COL_CMP = COL_CQ + C_HEADS * HEAD_DIM
COL_SEL = COL_CMP + 2 * C_KV_HEADS * HEAD_DIM
COL_WIN = COL_SEL + 2 * C_KV_HEADS * HEAD_DIM
COL_GATE = COL_WIN + 2 * C_KV_HEADS * HEAD_DIM
IN_WIDTH = COL_GATE + 3 * C_HEADS
IN_PAD = 5248
Z_OFF = COL_B
Z_WIDTH = IN_PAD - Z_OFF

LANE = 128
SUBLANE = 8
VMEM_LIMIT = 56 * 1024 * 1024

TM_PROJ = 512
TM_XATTN = 1024
TM_MLP = 1024
TF_MLP = 512
NORM_ROWS = 256
ATT_TQ = 256
ATT_TK = 256
ATT_CH = 128
SEL_UNROLL = 4
CAST_CHUNKS = 128


def _cparams(sem):
    return pltpu.CompilerParams(dimension_semantics=sem, vmem_limit_bytes=VMEM_LIMIT)


def _rms_rows(x, g):
    ms = jnp.mean(x * x, axis=-1, keepdims=True)
    return x * lax.rsqrt(ms + EPS) * g


def _gelu_tanh(x):
    c = math.sqrt(2.0 / math.pi)
    return x * (0.5 * (1.0 + jnp.tanh(c * (x + 0.044715 * (x * x * x)))))


def _dot(a, b):
    return jnp.dot(a, b, preferred_element_type=F32)


def _dot_nt(a, b):
    return lax.dot_general(a, b, (((1,), (1,)), ((), ())), preferred_element_type=F32)


def _norm_into(h_ref, x_ref, g_ref, copy_ref=None):
    rows = x_ref.shape[0]

    def body(c, carry):
        r0 = pl.multiple_of(c * NORM_ROWS, NORM_ROWS)
        x = x_ref[pl.ds(r0, NORM_ROWS), :]
        h_ref[pl.ds(r0, NORM_ROWS), :] = _rms_rows(x, g_ref[...]).astype(h_ref.dtype)
        if copy_ref is not None:
            copy_ref[pl.ds(r0, NORM_ROWS), :] = x
        return carry

    lax.fori_loop(0, rows // NORM_ROWS, body, 0)


def _norm_matmul_kernel(x_ref, g_ref, w_ref, o_ref):
    half = x_ref.shape[0] // 2
    for c in range(2):
        r = slice(c * half, (c + 1) * half)
        hn = _rms_rows(x_ref[r, :], g_ref[...]).astype(BF16)
        o_ref[r, :] = _dot(hn, w_ref[...]).astype(o_ref.dtype)


def norm_matmul(x, g, w, layer, *, tm, out_dtype=F32):
    m, k = x.shape
    n = w.shape[2]
    return pl.pallas_call(
        _norm_matmul_kernel,
        out_shape=jax.ShapeDtypeStruct((m, n), out_dtype),
        grid=(m // tm,),
        in_specs=[
            pl.BlockSpec((tm, k), lambda i: (i, 0)),
            pl.BlockSpec((1, k), lambda i: (0, 0)),
            pl.BlockSpec((None, k, n), lambda i: (layer, 0, 0), pipeline_mode=pl.Buffered(1)),
        ],
        out_specs=pl.BlockSpec((tm, n), lambda i: (i, 0)),
        compiler_params=_cparams(("parallel",)),
        name="norm_matmul",
    )(x, g.reshape(1, k), w)


def _cast_specs(cast, cast_layer, step_of, n_steps):
    per_chunk = n_steps // CAST_CHUNKS
    in_specs, out_specs, out_shapes = [], [], []
    for w in cast:
        rows = w.shape[1] // CAST_CHUNKS
        in_specs.append(pl.BlockSpec((None, rows, w.shape[2]),
                                     lambda *ids: (cast_layer, step_of(*ids) // per_chunk, 0)))
        out_specs.append(pl.BlockSpec((rows, w.shape[2]), lambda *ids: (step_of(*ids) // per_chunk, 0)))
        out_shapes.append(jax.ShapeDtypeStruct(w.shape[1:], BF16))
    return in_specs, out_specs, out_shapes


def _cast_slices(cast_in, cast_out):
    for src, dst in zip(cast_in, cast_out):
        dst[...] = src[...].astype(dst.dtype)


def _mlp_kernel(*refs, norm_out, n_cast):
    x_ref, g_ref, wu_ref, wd_ref, gout_ref = refs[:5]
    cast_in = refs[5:5 + n_cast]
    o_ref = refs[5 + n_cast]
    cast_out = refs[6 + n_cast:6 + 2 * n_cast]
    h_ref = refs[6 + 2 * n_cast]
    f = pl.program_id(1)

    @pl.when(f == 0)
    def _():
        _norm_into(h_ref, x_ref, g_ref, copy_ref=o_ref)

    a = _dot(h_ref[...], wu_ref[...])
    a = jnp.square(jnp.maximum(a, 0.0)).astype(BF16)
    o_ref[...] += _dot(a, wd_ref[...])

    _cast_slices(cast_in, cast_out)

    if norm_out:
        @pl.when(f == pl.num_programs(1) - 1)
        def _():
            _norm_into(o_ref, o_ref, gout_ref)


def mlp(x, g, wu, wd, layer, g_out, cast=(), cast_layer=0, *, tm, tf, norm_out):
    m, d = x.shape
    ff = wu.shape[2]
    n_f = ff // tf
    cast_in, cast_out, cast_shapes = _cast_specs(cast, cast_layer, lambda i, f: i * n_f + f, (m // tm) * n_f)
    res = pl.pallas_call(
        functools.partial(_mlp_kernel, norm_out=norm_out, n_cast=len(cast)),
        out_shape=[jax.ShapeDtypeStruct((m, d), F32)] + cast_shapes,
        grid=(m // tm, n_f),
        in_specs=[
            pl.BlockSpec((tm, d), lambda i, f: (i, 0)),
            pl.BlockSpec((1, d), lambda i, f: (0, 0)),
            pl.BlockSpec((None, d, tf), lambda i, f: (layer, 0, f)),
            pl.BlockSpec((None, tf, d), lambda i, f: (layer, f, 0)),
            pl.BlockSpec((1, d), lambda i, f: (0, 0)),
        ] + cast_in,
        out_specs=[pl.BlockSpec((tm, d), lambda i, f: (i, 0))] + cast_out,
        scratch_shapes=[pltpu.VMEM((tm, d), BF16)],
        compiler_params=_cparams(("parallel", "arbitrary")),
        name="mlp",
    )(x, g.reshape(1, d), wu, wd, g_out.reshape(1, d), *cast)
    return res[0], res[1:]


def _in_proj_kernel(x_ref, g_ref, w_ref, lng_ref, lnb_ref, ws_ref, bst_ref, z_ref, oa_ref):
    row = lax.broadcasted_iota(jnp.int32, (CHUNK, CHUNK), 0)
    col = lax.broadcasted_iota(jnp.int32, (CHUNK, CHUNK), 1)
    causal = row >= col
    w_tril = [jnp.where(causal, ws_ref[gi], 0.0).astype(BF16) for gi in range(A_GROUPS)]

    def gate_inputs(za):
        u = _gelu_tanh(za[:, 0:A_WIDTH])
        v = _gelu_tanh(za[:, A_WIDTH:2 * A_WIDTH])
        vc = v - jnp.mean(v, axis=-1, keepdims=True)
        vn = vc * lax.rsqrt(jnp.mean(vc * vc, axis=-1, keepdims=True) + EPS)
        return u, (vn * lng_ref[...] + lnb_ref[...]).astype(BF16)

    def gate_store(r0, u, vn):
        for gi in range(A_GROUPS):
            cols = slice(gi * HEAD_DIM, (gi + 1) * HEAD_DIM)
            sv = _dot(w_tril[gi], vn[:, cols]) + bst_ref[:, gi:gi + 1]
            oa_ref[r0:r0 + CHUNK, cols] = (u[:, cols] * sv).astype(oa_ref.dtype)

    half = x_ref.shape[0] // 2
    pending = []
    for c in range(2):
        r = slice(c * half, (c + 1) * half)
        hn = _rms_rows(x_ref[r, :], g_ref[...]).astype(BF16)
        za = _dot(hn, w_ref[:, 0:Z_OFF])
        for r0, u, vn in pending:
            gate_store(r0, u, vn)
        z_ref[r, :] = _dot(hn, w_ref[:, Z_OFF:IN_PAD]).astype(z_ref.dtype)
        pending = [(c * half + k * CHUNK,) + gate_inputs(za[k * CHUNK:(k + 1) * CHUNK, :])
                   for k in range(half // CHUNK)]
    for r0, u, vn in pending:
        gate_store(r0, u, vn)


def in_proj(x, g, w, layer, ln_g, ln_b, w_s, b_s, *, tm):
    m, k = x.shape
    return pl.pallas_call(
        _in_proj_kernel,
        out_shape=[jax.ShapeDtypeStruct((m, Z_WIDTH), BF16), jax.ShapeDtypeStruct((m, A_WIDTH), BF16)],
        grid=(m // tm,),
        in_specs=[
            pl.BlockSpec((tm, k), lambda i: (i, 0)),
            pl.BlockSpec((1, k), lambda i: (0, 0)),
            pl.BlockSpec((None, k, IN_PAD), lambda i: (layer, 0, 0), pipeline_mode=pl.Buffered(1)),
            pl.BlockSpec((1, A_WIDTH), lambda i: (0, 0)),
            pl.BlockSpec((1, A_WIDTH), lambda i: (0, 0)),
            pl.BlockSpec((A_GROUPS, CHUNK, CHUNK), lambda i: (0, 0, 0)),
            pl.BlockSpec((CHUNK, A_GROUPS), lambda i: (0, 0)),
        ],
        out_specs=[pl.BlockSpec((tm, Z_WIDTH), lambda i: (i, 0)), pl.BlockSpec((tm, A_WIDTH), lambda i: (i, 0))],
        compiler_params=_cparams(("parallel",)),
        name="in_proj",
    )(x, g.reshape(1, k), w, ln_g.reshape(1, A_WIDTH), ln_b.reshape(1, A_WIDTH), w_s, b_s.T)


DIL_CH = 128
DIL_KW = 2 * DIL_CH
DIL_UNITS = 8


def _dilated_kernel(q_in, k_in, v_in, o_ref, q_ref, k_ref, v_ref, m_ref, acc_ref, l_ref):
    ch, kw = DIL_CH, DIL_KW
    seq = q_in.shape[0]
    for src, dst in ((q_in, q_ref), (k_in, k_ref), (v_in, v_ref)):
        def stage(c, carry, src=src, dst=dst):
            r0 = pl.multiple_of(c * NORM_ROWS, NORM_ROWS)
            dst[pl.ds(r0, NORM_ROWS), :] = src[pl.ds(r0, NORM_ROWS), :].astype(F32)
            return carry
        lax.fori_loop(0, seq // NORM_ROWS, stage, 0)
    qscale = (HEAD_DIM ** -0.5) * LOG2E
    row = lax.broadcasted_iota(jnp.int32, (ch, kw), 0)
    col = lax.broadcasted_iota(jnp.int32, (ch, kw), 1)
    ones = jnp.ones((kw, HEAD_DIM), BF16)
    patterns = sorted(DILATED_PAIRS, key=lambda wd: -wd[1])

    def rows(start, n, dil):
        return pl.ds(start, n) if dil == 1 else pl.ds(start, n, stride=dil)

    def band_bias(first, span):
        dist = row - col if first else ch + row - col
        return jnp.where((dist >= 0) & (dist <= span), 0.0, NEG_INF)

    def run_units(units, dil, span, phase):
        windows = {}
        scores = []
        for q0, k0, first, wid in units:
            if wid not in windows:
                k = k_ref[rows(k0, kw, dil), :].astype(BF16)
                v = v_ref[rows(k0, kw, dil), :].astype(BF16)
                windows[wid] = (k, jnp.concatenate([v, ones], axis=1))
            q = (q_ref[rows(q0, ch, dil), :] * qscale).astype(BF16)
            scores.append(_dot_nt(q, windows[wid][0]) + band_bias(first, span))
        for (q0, k0, first, wid), s in zip(units, scores):
            qrows = rows(q0, ch, dil)
            m_cur = jnp.max(s, axis=-1, keepdims=True)
            if phase == "first":
                m_new = jnp.broadcast_to(m_cur, (ch, LANE))
            else:
                m_old = m_ref[qrows, :]
                m_new = jnp.maximum(m_old, m_cur)
                alpha = jnp.exp2(m_old - m_new)
            p = jnp.exp2((s - jnp.concatenate([m_new] * (kw // LANE), axis=1)).astype(BF16))
            pv = _dot(p, windows[wid][1])
            acc, l = pv[:, 0:HEAD_DIM], pv[:, HEAD_DIM:2 * HEAD_DIM]
            if phase != "first":
                acc = alpha * acc_ref[qrows, :] + acc
                l = alpha * l_ref[qrows, :] + l
            if phase == "last":
                o_ref[pl.ds(pl.multiple_of(q0, ch), ch), :] = (acc / l).astype(o_ref.dtype)
            else:
                m_ref[qrows, :] = m_new
                acc_ref[qrows, :] = acc
                l_ref[qrows, :] = l

    for idx, (window, dil) in enumerate(patterns):
        phase = "first" if idx == 0 else ("last" if idx == len(patterns) - 1 else "mid")
        assert phase != "last" or dil == 1
        span = window // dil
        assert span <= ch
        n_units = seq // dil // ch
        per_class = min(n_units, max(DIL_UNITS // dil, 2))
        n_classes = DIL_UNITS // per_class
        assert n_units % per_class == 0 and dil % n_classes == 0

        def group(r0, j0, head, dil=dil, span=span, phase=phase, per_class=per_class, n_classes=n_classes):
            units = []
            for rc in range(n_classes):
                r = r0 + rc
                for u in range(per_class):
                    first = head and u == 0
                    q0 = r + dil * ch * (j0 + u)
                    k0 = r if (head and u <= 1) else r + dil * ch * (j0 + u - 1)
                    wid = (rc, 0) if (head and u <= 1) else (rc, u)
                    units.append((q0, k0, first, wid))
            run_units(units, dil, span, phase)

        def class_loop(g, carry, group=group, per_class=per_class, n_classes=n_classes, n_units=n_units):
            r0 = g * n_classes
            group(r0, 0, True)
            if n_units > per_class:
                def tail(jb, c):
                    group(r0, jb * per_class, False)
                    return c
                lax.fori_loop(1, n_units // per_class, tail, 0)
            return carry

        if dil // n_classes == 1:
            class_loop(0, 0)
        else:
            lax.fori_loop(0, dil // n_classes, class_loop, 0)


def dilated_attention(z, *, n_batch, seq):
    bw = B_HEADS * HEAD_DIM
    base = (COL_B - Z_OFF) // HEAD_DIM
    blk = (seq, HEAD_DIM)
    return pl.pallas_call(
        _dilated_kernel,
        out_shape=jax.ShapeDtypeStruct((n_batch * seq, bw), BF16),
        grid=(n_batch, B_HEADS),
        in_specs=[
            pl.BlockSpec(blk, lambda b, h: (b, base + h)),
            pl.BlockSpec(blk, lambda b, h: (b, base + B_HEADS + h)),
            pl.BlockSpec(blk, lambda b, h: (b, base + 2 * B_HEADS + h)),
        ],
        out_specs=pl.BlockSpec(blk, lambda b, h: (b, h)),
        scratch_shapes=[pltpu.VMEM(blk, F32)] * 6,
        compiler_params=_cparams(("parallel", "parallel")),
        name="dilated_attention",
    )(z, z, z)


N_SUB = 256
N_CMP = 2 * C_KV_HEADS


def _compress_kernel(x_in, pos_ref, w1_ref, w2_ref, o_ref, x_ref):
    x_ref[...] = x_in[...].astype(F32)
    p = jnp.zeros((N_SUB, HEAD_DIM), F32)
    q = jnp.zeros((N_SUB, HEAD_DIM), F32)
    for i in range(CMP_STRIDE):
        a = x_ref[pl.ds(i, N_SUB, stride=CMP_STRIDE), :]
        top = (a + pos_ref[i:i + 1, :]).astype(BF16)
        bot = (a + pos_ref[CMP_STRIDE + i:CMP_STRIDE + i + 1, :]).astype(BF16)
        p = p + _dot(top, w1_ref[i * HEAD_DIM:(i + 1) * HEAD_DIM, :])
        q = q + _dot(bot, w1_ref[(CMP_STRIDE + i) * HEAD_DIM:(CMP_STRIDE + i + 1) * HEAD_DIM, :])
    h = _gelu_tanh(p + pltpu.roll(q, N_SUB - 1, 0))
    o_ref[...] = _dot(h.astype(BF16), w2_ref[...]).astype(o_ref.dtype)


def compress(z, pos, w1, w2, layer, *, n_batch, seq):
    return pl.pallas_call(
        _compress_kernel,
        out_shape=jax.ShapeDtypeStruct((n_batch * N_CMP * N_SUB, HEAD_DIM), BF16),
        grid=(n_batch, N_CMP),
        in_specs=[
            pl.BlockSpec((seq, HEAD_DIM), lambda b, n: (b, (COL_CMP - Z_OFF) // HEAD_DIM + n)),
            pl.BlockSpec((CMP_LEN, HEAD_DIM), lambda b, n: (0, 0)),
            pl.BlockSpec((None, None, CMP_LEN * HEAD_DIM, HEAD_DIM), lambda b, n: (layer, n // C_KV_HEADS, 0, 0)),
            pl.BlockSpec((None, None, HEAD_DIM, HEAD_DIM), lambda b, n: (layer, n // C_KV_HEADS, 0, 0)),
        ],
        out_specs=pl.BlockSpec((N_SUB, HEAD_DIM), lambda b, n: (b * N_CMP + n, 0)),
        scratch_shapes=[pltpu.VMEM((seq, HEAD_DIM), F32)],
        compiler_params=_cparams(("parallel", "parallel")),
        name="nsa_compress",
    )(z, pos, w1, w2)


N_SEL = 64


def _nsa_kernel(*refs, n_cast):
    q_ref, kc_ref, vc_ref, ovt_ref, ks_ref, vs_ref, kw_ref, vw_ref, gate_ref = refs[:9]
    cast_in = refs[9:9 + n_cast]
    o_ref = refs[9 + n_cast]
    cast_out = refs[10 + n_cast:10 + 2 * n_cast]
    (qs_ref, kbs_ref, vbs_ref, kbw_ref, vbw_ref, ms_ref, as_ref, mw_ref, aw_ref, oc_ref,
     sc_ref) = refs[10 + 2 * n_cast:]
    _cast_slices(cast_in, cast_out)
    tq, tk, ch = ATT_TQ, ATT_TK, ATT_CH
    n_rep = C_GROUP
    n_sub = tq // ch
    n_ch = n_rep * n_sub
    seq = ks_ref.shape[0]
    i = pl.program_id(2)
    q0 = i * tq

    @pl.when(i == 0)
    def _():
        def prep(c, carry):
            rows = pl.ds(pl.multiple_of(c * tk, tk), tk)
            blk = (c * tk + lax.broadcasted_iota(jnp.int32, (tk, LANE), 0)) >> SEL_SHIFT
            lane = lax.broadcasted_iota(jnp.int32, (tk, LANE), 1)
            ones = jnp.ones((tk, HEAD_DIM), BF16)
            kbs_ref[rows, 0:HEAD_DIM] = ks_ref[rows, :]
            kbs_ref[rows, HEAD_DIM:HEAD_DIM + LANE] = jnp.where(blk == lane, 1.0, 0.0).astype(BF16)
            vbs_ref[rows, 0:HEAD_DIM] = vs_ref[rows, :]
            vbs_ref[rows, HEAD_DIM:2 * HEAD_DIM] = ones
            kbw_ref[rows, :] = kw_ref[rows, :]
            vbw_ref[rows, 0:HEAD_DIM] = vw_ref[rows, :]
            vbw_ref[rows, HEAD_DIM:2 * HEAD_DIM] = ones
            return carry

        lax.fori_loop(0, seq // tk, prep, 0)

    qscale = (HEAD_DIM ** -0.5) * LOG2E
    for r in range(n_rep):
        qs_ref[r * tq:(r + 1) * tq, 0:HEAD_DIM] = (q_ref[:, r * HEAD_DIM:(r + 1) * HEAD_DIM].astype(F32)
                                                   * qscale).astype(BF16)

    for m_ref, accl_ref in ((ms_ref, as_ref), (mw_ref, aw_ref)):
        m_ref[...] = jnp.full(m_ref.shape, NEG_INF, F32)
        accl_ref[...] = jnp.zeros(accl_ref.shape, F32)

    q_idx = lax.broadcasted_iota(jnp.int32, (tq, tk), 0)
    k_idx = lax.broadcasted_iota(jnp.int32, (tq, tk), 1)

    def tile_bias(off, max_dist):
        dist = off * tk + q_idx - k_idx
        if off == 0:
            ok = dist >= 0
        elif max_dist is not None and (off + 1) * tk - 1 > max_dist:
            ok = dist <= max_dist
        else:
            return None
        return jnp.where(ok, 0.0, NEG_INF)

    def issue(tiles, load_q, kb_ref):
        scores = []
        for kt, tbias in tiles:
            k0 = pl.multiple_of(kt * tk, tk)
            scores.append((_dot_nt(load_q(), kb_ref[pl.ds(k0, tk), :]), k0, tbias))
        return scores

    def finish(scores, vb_ref, m_ref, accl_ref):
        for s_tile, k0, tbias in scores:
            pts, alphas = [], []
            for c in range(n_ch):
                rows = slice(c * ch, (c + 1) * ch)
                s = s_tile[rows, :]
                if tbias is not None:
                    b0 = (c % n_sub) * ch
                    s = s + tbias[b0:b0 + ch, :]
                m_old = m_ref[rows, :]
                m_new = jnp.maximum(m_old, jnp.max(s, axis=-1, keepdims=True))
                alphas.append(jnp.exp2(m_old - m_new))
                pts.append(jnp.exp2((s - jnp.concatenate([m_new] * (tk // LANE), axis=1)).astype(BF16)))
                m_ref[rows, :] = m_new
            pv = _dot(jnp.concatenate(pts, axis=0), vb_ref[pl.ds(k0, tk), :])
            for c in range(n_ch):
                rows = slice(c * ch, (c + 1) * ch)
                alpha2 = jnp.concatenate([alphas[c], alphas[c]], axis=1)
                accl_ref[rows, :] = alpha2 * accl_ref[rows, :] + pv[rows, :]

    max_dist = WIN_LEN - 1
    win_tiles = []
    for off in range(-(-max_dist // tk), -1, -1):
        tbias = tile_bias(off, max_dist)
        if off > 0:
            before_start = jnp.where(i >= off, 0.0, NEG_INF)
            if tbias is None:
                tbias = jnp.where(q_idx + k_idx >= 0, 0.0, NEG_INF)
            tbias = tbias + before_start
        win_tiles.append((jnp.maximum(i - off, 0), tbias))
    win_scores = issue(win_tiles, lambda: qs_ref[:, 0:HEAD_DIM], kbw_ref)

    s_all = _dot_nt(qs_ref[:, 0:HEAD_DIM], kc_ref[...])
    n_cols = lax.broadcasted_iota(jnp.int32, (ch, N_SUB), 1)
    bias, has_any = [], []
    for c in range(n_sub):
        t = q0 + c * ch + lax.broadcasted_iota(jnp.int32, (ch, N_SUB), 0)
        bias.append(jnp.where((n_cols * CMP_STRIDE + (CMP_LEN - 1)) <= t, 0.0, NEG_INF))
        has_any.append(t[:, 0:1] >= CMP_LEN - 1)
    ps = []
    psum = [jnp.zeros((ch, N_SUB), F32) for _ in range(n_sub)]
    for c in range(n_ch):
        s = s_all[c * ch:(c + 1) * ch, :] + bias[c % n_sub]
        e = jnp.exp2(s - jnp.max(s, axis=-1, keepdims=True))
        den = jnp.sum(e, axis=-1, keepdims=True)
        p = e * jnp.where(has_any[c % n_sub], 1.0 / den, 0.0)
        ps.append(p.astype(BF16))
        psum[c % n_sub] = psum[c % n_sub] + p
    oc_ref[...] = _dot(jnp.concatenate(ps, axis=0), vc_ref[...])
    psum = jnp.concatenate(psum, axis=0)

    hi = psum.astype(BF16)
    lo = (psum - hi.astype(F32)).astype(BF16)
    ovt = ovt_ref[...]
    imp_t = _dot_nt(ovt, hi) + _dot_nt(ovt, lo)
    jrow = lax.broadcasted_iota(jnp.int32, (LANE, tq), 0)
    tl = q0 + lax.broadcasted_iota(jnp.int32, (LANE, tq), 1)
    jt = tl >> SEL_SHIFT
    forced = (jrow == 0) | (jrow == jt) | (jrow == jt - 1)
    valid_s = jrow * SEL_LEN <= tl
    sc_ref[...] = jnp.where(forced, 1e4, jnp.where(valid_s, imp_t, -1.0))
    finish(win_scores, vbw_ref, mw_ref, aw_ref)
    n_grp = N_SEL // SUBLANE
    grp = [sc_ref[g * SUBLANE:(g + 1) * SUBLANE, :] for g in range(n_grp)]
    cnt = [jnp.zeros((SUBLANE, tq), F32) for _ in range(n_grp)]
    sub = lax.broadcasted_iota(jnp.int32, (SUBLANE, tq), 0)
    for j in range(N_SEL):
        row = sc_ref[j:j + 1, :]
        for g in range(n_grp):
            if g < j // SUBLANE:
                beats = row > grp[g]
            elif g > j // SUBLANE:
                beats = row >= grp[g]
            else:
                beats = (row > grp[g]) | ((row == grp[g]) & (sub > j % SUBLANE))
            cnt[g] = cnt[g] + jnp.where(beats, 1.0, 0.0)
    pen_t = [jnp.where(c < float(SEL_TOP), 0.0, NEG_INF) for c in cnt]
    pen_t = jnp.concatenate(pen_t + [jnp.full((LANE - N_SEL, tq), NEG_INF, F32)], axis=0)
    pen = pen_t.T.astype(BF16)
    for r in range(n_rep):
        qs_ref[r * tq:(r + 1) * tq, HEAD_DIM:HEAD_DIM + LANE] = pen

    def sel_steps(tiles):
        finish(issue(tiles, lambda: qs_ref[...], kbs_ref), vbs_ref, ms_ref, as_ref)

    def sel_body(j, carry):
        sel_steps([(SEL_UNROLL * j + u, None) for u in range(SEL_UNROLL)])
        return carry

    lax.fori_loop(0, i // SEL_UNROLL, sel_body, 0)

    for rem in range(SEL_UNROLL):
        @pl.when(i % SEL_UNROLL == rem)
        def _(rem=rem):
            sel_steps([(i - rem + u, None) for u in range(rem)] + [(i, tile_bias(0, None))])

    g = 1.0 / (1.0 + jnp.exp(-gate_ref[...].astype(F32)))
    lane = lax.broadcasted_iota(jnp.int32, g.shape, 1)
    head0 = pl.program_id(1) * n_rep

    def gate(head, branch):
        return jnp.sum(jnp.where(lane == head * 3 + branch, g, 0.0), axis=-1, keepdims=True)

    for r in range(n_rep):
        rows = slice(r * tq, (r + 1) * tq)
        o = (gate(head0 + r, 0) * oc_ref[rows, :]
             + gate(head0 + r, 1) * (as_ref[rows, 0:HEAD_DIM] / as_ref[rows, HEAD_DIM:2 * HEAD_DIM])
             + gate(head0 + r, 2) * (aw_ref[rows, 0:HEAD_DIM] / aw_ref[rows, HEAD_DIM:2 * HEAD_DIM]))
        o_ref[:, r * HEAD_DIM:(r + 1) * HEAD_DIM] = o.astype(o_ref.dtype)


def nsa_attention(z, kvc, ovt, cast=(), cast_layer=0, *, n_batch, seq):
    tq = ATT_TQ
    nq = seq // tq
    n_rep = C_GROUP
    qw = n_rep * HEAD_DIM
    kv_blk = (seq, HEAD_DIM)
    sel0, win0 = (COL_SEL - Z_OFF) // HEAD_DIM, (COL_WIN - Z_OFF) // HEAD_DIM
    cast_in, cast_out, cast_shapes = _cast_specs(
        cast, cast_layer, lambda b, g, i: (b * C_KV_HEADS + g) * nq + i, n_batch * C_KV_HEADS * nq)
    res = pl.pallas_call(
        functools.partial(_nsa_kernel, n_cast=len(cast)),
        out_shape=[jax.ShapeDtypeStruct((n_batch * seq, C_HEADS * HEAD_DIM), BF16)] + cast_shapes,
        grid=(n_batch, C_KV_HEADS, nq),
        in_specs=[
            pl.BlockSpec((tq, qw), lambda b, g, i: (b * nq + i, (COL_CQ - Z_OFF) // qw + g)),
            pl.BlockSpec((N_SUB, HEAD_DIM), lambda b, g, i: (b * N_CMP + g, 0)),
            pl.BlockSpec((N_SUB, HEAD_DIM), lambda b, g, i: (b * N_CMP + C_KV_HEADS + g, 0)),
            pl.BlockSpec((LANE, N_SUB), lambda b, g, i: (0, 0)),
            pl.BlockSpec(kv_blk, lambda b, g, i: (b, sel0 + g)),
            pl.BlockSpec(kv_blk, lambda b, g, i: (b, sel0 + C_KV_HEADS + g)),
            pl.BlockSpec(kv_blk, lambda b, g, i: (b, win0 + g)),
            pl.BlockSpec(kv_blk, lambda b, g, i: (b, win0 + C_KV_HEADS + g)),
            pl.BlockSpec((tq, LANE), lambda b, g, i: (b * nq + i, (COL_GATE - Z_OFF) // LANE)),
        ] + cast_in,
        out_specs=[pl.BlockSpec((tq, qw), lambda b, g, i: (b * nq + i, g))] + cast_out,
        scratch_shapes=[
            pltpu.VMEM((n_rep * tq, HEAD_DIM + LANE), BF16),
            pltpu.VMEM((seq, HEAD_DIM + LANE), BF16),
            pltpu.VMEM((seq, 2 * HEAD_DIM), BF16),
            pltpu.VMEM((seq, HEAD_DIM), BF16),
            pltpu.VMEM((seq, 2 * HEAD_DIM), BF16),
            pltpu.VMEM((n_rep * tq, LANE), F32),
            pltpu.VMEM((n_rep * tq, 2 * HEAD_DIM), F32),
            pltpu.VMEM((n_rep * tq, LANE), F32),
            pltpu.VMEM((n_rep * tq, 2 * HEAD_DIM), F32),
            pltpu.VMEM((n_rep * tq, HEAD_DIM), F32),
            pltpu.VMEM((LANE, tq), F32),
        ],
        compiler_params=_cparams(("parallel", "parallel", "arbitrary")),
        name="nsa_attention",
    )(z, kvc, kvc, ovt, z, z, z, z, z, *cast)
    return res[0], res[1:]


def _overlap_t():
    n_c = N_SUB - 1
    c_start = np.arange(n_c) * CMP_STRIDE
    s_start = np.arange(N_SEL) * SEL_LEN
    ov = ((c_start[:, None] <= s_start[None, :] + SEL_LEN - 1)
          & (c_start[:, None] + CMP_LEN - 1 >= s_start[None, :])).astype(np.float32)
    out = np.zeros((LANE, N_SUB), np.float32)
    out[:N_SEL, :n_c] = ov.T
    return out


def _mix_out_kernel(x_ref, oa_ref, ob_ref, oc_ref, w_ref, o_ref):
    mix = jnp.concatenate([oa_ref[...], ob_ref[...], oc_ref[...]], axis=1)
    o_ref[...] = x_ref[...] + _dot(mix, w_ref[...])


def mix_out(x, out_a, out_b, out_c, w, layer, *, tm):
    m, n = x.shape
    return pl.pallas_call(
        _mix_out_kernel,
        out_shape=jax.ShapeDtypeStruct((m, n), F32),
        grid=(m // tm,),
        in_specs=[
            pl.BlockSpec((tm, n), lambda i: (i, 0)),
            pl.BlockSpec((tm, out_a.shape[1]), lambda i: (i, 0)),
            pl.BlockSpec((tm, out_b.shape[1]), lambda i: (i, 0)),
            pl.BlockSpec((tm, out_c.shape[1]), lambda i: (i, 0)),
            pl.BlockSpec((None,) + w.shape[1:], lambda i: (layer, 0, 0), pipeline_mode=pl.Buffered(1)),
        ],
        out_specs=pl.BlockSpec((tm, n), lambda i: (i, 0)),
        compiler_params=_cparams(("parallel",)),
        name="mix_out",
    )(x, out_a, out_b, out_c, w)


def _xattn_kernel(x_ref, g_ref, wq_ref, k_ref, v_ref, wo_ref, o_ref):
    n_mem = k_ref.shape[0]
    rows = x_ref.shape[0]
    ones = jnp.ones((n_mem, HEAD_DIM), BF16)
    kv = [(k_ref[:, h * HEAD_DIM:(h + 1) * HEAD_DIM].astype(BF16),
           jnp.concatenate([v_ref[:, h * HEAD_DIM:(h + 1) * HEAD_DIM].astype(BF16), ones], axis=1))
          for h in range(X_HEADS)]
    for c in range(rows // NORM_ROWS):
        r = slice(c * NORM_ROWS, (c + 1) * NORM_ROWS)
        x = x_ref[r, :]
        hn = _rms_rows(x, g_ref[...]).astype(BF16)
        q = _dot(hn, wq_ref[...]) * ((HEAD_DIM ** -0.5) * LOG2E)
        outs = []
        for h in range(X_HEADS):
            s = _dot_nt(q[:, h * HEAD_DIM:(h + 1) * HEAD_DIM].astype(BF16), kv[h][0])
            e = jnp.exp2((s - jnp.max(s, axis=-1, keepdims=True)).astype(BF16))
            pv = _dot(e, kv[h][1])
            outs.append((pv[:, 0:HEAD_DIM] / pv[:, HEAD_DIM:2 * HEAD_DIM]).astype(BF16))
        o_ref[r, :] = x + _dot(jnp.concatenate(outs, axis=1), wo_ref[...])


def cross_attention_block(x, g, wq, kv, wo, layer, *, n_batch, seq, n_mem, tm):
    nq = seq // tm
    d = x.shape[1]
    return pl.pallas_call(
        _xattn_kernel,
        out_shape=jax.ShapeDtypeStruct(x.shape, F32),
        grid=(n_batch, nq),
        in_specs=[
            pl.BlockSpec((tm, d), lambda b, i: (b * nq + i, 0)),
            pl.BlockSpec((1, d), lambda b, i: (0, 0)),
            pl.BlockSpec((None, d, X_WIDTH), lambda b, i: (layer, 0, 0), pipeline_mode=pl.Buffered(1)),
            pl.BlockSpec((n_mem, X_WIDTH), lambda b, i: (b, 0)),
            pl.BlockSpec((n_mem, X_WIDTH), lambda b, i: (b, 1)),
            pl.BlockSpec((None, X_WIDTH, d), lambda b, i: (layer, 0, 0), pipeline_mode=pl.Buffered(1)),
        ],
        out_specs=pl.BlockSpec((tm, d), lambda b, i: (b * nq + i, 0)),
        compiler_params=_cparams(("parallel", "parallel")),
        name="cross_attention",
    )(x, g.reshape(1, d), wq, kv, kv, wo)


def kernel(x, mem, norm_mix, w_in, gmlp_ln_g, gmlp_ln_b, gmlp_w_s, gmlp_b_s, cmp_pos, cmp_k_w1, cmp_k_w2,
           cmp_v_w1, cmp_v_w2, w_out, norm_xattn, norm_mem, xattn_wq, xattn_wkv, xattn_wo, norm_mlp, w_up,
           w_down, final_norm):
    bsz, seq, d = x.shape
    n_mem = mem.shape[1]
    t = bsz * seq
    assert d == D_MODEL and seq // CMP_STRIDE == N_SUB and seq // SEL_LEN == N_SEL
    assert seq % (DIL_KW * max(dil for _, dil in DILATED_PAIRS)) == 0 and seq % ATT_TQ == 0

    w_in_b = jnp.pad(w_in, ((0, 0), (0, 0), (0, IN_PAD - IN_WIDTH))).astype(BF16)
    big = (w_out, w_up, w_down)
    wq_b = xattn_wq.astype(BF16)
    wkv_b = xattn_wkv.astype(BF16)
    wo_b = xattn_wo.astype(BF16)
    cmp_w1 = jnp.stack([cmp_k_w1, cmp_v_w1], axis=1).astype(BF16)
    cmp_w2 = jnp.stack([cmp_k_w2, cmp_v_w2], axis=1).astype(BF16)
    ovt = jnp.asarray(_overlap_t()).astype(BF16)
    xf = x.reshape(t, d)
    memf = mem.reshape(bsz * n_mem, d)

    for l in range(DEPTH):
        z, out_a = in_proj(xf, norm_mix[l], w_in_b, l, gmlp_ln_g[l], gmlp_ln_b[l], gmlp_w_s[l], gmlp_b_s[l],
                           tm=TM_PROJ)

        out_b = dilated_attention(z, n_batch=bsz, seq=seq)

        kvc = compress(z, cmp_pos[l], cmp_w1, cmp_w2, l, n_batch=bsz, seq=seq)
        out_c, cast0 = nsa_attention(z, kvc, ovt, big if l == 0 else (), 0, n_batch=bsz, seq=seq)
        if l == 0:
            w_out_l, w_up_l, w_down_l = (w[None] for w in cast0)

        xf = mix_out(xf, out_a, out_b, out_c, w_out_l, 0, tm=TM_PROJ)

        kv = norm_matmul(memf, norm_mem[l], wkv_b, l, tm=TM_PROJ)
        xf = cross_attention_block(xf, norm_xattn[l], wq_b, kv, wo_b, l, n_batch=bsz, seq=seq, n_mem=n_mem,
                                   tm=TM_XATTN)

        nxt = () if l == DEPTH - 1 else big
        xf, nxt_b = mlp(xf, norm_mlp[l], w_up_l, w_down_l, 0, final_norm, nxt, l + 1, tm=TM_MLP, tf=TF_MLP,
                        norm_out=(l == DEPTH - 1))
        if nxt_b:
            w_out_l, w_up_l, w_down_l = (w[None] for w in nxt_b)

    return xf.reshape(bsz, seq, d)
```

```python
import functools
import math

import numpy as np
import jax
import jax.numpy as jnp
from jax import lax
from jax.experimental import pallas as pl
from jax.experimental.pallas import tpu as pltpu

F32 = jnp.float32
BF16 = jnp.bfloat16

D_MODEL = 2048
DEPTH = 4
HEAD_DIM = 128
A_GROUPS = 4
A_WIDTH = A_GROUPS * HEAD_DIM
CHUNK = 128
B_HEADS = 4
DILATED_PAIRS = ((128, 1), (512, 4), (2048, 16))
C_HEADS = 8
C_KV_HEADS = 2
C_GROUP = C_HEADS // C_KV_HEADS
CMP_LEN = 32
CMP_STRIDE = 16
SEL_LEN = 64
SEL_SHIFT = SEL_LEN.bit_length() - 1
SEL_TOP = 16
WIN_LEN = 512
X_HEADS = 4
X_WIDTH = X_HEADS * HEAD_DIM
EPS = 1e-6
NEG_INF = -1e30
LOG2E = math.log2(math.e)

COL_B = 2 * A_WIDTH
COL_CQ = COL_B + 3 * B_HEADS * HEAD_DIM
COL_CMP = COL_CQ + C_HEADS * HEAD_DIM
COL_SEL = COL_CMP + 2 * C_KV_HEADS * HEAD_DIM
COL_WIN = COL_SEL + 2 * C_KV_HEADS * HEAD_DIM
COL_GATE = COL_WIN + 2 * C_KV_HEADS * HEAD_DIM
IN_WIDTH = COL_GATE + 3 * C_HEADS
IN_PAD = 5248
Z_OFF = COL_B
Z_WIDTH = IN_PAD - Z_OFF

LANE = 128
SUBLANE = 8
VMEM_LIMIT = 56 * 1024 * 1024

TM_PROJ = 512
TM_XATTN = 1024
TM_MLP = 1024
TF_MLP = 512
NORM_ROWS = 256
ATT_TQ = 256
ATT_TK = 256
ATT_CH = 128
SEL_UNROLL = 4
CAST_CHUNKS = 128


def _cparams(sem):
    return pltpu.CompilerParams(dimension_semantics=sem, vmem_limit_bytes=VMEM_LIMIT)


def _rms_rows(x, g):
    ms = jnp.mean(x * x, axis=-1, keepdims=True)
    return x * lax.rsqrt(ms + EPS) * g


def _gelu_tanh(x):
    c = math.sqrt(2.0 / math.pi)
    return x * (0.5 * (1.0 + jnp.tanh(c * (x + 0.044715 * (x * x * x)))))


def _dot(a, b):
    return jnp.dot(a, b, preferred_element_type=F32)


def _dot_nt(a, b):
    return lax.dot_general(a, b, (((1,), (1,)), ((), ())), preferred_element_type=F32)


def _norm_into(h_ref, x_ref, g_ref, copy_ref=None):
    rows = x_ref.shape[0]

    def body(c, carry):
        r0 = pl.multiple_of(c * NORM_ROWS, NORM_ROWS)
        x = x_ref[pl.ds(r0, NORM_ROWS), :]
        h_ref[pl.ds(r0, NORM_ROWS), :] = _rms_rows(x, g_ref[...]).astype(h_ref.dtype)
        if copy_ref is not None:
            copy_ref[pl.ds(r0, NORM_ROWS), :] = x
        return carry

    lax.fori_loop(0, rows // NORM_ROWS, body, 0)


def _norm_matmul_kernel(x_ref, g_ref, w_ref, o_ref):
    half = x_ref.shape[0] // 2
    for c in range(2):
        r = slice(c * half, (c + 1) * half)
        hn = _rms_rows(x_ref[r, :], g_ref[...]).astype(BF16)
        o_ref[r, :] = _dot(hn, w_ref[...]).astype(o_ref.dtype)


def norm_matmul(x, g, w, layer, *, tm, out_dtype=F32):
    m, k = x.shape
    n = w.shape[2]
    return pl.pallas_call(
        _norm_matmul_kernel,
        out_shape=jax.ShapeDtypeStruct((m, n), out_dtype),
        grid=(m // tm,),
        in_specs=[
            pl.BlockSpec((tm, k), lambda i: (i, 0)),
            pl.BlockSpec((1, k), lambda i: (0, 0)),
            pl.BlockSpec((None, k, n), lambda i: (layer, 0, 0), pipeline_mode=pl.Buffered(1)),
        ],
        out_specs=pl.BlockSpec((tm, n), lambda i: (i, 0)),
        compiler_params=_cparams(("parallel",)),
        name="norm_matmul",
    )(x, g.reshape(1, k), w)


def _cast_specs(cast, cast_layer, step_of, n_steps):
    per_chunk = n_steps // CAST_CHUNKS
    in_specs, out_specs, out_shapes = [], [], []
    for w in cast:
        rows = w.shape[1] // CAST_CHUNKS
        in_specs.append(pl.BlockSpec((None, rows, w.shape[2]),
                                     lambda *ids: (cast_layer, step_of(*ids) // per_chunk, 0)))
        out_specs.append(pl.BlockSpec((rows, w.shape[2]), lambda *ids: (step_of(*ids) // per_chunk, 0)))
        out_shapes.append(jax.ShapeDtypeStruct(w.shape[1:], BF16))
    return in_specs, out_specs, out_shapes


def _cast_slices(cast_in, cast_out):
    for src, dst in zip(cast_in, cast_out):
        dst[...] = src[...].astype(dst.dtype)


def _mlp_kernel(*refs, norm_out, n_cast):
    x_ref, g_ref, wu_ref, wd_ref, gout_ref = refs[:5]
    cast_in = refs[5:5 + n_cast]
    o_ref = refs[5 + n_cast]
    cast_out = refs[6 + n_cast:6 + 2 * n_cast]
    h_ref = refs[6 + 2 * n_cast]
    f = pl.program_id(1)

    @pl.when(f == 0)
    def _():
        _norm_into(h_ref, x_ref, g_ref, copy_ref=o_ref)

    a = _dot(h_ref[...], wu_ref[...])
    a = jnp.square(jnp.maximum(a, 0.0)).astype(BF16)
    o_ref[...] += _dot(a, wd_ref[...])

    _cast_slices(cast_in, cast_out)

    if norm_out:
        @pl.when(f == pl.num_programs(1) - 1)
        def _():
            _norm_into(o_ref, o_ref, gout_ref)


def mlp(x, g, wu, wd, layer, g_out, cast=(), cast_layer=0, *, tm, tf, norm_out):
    m, d = x.shape
    ff = wu.shape[2]
    n_f = ff // tf
    cast_in, cast_out, cast_shapes = _cast_specs(cast, cast_layer, lambda i, f: i * n_f + f, (m // tm) * n_f)
    res = pl.pallas_call(
        functools.partial(_mlp_kernel, norm_out=norm_out, n_cast=len(cast)),
        out_shape=[jax.ShapeDtypeStruct((m, d), F32)] + cast_shapes,
        grid=(m // tm, n_f),
        in_specs=[
            pl.BlockSpec((tm, d), lambda i, f: (i, 0)),
            pl.BlockSpec((1, d), lambda i, f: (0, 0)),
            pl.BlockSpec((None, d, tf), lambda i, f: (layer, 0, f)),
            pl.BlockSpec((None, tf, d), lambda i, f: (layer, f, 0)),
            pl.BlockSpec((1, d), lambda i, f: (0, 0)),
        ] + cast_in,
        out_specs=[pl.BlockSpec((tm, d), lambda i, f: (i, 0))] + cast_out,
        scratch_shapes=[pltpu.VMEM((tm, d), BF16)],
        compiler_params=_cparams(("parallel", "arbitrary")),
        name="mlp",
    )(x, g.reshape(1, d), wu, wd, g_out.reshape(1, d), *cast)
    return res[0], res[1:]


def _in_proj_kernel(x_ref, g_ref, w_ref, lng_ref, lnb_ref, ws_ref, bst_ref, z_ref, oa_ref):
    row = lax.broadcasted_iota(jnp.int32, (CHUNK, CHUNK), 0)
    col = lax.broadcasted_iota(jnp.int32, (CHUNK, CHUNK), 1)
    causal = row >= col
    w_tril = [jnp.where(causal, ws_ref[gi], 0.0).astype(BF16) for gi in range(A_GROUPS)]

    def gate_inputs(za):
        u = _gelu_tanh(za[:, 0:A_WIDTH])
        v = _gelu_tanh(za[:, A_WIDTH:2 * A_WIDTH])
        vc = v - jnp.mean(v, axis=-1, keepdims=True)
        vn = vc * lax.rsqrt(jnp.mean(vc * vc, axis=-1, keepdims=True) + EPS)
        return u, (vn * lng_ref[...] + lnb_ref[...]).astype(BF16)

    def gate_store(r0, u, vn):
        for gi in range(A_GROUPS):
            cols = slice(gi * HEAD_DIM, (gi + 1) * HEAD_DIM)
            sv = _dot(w_tril[gi], vn[:, cols]) + bst_ref[:, gi:gi + 1]
            oa_ref[r0:r0 + CHUNK, cols] = (u[:, cols] * sv).astype(oa_ref.dtype)

    half = x_ref.shape[0] // 2
    pending = []
    for c in range(2):
        r = slice(c * half, (c + 1) * half)
        hn = _rms_rows(x_ref[r, :], g_ref[...]).astype(BF16)
        za = _dot(hn, w_ref[:, 0:Z_OFF])
        for r0, u, vn in pending:
            gate_store(r0, u, vn)
        z_ref[r, :] = _dot(hn, w_ref[:, Z_OFF:IN_PAD]).astype(z_ref.dtype)
        pending = [(c * half + k * CHUNK,) + gate_inputs(za[k * CHUNK:(k + 1) * CHUNK, :])
                   for k in range(half // CHUNK)]
    for r0, u, vn in pending:
        gate_store(r0, u, vn)


def in_proj(x, g, w, layer, ln_g, ln_b, w_s, b_s, *, tm):
    m, k = x.shape
    return pl.pallas_call(
        _in_proj_kernel,
        out_shape=[jax.ShapeDtypeStruct((m, Z_WIDTH), BF16), jax.ShapeDtypeStruct((m, A_WIDTH), BF16)],
        grid=(m // tm,),
        in_specs=[
            pl.BlockSpec((tm, k), lambda i: (i, 0)),
            pl.BlockSpec((1, k), lambda i: (0, 0)),
            pl.BlockSpec((None, k, IN_PAD), lambda i: (layer, 0, 0), pipeline_mode=pl.Buffered(1)),
            pl.BlockSpec((1, A_WIDTH), lambda i: (0, 0)),
            pl.BlockSpec((1, A_WIDTH), lambda i: (0, 0)),
            pl.BlockSpec((A_GROUPS, CHUNK, CHUNK), lambda i: (0, 0, 0)),
            pl.BlockSpec((CHUNK, A_GROUPS), lambda i: (0, 0)),
        ],
        out_specs=[pl.BlockSpec((tm, Z_WIDTH), lambda i: (i, 0)), pl.BlockSpec((tm, A_WIDTH), lambda i: (i, 0))],
        compiler_params=_cparams(("parallel",)),
        name="in_proj",
    )(x, g.reshape(1, k), w, ln_g.reshape(1, A_WIDTH), ln_b.reshape(1, A_WIDTH), w_s, b_s.T)


DIL_CH = 128
DIL_KW = 2 * DIL_CH
DIL_UNITS = 8


def _dilated_kernel(q_in, k_in, v_in, o_ref, q_ref, k_ref, v_ref, m_ref, acc_ref, l_ref):
    ch, kw = DIL_CH, DIL_KW
    seq = q_in.shape[0]
    for src, dst in ((q_in, q_ref), (k_in, k_ref), (v_in, v_ref)):
        def stage(c, carry, src=src, dst=dst):
            r0 = pl.multiple_of(c * NORM_ROWS, NORM_ROWS)
            dst[pl.ds(r0, NORM_ROWS), :] = src[pl.ds(r0, NORM_ROWS), :].astype(F32)
            return carry
        lax.fori_loop(0, seq // NORM_ROWS, stage, 0)
    qscale = (HEAD_DIM ** -0.5) * LOG2E
    row = lax.broadcasted_iota(jnp.int32, (ch, kw), 0)
    col = lax.broadcasted_iota(jnp.int32, (ch, kw), 1)
    ones = jnp.ones((kw, HEAD_DIM), BF16)
    patterns = sorted(DILATED_PAIRS, key=lambda wd: -wd[1])

    def rows(start, n, dil):
        return pl.ds(start, n) if dil == 1 else pl.ds(start, n, stride=dil)

    def band_bias(first, span):
        dist = row - col if first else ch + row - col
        return jnp.where((dist >= 0) & (dist <= span), 0.0, NEG_INF)

    def run_units(units, dil, span, phase):
        windows = {}
        scores = []
        for q0, k0, first, wid in units:
            if wid not in windows:
                k = k_ref[rows(k0, kw, dil), :].astype(BF16)
                v = v_ref[rows(k0, kw, dil), :].astype(BF16)
                windows[wid] = (k, jnp.concatenate([v, ones], axis=1))
            q = (q_ref[rows(q0, ch, dil), :] * qscale).astype(BF16)
            scores.append(_dot_nt(q, windows[wid][0]) + band_bias(first, span))
        for (q0, k0, first, wid), s in zip(units, scores):
            qrows = rows(q0, ch, dil)
            m_cur = jnp.max(s, axis=-1, keepdims=True)
            if phase == "first":
                m_new = jnp.broadcast_to(m_cur, (ch, LANE))
            else:
                m_old = m_ref[qrows, :]
                m_new = jnp.maximum(m_old, m_cur)
                alpha = jnp.exp2(m_old - m_new)
            p = jnp.exp2((s - jnp.concatenate([m_new] * (kw // LANE), axis=1)).astype(BF16))
            pv = _dot(p, windows[wid][1])
            acc, l = pv[:, 0:HEAD_DIM], pv[:, HEAD_DIM:2 * HEAD_DIM]
            if phase != "first":
                acc = alpha * acc_ref[qrows, :] + acc
                l = alpha * l_ref[qrows, :] + l
            if phase == "last":
                o_ref[pl.ds(pl.multiple_of(q0, ch), ch), :] = (acc / l).astype(o_ref.dtype)
            else:
                m_ref[qrows, :] = m_new
                acc_ref[qrows, :] = acc
                l_ref[qrows, :] = l

    for idx, (window, dil) in enumerate(patterns):
        phase = "first" if idx == 0 else ("last" if idx == len(patterns) - 1 else "mid")
        assert phase != "last" or dil == 1
        span = window // dil
        assert span <= ch
        n_units = seq // dil // ch
        per_class = min(n_units, max(DIL_UNITS // dil, 2))
        n_classes = DIL_UNITS // per_class
        assert n_units % per_class == 0 and dil % n_classes == 0

        def group(r0, j0, head, dil=dil, span=span, phase=phase, per_class=per_class, n_classes=n_classes):
            units = []
            for rc in range(n_classes):
                r = r0 + rc
                for u in range(per_class):
                    first = head and u == 0
                    q0 = r + dil * ch * (j0 + u)
                    k0 = r if (head and u <= 1) else r + dil * ch * (j0 + u - 1)
                    wid = (rc, 0) if (head and u <= 1) else (rc, u)
                    units.append((q0, k0, first, wid))
            run_units(units, dil, span, phase)

        def class_loop(g, carry, group=group, per_class=per_class, n_classes=n_classes, n_units=n_units):
            r0 = g * n_classes
            group(r0, 0, True)
            if n_units > per_class:
                def tail(jb, c):
                    group(r0, jb * per_class, False)
                    return c
                lax.fori_loop(1, n_units // per_class, tail, 0)
            return carry

        if dil // n_classes == 1:
            class_loop(0, 0)
        else:
            lax.fori_loop(0, dil // n_classes, class_loop, 0)


def dilated_attention(z, *, n_batch, seq):
    bw = B_HEADS * HEAD_DIM
    base = (COL_B - Z_OFF) // HEAD_DIM
    blk = (seq, HEAD_DIM)
    return pl.pallas_call(
        _dilated_kernel,
        out_shape=jax.ShapeDtypeStruct((n_batch * seq, bw), BF16),
        grid=(n_batch, B_HEADS),
        in_specs=[
            pl.BlockSpec(blk, lambda b, h: (b, base + h)),
            pl.BlockSpec(blk, lambda b, h: (b, base + B_HEADS + h)),
            pl.BlockSpec(blk, lambda b, h: (b, base + 2 * B_HEADS + h)),
        ],
        out_specs=pl.BlockSpec(blk, lambda b, h: (b, h)),
        scratch_shapes=[pltpu.VMEM(blk, F32)] * 6,
        compiler_params=_cparams(("parallel", "parallel")),
        name="dilated_attention",
    )(z, z, z)


N_SUB = 256
N_CMP = 2 * C_KV_HEADS


def _compress_kernel(x_in, pos_ref, w1_ref, w2_ref, o_ref, x_ref):
    x_ref[...] = x_in[...].astype(F32)
    p = jnp.zeros((N_SUB, HEAD_DIM), F32)
    q = jnp.zeros((N_SUB, HEAD_DIM), F32)
    for i in range(CMP_STRIDE):
        a = x_ref[pl.ds(i, N_SUB, stride=CMP_STRIDE), :]
        top = (a + pos_ref[i:i + 1, :]).astype(BF16)
        bot = (a + pos_ref[CMP_STRIDE + i:CMP_STRIDE + i + 1, :]).astype(BF16)
        p = p + _dot(top, w1_ref[i * HEAD_DIM:(i + 1) * HEAD_DIM, :])
        q = q + _dot(bot, w1_ref[(CMP_STRIDE + i) * HEAD_DIM:(CMP_STRIDE + i + 1) * HEAD_DIM, :])
    h = _gelu_tanh(p + pltpu.roll(q, N_SUB - 1, 0))
    o_ref[...] = _dot(h.astype(BF16), w2_ref[...]).astype(o_ref.dtype)


def compress(z, pos, w1, w2, layer, *, n_batch, seq):
    return pl.pallas_call(
        _compress_kernel,
        out_shape=jax.ShapeDtypeStruct((n_batch * N_CMP * N_SUB, HEAD_DIM), BF16),
        grid=(n_batch, N_CMP),
        in_specs=[
            pl.BlockSpec((seq, HEAD_DIM), lambda b, n: (b, (COL_CMP - Z_OFF) // HEAD_DIM + n)),
            pl.BlockSpec((CMP_LEN, HEAD_DIM), lambda b, n: (0, 0)),
            pl.BlockSpec((None, None, CMP_LEN * HEAD_DIM, HEAD_DIM), lambda b, n: (layer, n // C_KV_HEADS, 0, 0)),
            pl.BlockSpec((None, None, HEAD_DIM, HEAD_DIM), lambda b, n: (layer, n // C_KV_HEADS, 0, 0)),
        ],
        out_specs=pl.BlockSpec((N_SUB, HEAD_DIM), lambda b, n: (b * N_CMP + n, 0)),
        scratch_shapes=[pltpu.VMEM((seq, HEAD_DIM), F32)],
        compiler_params=_cparams(("parallel", "parallel")),
        name="nsa_compress",
    )(z, pos, w1, w2)


N_SEL = 64


def _nsa_kernel(*refs, n_cast):
    q_ref, kc_ref, vc_ref, ovt_ref, ks_ref, vs_ref, kw_ref, vw_ref, gate_ref = refs[:9]
    cast_in = refs[9:9 + n_cast]
    o_ref = refs[9 + n_cast]
    cast_out = refs[10 + n_cast:10 + 2 * n_cast]
    (qs_ref, kbs_ref, vbs_ref, kbw_ref, vbw_ref, ms_ref, as_ref, mw_ref, aw_ref, oc_ref,
     sc_ref) = refs[10 + 2 * n_cast:]
    _cast_slices(cast_in, cast_out)
    tq, tk, ch = ATT_TQ, ATT_TK, ATT_CH
    n_rep = C_GROUP
    n_sub = tq // ch
    n_ch = n_rep * n_sub
    seq = ks_ref.shape[0]
    i = pl.program_id(2)
    q0 = i * tq

    @pl.when(i == 0)
    def _():
        def prep(c, carry):
            rows = pl.ds(pl.multiple_of(c * tk, tk), tk)
            blk = (c * tk + lax.broadcasted_iota(jnp.int32, (tk, LANE), 0)) >> SEL_SHIFT
            lane = lax.broadcasted_iota(jnp.int32, (tk, LANE), 1)
            ones = jnp.ones((tk, HEAD_DIM), BF16)
            kbs_ref[rows, 0:HEAD_DIM] = ks_ref[rows, :]
            kbs_ref[rows, HEAD_DIM:HEAD_DIM + LANE] = jnp.where(blk == lane, 1.0, 0.0).astype(BF16)
            vbs_ref[rows, 0:HEAD_DIM] = vs_ref[rows, :]
            vbs_ref[rows, HEAD_DIM:2 * HEAD_DIM] = ones
            kbw_ref[rows, :] = kw_ref[rows, :]
            vbw_ref[rows, 0:HEAD_DIM] = vw_ref[rows, :]
            vbw_ref[rows, HEAD_DIM:2 * HEAD_DIM] = ones
            return carry

        lax.fori_loop(0, seq // tk, prep, 0)

    qscale = (HEAD_DIM ** -0.5) * LOG2E
    for r in range(n_rep):
        qs_ref[r * tq:(r + 1) * tq, 0:HEAD_DIM] = (q_ref[:, r * HEAD_DIM:(r + 1) * HEAD_DIM].astype(F32)
                                                   * qscale).astype(BF16)

    for m_ref, accl_ref in ((ms_ref, as_ref), (mw_ref, aw_ref)):
        m_ref[...] = jnp.full(m_ref.shape, NEG_INF, F32)
        accl_ref[...] = jnp.zeros(accl_ref.shape, F32)

    q_idx = lax.broadcasted_iota(jnp.int32, (tq, tk), 0)
    k_idx = lax.broadcasted_iota(jnp.int32, (tq, tk), 1)

    def tile_bias(off, max_dist):
        dist = off * tk + q_idx - k_idx
        if off == 0:
            ok = dist >= 0
        elif max_dist is not None and (off + 1) * tk - 1 > max_dist:
            ok = dist <= max_dist
        else:
            return None
        return jnp.where(ok, 0.0, NEG_INF)

    def issue(tiles, load_q, kb_ref):
        scores = []
        for kt, tbias, n_tiles in tiles:
            k0 = pl.multiple_of(kt * tk, tk)
            scores.append((_dot_nt(load_q(), kb_ref[pl.ds(k0, n_tiles * tk), :]), k0, tbias))
        return scores

    def finish(scores, vb_ref, m_ref, accl_ref):
        for s_tile, k0, tbias in scores:
            pts, alphas = [], []
            for c in range(n_ch):
                rows = slice(c * ch, (c + 1) * ch)
                s = s_tile[rows, :]
                if tbias is not None:
                    b0 = (c % n_sub) * ch
                    s = s + tbias[b0:b0 + ch, :]
                m_old = m_ref[rows, :]
                m_new = jnp.maximum(m_old, jnp.max(s, axis=-1, keepdims=True))
                alphas.append(jnp.exp2(m_old - m_new))
                width = s.shape[1]
                pts.append(jnp.exp2((s - jnp.concatenate([m_new] * (width // LANE), axis=1)).astype(BF16)))
                m_ref[rows, :] = m_new
            pv = _dot(jnp.concatenate(pts, axis=0), vb_ref[pl.ds(k0, width), :])
            for c in range(n_ch):
                rows = slice(c * ch, (c + 1) * ch)
                alpha2 = jnp.concatenate([alphas[c], alphas[c]], axis=1)
                accl_ref[rows, :] = alpha2 * accl_ref[rows, :] + pv[rows, :]

    max_dist = WIN_LEN - 1
    win_tiles = []
    for off in range(-(-max_dist // tk), -1, -1):
        tbias = tile_bias(off, max_dist)
        if off > 0:
            before_start = jnp.where(i >= off, 0.0, NEG_INF)
            if tbias is None:
                tbias = jnp.where(q_idx + k_idx >= 0, 0.0, NEG_INF)
            tbias = tbias + before_start
        win_tiles.append((jnp.maximum(i - off, 0), tbias, 1))
    win_scores = issue(win_tiles, lambda: qs_ref[:, 0:HEAD_DIM], kbw_ref)

    s_all = _dot_nt(qs_ref[:, 0:HEAD_DIM], kc_ref[...])
    n_cols = lax.broadcasted_iota(jnp.int32, (ch, N_SUB), 1)
    bias, has_any = [], []
    for c in range(n_sub):
        t = q0 + c * ch + lax.broadcasted_iota(jnp.int32, (ch, N_SUB), 0)
        bias.append(jnp.where((n_cols * CMP_STRIDE + (CMP_LEN - 1)) <= t, 0.0, NEG_INF))
        has_any.append(t[:, 0:1] >= CMP_LEN - 1)
    ps = []
    psum = [jnp.zeros((ch, N_SUB), F32) for _ in range(n_sub)]
    for c in range(n_ch):
        s = s_all[c * ch:(c + 1) * ch, :] + bias[c % n_sub]
        e = jnp.exp2(s - jnp.max(s, axis=-1, keepdims=True))
        den = jnp.sum(e, axis=-1, keepdims=True)
        p = e * jnp.where(has_any[c % n_sub], 1.0 / den, 0.0)
        ps.append(p.astype(BF16))
        psum[c % n_sub] = psum[c % n_sub] + p
    oc_ref[...] = _dot(jnp.concatenate(ps, axis=0), vc_ref[...])
    psum = jnp.concatenate(psum, axis=0)

    hi = psum.astype(BF16)
    lo = (psum - hi.astype(F32)).astype(BF16)
    ovt = ovt_ref[...]
    imp_t = _dot_nt(ovt, hi) + _dot_nt(ovt, lo)
    jrow = lax.broadcasted_iota(jnp.int32, (LANE, tq), 0)
    tl = q0 + lax.broadcasted_iota(jnp.int32, (LANE, tq), 1)
    jt = tl >> SEL_SHIFT
    forced = (jrow == 0) | (jrow == jt) | (jrow == jt - 1)
    valid_s = jrow * SEL_LEN <= tl
    sc_ref[...] = jnp.where(forced, 1e4, jnp.where(valid_s, imp_t, -1.0))
    finish(win_scores, vbw_ref, mw_ref, aw_ref)
    n_grp = N_SEL // SUBLANE
    grp = [sc_ref[g * SUBLANE:(g + 1) * SUBLANE, :] for g in range(n_grp)]
    cnt = [jnp.zeros((SUBLANE, tq), F32) for _ in range(n_grp)]
    sub = lax.broadcasted_iota(jnp.int32, (SUBLANE, tq), 0)
    for j in range(N_SEL):
        row = sc_ref[j:j + 1, :]
        for g in range(n_grp):
            if g < j // SUBLANE:
                beats = row > grp[g]
            elif g > j // SUBLANE:
                beats = row >= grp[g]
            else:
                beats = (row > grp[g]) | ((row == grp[g]) & (sub > j % SUBLANE))
            cnt[g] = cnt[g] + jnp.where(beats, 1.0, 0.0)
    pen_t = [jnp.where(c < float(SEL_TOP), 0.0, NEG_INF) for c in cnt]
    pen_t = jnp.concatenate(pen_t + [jnp.full((LANE - N_SEL, tq), NEG_INF, F32)], axis=0)
    pen = pen_t.T.astype(BF16)
    for r in range(n_rep):
        qs_ref[r * tq:(r + 1) * tq, HEAD_DIM:HEAD_DIM + LANE] = pen

    def sel_steps(first, n_full, diagonal):
        tiles = [(first + u, None, 2) for u in range(0, n_full - n_full % 2, 2)]
        if n_full % 2:
            tiles.append((first + n_full - 1, None, 1))
        if diagonal:
            tiles.append((i, tile_bias(0, None), 1))
        finish(issue(tiles, lambda: qs_ref[...], kbs_ref), vbs_ref, ms_ref, as_ref)

    def sel_body(j, carry):
        sel_steps(SEL_UNROLL * j, SEL_UNROLL, False)
        return carry

    lax.fori_loop(0, i // SEL_UNROLL, sel_body, 0)

    for rem in range(SEL_UNROLL):
        @pl.when(i % SEL_UNROLL == rem)
        def _(rem=rem):
            sel_steps(i - rem, rem, True)

    g = 1.0 / (1.0 + jnp.exp(-gate_ref[...].astype(F32)))
    lane = lax.broadcasted_iota(jnp.int32, g.shape, 1)
    head0 = pl.program_id(1) * n_rep

    def gate(head, branch):
        return jnp.sum(jnp.where(lane == head * 3 + branch, g, 0.0), axis=-1, keepdims=True)

    for r in range(n_rep):
        rows = slice(r * tq, (r + 1) * tq)
        o = (gate(head0 + r, 0) * oc_ref[rows, :]
             + gate(head0 + r, 1) * (as_ref[rows, 0:HEAD_DIM] / as_ref[rows, HEAD_DIM:2 * HEAD_DIM])
             + gate(head0 + r, 2) * (aw_ref[rows, 0:HEAD_DIM] / aw_ref[rows, HEAD_DIM:2 * HEAD_DIM]))
        o_ref[:, r * HEAD_DIM:(r + 1) * HEAD_DIM] = o.astype(o_ref.dtype)


def nsa_attention(z, kvc, ovt, cast=(), cast_layer=0, *, n_batch, seq):
    tq = ATT_TQ
    nq = seq // tq
    n_rep = C_GROUP
    qw = n_rep * HEAD_DIM
    kv_blk = (seq, HEAD_DIM)
    sel0, win0 = (COL_SEL - Z_OFF) // HEAD_DIM, (COL_WIN - Z_OFF) // HEAD_DIM
    cast_in, cast_out, cast_shapes = _cast_specs(
        cast, cast_layer, lambda b, g, i: (b * C_KV_HEADS + g) * nq + i, n_batch * C_KV_HEADS * nq)
    res = pl.pallas_call(
        functools.partial(_nsa_kernel, n_cast=len(cast)),
        out_shape=[jax.ShapeDtypeStruct((n_batch * seq, C_HEADS * HEAD_DIM), BF16)] + cast_shapes,
        grid=(n_batch, C_KV_HEADS, nq),
        in_specs=[
            pl.BlockSpec((tq, qw), lambda b, g, i: (b * nq + i, (COL_CQ - Z_OFF) // qw + g)),
            pl.BlockSpec((N_SUB, HEAD_DIM), lambda b, g, i: (b * N_CMP + g, 0)),
            pl.BlockSpec((N_SUB, HEAD_DIM), lambda b, g, i: (b * N_CMP + C_KV_HEADS + g, 0)),
            pl.BlockSpec((LANE, N_SUB), lambda b, g, i: (0, 0)),
            pl.BlockSpec(kv_blk, lambda b, g, i: (b, sel0 + g)),
            pl.BlockSpec(kv_blk, lambda b, g, i: (b, sel0 + C_KV_HEADS + g)),
            pl.BlockSpec(kv_blk, lambda b, g, i: (b, win0 + g)),
            pl.BlockSpec(kv_blk, lambda b, g, i: (b, win0 + C_KV_HEADS + g)),
            pl.BlockSpec((tq, LANE), lambda b, g, i: (b * nq + i, (COL_GATE - Z_OFF) // LANE)),
        ] + cast_in,
        out_specs=[pl.BlockSpec((tq, qw), lambda b, g, i: (b * nq + i, g))] + cast_out,
        scratch_shapes=[
            pltpu.VMEM((n_rep * tq, HEAD_DIM + LANE), BF16),
            pltpu.VMEM((seq, HEAD_DIM + LANE), BF16),
            pltpu.VMEM((seq, 2 * HEAD_DIM), BF16),
            pltpu.VMEM((seq, HEAD_DIM), BF16),
            pltpu.VMEM((seq, 2 * HEAD_DIM), BF16),
            pltpu.VMEM((n_rep * tq, LANE), F32),
            pltpu.VMEM((n_rep * tq, 2 * HEAD_DIM), F32),
            pltpu.VMEM((n_rep * tq, LANE), F32),
            pltpu.VMEM((n_rep * tq, 2 * HEAD_DIM), F32),
            pltpu.VMEM((n_rep * tq, HEAD_DIM), F32),
            pltpu.VMEM((LANE, tq), F32),
        ],
        compiler_params=_cparams(("parallel", "parallel", "arbitrary")),
        name="nsa_attention",
    )(z, kvc, kvc, ovt, z, z, z, z, z, *cast)
    return res[0], res[1:]


def _overlap_t():
    n_c = N_SUB - 1
    c_start = np.arange(n_c) * CMP_STRIDE
    s_start = np.arange(N_SEL) * SEL_LEN
    ov = ((c_start[:, None] <= s_start[None, :] + SEL_LEN - 1)
          & (c_start[:, None] + CMP_LEN - 1 >= s_start[None, :])).astype(np.float32)
    out = np.zeros((LANE, N_SUB), np.float32)
    out[:N_SEL, :n_c] = ov.T
    return out


def _mix_out_kernel(x_ref, oa_ref, ob_ref, oc_ref, w_ref, o_ref):
    mix = jnp.concatenate([oa_ref[...], ob_ref[...], oc_ref[...]], axis=1)
    o_ref[...] = x_ref[...] + _dot(mix, w_ref[...])


def mix_out(x, out_a, out_b, out_c, w, layer, *, tm):
    m, n = x.shape
    return pl.pallas_call(
        _mix_out_kernel,
        out_shape=jax.ShapeDtypeStruct((m, n), F32),
        grid=(m // tm,),
        in_specs=[
            pl.BlockSpec((tm, n), lambda i: (i, 0)),
            pl.BlockSpec((tm, out_a.shape[1]), lambda i: (i, 0)),
            pl.BlockSpec((tm, out_b.shape[1]), lambda i: (i, 0)),
            pl.BlockSpec((tm, out_c.shape[1]), lambda i: (i, 0)),
            pl.BlockSpec((None,) + w.shape[1:], lambda i: (layer, 0, 0), pipeline_mode=pl.Buffered(1)),
        ],
        out_specs=pl.BlockSpec((tm, n), lambda i: (i, 0)),
        compiler_params=_cparams(("parallel",)),
        name="mix_out",
    )(x, out_a, out_b, out_c, w)


def _xattn_kernel(x_ref, g_ref, wq_ref, k_ref, v_ref, wo_ref, o_ref):
    n_mem = k_ref.shape[0]
    rows = x_ref.shape[0]
    ones = jnp.ones((n_mem, HEAD_DIM), BF16)
    kv = [(k_ref[:, h * HEAD_DIM:(h + 1) * HEAD_DIM].astype(BF16),
           jnp.concatenate([v_ref[:, h * HEAD_DIM:(h + 1) * HEAD_DIM].astype(BF16), ones], axis=1))
          for h in range(X_HEADS)]
    for c in range(rows // NORM_ROWS):
        r = slice(c * NORM_ROWS, (c + 1) * NORM_ROWS)
        x = x_ref[r, :]
        hn = _rms_rows(x, g_ref[...]).astype(BF16)
        q = _dot(hn, wq_ref[...]) * ((HEAD_DIM ** -0.5) * LOG2E)
        outs = []
        for h in range(X_HEADS):
            s = _dot_nt(q[:, h * HEAD_DIM:(h + 1) * HEAD_DIM].astype(BF16), kv[h][0])
            e = jnp.exp2((s - jnp.max(s, axis=-1, keepdims=True)).astype(BF16))
            pv = _dot(e, kv[h][1])
            outs.append((pv[:, 0:HEAD_DIM] / pv[:, HEAD_DIM:2 * HEAD_DIM]).astype(BF16))
        o_ref[r, :] = x + _dot(jnp.concatenate(outs, axis=1), wo_ref[...])


def cross_attention_block(x, g, wq, kv, wo, layer, *, n_batch, seq, n_mem, tm):
    nq = seq // tm
    d = x.shape[1]
    return pl.pallas_call(
        _xattn_kernel,
        out_shape=jax.ShapeDtypeStruct(x.shape, F32),
        grid=(n_batch, nq),
        in_specs=[
            pl.BlockSpec((tm, d), lambda b, i: (b * nq + i, 0)),
            pl.BlockSpec((1, d), lambda b, i: (0, 0)),
            pl.BlockSpec((None, d, X_WIDTH), lambda b, i: (layer, 0, 0), pipeline_mode=pl.Buffered(1)),
            pl.BlockSpec((n_mem, X_WIDTH), lambda b, i: (b, 0)),
            pl.BlockSpec((n_mem, X_WIDTH), lambda b, i: (b, 1)),
            pl.BlockSpec((None, X_WIDTH, d), lambda b, i: (layer, 0, 0), pipeline_mode=pl.Buffered(1)),
        ],
        out_specs=pl.BlockSpec((tm, d), lambda b, i: (b * nq + i, 0)),
        compiler_params=_cparams(("parallel", "parallel")),
        name="cross_attention",
    )(x, g.reshape(1, d), wq, kv, kv, wo)


def kernel(x, mem, norm_mix, w_in, gmlp_ln_g, gmlp_ln_b, gmlp_w_s, gmlp_b_s, cmp_pos, cmp_k_w1, cmp_k_w2,
           cmp_v_w1, cmp_v_w2, w_out, norm_xattn, norm_mem, xattn_wq, xattn_wkv, xattn_wo, norm_mlp, w_up,
           w_down, final_norm):
    bsz, seq, d = x.shape
    n_mem = mem.shape[1]
    t = bsz * seq
    assert d == D_MODEL and seq // CMP_STRIDE == N_SUB and seq // SEL_LEN == N_SEL
    assert seq % (DIL_KW * max(dil for _, dil in DILATED_PAIRS)) == 0 and seq % ATT_TQ == 0

    w_in_b = jnp.pad(w_in, ((0, 0), (0, 0), (0, IN_PAD - IN_WIDTH))).astype(BF16)
    big = (w_out, w_up, w_down)
    wq_b = xattn_wq.astype(BF16)
    wkv_b = xattn_wkv.astype(BF16)
    wo_b = xattn_wo.astype(BF16)
    cmp_w1 = jnp.stack([cmp_k_w1, cmp_v_w1], axis=1).astype(BF16)
    cmp_w2 = jnp.stack([cmp_k_w2, cmp_v_w2], axis=1).astype(BF16)
    ovt = jnp.asarray(_overlap_t()).astype(BF16)
    xf = x.reshape(t, d)
    memf = mem.reshape(bsz * n_mem, d)

    for l in range(DEPTH):
        z, out_a = in_proj(xf, norm_mix[l], w_in_b, l, gmlp_ln_g[l], gmlp_ln_b[l], gmlp_w_s[l], gmlp_b_s[l],
                           tm=TM_PROJ)

        out_b = dilated_attention(z, n_batch=bsz, seq=seq)

        kvc = compress(z, cmp_pos[l], cmp_w1, cmp_w2, l, n_batch=bsz, seq=seq)
        out_c, cast0 = nsa_attention(z, kvc, ovt, big if l == 0 else (), 0, n_batch=bsz, seq=seq)
        if l == 0:
            w_out_l, w_up_l, w_down_l = (w[None] for w in cast0)

        xf = mix_out(xf, out_a, out_b, out_c, w_out_l, 0, tm=TM_PROJ)

        kv = norm_matmul(memf, norm_mem[l], wkv_b, l, tm=TM_PROJ)
        xf = cross_attention_block(xf, norm_xattn[l], wq_b, kv, wo_b, l, n_batch=bsz, seq=seq, n_mem=n_mem,
                                   tm=TM_XATTN)

        nxt = () if l == DEPTH - 1 else big
        xf, nxt_b = mlp(xf, norm_mlp[l], w_up_l, w_down_l, 0, final_norm, nxt, l + 1, tm=TM_MLP, tf=TF_MLP,
                        norm_out=(l == DEPTH - 1))
        if nxt_b:
            w_out_l, w_up_l, w_down_l = (w[None] for w in nxt_b)

    return xf.reshape(bsz, seq, d)
```

```python
import functools
import math

import numpy as np
import jax
import jax.numpy as jnp
from jax import lax
from jax.experimental import pallas as pl
from jax.experimental.pallas import tpu as pltpu

F32 = jnp.float32
BF16 = jnp.bfloat16

D_MODEL = 2048
DEPTH = 4
HEAD_DIM = 128
A_GROUPS = 4
A_WIDTH = A_GROUPS * HEAD_DIM
CHUNK = 128
B_HEADS = 4
DILATED_PAIRS = ((128, 1), (512, 4), (2048, 16))
C_HEADS = 8
C_KV_HEADS = 2
C_GROUP = C_HEADS // C_KV_HEADS
CMP_LEN = 32
CMP_STRIDE = 16
SEL_LEN = 64
SEL_SHIFT = SEL_LEN.bit_length() - 1
SEL_TOP = 16
WIN_LEN = 512
X_HEADS = 4
X_WIDTH = X_HEADS * HEAD_DIM
EPS = 1e-6
NEG_INF = -1e30
LOG2E = math.log2(math.e)

COL_B = 2 * A_WIDTH
COL_CQ = COL_B + 3 * B_HEADS * HEAD_DIM
COL_CMP = COL_CQ + C_HEADS * HEAD_DIM
COL_SEL = COL_CMP + 2 * C_KV_HEADS * HEAD_DIM
COL_WIN = COL_SEL + 2 * C_KV_HEADS * HEAD_DIM
COL_GATE = COL_WIN + 2 * C_KV_HEADS * HEAD_DIM
IN_WIDTH = COL_GATE + 3 * C_HEADS
IN_PAD = 5248
Z_OFF = COL_B
Z_WIDTH = IN_PAD - Z_OFF

LANE = 128
SUBLANE = 8
VMEM_LIMIT = 56 * 1024 * 1024

TM_PROJ = 512
TM_XATTN = 1024
TM_MLP = 1024
TF_MLP = 512
NORM_ROWS = 256
ATT_TQ = 256
ATT_TK = 256
ATT_CH = 128
SEL_UNROLL = 4
CAST_CHUNKS = 128


def _cparams(sem):
    return pltpu.CompilerParams(dimension_semantics=sem, vmem_limit_bytes=VMEM_LIMIT)


def _rms_rows(x, g):
    ms = jnp.mean(x * x, axis=-1, keepdims=True)
    return x * lax.rsqrt(ms + EPS) * g


def _gelu_tanh(x):
    c = math.sqrt(2.0 / math.pi)
    return x * (0.5 * (1.0 + jnp.tanh(c * (x + 0.044715 * (x * x * x)))))


def _dot(a, b):
    return jnp.dot(a, b, preferred_element_type=F32)


def _dot_nt(a, b):
    return lax.dot_general(a, b, (((1,), (1,)), ((), ())), preferred_element_type=F32)


def _norm_into(h_ref, x_ref, g_ref, copy_ref=None):
    rows = x_ref.shape[0]

    def body(c, carry):
        r0 = pl.multiple_of(c * NORM_ROWS, NORM_ROWS)
        x = x_ref[pl.ds(r0, NORM_ROWS), :]
        h_ref[pl.ds(r0, NORM_ROWS), :] = _rms_rows(x, g_ref[...]).astype(h_ref.dtype)
        if copy_ref is not None:
            copy_ref[pl.ds(r0, NORM_ROWS), :] = x
        return carry

    lax.fori_loop(0, rows // NORM_ROWS, body, 0)


def _norm_matmul_kernel(x_ref, g_ref, w_ref, o_ref):
    half = x_ref.shape[0] // 2
    for c in range(2):
        r = slice(c * half, (c + 1) * half)
        hn = _rms_rows(x_ref[r, :], g_ref[...]).astype(BF16)
        o_ref[r, :] = _dot(hn, w_ref[...]).astype(o_ref.dtype)


def norm_matmul(x, g, w, layer, *, tm, out_dtype=F32):
    m, k = x.shape
    n = w.shape[2]
    return pl.pallas_call(
        _norm_matmul_kernel,
        out_shape=jax.ShapeDtypeStruct((m, n), out_dtype),
        grid=(m // tm,),
        in_specs=[
            pl.BlockSpec((tm, k), lambda i: (i, 0)),
            pl.BlockSpec((1, k), lambda i: (0, 0)),
            pl.BlockSpec((None, k, n), lambda i: (layer, 0, 0), pipeline_mode=pl.Buffered(1)),
        ],
        out_specs=pl.BlockSpec((tm, n), lambda i: (i, 0)),
        compiler_params=_cparams(("parallel",)),
        name="norm_matmul",
    )(x, g.reshape(1, k), w)


def _cast_specs(cast, cast_layer, step_of, n_steps):
    per_chunk = n_steps // CAST_CHUNKS
    in_specs, out_specs, out_shapes = [], [], []
    for w in cast:
        rows = w.shape[1] // CAST_CHUNKS
        in_specs.append(pl.BlockSpec((None, rows, w.shape[2]),
                                     lambda *ids: (cast_layer, step_of(*ids) // per_chunk, 0)))
        out_specs.append(pl.BlockSpec((rows, w.shape[2]), lambda *ids: (step_of(*ids) // per_chunk, 0)))
        out_shapes.append(jax.ShapeDtypeStruct(w.shape[1:], BF16))
    return in_specs, out_specs, out_shapes


def _cast_slices(cast_in, cast_out):
    for src, dst in zip(cast_in, cast_out):
        dst[...] = src[...].astype(dst.dtype)


def _mlp_kernel(*refs, norm_out, n_cast):
    x_ref, g_ref, wu_ref, wd_ref, gout_ref = refs[:5]
    cast_in = refs[5:5 + n_cast]
    o_ref = refs[5 + n_cast]
    cast_out = refs[6 + n_cast:6 + 2 * n_cast]
    h_ref = refs[6 + 2 * n_cast]
    f = pl.program_id(1)

    @pl.when(f == 0)
    def _():
        _norm_into(h_ref, x_ref, g_ref, copy_ref=o_ref)

    a = _dot(h_ref[...], wu_ref[...])
    a = jnp.square(jnp.maximum(a, 0.0)).astype(BF16)
    o_ref[...] += _dot(a, wd_ref[...])

    _cast_slices(cast_in, cast_out)

    if norm_out:
        @pl.when(f == pl.num_programs(1) - 1)
        def _():
            _norm_into(o_ref, o_ref, gout_ref)


def mlp(x, g, wu, wd, layer, g_out, cast=(), cast_layer=0, *, tm, tf, norm_out):
    m, d = x.shape
    ff = wu.shape[2]
    n_f = ff // tf
    cast_in, cast_out, cast_shapes = _cast_specs(cast, cast_layer, lambda i, f: i * n_f + f, (m // tm) * n_f)
    res = pl.pallas_call(
        functools.partial(_mlp_kernel, norm_out=norm_out, n_cast=len(cast)),
        out_shape=[jax.ShapeDtypeStruct((m, d), F32)] + cast_shapes,
        grid=(m // tm, n_f),
        in_specs=[
            pl.BlockSpec((tm, d), lambda i, f: (i, 0)),
            pl.BlockSpec((1, d), lambda i, f: (0, 0)),
            pl.BlockSpec((None, d, tf), lambda i, f: (layer, 0, f)),
            pl.BlockSpec((None, tf, d), lambda i, f: (layer, f, 0)),
            pl.BlockSpec((1, d), lambda i, f: (0, 0)),
        ] + cast_in,
        out_specs=[pl.BlockSpec((tm, d), lambda i, f: (i, 0))] + cast_out,
        scratch_shapes=[pltpu.VMEM((tm, d), BF16)],
        compiler_params=_cparams(("parallel", "arbitrary")),
        name="mlp",
    )(x, g.reshape(1, d), wu, wd, g_out.reshape(1, d), *cast)
    return res[0], res[1:]


def _in_proj_kernel(x_ref, g_ref, w_ref, lng_ref, lnb_ref, ws_ref, bst_ref, z_ref, oa_ref):
    row = lax.broadcasted_iota(jnp.int32, (CHUNK, CHUNK), 0)
    col = lax.broadcasted_iota(jnp.int32, (CHUNK, CHUNK), 1)
    causal = row >= col
    w_tril = [jnp.where(causal, ws_ref[gi], 0.0).astype(BF16) for gi in range(A_GROUPS)]

    def gate_inputs(za):
        u = _gelu_tanh(za[:, 0:A_WIDTH])
        v = _gelu_tanh(za[:, A_WIDTH:2 * A_WIDTH])
        vc = v - jnp.mean(v, axis=-1, keepdims=True)
        vn = vc * lax.rsqrt(jnp.mean(vc * vc, axis=-1, keepdims=True) + EPS)
        return u, (vn * lng_ref[...] + lnb_ref[...]).astype(BF16)

    def gate_store(r0, u, vn):
        for gi in range(A_GROUPS):
            cols = slice(gi * HEAD_DIM, (gi + 1) * HEAD_DIM)
            sv = _dot(w_tril[gi], vn[:, cols]) + bst_ref[:, gi:gi + 1]
            oa_ref[r0:r0 + CHUNK, cols] = (u[:, cols] * sv).astype(oa_ref.dtype)

    half = x_ref.shape[0] // 2
    pending = []
    for c in range(2):
        r = slice(c * half, (c + 1) * half)
        hn = _rms_rows(x_ref[r, :], g_ref[...]).astype(BF16)
        za = _dot(hn, w_ref[:, 0:Z_OFF])
        for r0, u, vn in pending:
            gate_store(r0, u, vn)
        z_ref[r, :] = _dot(hn, w_ref[:, Z_OFF:IN_PAD]).astype(z_ref.dtype)
        pending = [(c * half + k * CHUNK,) + gate_inputs(za[k * CHUNK:(k + 1) * CHUNK, :])
                   for k in range(half // CHUNK)]
    for r0, u, vn in pending:
        gate_store(r0, u, vn)


def in_proj(x, g, w, layer, ln_g, ln_b, w_s, b_s, *, tm):
    m, k = x.shape
    return pl.pallas_call(
        _in_proj_kernel,
        out_shape=[jax.ShapeDtypeStruct((m, Z_WIDTH), BF16), jax.ShapeDtypeStruct((m, A_WIDTH), BF16)],
        grid=(m // tm,),
        in_specs=[
            pl.BlockSpec((tm, k), lambda i: (i, 0)),
            pl.BlockSpec((1, k), lambda i: (0, 0)),
            pl.BlockSpec((None, k, IN_PAD), lambda i: (layer, 0, 0), pipeline_mode=pl.Buffered(1)),
            pl.BlockSpec((1, A_WIDTH), lambda i: (0, 0)),
            pl.BlockSpec((1, A_WIDTH), lambda i: (0, 0)),
            pl.BlockSpec((A_GROUPS, CHUNK, CHUNK), lambda i: (0, 0, 0)),
            pl.BlockSpec((CHUNK, A_GROUPS), lambda i: (0, 0)),
        ],
        out_specs=[pl.BlockSpec((tm, Z_WIDTH), lambda i: (i, 0)), pl.BlockSpec((tm, A_WIDTH), lambda i: (i, 0))],
        compiler_params=_cparams(("parallel",)),
        name="in_proj",
    )(x, g.reshape(1, k), w, ln_g.reshape(1, A_WIDTH), ln_b.reshape(1, A_WIDTH), w_s, b_s.T)


DIL_CH = 128
DIL_KW = 2 * DIL_CH
DIL_UNITS = 8


def _dilated_kernel(q_in, k_in, v_in, o_ref, q_ref, k_ref, v_ref, m_ref, acc_ref, l_ref):
    ch, kw = DIL_CH, DIL_KW
    seq = q_in.shape[0]
    for src, dst in ((q_in, q_ref), (k_in, k_ref), (v_in, v_ref)):
        def stage(c, carry, src=src, dst=dst):
            r0 = pl.multiple_of(c * NORM_ROWS, NORM_ROWS)
            dst[pl.ds(r0, NORM_ROWS), :] = src[pl.ds(r0, NORM_ROWS), :].astype(F32)
            return carry
        lax.fori_loop(0, seq // NORM_ROWS, stage, 0)
    qscale = (HEAD_DIM ** -0.5) * LOG2E
    row = lax.broadcasted_iota(jnp.int32, (ch, kw), 0)
    col = lax.broadcasted_iota(jnp.int32, (ch, kw), 1)
    ones = jnp.ones((kw, HEAD_DIM), BF16)
    patterns = sorted(DILATED_PAIRS, key=lambda wd: -wd[1])

    def rows(start, n, dil):
        return pl.ds(start, n) if dil == 1 else pl.ds(start, n, stride=dil)

    def band_bias(first, span):
        dist = row - col if first else ch + row - col
        return jnp.where((dist >= 0) & (dist <= span), 0.0, NEG_INF)

    def run_units(units, dil, span, phase):
        windows = {}
        scores = []
        for q0, k0, first, wid in units:
            if wid not in windows:
                k = k_ref[rows(k0, kw, dil), :].astype(BF16)
                v = v_ref[rows(k0, kw, dil), :].astype(BF16)
                windows[wid] = (k, jnp.concatenate([v, ones], axis=1))
            q = (q_ref[rows(q0, ch, dil), :] * qscale).astype(BF16)
            scores.append(_dot_nt(q, windows[wid][0]) + band_bias(first, span))
        for (q0, k0, first, wid), s in zip(units, scores):
            qrows = rows(q0, ch, dil)
            m_cur = jnp.max(s, axis=-1, keepdims=True)
            if phase == "first":
                m_new = jnp.broadcast_to(m_cur, (ch, LANE))
            else:
                m_old = m_ref[qrows, :]
                m_new = jnp.maximum(m_old, m_cur)
                alpha = jnp.exp2(m_old - m_new)
            p = jnp.exp2((s - jnp.concatenate([m_new] * (kw // LANE), axis=1)).astype(BF16))
            pv = _dot(p, windows[wid][1])
            acc, l = pv[:, 0:HEAD_DIM], pv[:, HEAD_DIM:2 * HEAD_DIM]
            if phase != "first":
                acc = alpha * acc_ref[qrows, :] + acc
                l = alpha * l_ref[qrows, :] + l
            if phase == "last":
                o_ref[pl.ds(pl.multiple_of(q0, ch), ch), :] = (acc / l).astype(o_ref.dtype)
            else:
                m_ref[qrows, :] = m_new
                acc_ref[qrows, :] = acc
                l_ref[qrows, :] = l

    for idx, (window, dil) in enumerate(patterns):
        phase = "first" if idx == 0 else ("last" if idx == len(patterns) - 1 else "mid")
        assert phase != "last" or dil == 1
        span = window // dil
        assert span <= ch
        n_units = seq // dil // ch
        per_class = min(n_units, max(DIL_UNITS // dil, 2))
        n_classes = DIL_UNITS // per_class
        assert n_units % per_class == 0 and dil % n_classes == 0

        def group(r0, j0, head, dil=dil, span=span, phase=phase, per_class=per_class, n_classes=n_classes):
            units = []
            for rc in range(n_classes):
                r = r0 + rc
                for u in range(per_class):
                    first = head and u == 0
                    q0 = r + dil * ch * (j0 + u)
                    k0 = r if (head and u <= 1) else r + dil * ch * (j0 + u - 1)
                    wid = (rc, 0) if (head and u <= 1) else (rc, u)
                    units.append((q0, k0, first, wid))
            run_units(units, dil, span, phase)

        def class_loop(g, carry, group=group, per_class=per_class, n_classes=n_classes, n_units=n_units):
            r0 = g * n_classes
            group(r0, 0, True)
            if n_units > per_class:
                def tail(jb, c):
                    group(r0, jb * per_class, False)
                    return c
                lax.fori_loop(1, n_units // per_class, tail, 0)
            return carry

        if dil // n_classes == 1:
            class_loop(0, 0)
        else:
            lax.fori_loop(0, dil // n_classes, class_loop, 0)


def dilated_attention(z, *, n_batch, seq):
    bw = B_HEADS * HEAD_DIM
    base = (COL_B - Z_OFF) // HEAD_DIM
    blk = (seq, HEAD_DIM)
    return pl.pallas_call(
        _dilated_kernel,
        out_shape=jax.ShapeDtypeStruct((n_batch * seq, bw), BF16),
        grid=(n_batch, B_HEADS),
        in_specs=[
            pl.BlockSpec(blk, lambda b, h: (b, base + h)),
            pl.BlockSpec(blk, lambda b, h: (b, base + B_HEADS + h)),
            pl.BlockSpec(blk, lambda b, h: (b, base + 2 * B_HEADS + h)),
        ],
        out_specs=pl.BlockSpec(blk, lambda b, h: (b, h)),
        scratch_shapes=[pltpu.VMEM(blk, F32)] * 6,
        compiler_params=_cparams(("parallel", "parallel")),
        name="dilated_attention",
    )(z, z, z)


N_SUB = 256
N_CMP = 2 * C_KV_HEADS


def _compress_kernel(x_in, pos_ref, w1_ref, w2_ref, o_ref, x_ref):
    x_ref[...] = x_in[...].astype(F32)
    p = jnp.zeros((N_SUB, HEAD_DIM), F32)
    q = jnp.zeros((N_SUB, HEAD_DIM), F32)
    for i in range(CMP_STRIDE):
        a = x_ref[pl.ds(i, N_SUB, stride=CMP_STRIDE), :]
        top = (a + pos_ref[i:i + 1, :]).astype(BF16)
        bot = (a + pos_ref[CMP_STRIDE + i:CMP_STRIDE + i + 1, :]).astype(BF16)
        p = p + _dot(top, w1_ref[i * HEAD_DIM:(i + 1) * HEAD_DIM, :])
        q = q + _dot(bot, w1_ref[(CMP_STRIDE + i) * HEAD_DIM:(CMP_STRIDE + i + 1) * HEAD_DIM, :])
    h = _gelu_tanh(p + pltpu.roll(q, N_SUB - 1, 0))
    o_ref[...] = _dot(h.astype(BF16), w2_ref[...]).astype(o_ref.dtype)


def compress(z, pos, w1, w2, layer, *, n_batch, seq):
    return pl.pallas_call(
        _compress_kernel,
        out_shape=jax.ShapeDtypeStruct((n_batch * N_CMP * N_SUB, HEAD_DIM), BF16),
        grid=(n_batch, N_CMP),
        in_specs=[
            pl.BlockSpec((seq, HEAD_DIM), lambda b, n: (b, (COL_CMP - Z_OFF) // HEAD_DIM + n)),
            pl.BlockSpec((CMP_LEN, HEAD_DIM), lambda b, n: (0, 0)),
            pl.BlockSpec((None, None, CMP_LEN * HEAD_DIM, HEAD_DIM), lambda b, n: (layer, n // C_KV_HEADS, 0, 0)),
            pl.BlockSpec((None, None, HEAD_DIM, HEAD_DIM), lambda b, n: (layer, n // C_KV_HEADS, 0, 0)),
        ],
        out_specs=pl.BlockSpec((N_SUB, HEAD_DIM), lambda b, n: (b * N_CMP + n, 0)),
        scratch_shapes=[pltpu.VMEM((seq, HEAD_DIM), F32)],
        compiler_params=_cparams(("parallel", "parallel")),
        name="nsa_compress",
    )(z, pos, w1, w2)


N_SEL = 64


def _nsa_kernel(*refs, n_cast):
    q_ref, kc_ref, vc_ref, ovt_ref, ks_ref, vs_ref, kw_ref, vw_ref, gate_ref = refs[:9]
    cast_in = refs[9:9 + n_cast]
    o_ref = refs[9 + n_cast]
    cast_out = refs[10 + n_cast:10 + 2 * n_cast]
    (qs_ref, kbs_ref, vbs_ref, kbw_ref, vbw_ref, ms_ref, as_ref, mw_ref, aw_ref, oc_ref,
     sc_ref) = refs[10 + 2 * n_cast:]
    _cast_slices(cast_in, cast_out)
    tq, tk, ch = ATT_TQ, ATT_TK, ATT_CH
    n_rep = C_GROUP
    n_sub = tq // ch
    n_ch = n_rep * n_sub
    seq = ks_ref.shape[0]
    i = pl.program_id(2)
    q0 = i * tq

    @pl.when(i == 0)
    def _():
        def prep(c, carry):
            rows = pl.ds(pl.multiple_of(c * tk, tk), tk)
            blk = (c * tk + lax.broadcasted_iota(jnp.int32, (tk, LANE), 0)) >> SEL_SHIFT
            lane = lax.broadcasted_iota(jnp.int32, (tk, LANE), 1)
            ones = jnp.ones((tk, HEAD_DIM), BF16)
            kbs_ref[rows, 0:HEAD_DIM] = ks_ref[rows, :]
            kbs_ref[rows, HEAD_DIM:HEAD_DIM + LANE] = jnp.where(blk == lane, 1.0, 0.0).astype(BF16)
            vbs_ref[rows, 0:HEAD_DIM] = vs_ref[rows, :]
            vbs_ref[rows, HEAD_DIM:2 * HEAD_DIM] = ones
            kbw_ref[rows, :] = kw_ref[rows, :]
            vbw_ref[rows, 0:HEAD_DIM] = vw_ref[rows, :]
            vbw_ref[rows, HEAD_DIM:2 * HEAD_DIM] = ones
            return carry

        lax.fori_loop(0, seq // tk, prep, 0)

    qscale = (HEAD_DIM ** -0.5) * LOG2E
    for r in range(n_rep):
        qs_ref[r * tq:(r + 1) * tq, 0:HEAD_DIM] = (q_ref[:, r * HEAD_DIM:(r + 1) * HEAD_DIM].astype(F32)
                                                   * qscale).astype(BF16)

    for m_ref, accl_ref in ((ms_ref, as_ref), (mw_ref, aw_ref)):
        m_ref[...] = jnp.full(m_ref.shape, NEG_INF, F32)
        accl_ref[...] = jnp.zeros(accl_ref.shape, F32)

    q_idx = lax.broadcasted_iota(jnp.int32, (tq, tk), 0)
    k_idx = lax.broadcasted_iota(jnp.int32, (tq, tk), 1)

    def tile_bias(off, max_dist):
        dist = off * tk + q_idx - k_idx
        if off == 0:
            ok = dist >= 0
        elif max_dist is not None and (off + 1) * tk - 1 > max_dist:
            ok = dist <= max_dist
        else:
            return None
        return jnp.where(ok, 0.0, NEG_INF)

    def issue(tiles, load_q, kb_ref):
        scores = []
        for kt, tbias, n_tiles in tiles:
            k0 = pl.multiple_of(kt * tk, tk)
            scores.append((_dot_nt(load_q(), kb_ref[pl.ds(k0, n_tiles * tk), :]), k0, tbias))
        return scores

    def finish(scores, vb_ref, m_ref, accl_ref):
        for s_tile, k0, tbias in scores:
            pts, alphas = [], []
            for c in range(n_ch):
                rows = slice(c * ch, (c + 1) * ch)
                s = s_tile[rows, :]
                if tbias is not None:
                    b0 = (c % n_sub) * ch
                    s = s + tbias[b0:b0 + ch, :]
                m_old = m_ref[rows, :]
                m_new = jnp.maximum(m_old, jnp.max(s, axis=-1, keepdims=True))
                alphas.append(jnp.exp2(m_old - m_new))
                width = s.shape[1]
                pts.append(jnp.exp2((s - jnp.concatenate([m_new] * (width // LANE), axis=1)).astype(BF16)))
                m_ref[rows, :] = m_new
            pv = _dot(jnp.concatenate(pts, axis=0), vb_ref[pl.ds(k0, width), :])
            for c in range(n_ch):
                rows = slice(c * ch, (c + 1) * ch)
                alpha2 = jnp.concatenate([alphas[c], alphas[c]], axis=1)
                accl_ref[rows, :] = alpha2 * accl_ref[rows, :] + pv[rows, :]

    max_dist = WIN_LEN - 1
    n_prev = -(-max_dist // tk)
    assert n_prev == 2
    in_seq = jnp.where(q_idx + k_idx >= 0, 0.0, NEG_INF)
    far = jnp.where(i >= 2, tile_bias(2, max_dist), in_seq) + jnp.where(i >= 1, 0.0, NEG_INF)
    near = in_seq + jnp.where(i >= 2, 0.0, NEG_INF)
    win_tiles = [(jnp.maximum(i - 2, 0), jnp.concatenate([far, near], axis=1), 2),
                 (i, tile_bias(0, max_dist), 1)]
    win_scores = issue(win_tiles, lambda: qs_ref[:, 0:HEAD_DIM], kbw_ref)

    s_all = _dot_nt(qs_ref[:, 0:HEAD_DIM], kc_ref[...])
    n_cols = lax.broadcasted_iota(jnp.int32, (ch, N_SUB), 1)
    bias, has_any = [], []
    for c in range(n_sub):
        t = q0 + c * ch + lax.broadcasted_iota(jnp.int32, (ch, N_SUB), 0)
        bias.append(jnp.where((n_cols * CMP_STRIDE + (CMP_LEN - 1)) <= t, 0.0, NEG_INF))
        has_any.append(t[:, 0:1] >= CMP_LEN - 1)
    ps = []
    psum = [jnp.zeros((ch, N_SUB), F32) for _ in range(n_sub)]
    for c in range(n_ch):
        s = s_all[c * ch:(c + 1) * ch, :] + bias[c % n_sub]
        e = jnp.exp2(s - jnp.max(s, axis=-1, keepdims=True))
        den = jnp.sum(e, axis=-1, keepdims=True)
        p = e * jnp.where(has_any[c % n_sub], 1.0 / den, 0.0)
        ps.append(p.astype(BF16))
        psum[c % n_sub] = psum[c % n_sub] + p
    oc_ref[...] = _dot(jnp.concatenate(ps, axis=0), vc_ref[...])
    psum = jnp.concatenate(psum, axis=0)

    hi = psum.astype(BF16)
    lo = (psum - hi.astype(F32)).astype(BF16)
    ovt = ovt_ref[...]
    imp_t = _dot_nt(ovt, hi) + _dot_nt(ovt, lo)
    jrow = lax.broadcasted_iota(jnp.int32, (LANE, tq), 0)
    tl = q0 + lax.broadcasted_iota(jnp.int32, (LANE, tq), 1)
    jt = tl >> SEL_SHIFT
    forced = (jrow == 0) | (jrow == jt) | (jrow == jt - 1)
    valid_s = jrow * SEL_LEN <= tl
    sc_ref[...] = jnp.where(forced, 1e4, jnp.where(valid_s, imp_t, -1.0))
    finish(win_scores, vbw_ref, mw_ref, aw_ref)
    n_grp = N_SEL // SUBLANE
    grp = [sc_ref[g * SUBLANE:(g + 1) * SUBLANE, :] for g in range(n_grp)]
    cnt = [jnp.zeros((SUBLANE, tq), F32) for _ in range(n_grp)]
    sub = lax.broadcasted_iota(jnp.int32, (SUBLANE, tq), 0)
    for j in range(N_SEL):
        row = sc_ref[j:j + 1, :]
        for g in range(n_grp):
            if g < j // SUBLANE:
                beats = row > grp[g]
            elif g > j // SUBLANE:
                beats = row >= grp[g]
            else:
                beats = (row > grp[g]) | ((row == grp[g]) & (sub > j % SUBLANE))
            cnt[g] = cnt[g] + jnp.where(beats, 1.0, 0.0)
    pen_t = [jnp.where(c < float(SEL_TOP), 0.0, NEG_INF) for c in cnt]
    pen_t = jnp.concatenate(pen_t + [jnp.full((LANE - N_SEL, tq), NEG_INF, F32)], axis=0)
    pen = pen_t.T.astype(BF16)
    for r in range(n_rep):
        qs_ref[r * tq:(r + 1) * tq, HEAD_DIM:HEAD_DIM + LANE] = pen

    def sel_steps(first, n_full, diagonal):
        tiles = [(first + u, None, 2) for u in range(0, n_full - n_full % 2, 2)]
        if n_full % 2:
            tiles.append((first + n_full - 1, None, 1))
        if diagonal:
            tiles.append((i, tile_bias(0, None), 1))
        finish(issue(tiles, lambda: qs_ref[...], kbs_ref), vbs_ref, ms_ref, as_ref)

    def sel_body(j, carry):
        sel_steps(SEL_UNROLL * j, SEL_UNROLL, False)
        return carry

    lax.fori_loop(0, i // SEL_UNROLL, sel_body, 0)

    for rem in range(SEL_UNROLL):
        @pl.when(i % SEL_UNROLL == rem)
        def _(rem=rem):
            sel_steps(i - rem, rem, True)

    g = 1.0 / (1.0 + jnp.exp(-gate_ref[...].astype(F32)))
    lane = lax.broadcasted_iota(jnp.int32, g.shape, 1)
    head0 = pl.program_id(1) * n_rep

    def gate(head, branch):
        return jnp.sum(jnp.where(lane == head * 3 + branch, g, 0.0), axis=-1, keepdims=True)

    for r in range(n_rep):
        rows = slice(r * tq, (r + 1) * tq)
        o = (gate(head0 + r, 0) * oc_ref[rows, :]
             + gate(head0 + r, 1) * (as_ref[rows, 0:HEAD_DIM] / as_ref[rows, HEAD_DIM:2 * HEAD_DIM])
             + gate(head0 + r, 2) * (aw_ref[rows, 0:HEAD_DIM] / aw_ref[rows, HEAD_DIM:2 * HEAD_DIM]))
        o_ref[:, r * HEAD_DIM:(r + 1) * HEAD_DIM] = o.astype(o_ref.dtype)


def nsa_attention(z, kvc, ovt, cast=(), cast_layer=0, *, n_batch, seq):
    tq = ATT_TQ
    nq = seq // tq
    n_rep = C_GROUP
    qw = n_rep * HEAD_DIM
    kv_blk = (seq, HEAD_DIM)
    sel0, win0 = (COL_SEL - Z_OFF) // HEAD_DIM, (COL_WIN - Z_OFF) // HEAD_DIM
    cast_in, cast_out, cast_shapes = _cast_specs(
        cast, cast_layer, lambda b, g, i: (b * C_KV_HEADS + g) * nq + i, n_batch * C_KV_HEADS * nq)
    res = pl.pallas_call(
        functools.partial(_nsa_kernel, n_cast=len(cast)),
        out_shape=[jax.ShapeDtypeStruct((n_batch * seq, C_HEADS * HEAD_DIM), BF16)] + cast_shapes,
        grid=(n_batch, C_KV_HEADS, nq),
        in_specs=[
            pl.BlockSpec((tq, qw), lambda b, g, i: (b * nq + i, (COL_CQ - Z_OFF) // qw + g)),
            pl.BlockSpec((N_SUB, HEAD_DIM), lambda b, g, i: (b * N_CMP + g, 0)),
            pl.BlockSpec((N_SUB, HEAD_DIM), lambda b, g, i: (b * N_CMP + C_KV_HEADS + g, 0)),
            pl.BlockSpec((LANE, N_SUB), lambda b, g, i: (0, 0)),
            pl.BlockSpec(kv_blk, lambda b, g, i: (b, sel0 + g)),
            pl.BlockSpec(kv_blk, lambda b, g, i: (b, sel0 + C_KV_HEADS + g)),
            pl.BlockSpec(kv_blk, lambda b, g, i: (b, win0 + g)),
            pl.BlockSpec(kv_blk, lambda b, g, i: (b, win0 + C_KV_HEADS + g)),
            pl.BlockSpec((tq, LANE), lambda b, g, i: (b * nq + i, (COL_GATE - Z_OFF) // LANE)),
        ] + cast_in,
        out_specs=[pl.BlockSpec((tq, qw), lambda b, g, i: (b * nq + i, g))] + cast_out,
        scratch_shapes=[
            pltpu.VMEM((n_rep * tq, HEAD_DIM + LANE), BF16),
            pltpu.VMEM((seq, HEAD_DIM + LANE), BF16),
            pltpu.VMEM((seq, 2 * HEAD_DIM), BF16),
            pltpu.VMEM((seq, HEAD_DIM), BF16),
            pltpu.VMEM((seq, 2 * HEAD_DIM), BF16),
            pltpu.VMEM((n_rep * tq, LANE), F32),
            pltpu.VMEM((n_rep * tq, 2 * HEAD_DIM), F32),
            pltpu.VMEM((n_rep * tq, LANE), F32),
            pltpu.VMEM((n_rep * tq, 2 * HEAD_DIM), F32),
            pltpu.VMEM((n_rep * tq, HEAD_DIM), F32),
            pltpu.VMEM((LANE, tq), F32),
        ],
        compiler_params=_cparams(("parallel", "parallel", "arbitrary")),
        name="nsa_attention",
    )(z, kvc, kvc, ovt, z, z, z, z, z, *cast)
    return res[0], res[1:]


def _overlap_t():
    n_c = N_SUB - 1
    c_start = np.arange(n_c) * CMP_STRIDE
    s_start = np.arange(N_SEL) * SEL_LEN
    ov = ((c_start[:, None] <= s_start[None, :] + SEL_LEN - 1)
          & (c_start[:, None] + CMP_LEN - 1 >= s_start[None, :])).astype(np.float32)
    out = np.zeros((LANE, N_SUB), np.float32)
    out[:N_SEL, :n_c] = ov.T
    return out


def _mix_out_kernel(x_ref, oa_ref, ob_ref, oc_ref, w_ref, o_ref):
    mix = jnp.concatenate([oa_ref[...], ob_ref[...], oc_ref[...]], axis=1)
    o_ref[...] = x_ref[...] + _dot(mix, w_ref[...])


def mix_out(x, out_a, out_b, out_c, w, layer, *, tm):
    m, n = x.shape
    return pl.pallas_call(
        _mix_out_kernel,
        out_shape=jax.ShapeDtypeStruct((m, n), F32),
        grid=(m // tm,),
        in_specs=[
            pl.BlockSpec((tm, n), lambda i: (i, 0)),
            pl.BlockSpec((tm, out_a.shape[1]), lambda i: (i, 0)),
            pl.BlockSpec((tm, out_b.shape[1]), lambda i: (i, 0)),
            pl.BlockSpec((tm, out_c.shape[1]), lambda i: (i, 0)),
            pl.BlockSpec((None,) + w.shape[1:], lambda i: (layer, 0, 0), pipeline_mode=pl.Buffered(1)),
        ],
        out_specs=pl.BlockSpec((tm, n), lambda i: (i, 0)),
        compiler_params=_cparams(("parallel",)),
        name="mix_out",
    )(x, out_a, out_b, out_c, w)


def _xattn_kernel(x_ref, g_ref, wq_ref, k_ref, v_ref, wo_ref, o_ref):
    n_mem = k_ref.shape[0]
    rows = x_ref.shape[0]
    ones = jnp.ones((n_mem, HEAD_DIM), BF16)
    kv = [(k_ref[:, h * HEAD_DIM:(h + 1) * HEAD_DIM].astype(BF16),
           jnp.concatenate([v_ref[:, h * HEAD_DIM:(h + 1) * HEAD_DIM].astype(BF16), ones], axis=1))
          for h in range(X_HEADS)]
    for c in range(rows // NORM_ROWS):
        r = slice(c * NORM_ROWS, (c + 1) * NORM_ROWS)
        x = x_ref[r, :]
        hn = _rms_rows(x, g_ref[...]).astype(BF16)
        q = _dot(hn, wq_ref[...]) * ((HEAD_DIM ** -0.5) * LOG2E)
        outs = []
        for h in range(X_HEADS):
            s = _dot_nt(q[:, h * HEAD_DIM:(h + 1) * HEAD_DIM].astype(BF16), kv[h][0])
            e = jnp.exp2((s - jnp.max(s, axis=-1, keepdims=True)).astype(BF16))
            pv = _dot(e, kv[h][1])
            outs.append((pv[:, 0:HEAD_DIM] / pv[:, HEAD_DIM:2 * HEAD_DIM]).astype(BF16))
        o_ref[r, :] = x + _dot(jnp.concatenate(outs, axis=1), wo_ref[...])


def cross_attention_block(x, g, wq, kv, wo, layer, *, n_batch, seq, n_mem, tm):
    nq = seq // tm
    d = x.shape[1]
    return pl.pallas_call(
        _xattn_kernel,
        out_shape=jax.ShapeDtypeStruct(x.shape, F32),
        grid=(n_batch, nq),
        in_specs=[
            pl.BlockSpec((tm, d), lambda b, i: (b * nq + i, 0)),
            pl.BlockSpec((1, d), lambda b, i: (0, 0)),
            pl.BlockSpec((None, d, X_WIDTH), lambda b, i: (layer, 0, 0), pipeline_mode=pl.Buffered(1)),
            pl.BlockSpec((n_mem, X_WIDTH), lambda b, i: (b, 0)),
            pl.BlockSpec((n_mem, X_WIDTH), lambda b, i: (b, 1)),
            pl.BlockSpec((None, X_WIDTH, d), lambda b, i: (layer, 0, 0), pipeline_mode=pl.Buffered(1)),
        ],
        out_specs=pl.BlockSpec((tm, d), lambda b, i: (b * nq + i, 0)),
        compiler_params=_cparams(("parallel", "parallel")),
        name="cross_attention",
    )(x, g.reshape(1, d), wq, kv, kv, wo)


def kernel(x, mem, norm_mix, w_in, gmlp_ln_g, gmlp_ln_b, gmlp_w_s, gmlp_b_s, cmp_pos, cmp_k_w1, cmp_k_w2,
           cmp_v_w1, cmp_v_w2, w_out, norm_xattn, norm_mem, xattn_wq, xattn_wkv, xattn_wo, norm_mlp, w_up,
           w_down, final_norm):
    bsz, seq, d = x.shape
    n_mem = mem.shape[1]
    t = bsz * seq
    assert d == D_MODEL and seq // CMP_STRIDE == N_SUB and seq // SEL_LEN == N_SEL
    assert seq % (DIL_KW * max(dil for _, dil in DILATED_PAIRS)) == 0 and seq % ATT_TQ == 0

    w_in_b = jnp.pad(w_in, ((0, 0), (0, 0), (0, IN_PAD - IN_WIDTH))).astype(BF16)
    big = (w_out, w_up, w_down)
    wq_b = xattn_wq.astype(BF16)
    wkv_b = xattn_wkv.astype(BF16)
    wo_b = xattn_wo.astype(BF16)
    cmp_w1 = jnp.stack([cmp_k_w1, cmp_v_w1], axis=1).astype(BF16)
    cmp_w2 = jnp.stack([cmp_k_w2, cmp_v_w2], axis=1).astype(BF16)
    ovt = jnp.asarray(_overlap_t()).astype(BF16)
    xf = x.reshape(t, d)
    memf = mem.reshape(bsz * n_mem, d)

    for l in range(DEPTH):
        z, out_a = in_proj(xf, norm_mix[l], w_in_b, l, gmlp_ln_g[l], gmlp_ln_b[l], gmlp_w_s[l], gmlp_b_s[l],
                           tm=TM_PROJ)

        out_b = dilated_attention(z, n_batch=bsz, seq=seq)

        kvc = compress(z, cmp_pos[l], cmp_w1, cmp_w2, l, n_batch=bsz, seq=seq)
        out_c, cast0 = nsa_attention(z, kvc, ovt, big if l == 0 else (), 0, n_batch=bsz, seq=seq)
        if l == 0:
            w_out_l, w_up_l, w_down_l = (w[None] for w in cast0)

        xf = mix_out(xf, out_a, out_b, out_c, w_out_l, 0, tm=TM_PROJ)

        kv = norm_matmul(memf, norm_mem[l], wkv_b, l, tm=TM_PROJ)
        xf = cross_attention_block(xf, norm_xattn[l], wq_b, kv, wo_b, l, n_batch=bsz, seq=seq, n_mem=n_mem,
                                   tm=TM_XATTN)

        nxt = () if l == DEPTH - 1 else big
        xf, nxt_b = mlp(xf, norm_mlp[l], w_up_l, w_down_l, 0, final_norm, nxt, l + 1, tm=TM_MLP, tf=TF_MLP,
                        norm_out=(l == DEPTH - 1))
        if nxt_b:
            w_out_l, w_up_l, w_down_l = (w[None] for w in nxt_b)

    return xf.reshape(bsz, seq, d)
```

```python
import functools
import math

import numpy as np
import jax
import jax.numpy as jnp
from jax import lax
from jax.experimental import pallas as pl
from jax.experimental.pallas import tpu as pltpu

F32 = jnp.float32
BF16 = jnp.bfloat16

D_MODEL = 2048
DEPTH = 4
HEAD_DIM = 128
A_GROUPS = 4
A_WIDTH = A_GROUPS * HEAD_DIM
CHUNK = 128
B_HEADS = 4
DILATED_PAIRS = ((128, 1), (512, 4), (2048, 16))
C_HEADS = 8
C_KV_HEADS = 2
C_GROUP = C_HEADS // C_KV_HEADS
CMP_LEN = 32
CMP_STRIDE = 16
SEL_LEN = 64
SEL_SHIFT = SEL_LEN.bit_length() - 1
SEL_TOP = 16
WIN_LEN = 512
X_HEADS = 4
X_WIDTH = X_HEADS * HEAD_DIM
EPS = 1e-6
NEG_INF = -1e30
LOG2E = math.log2(math.e)

COL_B = 2 * A_WIDTH
COL_CQ = COL_B + 3 * B_HEADS * HEAD_DIM
COL_CMP = COL_CQ + C_HEADS * HEAD_DIM
COL_SEL = COL_CMP + 2 * C_KV_HEADS * HEAD_DIM
COL_WIN = COL_SEL + 2 * C_KV_HEADS * HEAD_DIM
COL_GATE = COL_WIN + 2 * C_KV_HEADS * HEAD_DIM
IN_WIDTH = COL_GATE + 3 * C_HEADS
IN_PAD = 5248
Z_OFF = COL_B
Z_WIDTH = IN_PAD - Z_OFF

LANE = 128
SUBLANE = 8
VMEM_LIMIT = 56 * 1024 * 1024

TM_PROJ = 512
TM_XATTN = 1024
TM_MLP = 1024
TF_MLP = 512
NORM_ROWS = 256
ATT_TQ = 256
ATT_TK = 256
ATT_CH = 128
SEL_UNROLL = 8
SEL_GROUP = 4
CAST_CHUNKS = 128


def _cparams(sem):
    return pltpu.CompilerParams(dimension_semantics=sem, vmem_limit_bytes=VMEM_LIMIT)


def _rms_rows(x, g):
    ms = jnp.mean(x * x, axis=-1, keepdims=True)
    return x * lax.rsqrt(ms + EPS) * g


def _gelu_tanh(x):
    c = math.sqrt(2.0 / math.pi)
    return x * (0.5 * (1.0 + jnp.tanh(c * (x + 0.044715 * (x * x * x)))))


def _dot(a, b):
    return jnp.dot(a, b, preferred_element_type=F32)


def _dot_nt(a, b):
    return lax.dot_general(a, b, (((1,), (1,)), ((), ())), preferred_element_type=F32)


def _norm_into(h_ref, x_ref, g_ref):
    rows = x_ref.shape[0]

    def body(c, carry):
        r0 = pl.multiple_of(c * NORM_ROWS, NORM_ROWS)
        x = x_ref[pl.ds(r0, NORM_ROWS), :]
        h_ref[pl.ds(r0, NORM_ROWS), :] = _rms_rows(x, g_ref[...]).astype(h_ref.dtype)
        return carry

    lax.fori_loop(0, rows // NORM_ROWS, body, 0)


def _norm_matmul_kernel(x_ref, g_ref, w_ref, o_ref):
    half = x_ref.shape[0] // 2
    for c in range(2):
        r = slice(c * half, (c + 1) * half)
        hn = _rms_rows(x_ref[r, :], g_ref[...]).astype(BF16)
        o_ref[r, :] = _dot(hn, w_ref[...]).astype(o_ref.dtype)


def norm_matmul(x, g, w, layer, *, tm, out_dtype=F32):
    m, k = x.shape
    n = w.shape[2]
    return pl.pallas_call(
        _norm_matmul_kernel,
        out_shape=jax.ShapeDtypeStruct((m, n), out_dtype),
        grid=(m // tm,),
        in_specs=[
            pl.BlockSpec((tm, k), lambda i: (i, 0)),
            pl.BlockSpec((1, k), lambda i: (0, 0)),
            pl.BlockSpec((None, k, n), lambda i: (layer, 0, 0), pipeline_mode=pl.Buffered(1)),
        ],
        out_specs=pl.BlockSpec((tm, n), lambda i: (i, 0)),
        compiler_params=_cparams(("parallel",)),
        name="norm_matmul",
    )(x, g.reshape(1, k), w)


def _cast_specs(cast, cast_layer, step_of, n_steps):
    per_chunk = n_steps // CAST_CHUNKS
    in_specs, out_specs, out_shapes = [], [], []
    for w in cast:
        rows = w.shape[1] // CAST_CHUNKS
        in_specs.append(pl.BlockSpec((None, rows, w.shape[2]),
                                     lambda *ids: (cast_layer, step_of(*ids) // per_chunk, 0)))
        out_specs.append(pl.BlockSpec((rows, w.shape[2]), lambda *ids: (step_of(*ids) // per_chunk, 0)))
        out_shapes.append(jax.ShapeDtypeStruct(w.shape[1:], BF16))
    return in_specs, out_specs, out_shapes


def _cast_slices(cast_in, cast_out):
    for src, dst in zip(cast_in, cast_out):
        dst[...] = src[...].astype(dst.dtype)


def _mlp_kernel(*refs, norm_out, n_cast):
    x_ref, g_ref, wu_ref, wd_ref, gout_ref = refs[:5]
    cast_in = refs[5:5 + n_cast]
    o_ref = refs[5 + n_cast]
    cast_out = refs[6 + n_cast:6 + 2 * n_cast]
    h_ref = refs[6 + 2 * n_cast]
    f = pl.program_id(1)

    def step():
        a = _dot(h_ref[...], wu_ref[...])
        a = jnp.square(jnp.maximum(a, 0.0)).astype(BF16)
        o_ref[...] += _dot(a, wd_ref[...])
        _cast_slices(cast_in, cast_out)

    @pl.when(f == 0)
    def _():
        for c in range(x_ref.shape[0] // NORM_ROWS):
            r = slice(c * NORM_ROWS, (c + 1) * NORM_ROWS)
            x = x_ref[r, :]
            h_ref[r, :] = _rms_rows(x, g_ref[...]).astype(h_ref.dtype)
            o_ref[r, :] = x
        step()

    @pl.when(f != 0)
    def _():
        step()

    if norm_out:
        @pl.when(f == pl.num_programs(1) - 1)
        def _():
            _norm_into(o_ref, o_ref, gout_ref)


def mlp(x, g, wu, wd, layer, g_out, cast=(), cast_layer=0, *, tm, tf, norm_out):
    m, d = x.shape
    ff = wu.shape[2]
    n_f = ff // tf
    cast_in, cast_out, cast_shapes = _cast_specs(cast, cast_layer, lambda i, f: i * n_f + f, (m // tm) * n_f)
    res = pl.pallas_call(
        functools.partial(_mlp_kernel, norm_out=norm_out, n_cast=len(cast)),
        out_shape=[jax.ShapeDtypeStruct((m, d), F32)] + cast_shapes,
        grid=(m // tm, n_f),
        in_specs=[
            pl.BlockSpec((tm, d), lambda i, f: (i, 0)),
            pl.BlockSpec((1, d), lambda i, f: (0, 0)),
            pl.BlockSpec((None, d, tf), lambda i, f: (layer, 0, f)),
            pl.BlockSpec((None, tf, d), lambda i, f: (layer, f, 0)),
            pl.BlockSpec((1, d), lambda i, f: (0, 0)),
        ] + cast_in,
        out_specs=[pl.BlockSpec((tm, d), lambda i, f: (i, 0))] + cast_out,
        scratch_shapes=[pltpu.VMEM((tm, d), BF16)],
        compiler_params=_cparams(("parallel", "arbitrary")),
        name="mlp",
    )(x, g.reshape(1, d), wu, wd, g_out.reshape(1, d), *cast)
    return res[0], res[1:]


def _in_proj_kernel(x_ref, g_ref, w_ref, lng_ref, lnb_ref, ws_ref, bst_ref, z_ref, oa_ref):
    row = lax.broadcasted_iota(jnp.int32, (CHUNK, CHUNK), 0)
    col = lax.broadcasted_iota(jnp.int32, (CHUNK, CHUNK), 1)
    causal = row >= col
    w_tril = [jnp.where(causal, ws_ref[gi], 0.0).astype(BF16) for gi in range(A_GROUPS)]

    def gate_inputs(za):
        u = _gelu_tanh(za[:, 0:A_WIDTH])
        v = _gelu_tanh(za[:, A_WIDTH:2 * A_WIDTH])
        vc = v - jnp.mean(v, axis=-1, keepdims=True)
        vn = vc * lax.rsqrt(jnp.mean(vc * vc, axis=-1, keepdims=True) + EPS)
        return u, (vn * lng_ref[...] + lnb_ref[...]).astype(BF16)

    def gate_store(r0, u, vn):
        for gi in range(A_GROUPS):
            cols = slice(gi * HEAD_DIM, (gi + 1) * HEAD_DIM)
            sv = _dot(w_tril[gi], vn[:, cols]) + bst_ref[:, gi:gi + 1]
            oa_ref[r0:r0 + CHUNK, cols] = (u[:, cols] * sv).astype(oa_ref.dtype)

    half = x_ref.shape[0] // 2
    pending = []
    for c in range(2):
        r = slice(c * half, (c + 1) * half)
        hn = _rms_rows(x_ref[r, :], g_ref[...]).astype(BF16)
        za = _dot(hn, w_ref[:, 0:Z_OFF])
        for r0, u, vn in pending:
            gate_store(r0, u, vn)
        z_ref[r, :] = _dot(hn, w_ref[:, Z_OFF:IN_PAD]).astype(z_ref.dtype)
        pending = [(c * half + k * CHUNK,) + gate_inputs(za[k * CHUNK:(k + 1) * CHUNK, :])
                   for k in range(half // CHUNK)]
    for r0, u, vn in pending:
        gate_store(r0, u, vn)


def in_proj(x, g, w, layer, ln_g, ln_b, w_s, b_s, *, tm):
    m, k = x.shape
    return pl.pallas_call(
        _in_proj_kernel,
        out_shape=[jax.ShapeDtypeStruct((m, Z_WIDTH), BF16), jax.ShapeDtypeStruct((m, A_WIDTH), BF16)],
        grid=(m // tm,),
        in_specs=[
            pl.BlockSpec((tm, k), lambda i: (i, 0)),
            pl.BlockSpec((1, k), lambda i: (0, 0)),
            pl.BlockSpec((None, k, IN_PAD), lambda i: (layer, 0, 0), pipeline_mode=pl.Buffered(1)),
            pl.BlockSpec((1, A_WIDTH), lambda i: (0, 0)),
            pl.BlockSpec((1, A_WIDTH), lambda i: (0, 0)),
            pl.BlockSpec((A_GROUPS, CHUNK, CHUNK), lambda i: (0, 0, 0)),
            pl.BlockSpec((CHUNK, A_GROUPS), lambda i: (0, 0)),
        ],
        out_specs=[pl.BlockSpec((tm, Z_WIDTH), lambda i: (i, 0)), pl.BlockSpec((tm, A_WIDTH), lambda i: (i, 0))],
        compiler_params=_cparams(("parallel",)),
        name="in_proj",
    )(x, g.reshape(1, k), w, ln_g.reshape(1, A_WIDTH), ln_b.reshape(1, A_WIDTH), w_s, b_s.T)


DIL_CH = 128
DIL_KW = 2 * DIL_CH
DIL_UNITS = 8


def _dilated_kernel(q_in, k_in, v_in, o_ref, q_ref, k_ref, v_ref, m_ref, acc_ref, l_ref):
    ch, kw = DIL_CH, DIL_KW
    seq = q_in.shape[0]
    for src, dst in ((q_in, q_ref), (k_in, k_ref), (v_in, v_ref)):
        def stage(c, carry, src=src, dst=dst):
            r0 = pl.multiple_of(c * NORM_ROWS, NORM_ROWS)
            dst[pl.ds(r0, NORM_ROWS), :] = src[pl.ds(r0, NORM_ROWS), :].astype(F32)
            return carry
        lax.fori_loop(0, seq // NORM_ROWS, stage, 0)
    qscale = (HEAD_DIM ** -0.5) * LOG2E
    row = lax.broadcasted_iota(jnp.int32, (ch, kw), 0)
    col = lax.broadcasted_iota(jnp.int32, (ch, kw), 1)
    ones = jnp.ones((kw, HEAD_DIM), BF16)
    patterns = sorted(DILATED_PAIRS, key=lambda wd: -wd[1])

    def rows(start, n, dil):
        return pl.ds(start, n) if dil == 1 else pl.ds(start, n, stride=dil)

    def band_bias(first, span):
        dist = row - col if first else ch + row - col
        return jnp.where((dist >= 0) & (dist <= span), 0.0, NEG_INF)

    def run_units(units, dil, span, phase):
        windows = {}
        scores = []
        for q0, k0, first, wid in units:
            if wid not in windows:
                k = k_ref[rows(k0, kw, dil), :].astype(BF16)
                v = v_ref[rows(k0, kw, dil), :].astype(BF16)
                windows[wid] = (k, jnp.concatenate([v, ones], axis=1))
            q = (q_ref[rows(q0, ch, dil), :] * qscale).astype(BF16)
            scores.append(_dot_nt(q, windows[wid][0]) + band_bias(first, span))
        for (q0, k0, first, wid), s in zip(units, scores):
            qrows = rows(q0, ch, dil)
            m_cur = jnp.max(s, axis=-1, keepdims=True)
            if phase == "first":
                m_new = jnp.broadcast_to(m_cur, (ch, LANE))
            else:
                m_old = m_ref[qrows, :]
                m_new = jnp.maximum(m_old, m_cur)
                alpha = jnp.exp2(m_old - m_new)
            p = jnp.exp2((s - jnp.concatenate([m_new] * (kw // LANE), axis=1)).astype(BF16))
            pv = _dot(p, windows[wid][1])
            acc, l = pv[:, 0:HEAD_DIM], pv[:, HEAD_DIM:2 * HEAD_DIM]
            if phase != "first":
                acc = alpha * acc_ref[qrows, :] + acc
                l = alpha * l_ref[qrows, :] + l
            if phase == "last":
                o_ref[pl.ds(pl.multiple_of(q0, ch), ch), :] = (acc / l).astype(o_ref.dtype)
            else:
                m_ref[qrows, :] = m_new
                acc_ref[qrows, :] = acc
                l_ref[qrows, :] = l

    for idx, (window, dil) in enumerate(patterns):
        phase = "first" if idx == 0 else ("last" if idx == len(patterns) - 1 else "mid")
        assert phase != "last" or dil == 1
        span = window // dil
        assert span <= ch
        n_units = seq // dil // ch
        per_class = min(n_units, max(DIL_UNITS // dil, 2))
        n_classes = DIL_UNITS // per_class
        assert n_units % per_class == 0 and dil % n_classes == 0

        def group(r0, j0, head, dil=dil, span=span, phase=phase, per_class=per_class, n_classes=n_classes):
            units = []
            for rc in range(n_classes):
                r = r0 + rc
                for u in range(per_class):
                    first = head and u == 0
                    q0 = r + dil * ch * (j0 + u)
                    k0 = r if (head and u <= 1) else r + dil * ch * (j0 + u - 1)
                    wid = (rc, 0) if (head and u <= 1) else (rc, u)
                    units.append((q0, k0, first, wid))
            run_units(units, dil, span, phase)

        def class_loop(g, carry, group=group, per_class=per_class, n_classes=n_classes, n_units=n_units):
            r0 = g * n_classes
            group(r0, 0, True)
            if n_units > per_class:
                def tail(jb, c):
                    group(r0, jb * per_class, False)
                    return c
                lax.fori_loop(1, n_units // per_class, tail, 0)
            return carry

        if dil // n_classes == 1:
            class_loop(0, 0)
        else:
            lax.fori_loop(0, dil // n_classes, class_loop, 0)


def dilated_attention(z, *, n_batch, seq):
    bw = B_HEADS * HEAD_DIM
    base = (COL_B - Z_OFF) // HEAD_DIM
    blk = (seq, HEAD_DIM)
    return pl.pallas_call(
        _dilated_kernel,
        out_shape=jax.ShapeDtypeStruct((n_batch * seq, bw), BF16),
        grid=(n_batch, B_HEADS),
        in_specs=[
            pl.BlockSpec(blk, lambda b, h: (b, base + h)),
            pl.BlockSpec(blk, lambda b, h: (b, base + B_HEADS + h)),
            pl.BlockSpec(blk, lambda b, h: (b, base + 2 * B_HEADS + h)),
        ],
        out_specs=pl.BlockSpec(blk, lambda b, h: (b, h)),
        scratch_shapes=[pltpu.VMEM(blk, F32)] * 6,
        compiler_params=_cparams(("parallel", "parallel")),
        name="dilated_attention",
    )(z, z, z)


N_SUB = 256
N_CMP = 2 * C_KV_HEADS


def _compress_kernel(x_in, pos_ref, w1_ref, w2_ref, o_ref, x_ref):
    x_ref[...] = x_in[...].astype(F32)
    p = jnp.zeros((N_SUB, HEAD_DIM), F32)
    q = jnp.zeros((N_SUB, HEAD_DIM), F32)
    for i in range(CMP_STRIDE):
        a = x_ref[pl.ds(i, N_SUB, stride=CMP_STRIDE), :]
        top = (a + pos_ref[i:i + 1, :]).astype(BF16)
        bot = (a + pos_ref[CMP_STRIDE + i:CMP_STRIDE + i + 1, :]).astype(BF16)
        p = p + _dot(top, w1_ref[i * HEAD_DIM:(i + 1) * HEAD_DIM, :])
        q = q + _dot(bot, w1_ref[(CMP_STRIDE + i) * HEAD_DIM:(CMP_STRIDE + i + 1) * HEAD_DIM, :])
    h = _gelu_tanh(p + pltpu.roll(q, N_SUB - 1, 0))
    o_ref[...] = _dot(h.astype(BF16), w2_ref[...]).astype(o_ref.dtype)


def compress(z, pos, w1, w2, layer, *, n_batch, seq):
    return pl.pallas_call(
        _compress_kernel,
        out_shape=jax.ShapeDtypeStruct((n_batch * N_CMP * N_SUB, HEAD_DIM), BF16),
        grid=(n_batch, N_CMP),
        in_specs=[
            pl.BlockSpec((seq, HEAD_DIM), lambda b, n: (b, (COL_CMP - Z_OFF) // HEAD_DIM + n)),
            pl.BlockSpec((CMP_LEN, HEAD_DIM), lambda b, n: (0, 0)),
            pl.BlockSpec((None, None, CMP_LEN * HEAD_DIM, HEAD_DIM), lambda b, n: (layer, n // C_KV_HEADS, 0, 0)),
            pl.BlockSpec((None, None, HEAD_DIM, HEAD_DIM), lambda b, n: (layer, n // C_KV_HEADS, 0, 0)),
        ],
        out_specs=pl.BlockSpec((N_SUB, HEAD_DIM), lambda b, n: (b * N_CMP + n, 0)),
        scratch_shapes=[pltpu.VMEM((seq, HEAD_DIM), F32)],
        compiler_params=_cparams(("parallel", "parallel")),
        name="nsa_compress",
    )(z, pos, w1, w2)


N_SEL = 64


def _nsa_kernel(*refs, n_cast):
    q_ref, kc_ref, vc_ref, ovt_ref, ks_ref, vs_ref, kw_ref, vw_ref, gate_ref = refs[:9]
    cast_in = refs[9:9 + n_cast]
    o_ref = refs[9 + n_cast]
    cast_out = refs[10 + n_cast:10 + 2 * n_cast]
    (qs_ref, kbs_ref, vbs_ref, kbw_ref, vbw_ref, ms_ref, as_ref, mw_ref, aw_ref, oc_ref,
     sc_ref) = refs[10 + 2 * n_cast:]
    _cast_slices(cast_in, cast_out)
    tq, tk, ch = ATT_TQ, ATT_TK, ATT_CH
    n_rep = C_GROUP
    n_sub = tq // ch
    n_ch = n_rep * n_sub
    seq = ks_ref.shape[0]
    i = pl.program_id(2)
    q0 = i * tq

    @pl.when(i == 0)
    def _():
        def prep(c, carry):
            rows = pl.ds(pl.multiple_of(c * tk, tk), tk)
            blk = (c * tk + lax.broadcasted_iota(jnp.int32, (tk, LANE), 0)) >> SEL_SHIFT
            lane = lax.broadcasted_iota(jnp.int32, (tk, LANE), 1)
            ones = jnp.ones((tk, HEAD_DIM), BF16)
            kbs_ref[rows, 0:HEAD_DIM] = ks_ref[rows, :]
            kbs_ref[rows, HEAD_DIM:HEAD_DIM + LANE] = jnp.where(blk == lane, 1.0, 0.0).astype(BF16)
            vbs_ref[rows, 0:HEAD_DIM] = vs_ref[rows, :]
            vbs_ref[rows, HEAD_DIM:2 * HEAD_DIM] = ones
            kbw_ref[rows, :] = kw_ref[rows, :]
            vbw_ref[rows, 0:HEAD_DIM] = vw_ref[rows, :]
            vbw_ref[rows, HEAD_DIM:2 * HEAD_DIM] = ones
            return carry

        lax.fori_loop(0, seq // tk, prep, 0)

    qscale = (HEAD_DIM ** -0.5) * LOG2E
    for r in range(n_rep):
        qs_ref[r * tq:(r + 1) * tq, 0:HEAD_DIM] = (q_ref[:, r * HEAD_DIM:(r + 1) * HEAD_DIM].astype(F32)
                                                   * qscale).astype(BF16)

    for m_ref, accl_ref in ((ms_ref, as_ref), (mw_ref, aw_ref)):
        m_ref[...] = jnp.full(m_ref.shape, NEG_INF, F32)
        accl_ref[...] = jnp.zeros(accl_ref.shape, F32)

    q_idx = lax.broadcasted_iota(jnp.int32, (tq, tk), 0)
    k_idx = lax.broadcasted_iota(jnp.int32, (tq, tk), 1)

    def tile_bias(off, max_dist):
        dist = off * tk + q_idx - k_idx
        if off == 0:
            ok = dist >= 0
        elif max_dist is not None and (off + 1) * tk - 1 > max_dist:
            ok = dist <= max_dist
        else:
            return None
        return jnp.where(ok, 0.0, NEG_INF)

    def issue(tiles, load_q, kb_ref):
        scores = []
        for kt, tbias, n_tiles in tiles:
            k0 = pl.multiple_of(kt * tk, tk)
            scores.append((_dot_nt(load_q(), kb_ref[pl.ds(k0, n_tiles * tk), :]), k0, tbias))
        return scores

    def finish(scores, vb_ref, m_ref, accl_ref):
        for s_tile, k0, tbias in scores:
            pts, alphas = [], []
            for c in range(n_ch):
                rows = slice(c * ch, (c + 1) * ch)
                s = s_tile[rows, :]
                if tbias is not None:
                    b0 = (c % n_sub) * ch
                    s = s + tbias[b0:b0 + ch, :]
                m_old = m_ref[rows, :]
                m_new = jnp.maximum(m_old, jnp.max(s, axis=-1, keepdims=True))
                alphas.append(jnp.exp2(m_old - m_new))
                width = s.shape[1]
                pts.append(jnp.exp2((s - jnp.concatenate([m_new] * (width // LANE), axis=1)).astype(BF16)))
                m_ref[rows, :] = m_new
            pv = _dot(jnp.concatenate(pts, axis=0), vb_ref[pl.ds(k0, width), :])
            for c in range(n_ch):
                rows = slice(c * ch, (c + 1) * ch)
                alpha2 = jnp.concatenate([alphas[c], alphas[c]], axis=1)
                accl_ref[rows, :] = alpha2 * accl_ref[rows, :] + pv[rows, :]

    max_dist = WIN_LEN - 1
    n_prev = -(-max_dist // tk)
    assert n_prev == 2
    in_seq = jnp.where(q_idx + k_idx >= 0, 0.0, NEG_INF)
    gone = in_seq + NEG_INF
    diag = tile_bias(0, max_dist)
    slot0 = jnp.where(i >= 2, tile_bias(2, max_dist), jnp.where(i == 1, in_seq, diag))
    slot1 = jnp.where(i >= 2, in_seq, jnp.where(i == 1, diag, gone))
    slot2 = jnp.where(i >= 2, diag, gone)
    win_tiles = [(jnp.maximum(i - 2, 0), jnp.concatenate([slot0, slot1, slot2], axis=1), 3)]
    win_scores = issue(win_tiles, lambda: qs_ref[:, 0:HEAD_DIM], kbw_ref)

    s_all = _dot_nt(qs_ref[:, 0:HEAD_DIM], kc_ref[...])
    n_cols = lax.broadcasted_iota(jnp.int32, (ch, N_SUB), 1)
    bias, has_any = [], []
    for c in range(n_sub):
        t = q0 + c * ch + lax.broadcasted_iota(jnp.int32, (ch, N_SUB), 0)
        bias.append(jnp.where((n_cols * CMP_STRIDE + (CMP_LEN - 1)) <= t, 0.0, NEG_INF))
        has_any.append(t[:, 0:1] >= CMP_LEN - 1)
    ps = []
    psum = [jnp.zeros((ch, N_SUB), F32) for _ in range(n_sub)]
    for c in range(n_ch):
        s = s_all[c * ch:(c + 1) * ch, :] + bias[c % n_sub]
        e = jnp.exp2(s - jnp.max(s, axis=-1, keepdims=True))
        den = jnp.sum(e, axis=-1, keepdims=True)
        p = e * jnp.where(has_any[c % n_sub], 1.0 / den, 0.0)
        ps.append(p.astype(BF16))
        psum[c % n_sub] = psum[c % n_sub] + p
    oc_ref[...] = _dot(jnp.concatenate(ps, axis=0), vc_ref[...])
    psum = jnp.concatenate(psum, axis=0)

    hi = psum.astype(BF16)
    lo = (psum - hi.astype(F32)).astype(BF16)
    ovt = ovt_ref[...]
    imp_t = _dot_nt(ovt, hi) + _dot_nt(ovt, lo)
    jrow = lax.broadcasted_iota(jnp.int32, (LANE, tq), 0)
    tl = q0 + lax.broadcasted_iota(jnp.int32, (LANE, tq), 1)
    jt = tl >> SEL_SHIFT
    forced = (jrow == 0) | (jrow == jt) | (jrow == jt - 1)
    valid_s = jrow * SEL_LEN <= tl
    sc_ref[...] = jnp.where(forced, 1e4, jnp.where(valid_s, imp_t, -1.0))
    finish(win_scores, vbw_ref, mw_ref, aw_ref)
    n_grp = N_SEL // SUBLANE
    grp = [sc_ref[g * SUBLANE:(g + 1) * SUBLANE, :] for g in range(n_grp)]
    cnt = [jnp.zeros((SUBLANE, tq), F32) for _ in range(n_grp)]
    sub = lax.broadcasted_iota(jnp.int32, (SUBLANE, tq), 0)
    for j in range(N_SEL):
        row = sc_ref[j:j + 1, :]
        for g in range(n_grp):
            if g < j // SUBLANE:
                beats = row > grp[g]
            elif g > j // SUBLANE:
                beats = row >= grp[g]
            else:
                beats = (row > grp[g]) | ((row == grp[g]) & (sub > j % SUBLANE))
            cnt[g] = cnt[g] + jnp.where(beats, 1.0, 0.0)
    pen_t = [jnp.where(c < float(SEL_TOP), 0.0, NEG_INF) for c in cnt]
    pen_t = jnp.concatenate(pen_t + [jnp.full((LANE - N_SEL, tq), NEG_INF, F32)], axis=0)
    pen = pen_t.T.astype(BF16)
    for r in range(n_rep):
        qs_ref[r * tq:(r + 1) * tq, HEAD_DIM:HEAD_DIM + LANE] = pen

    def sel_steps(first, n_full, diagonal):
        tiles = [(first + u, None, SEL_GROUP) for u in range(0, n_full - n_full % SEL_GROUP, SEL_GROUP)]
        if n_full % SEL_GROUP:
            tiles.append((first + n_full - n_full % SEL_GROUP, None, n_full % SEL_GROUP))
        if diagonal:
            tiles.append((i, tile_bias(0, None), 1))
        finish(issue(tiles, lambda: qs_ref[...], kbs_ref), vbs_ref, ms_ref, as_ref)

    def sel_body(j, carry):
        sel_steps(SEL_UNROLL * j, SEL_UNROLL, False)
        return carry

    lax.fori_loop(0, i // SEL_UNROLL, sel_body, 0)

    for rem in range(SEL_UNROLL):
        @pl.when(i % SEL_UNROLL == rem)
        def _(rem=rem):
            sel_steps(i - rem, rem, True)

    g = 1.0 / (1.0 + jnp.exp(-gate_ref[...].astype(F32)))
    lane = lax.broadcasted_iota(jnp.int32, g.shape, 1)
    head0 = pl.program_id(1) * n_rep

    def gate(head, branch):
        return jnp.sum(jnp.where(lane == head * 3 + branch, g, 0.0), axis=-1, keepdims=True)

    for r in range(n_rep):
        rows = slice(r * tq, (r + 1) * tq)
        o = (gate(head0 + r, 0) * oc_ref[rows, :]
             + gate(head0 + r, 1) * (as_ref[rows, 0:HEAD_DIM] / as_ref[rows, HEAD_DIM:2 * HEAD_DIM])
             + gate(head0 + r, 2) * (aw_ref[rows, 0:HEAD_DIM] / aw_ref[rows, HEAD_DIM:2 * HEAD_DIM]))
        o_ref[:, r * HEAD_DIM:(r + 1) * HEAD_DIM] = o.astype(o_ref.dtype)


def nsa_attention(z, kvc, ovt, cast=(), cast_layer=0, *, n_batch, seq):
    tq = ATT_TQ
    nq = seq // tq
    n_rep = C_GROUP
    qw = n_rep * HEAD_DIM
    kv_blk = (seq, HEAD_DIM)
    sel0, win0 = (COL_SEL - Z_OFF) // HEAD_DIM, (COL_WIN - Z_OFF) // HEAD_DIM
    cast_in, cast_out, cast_shapes = _cast_specs(
        cast, cast_layer, lambda b, g, i: (b * C_KV_HEADS + g) * nq + i, n_batch * C_KV_HEADS * nq)
    res = pl.pallas_call(
        functools.partial(_nsa_kernel, n_cast=len(cast)),
        out_shape=[jax.ShapeDtypeStruct((n_batch * seq, C_HEADS * HEAD_DIM), BF16)] + cast_shapes,
        grid=(n_batch, C_KV_HEADS, nq),
        in_specs=[
            pl.BlockSpec((tq, qw), lambda b, g, i: (b * nq + i, (COL_CQ - Z_OFF) // qw + g)),
            pl.BlockSpec((N_SUB, HEAD_DIM), lambda b, g, i: (b * N_CMP + g, 0)),
            pl.BlockSpec((N_SUB, HEAD_DIM), lambda b, g, i: (b * N_CMP + C_KV_HEADS + g, 0)),
            pl.BlockSpec((LANE, N_SUB), lambda b, g, i: (0, 0)),
            pl.BlockSpec(kv_blk, lambda b, g, i: (b, sel0 + g)),
            pl.BlockSpec(kv_blk, lambda b, g, i: (b, sel0 + C_KV_HEADS + g)),
            pl.BlockSpec(kv_blk, lambda b, g, i: (b, win0 + g)),
            pl.BlockSpec(kv_blk, lambda b, g, i: (b, win0 + C_KV_HEADS + g)),
            pl.BlockSpec((tq, LANE), lambda b, g, i: (b * nq + i, (COL_GATE - Z_OFF) // LANE)),
        ] + cast_in,
        out_specs=[pl.BlockSpec((tq, qw), lambda b, g, i: (b * nq + i, g))] + cast_out,
        scratch_shapes=[
            pltpu.VMEM((n_rep * tq, HEAD_DIM + LANE), BF16),
            pltpu.VMEM((seq, HEAD_DIM + LANE), BF16),
            pltpu.VMEM((seq, 2 * HEAD_DIM), BF16),
            pltpu.VMEM((seq, HEAD_DIM), BF16),
            pltpu.VMEM((seq, 2 * HEAD_DIM), BF16),
            pltpu.VMEM((n_rep * tq, LANE), F32),
            pltpu.VMEM((n_rep * tq, 2 * HEAD_DIM), F32),
            pltpu.VMEM((n_rep * tq, LANE), F32),
            pltpu.VMEM((n_rep * tq, 2 * HEAD_DIM), F32),
            pltpu.VMEM((n_rep * tq, HEAD_DIM), F32),
            pltpu.VMEM((LANE, tq), F32),
        ],
        compiler_params=_cparams(("parallel", "parallel", "arbitrary")),
        name="nsa_attention",
    )(z, kvc, kvc, ovt, z, z, z, z, z, *cast)
    return res[0], res[1:]


def _overlap_t():
    n_c = N_SUB - 1
    c_start = np.arange(n_c) * CMP_STRIDE
    s_start = np.arange(N_SEL) * SEL_LEN
    ov = ((c_start[:, None] <= s_start[None, :] + SEL_LEN - 1)
          & (c_start[:, None] + CMP_LEN - 1 >= s_start[None, :])).astype(np.float32)
    out = np.zeros((LANE, N_SUB), np.float32)
    out[:N_SEL, :n_c] = ov.T
    return out


def _mix_out_kernel(x_ref, oa_ref, ob_ref, oc_ref, w_ref, o_ref):
    mix = jnp.concatenate([oa_ref[...], ob_ref[...], oc_ref[...]], axis=1)
    o_ref[...] = x_ref[...] + _dot(mix, w_ref[...])


def mix_out(x, out_a, out_b, out_c, w, layer, *, tm):
    m, n = x.shape
    return pl.pallas_call(
        _mix_out_kernel,
        out_shape=jax.ShapeDtypeStruct((m, n), F32),
        grid=(m // tm,),
        in_specs=[
            pl.BlockSpec((tm, n), lambda i: (i, 0)),
            pl.BlockSpec((tm, out_a.shape[1]), lambda i: (i, 0)),
            pl.BlockSpec((tm, out_b.shape[1]), lambda i: (i, 0)),
            pl.BlockSpec((tm, out_c.shape[1]), lambda i: (i, 0)),
            pl.BlockSpec((None,) + w.shape[1:], lambda i: (layer, 0, 0), pipeline_mode=pl.Buffered(1)),
        ],
        out_specs=pl.BlockSpec((tm, n), lambda i: (i, 0)),
        compiler_params=_cparams(("parallel",)),
        name="mix_out",
    )(x, out_a, out_b, out_c, w)


def _xattn_kernel(x_ref, g_ref, wq_ref, k_ref, v_ref, wo_ref, o_ref):
    n_mem = k_ref.shape[0]
    rows = x_ref.shape[0]
    ones = jnp.ones((n_mem, HEAD_DIM), BF16)
    kv = [(k_ref[:, h * HEAD_DIM:(h + 1) * HEAD_DIM].astype(BF16),
           jnp.concatenate([v_ref[:, h * HEAD_DIM:(h + 1) * HEAD_DIM].astype(BF16), ones], axis=1))
          for h in range(X_HEADS)]
    for c in range(rows // NORM_ROWS):
        r = slice(c * NORM_ROWS, (c + 1) * NORM_ROWS)
        x = x_ref[r, :]
        hn = _rms_rows(x, g_ref[...]).astype(BF16)
        q = _dot(hn, wq_ref[...]) * ((HEAD_DIM ** -0.5) * LOG2E)
        outs = []
        for h in range(X_HEADS):
            s = _dot_nt(q[:, h * HEAD_DIM:(h + 1) * HEAD_DIM].astype(BF16), kv[h][0])
            e = jnp.exp2((s - jnp.max(s, axis=-1, keepdims=True)).astype(BF16))
            pv = _dot(e, kv[h][1])
            outs.append((pv[:, 0:HEAD_DIM] / pv[:, HEAD_DIM:2 * HEAD_DIM]).astype(BF16))
        o_ref[r, :] = x + _dot(jnp.concatenate(outs, axis=1), wo_ref[...])


def cross_attention_block(x, g, wq, kv, wo, layer, *, n_batch, seq, n_mem, tm):
    nq = seq // tm
    d = x.shape[1]
    return pl.pallas_call(
        _xattn_kernel,
        out_shape=jax.ShapeDtypeStruct(x.shape, F32),
        grid=(n_batch, nq),
        in_specs=[
            pl.BlockSpec((tm, d), lambda b, i: (b * nq + i, 0)),
            pl.BlockSpec((1, d), lambda b, i: (0, 0)),
            pl.BlockSpec((None, d, X_WIDTH), lambda b, i: (layer, 0, 0), pipeline_mode=pl.Buffered(1)),
            pl.BlockSpec((n_mem, X_WIDTH), lambda b, i: (b, 0)),
            pl.BlockSpec((n_mem, X_WIDTH), lambda b, i: (b, 1)),
            pl.BlockSpec((None, X_WIDTH, d), lambda b, i: (layer, 0, 0), pipeline_mode=pl.Buffered(1)),
        ],
        out_specs=pl.BlockSpec((tm, d), lambda b, i: (b * nq + i, 0)),
        compiler_params=_cparams(("parallel", "parallel")),
        name="cross_attention",
    )(x, g.reshape(1, d), wq, kv, kv, wo)


def kernel(x, mem, norm_mix, w_in, gmlp_ln_g, gmlp_ln_b, gmlp_w_s, gmlp_b_s, cmp_pos, cmp_k_w1, cmp_k_w2,
           cmp_v_w1, cmp_v_w2, w_out, norm_xattn, norm_mem, xattn_wq, xattn_wkv, xattn_wo, norm_mlp, w_up,
           w_down, final_norm):
    bsz, seq, d = x.shape
    n_mem = mem.shape[1]
    t = bsz * seq
    assert d == D_MODEL and seq // CMP_STRIDE == N_SUB and seq // SEL_LEN == N_SEL
    assert seq % (DIL_KW * max(dil for _, dil in DILATED_PAIRS)) == 0 and seq % ATT_TQ == 0

    w_in_b = jnp.pad(w_in, ((0, 0), (0, 0), (0, IN_PAD - IN_WIDTH))).astype(BF16)
    big = (w_out, w_up, w_down)
    wq_b = xattn_wq.astype(BF16)
    wkv_b = xattn_wkv.astype(BF16)
    wo_b = xattn_wo.astype(BF16)
    cmp_w1 = jnp.stack([cmp_k_w1, cmp_v_w1], axis=1).astype(BF16)
    cmp_w2 = jnp.stack([cmp_k_w2, cmp_v_w2], axis=1).astype(BF16)
    ovt = jnp.asarray(_overlap_t()).astype(BF16)
    xf = x.reshape(t, d)
    memf = mem.reshape(bsz * n_mem, d)

    for l in range(DEPTH):
        z, out_a = in_proj(xf, norm_mix[l], w_in_b, l, gmlp_ln_g[l], gmlp_ln_b[l], gmlp_w_s[l], gmlp_b_s[l],
                           tm=TM_PROJ)

        out_b = dilated_attention(z, n_batch=bsz, seq=seq)

        kvc = compress(z, cmp_pos[l], cmp_w1, cmp_w2, l, n_batch=bsz, seq=seq)
        out_c, cast0 = nsa_attention(z, kvc, ovt, big if l == 0 else (), 0, n_batch=bsz, seq=seq)
        if l == 0:
            w_out_l, w_up_l, w_down_l = (w[None] for w in cast0)

        xf = mix_out(xf, out_a, out_b, out_c, w_out_l, 0, tm=TM_PROJ)

        kv = norm_matmul(memf, norm_mem[l], wkv_b, l, tm=TM_PROJ)
        xf = cross_attention_block(xf, norm_xattn[l], wq_b, kv, wo_b, l, n_batch=bsz, seq=seq, n_mem=n_mem,
                                   tm=TM_XATTN)

        nxt = () if l == DEPTH - 1 else big
        xf, nxt_b = mlp(xf, norm_mlp[l], w_up_l, w_down_l, 0, final_norm, nxt, l + 1, tm=TM_MLP, tf=TF_MLP,
                        norm_out=(l == DEPTH - 1))
        if nxt_b:
            w_out_l, w_up_l, w_down_l = (w[None] for w in nxt_b)

    return xf.reshape(bsz, seq, d)
```
